```python
import jax
import jax.numpy as jnp
from jax import lax
import numpy as np

D_MODEL = 1024
BATCH = 8
SEQ = 2048
DEPTH = 1
DEC_BATCH = 128
DEC_SEQ = 4
PAST_LEN = 16384
PAGE_SIZE = 128

PLE_DIM = 256
GDN_WIDTH = D_MODEL // 2
GDN_HEAD_DIM = 128
GDN_HEADS = GDN_WIDTH // GDN_HEAD_DIM
GDN_QKV_DIM = 3 * GDN_WIDTH
GDN_CONV = 4
GDN_CHUNK = 64
SC_WIDTH = D_MODEL - GDN_WIDTH
SC_CONV = 3
D_FF = 4 * D_MODEL
EPS = 1e-6
OFF_QKV = 0
OFF_A = OFF_QKV + GDN_QKV_DIM
OFF_B = OFF_A + GDN_HEADS
OFF_Z = OFF_B + GDN_HEADS
OFF_SC_B = OFF_Z + GDN_WIDTH
OFF_SC_C = OFF_SC_B + SC_WIDTH
OFF_SC_H = OFF_SC_C + SC_WIDTH
IN_DIM = OFF_SC_H + SC_WIDTH

kernel_name = "hymba_gdn_shortconv_decode_step"


def rmsnorm(x, w):
    xf = x.astype(jnp.float32)
    y = xf * lax.rsqrt(jnp.mean(xf * xf, axis=-1, keepdims=True) + EPS)
    return (y * w.astype(jnp.float32)).astype(x.dtype)


def l2norm(x):
    xf = x.astype(jnp.float32)
    return xf * lax.rsqrt(jnp.sum(xf * xf, axis=-1, keepdims=True) + EPS)


def causal_dwconv(xp, w):
    width = w.shape[0]
    t = xp.shape[1] - (width - 1)
    out = xp[:, 0:t] * w[0]
    for j in range(1, width):
        out = out + xp[:, j:j + t] * w[j]
    return out


def gated_delta_rule(q, k, v, log_a, beta, s0):
    bsz, t, nh, dk = q.shape
    dv = v.shape[-1]
    c = GDN_CHUNK if t % GDN_CHUNK == 0 else t
    n = t // c

    def to_chunks(a):
        a = a.reshape(bsz, n, c, nh, *a.shape[3:])
        return jnp.moveaxis(a, 3, 1)

    q, k, v, log_a, beta = (to_chunks(a) for a in (q, k, v, log_a, beta))
    g = jnp.cumsum(log_a, axis=-1)
    idx = jnp.arange(c)
    lower = idx[:, None] >= idx[None, :]
    strict = idx[:, None] > idx[None, :]
    diff = g[..., :, None] - g[..., None, :]
    decay = jnp.where(lower, jnp.exp(jnp.where(lower, diff, 0.0)), 0.0)
    kk = jnp.einsum("bhnid,bhnjd->bhnij", k, k)
    a_mat = jnp.eye(c, dtype=jnp.float32) + jnp.where(strict, beta[..., :, None] * kk * decay, 0.0)
    rhs = jnp.concatenate([v * beta[..., None], k * (beta * jnp.exp(g))[..., None]], axis=-1)
    sol = lax.linalg.triangular_solve(a_mat, rhs, left_side=True, lower=True, unit_diagonal=True)
    u, w = sol[..., :dv], sol[..., dv:]
    qk = jnp.einsum("bhnid,bhnjd->bhnij", q, k) * decay

    def step(s, xs):
        q_c, k_c, u_c, w_c, g_c, qk_c = xs
        v_new = u_c - jnp.einsum("bhcd,bhde->bhce", w_c, s)
        o_c = (jnp.einsum("bhcd,bhde->bhce", q_c * jnp.exp(g_c)[..., None], s)
               + jnp.einsum("bhij,bhje->bhie", qk_c, v_new))
        g_last = g_c[..., -1:]
        s = (s * jnp.exp(g_last)[..., None]
             + jnp.einsum("bhcd,bhce->bhde", k_c * jnp.exp(g_last - g_c)[..., None], v_new))
        return s, o_c

    xs = tuple(jnp.moveaxis(a, 2, 0) for a in (q, k, u, w, g, qk))
    s_final, o = lax.scan(step, s0, xs)
    o = jnp.transpose(o, (1, 0, 3, 2, 4)).reshape(bsz, t, nh, dv)
    return o, s_final


def hybrid_layer(x, p, conv_qkv_buf, s0, conv_sc_buf, norm_mix, w_in, w_conv_qkv, a_log, dt_bias,
                 w_gdn_norm, w_conv_sc, w_out, norm_mlp, w_up, w_down, norm_ple, w_ple_gate, w_ple_proj):
    bsz, t, _ = x.shape
    h = rmsnorm(x, norm_mix)
    proj = h @ w_in
    qkv_in = jnp.concatenate([conv_qkv_buf.astype(proj.dtype), proj[..., OFF_QKV:OFF_A]], axis=1)
    qkv = jax.nn.silu(causal_dwconv(qkv_in, w_conv_qkv))
    q, k, v = jnp.split(qkv, 3, axis=-1)
    q = l2norm(q.reshape(bsz, t, GDN_HEADS, GDN_HEAD_DIM)) * (GDN_HEAD_DIM ** -0.5)
    k = l2norm(k.reshape(bsz, t, GDN_HEADS, GDN_HEAD_DIM))
    v = v.reshape(bsz, t, GDN_HEADS, GDN_HEAD_DIM).astype(jnp.float32)
    log_a = -jnp.exp(a_log.astype(jnp.float32)) * jax.nn.softplus(
        proj[..., OFF_A:OFF_B].astype(jnp.float32) + dt_bias.astype(jnp.float32))
    beta = jax.nn.sigmoid(proj[..., OFF_B:OFF_Z].astype(jnp.float32))
    o, s_new = gated_delta_rule(q, k, v, log_a, beta, s0.astype(jnp.float32))
    z = proj[..., OFF_Z:OFF_SC_B].reshape(bsz, t, GDN_HEADS, GDN_HEAD_DIM).astype(jnp.float32)
    o = (rmsnorm(o, w_gdn_norm) * jax.nn.silu(z)).reshape(bsz, t, GDN_WIDTH).astype(x.dtype)
    sc_in = jnp.concatenate([conv_sc_buf.astype(proj.dtype),
                             proj[..., OFF_SC_C:OFF_SC_H] * proj[..., OFF_SC_H:IN_DIM]], axis=1)
    y_sc = proj[..., OFF_SC_B:OFF_SC_C] * causal_dwconv(sc_in, w_conv_sc)
    x = x + jnp.concatenate([o, y_sc.astype(x.dtype)], axis=-1) @ w_out
    u_ff = jax.nn.relu(rmsnorm(x, norm_mlp) @ w_up)
    x = x + (u_ff * u_ff) @ w_down
    gate = jax.nn.sigmoid(rmsnorm(x, norm_ple) @ w_ple_gate)
    x = x + gate * (p @ w_ple_proj)
    return x, qkv_in[:, -(GDN_CONV - 1):], s_new.astype(s0.dtype), sc_in[:, -(SC_CONV - 1):]


def setup_inputs(seed: int = 0) -> dict:
    key = jax.random.key(seed)
    ks = jax.random.split(key, 24)
    f32 = jnp.float32
    nrm = lambda k_, shape, scale: jax.random.normal(k_, shape, f32) * scale
    gain = lambda k_, shape: 1.0 + 0.02 * jax.random.normal(k_, shape, f32)
    a_init = jax.random.uniform(ks[10], (DEPTH, GDN_HEADS), f32, 1.0, 16.0)
    dt = jnp.exp(jax.random.uniform(ks[11], (DEPTH, GDN_HEADS), f32, np.log(1e-3), np.log(1e-1)))
    return {
        "x_prompt": nrm(ks[0], (BATCH, SEQ, D_MODEL), 1.0),
        "x_sample": nrm(ks[1], (DEC_BATCH, DEC_SEQ, D_MODEL), 1.0),
        "state_gdn_conv": nrm(ks[2], (DEPTH, DEC_BATCH, GDN_CONV - 1, GDN_QKV_DIM), 1.0),
        "state_gdn": nrm(ks[3], (DEPTH, DEC_BATCH, GDN_HEADS, GDN_HEAD_DIM, GDN_HEAD_DIM), 0.3),
        "state_sc_conv": nrm(ks[4], (DEPTH, DEC_BATCH, SC_CONV - 1, SC_WIDTH), 1.0),
        "p_prompt": nrm(ks[5], (DEPTH, BATCH, SEQ, PLE_DIM), 1.0),
        "p_sample": nrm(ks[6], (DEPTH, DEC_BATCH, DEC_SEQ, PLE_DIM), 1.0),
        "norm_mix": gain(ks[7], (DEPTH, D_MODEL)),
        "w_in": nrm(ks[8], (DEPTH, D_MODEL, IN_DIM), D_MODEL ** -0.5),
        "w_conv_qkv": nrm(ks[9], (DEPTH, GDN_CONV, GDN_QKV_DIM), GDN_CONV ** -0.5),
        "a_log": jnp.log(a_init),
        "dt_bias": dt + jnp.log(-jnp.expm1(-dt)),
        "w_gdn_norm": gain(ks[12], (DEPTH, GDN_HEAD_DIM)),
        "w_conv_sc": nrm(ks[13], (DEPTH, SC_CONV, SC_WIDTH), SC_CONV ** -0.5),
        "w_out": nrm(ks[14], (DEPTH, D_MODEL, D_MODEL), D_MODEL ** -0.5),
        "norm_mlp": gain(ks[15], (DEPTH, D_MODEL)),
        "w_up": nrm(ks[16], (DEPTH, D_MODEL, D_FF), D_MODEL ** -0.5),
        "w_down": nrm(ks[17], (DEPTH, D_FF, D_MODEL), D_FF ** -0.5),
        "norm_ple": gain(ks[18], (DEPTH, D_MODEL)),
        "w_ple_gate": nrm(ks[19], (DEPTH, D_MODEL, D_MODEL), D_MODEL ** -0.5),
        "w_ple_proj": nrm(ks[20], (DEPTH, PLE_DIM, D_MODEL), PLE_DIM ** -0.5),
        "norm_f": gain(ks[21], (D_MODEL,)),
    }


def reference(x_prompt, x_sample, state_gdn_conv, state_gdn, state_sc_conv, p_prompt, p_sample,
              norm_mix, w_in, w_conv_qkv, a_log, dt_bias, w_gdn_norm, w_conv_sc, w_out,
              norm_mlp, w_up, w_down, norm_ple, w_ple_gate, w_ple_proj, norm_f):
    bp = x_prompt.shape[0]
    dt_ = x_prompt.dtype
    zero_conv_qkv = jnp.zeros((bp, GDN_CONV - 1, GDN_QKV_DIM), dt_)
    zero_s = jnp.zeros((bp, GDN_HEADS, GDN_HEAD_DIM, GDN_HEAD_DIM), dt_)
    zero_conv_sc = jnp.zeros((bp, SC_CONV - 1, SC_WIDTH), dt_)
    hp, hs = x_prompt, x_sample
    pc, ps, psc, sc, ss, ssc = [], [], [], [], [], []
    for i in range(DEPTH):
        wts = (norm_mix[i], w_in[i], w_conv_qkv[i], a_log[i], dt_bias[i], w_gdn_norm[i], w_conv_sc[i],
               w_out[i], norm_mlp[i], w_up[i], w_down[i], norm_ple[i], w_ple_gate[i], w_ple_proj[i])
        hp, c1, s1, c2 = hybrid_layer(hp, p_prompt[i], zero_conv_qkv, zero_s, zero_conv_sc, *wts)
        hs, c3, s3, c4 = hybrid_layer(hs, p_sample[i], state_gdn_conv[i], state_gdn[i], state_sc_conv[i], *wts)
        pc.append(c1); ps.append(s1); psc.append(c2)
        sc.append(c3); ss.append(s3); ssc.append(c4)
    y_prompt = rmsnorm(hp, norm_f)
    y_sample = rmsnorm(hs, norm_f)
    return (y_prompt, y_sample, jnp.stack(pc), jnp.stack(ps), jnp.stack(psc),
            jnp.stack(sc), jnp.stack(ss), jnp.stack(ssc))
```

```python
import functools

import jax
import jax.numpy as jnp
from jax import lax
from jax.experimental import pallas as pl
from jax.experimental.pallas import tpu as pltpu

F32 = jnp.float32
BF16 = jnp.bfloat16

D_MODEL = 1024
PLE_DIM = 256
GDN_HEADS = 4
HEAD_DIM = 128
GDN_WIDTH = GDN_HEADS * HEAD_DIM
QKV_DIM = 3 * GDN_WIDTH
GDN_CONV = 4
SC_WIDTH = D_MODEL - GDN_WIDTH
SC_CONV = 3
D_FF = 4 * D_MODEL
EPS = 1e-6
CHUNK = 64
LANES = 128
SUBLANES = 8
VMEM_LIMIT_BYTES = 56 * 1024 * 1024

C_QKV = 0
C_Z = C_QKV + QKV_DIM
C_SCB = C_Z + GDN_WIDTH
C_SCC = C_SCB + SC_WIDTH
C_SCH = C_SCC + SC_WIDTH
C_AB = C_SCH + SC_WIDTH
W1_COLS = C_AB + LANES


def _mm(a, b):
    return jnp.dot(a.astype(BF16), b.astype(BF16), preferred_element_type=F32)


def _mm_nt(a, b):
    return lax.dot_general(a.astype(BF16), b.astype(BF16), (((1,), (1,)), ((), ())), preferred_element_type=F32)


def _mm_tn(a, b):
    return lax.dot_general(a.astype(BF16), b.astype(BF16), (((0,), (0,)), ((), ())), preferred_element_type=F32)


def _rmsnorm(x, w_row):
    return x * lax.rsqrt(jnp.mean(x * x, axis=-1, keepdims=True) + EPS) * w_row


def _sigmoid(x):
    return 1.0 / (1.0 + jnp.exp(-x))


def _silu(x):
    return x * _sigmoid(x)


def _softplus(x):
    return jnp.maximum(x, 0.0) + jnp.log1p(jnp.exp(-jnp.abs(x)))


def _log2(n):
    k = n.bit_length() - 1
    assert (1 << k) == n, n
    return k


def _wy_block(q, k, v, g_col, g_row, beta_col, seg_len, load_state, store_state):
    c = CHUNK
    n_seg = c // seg_len
    lg = _log2(seg_len)
    ri = lax.broadcasted_iota(jnp.int32, (c, c), 0)
    ci = lax.broadcasted_iota(jnp.int32, (c, c), 1)
    same = jnp.right_shift(ri, lg) == jnp.right_shift(ci, lg)
    lower = same & (ri >= ci)
    strict = same & (ri > ci)
    diff = g_col - g_row
    decay = jnp.where(lower, jnp.exp(jnp.where(lower, diff, 0.0)), 0.0)
    a_off = jnp.where(strict, beta_col * _mm_nt(k, k) * decay, 0.0)
    eye = jnp.where(ri == ci, 1.0, 0.0).astype(F32)
    x_inv = eye
    for lb in range(lg):
        rb = jnp.right_shift(ri, lb)
        cb = jnp.right_shift(ci, lb)
        join = (jnp.bitwise_and(rb, 1) == 1) & (cb == rb - 1)
        b_lvl = jnp.where(join, a_off, 0.0)
        if lb == 0:
            x_inv = eye - b_lvl
        else:
            x_inv = x_inv - _mm(_mm(x_inv, b_lvl), x_inv)
    rhs = jnp.concatenate([v * beta_col, k * (beta_col * jnp.exp(g_col))], axis=1)
    sol = _mm(x_inv, rhs)
    u = sol[:, :HEAD_DIM]
    w = sol[:, HEAD_DIM:]
    qk = _mm_nt(q, k) * decay
    stacked = jnp.concatenate([w, q * jnp.exp(g_col)], axis=0)
    row = lax.broadcasted_iota(jnp.int32, (c, 1), 0)
    states = [load_state(s) for s in range(n_seg)]
    w_s = None
    q_s = None
    for s in range(n_seg):
        r = _mm(stacked, states[s])
        if n_seg == 1:
            w_s, q_s = r[:c], r[c:]
        else:
            in_seg = jnp.right_shift(row, lg) == s
            w_s = jnp.where(in_seg, r[:c], 0.0 if w_s is None else w_s)
            q_s = jnp.where(in_seg, r[c:], 0.0 if q_s is None else q_s)
    v_new = u - w_s
    o = q_s + _mm(qk, v_new)
    for s in range(n_seg):
        last = s * seg_len + seg_len - 1
        g_last = g_col[last:last + 1, :]
        if n_seg == 1:
            k_dec = k * jnp.exp(g_last - g_col)
        else:
            in_seg = jnp.right_shift(row, lg) == s
            k_dec = jnp.where(in_seg, k * jnp.exp(jnp.where(in_seg, g_last - g_col, 0.0)), 0.0)
        store_state(s, states[s] * jnp.exp(g_last) + _mm_tn(k_dec, v_new))
    return o


def _shifted_history(buf, new, rows, n_taps, hist_buf, seg_len):
    buf[SUBLANES:SUBLANES + rows, :] = new
    out = []
    for s in range(1, n_taps):
        raw = buf[pl.ds(SUBLANES - s, rows), :]
        if hist_buf is not None:
            t = jnp.bitwise_and(lax.broadcasted_iota(jnp.int32, (rows, 1), 0), seg_len - 1)
            raw = jnp.where(t >= s, raw, hist_buf[pl.ds(n_taps - 1 - s, rows), :])
        out.append(raw)
    return out


def _mixer_kernel(rows, seg_len, per_seq_hist,
                  x_ref, hq_ref, hs_ref, s_in_ref, nmix_ref, w1_ref, cwq_ref, gate_ref, gnorm_ref, cws_ref, wout_ref,
                  xmid_ref, qtail_ref, stail_ref, s_out_ref, qbuf, sbuf, *hist_bufs):
    first = pl.program_id(1) == 0
    n_chunks = rows // CHUNK
    x = x_ref[...]
    h = _rmsnorm(x, nmix_ref[...])
    proj = _mm(h, w1_ref[...])

    @pl.when(first)
    def _():
        s_out_ref[...] = s_in_ref[...]

    qkv_pre = proj[:, C_QKV:C_QKV + QKV_DIM]
    sc_pre = proj[:, C_SCC:C_SCC + SC_WIDTH] * proj[:, C_SCH:C_SCH + SC_WIDTH]
    if per_seq_hist:
        hq_buf, hs_buf = hist_bufs
        zq = jnp.zeros((SUBLANES, QKV_DIM), F32)
        zs = jnp.zeros((SUBLANES, SC_WIDTH), F32)
        qbuf[0:SUBLANES, :] = zq
        sbuf[0:SUBLANES, :] = zs
        hq_buf[0:rows, :] = hq_ref[...]
        hq_buf[rows:rows + SUBLANES, :] = zq
        hs_buf[0:rows, :] = hs_ref[...]
        hs_buf[rows:rows + SUBLANES, :] = zs
    else:
        hq_buf = hs_buf = None

        @pl.when(first)
        def _():
            qbuf[0:SUBLANES, :] = hq_ref[...]
            sbuf[0:SUBLANES, :] = hs_ref[...]

    q1, q2, q3 = _shifted_history(qbuf, qkv_pre, rows, GDN_CONV, hq_buf, seg_len)
    cwq = cwq_ref[...]
    qkv = q3 * cwq[0:1, :] + q2 * cwq[1:2, :] + q1 * cwq[2:3, :] + qkv_pre * cwq[3:4, :]
    qkv = _silu(qkv)
    s1, s2 = _shifted_history(sbuf, sc_pre, rows, SC_CONV, hs_buf, seg_len)
    cws = cws_ref[...]
    y_sc = proj[:, C_SCB:C_SCB + SC_WIDTH] * (s2 * cws[0:1, :] + s1 * cws[1:2, :] + sc_pre * cws[2:3, :])
    if per_seq_hist:
        qtail_ref[...] = qkv_pre
        stail_ref[...] = sc_pre
    else:
        q_tail = qbuf[rows:rows + SUBLANES, :]
        s_tail = sbuf[rows:rows + SUBLANES, :]
        qtail_ref[...] = q_tail
        stail_ref[...] = s_tail
        qbuf[0:SUBLANES, :] = q_tail
        sbuf[0:SUBLANES, :] = s_tail

    ab = proj[:, C_AB:C_AB + LANES]
    gate = gate_ref[...]
    log_a = -jnp.exp(gate[0:1, :]) * _softplus(ab + gate[1:2, :])
    beta = _sigmoid(ab)
    pos = jnp.bitwise_and(lax.broadcasted_iota(jnp.int32, (rows, 1), 0), seg_len - 1)
    g = log_a
    shift = 1
    while shift < seg_len:
        g = g + jnp.where(pos >= shift, pltpu.roll(g, shift, 0), 0.0)
        shift *= 2
    g_t = g.T

    gnorm = gnorm_ref[...]
    o_heads = []
    for hd in range(GDN_HEADS):
        lo = hd * HEAD_DIM
        q_h = qkv[:, lo:lo + HEAD_DIM]
        k_h = qkv[:, GDN_WIDTH + lo:GDN_WIDTH + lo + HEAD_DIM]
        v_h = qkv[:, 2 * GDN_WIDTH + lo:2 * GDN_WIDTH + lo + HEAD_DIM]
        q_h = q_h * lax.rsqrt(jnp.sum(q_h * q_h, axis=-1, keepdims=True) + EPS) * (HEAD_DIM ** -0.5)
        k_h = k_h * lax.rsqrt(jnp.sum(k_h * k_h, axis=-1, keepdims=True) + EPS)
        o_blocks = []
        for cb in range(n_chunks):
            r0 = cb * CHUNK
            n_seg = CHUNK // seg_len

            base = cb * n_seg if per_seq_hist else 0

            def load_state(s, hd=hd, base=base):
                return s_out_ref[base + s, hd]

            def store_state(s, val, hd=hd, base=base):
                s_out_ref[base + s, hd] = val

            o_blocks.append(_wy_block(
                q_h[r0:r0 + CHUNK], k_h[r0:r0 + CHUNK], v_h[r0:r0 + CHUNK],
                g[r0:r0 + CHUNK, hd:hd + 1], g_t[hd:hd + 1, r0:r0 + CHUNK],
                beta[r0:r0 + CHUNK, GDN_HEADS + hd:GDN_HEADS + hd + 1],
                seg_len, load_state, store_state))
        o_h = o_blocks[0] if n_chunks == 1 else jnp.concatenate(o_blocks, axis=0)
        z_h = proj[:, C_Z + lo:C_Z + lo + HEAD_DIM]
        o_heads.append(_rmsnorm(o_h, gnorm) * _silu(z_h))

    mix = jnp.concatenate(o_heads + [y_sc], axis=1)
    xmid_ref[...] = x + _mm(mix, wout_ref[...])


def _ffn_kernel(x_ref, p_ref, nmlp_ref, wup_ref, wdown_ref, nple_ref, wg_ref, wp_ref, nf_ref, y_ref):
    x = x_ref[...]
    hn = _rmsnorm(x, nmlp_ref[...]).astype(BF16)
    acc = x
    for j in range(D_FF // D_MODEL):
        u = jnp.maximum(_mm(hn, wup_ref[:, j * D_MODEL:(j + 1) * D_MODEL]), 0.0)
        acc = acc + _mm(u * u, wdown_ref[j * D_MODEL:(j + 1) * D_MODEL, :])
    gate = _sigmoid(_mm(_rmsnorm(acc, nple_ref[...]), wg_ref[...]))
    x3 = acc + gate * _mm(p_ref[...], wp_ref[...])
    y_ref[...] = _rmsnorm(x3, nf_ref[...])


def _const_spec(shape):
    nd = len(shape)
    return pl.BlockSpec(shape, lambda *_: (0,) * nd, pipeline_mode=pl.Buffered(1))


def _mixer_call(x2d, hist_q, hist_s, s_in, weights, *, n_seq, seq_rows, rows, seg_len, per_seq_hist):
    nmix, w1, cwq, gate, gnorm, cws, wout = weights
    tiles = seq_rows // rows
    n_state = rows // seg_len if per_seq_hist else 1
    row_map = lambda b, i: (b * tiles + i, 0)
    if per_seq_hist:
        hq_spec = pl.BlockSpec((rows, QKV_DIM), row_map)
        hs_spec = pl.BlockSpec((rows, SC_WIDTH), row_map)
        qtail = jax.ShapeDtypeStruct((n_seq * seq_rows, QKV_DIM), F32)
        stail = jax.ShapeDtypeStruct((n_seq * seq_rows, SC_WIDTH), F32)
        qtail_spec, stail_spec = hq_spec, hs_spec
        hist_scratch = [pltpu.VMEM((rows + SUBLANES, QKV_DIM), F32), pltpu.VMEM((rows + SUBLANES, SC_WIDTH), F32)]
    else:
        hq_spec = pl.BlockSpec((SUBLANES, QKV_DIM), lambda b, i: (b, 0))
        hs_spec = pl.BlockSpec((SUBLANES, SC_WIDTH), lambda b, i: (b, 0))
        qtail = jax.ShapeDtypeStruct((n_seq * SUBLANES, QKV_DIM), F32)
        stail = jax.ShapeDtypeStruct((n_seq * SUBLANES, SC_WIDTH), F32)
        qtail_spec, stail_spec = hq_spec, hs_spec
        hist_scratch = []
    state_spec = pl.BlockSpec((n_state, GDN_HEADS, HEAD_DIM, HEAD_DIM), lambda b, i: (b, 0, 0, 0))
    kernel = functools.partial(_mixer_kernel, rows, seg_len, per_seq_hist)
    return pl.pallas_call(
        kernel,
        grid=(n_seq, tiles),
        in_specs=[
            pl.BlockSpec((rows, D_MODEL), row_map), hq_spec, hs_spec, state_spec,
            _const_spec(nmix.shape), _const_spec(w1.shape), _const_spec(cwq.shape), _const_spec(gate.shape),
            _const_spec(gnorm.shape), _const_spec(cws.shape), _const_spec(wout.shape),
        ],
        out_specs=[pl.BlockSpec((rows, D_MODEL), row_map), qtail_spec, stail_spec, state_spec],
        out_shape=[jax.ShapeDtypeStruct(x2d.shape, F32), qtail, stail, jax.ShapeDtypeStruct(s_in.shape, F32)],
        scratch_shapes=[pltpu.VMEM((rows + SUBLANES, QKV_DIM), F32), pltpu.VMEM((rows + SUBLANES, SC_WIDTH), F32)]
        + hist_scratch,
        compiler_params=pltpu.CompilerParams(
            dimension_semantics=("arbitrary", "arbitrary"), vmem_limit_bytes=VMEM_LIMIT_BYTES),
        name="mixer_seq" if not per_seq_hist else "mixer_step",
    )(x2d, hist_q, hist_s, s_in, nmix, w1, cwq, gate, gnorm, cws, wout)


def _ffn_call(x2d, p2d, weights, *, rows, name):
    nmlp, wup, wdown, nple, wg, wp, nf = weights
    n = x2d.shape[0]
    row_map = lambda i: (i, 0)
    return pl.pallas_call(
        _ffn_kernel,
        grid=(n // rows,),
        in_specs=[pl.BlockSpec((rows, D_MODEL), row_map), pl.BlockSpec((rows, PLE_DIM), row_map)]
        + [_const_spec(w.shape) for w in weights],
        out_specs=pl.BlockSpec((rows, D_MODEL), row_map),
        out_shape=jax.ShapeDtypeStruct(x2d.shape, F32),
        compiler_params=pltpu.CompilerParams(dimension_semantics=("arbitrary",), vmem_limit_bytes=VMEM_LIMIT_BYTES),
        name=name,
    )(x2d, p2d, *weights)


def _pad_rows_front(a, rows):
    b, r, c = a.shape
    return jnp.pad(a, ((0, 0), (rows - r, 0), (0, 0))).reshape(b * rows, c)


def _pad_rows_back(a, rows):
    b, r, c = a.shape
    return jnp.pad(a, ((0, 0), (0, rows - r), (0, 0))).reshape(b * rows, c)


def _layer(x_prompt, x_sample, conv_qkv, s_gdn, conv_sc, p_prompt, p_sample, norm_mix, w_in, w_conv_qkv, a_log,
           dt_bias, w_gdn_norm, w_conv_sc, w_out, norm_mlp, w_up, w_down, norm_ple, w_ple_gate, w_ple_proj, norm_f):
    bp, tp, _ = x_prompt.shape
    bs, ts, _ = x_sample.shape
    o_a = QKV_DIM
    o_z = o_a + 2 * GDN_HEADS
    ab_cols = jnp.pad(w_in[:, o_a:o_z], ((0, 0), (0, LANES - 2 * GDN_HEADS)))
    w1 = jnp.concatenate([w_in[:, :QKV_DIM], w_in[:, o_z:], ab_cols], axis=1).astype(BF16)
    gate = jnp.zeros((SUBLANES, LANES), F32)
    gate = gate.at[0, :GDN_HEADS].set(a_log.astype(F32)).at[1, :GDN_HEADS].set(dt_bias.astype(F32))
    mixer_w = (norm_mix.reshape(1, D_MODEL), w1, w_conv_qkv, gate, w_gdn_norm.reshape(1, HEAD_DIM), w_conv_sc,
               w_out.astype(BF16))
    ffn_w = (norm_mlp.reshape(1, D_MODEL), w_up.astype(BF16), w_down.astype(BF16), norm_ple.reshape(1, D_MODEL),
             w_ple_gate.astype(BF16), w_ple_proj.astype(BF16), norm_f.reshape(1, D_MODEL))

    zq = jnp.zeros((bp * SUBLANES, QKV_DIM), F32)
    zs = jnp.zeros((bp * SUBLANES, SC_WIDTH), F32)
    s0 = jnp.zeros((bp, GDN_HEADS, HEAD_DIM, HEAD_DIM), F32)
    xm_p, qt_p, st_p, s_p = _mixer_call(
        x_prompt.reshape(bp * tp, D_MODEL), zq, zs, s0, mixer_w,
        n_seq=bp, seq_rows=tp, rows=256, seg_len=CHUNK, per_seq_hist=False)
    y_p = _ffn_call(xm_p, p_prompt.reshape(bp * tp, PLE_DIM), ffn_w, rows=512, name="ffn_prompt")

    seq_per_tile = CHUNK // ts
    xm_s, qt_s, st_s, s_s = _mixer_call(
        x_sample.reshape(bs * ts, D_MODEL), _pad_rows_back(conv_qkv, ts), _pad_rows_back(conv_sc, ts), s_gdn,
        mixer_w, n_seq=bs // seq_per_tile, seq_rows=CHUNK, rows=CHUNK, seg_len=ts, per_seq_hist=True)
    y_s = _ffn_call(xm_s, p_sample.reshape(bs * ts, PLE_DIM), ffn_w, rows=bs * ts, name="ffn_sample")

    new_conv_p = qt_p.reshape(bp, SUBLANES, QKV_DIM)[:, SUBLANES - (GDN_CONV - 1):]
    new_sc_p = st_p.reshape(bp, SUBLANES, SC_WIDTH)[:, SUBLANES - (SC_CONV - 1):]
    new_conv_s = qt_s.reshape(bs, ts, QKV_DIM)[:, ts - (GDN_CONV - 1):]
    new_sc_s = st_s.reshape(bs, ts, SC_WIDTH)[:, ts - (SC_CONV - 1):]
    return (y_p.reshape(bp, tp, D_MODEL), y_s.reshape(bs, ts, D_MODEL), new_conv_p, s_p, new_sc_p,
            new_conv_s, s_s, new_sc_s)


def kernel(x_prompt, x_sample, state_gdn_conv, state_gdn, state_sc_conv, p_prompt, p_sample, norm_mix, w_in, w_conv_qkv, a_log, dt_bias, w_gdn_norm, w_conv_sc, w_out, norm_mlp, w_up, w_down, norm_ple, w_ple_gate, w_ple_proj, norm_f):
    depth = w_in.shape[0]
    assert depth == 1, "one layer per call"
    assert x_sample.shape[1] >= GDN_CONV - 1 and CHUNK % x_sample.shape[1] == 0
    assert x_prompt.shape[1] % 256 == 0
    outs = _layer(x_prompt, x_sample, state_gdn_conv[0], state_gdn[0], state_sc_conv[0], p_prompt[0], p_sample[0],
                  norm_mix[0], w_in[0], w_conv_qkv[0], a_log[0], dt_bias[0], w_gdn_norm[0], w_conv_sc[0], w_out[0],
                  norm_mlp[0], w_up[0], w_down[0], norm_ple[0], w_ple_gate[0], w_ple_proj[0], norm_f)
    y_p, y_s, c_p, s_p, sc_p, c_s, s_s, sc_s = outs
    return (y_p, y_s, c_p[None], s_p[None], sc_p[None], c_s[None], s_s[None], sc_s[None])
```

```python
import functools

import jax
import jax.numpy as jnp
from jax import lax
from jax.experimental import pallas as pl
from jax.experimental.pallas import tpu as pltpu

F32 = jnp.float32
BF16 = jnp.bfloat16

D_MODEL = 1024
PLE_DIM = 256
GDN_HEADS = 4
HEAD_DIM = 128
GDN_WIDTH = GDN_HEADS * HEAD_DIM
QKV_DIM = 3 * GDN_WIDTH
GDN_CONV = 4
SC_WIDTH = D_MODEL - GDN_WIDTH
SC_CONV = 3
D_FF = 4 * D_MODEL
EPS = 1e-6
CHUNK = 64
LANES = 128
SUBLANES = 8
VMEM_LIMIT_BYTES = 56 * 1024 * 1024

C_QKV = 0
C_Z = C_QKV + QKV_DIM
C_SCB = C_Z + GDN_WIDTH
C_SCC = C_SCB + SC_WIDTH
C_SCH = C_SCC + SC_WIDTH
C_AB = C_SCH + SC_WIDTH
W1_COLS = C_AB + LANES


def _mm(a, b):
    return jnp.dot(a.astype(BF16), b.astype(BF16), preferred_element_type=F32)


def _mm_nt(a, b):
    return lax.dot_general(a.astype(BF16), b.astype(BF16), (((1,), (1,)), ((), ())), preferred_element_type=F32)


def _mm_tn(a, b):
    return lax.dot_general(a.astype(BF16), b.astype(BF16), (((0,), (0,)), ((), ())), preferred_element_type=F32)


def _rmsnorm(x, w_row):
    return x * lax.rsqrt(jnp.mean(x * x, axis=-1, keepdims=True) + EPS) * w_row


def _sigmoid(x):
    return 1.0 / (1.0 + jnp.exp(-x))


def _silu(x):
    return x * _sigmoid(x)


def _softplus(x):
    return jnp.maximum(x, 0.0) + jnp.log1p(jnp.exp(-jnp.abs(x)))


def _log2(n):
    k = n.bit_length() - 1
    assert (1 << k) == n, n
    return k


def _wy_block(q, k, v, g_col, g_row, beta_col, seg_len, load_state, store_state):
    c = CHUNK
    n_seg = c // seg_len
    lg = _log2(seg_len)
    ri = lax.broadcasted_iota(jnp.int32, (c, c), 0)
    ci = lax.broadcasted_iota(jnp.int32, (c, c), 1)
    same = jnp.right_shift(ri, lg) == jnp.right_shift(ci, lg)
    lower = same & (ri >= ci)
    strict = same & (ri > ci)
    diff = g_col - g_row
    decay = jnp.where(lower, jnp.exp(jnp.where(lower, diff, 0.0)), 0.0)
    a_off = jnp.where(strict, beta_col * _mm_nt(k, k) * decay, 0.0)
    eye = jnp.where(ri == ci, 1.0, 0.0).astype(F32)
    x_inv = eye
    for lb in range(lg):
        rb = jnp.right_shift(ri, lb)
        cb = jnp.right_shift(ci, lb)
        join = (jnp.bitwise_and(rb, 1) == 1) & (cb == rb - 1)
        b_lvl = jnp.where(join, a_off, 0.0)
        if lb == 0:
            x_inv = eye - b_lvl
        else:
            x_inv = x_inv - _mm(_mm(x_inv, b_lvl), x_inv)
    rhs = jnp.concatenate([v * beta_col, k * (beta_col * jnp.exp(g_col))], axis=1)
    sol = _mm(x_inv, rhs)
    u = sol[:, :HEAD_DIM]
    w = sol[:, HEAD_DIM:]
    qk = _mm_nt(q, k) * decay
    stacked = jnp.concatenate([w, q * jnp.exp(g_col)], axis=0)
    row = lax.broadcasted_iota(jnp.int32, (c, 1), 0)
    states = [load_state(s) for s in range(n_seg)]
    w_s = None
    q_s = None
    for s in range(n_seg):
        r = _mm(stacked, states[s])
        if n_seg == 1:
            w_s, q_s = r[:c], r[c:]
        else:
            in_seg = jnp.right_shift(row, lg) == s
            w_s = jnp.where(in_seg, r[:c], 0.0 if w_s is None else w_s)
            q_s = jnp.where(in_seg, r[c:], 0.0 if q_s is None else q_s)
    v_new = u - w_s
    o = q_s + _mm(qk, v_new)
    for s in range(n_seg):
        last = s * seg_len + seg_len - 1
        g_last = g_col[last:last + 1, :]
        if n_seg == 1:
            k_dec = k * jnp.exp(g_last - g_col)
        else:
            in_seg = jnp.right_shift(row, lg) == s
            k_dec = jnp.where(in_seg, k * jnp.exp(jnp.where(in_seg, g_last - g_col, 0.0)), 0.0)
        store_state(s, states[s] * jnp.exp(g_last) + _mm_tn(k_dec, v_new))
    return o


def _expand_heads(slab, first_lane, width):
    rows = slab.shape[0]
    cols = [slab[:, first_lane + hd:first_lane + hd + 1] for hd in range(GDN_HEADS)]
    if width % LANES == 0:
        return jnp.concatenate([jnp.broadcast_to(c, (rows, width)) for c in cols], axis=1)
    total = GDN_HEADS * width
    lane_head = jnp.right_shift(lax.broadcasted_iota(jnp.int32, (rows, total), 1), _log2(width))
    out = jnp.broadcast_to(cols[-1], (rows, total))
    for hd in range(GDN_HEADS - 2, -1, -1):
        out = jnp.where(lane_head == hd, jnp.broadcast_to(cols[hd], (rows, total)), out)
    return out


def _wy_tile_packed(q_all, k_all, v_all, g, beta, s_ref, rows):
    c, nh = CHUNK, GDN_HEADS
    w = nh * c
    lg = _log2(c)
    n_chunks = rows // c
    ri = lax.broadcasted_iota(jnp.int32, (c, w), 0)
    cj = jnp.bitwise_and(lax.broadcasted_iota(jnp.int32, (c, w), 1), c - 1)
    lower, strict, eye = ri >= cj, ri > cj, ri == cj
    joins = [(jnp.bitwise_and(jnp.right_shift(ri, lb), 1) == 1)
             & (jnp.right_shift(cj, lb) == jnp.right_shift(ri, lb) - 1) for lb in range(lg)]
    bd_mask = (jnp.right_shift(lax.broadcasted_iota(jnp.int32, (w, w), 0), lg)
               == jnp.right_shift(lax.broadcasted_iota(jnp.int32, (w, w), 1), lg))
    k_mask = (jnp.right_shift(lax.broadcasted_iota(jnp.int32, (w, nh * HEAD_DIM), 0), lg)
              == jnp.right_shift(lax.broadcasted_iota(jnp.int32, (w, nh * HEAD_DIM), 1), _log2(HEAD_DIM)))

    def block_diag(x):
        return jnp.where(bd_mask, jnp.concatenate([x] * nh, axis=0), 0.0).astype(BF16)

    def head(x, hd, width):
        return x[:, hd * width:(hd + 1) * width]

    g_c = _expand_heads(g, 0, c)
    b_c = _expand_heads(beta, nh, c)
    g_d = _expand_heads(g, 0, HEAD_DIM)
    b_d = _expand_heads(beta, nh, HEAD_DIM)
    eg_d = jnp.exp(g_d)
    v_rhs = v_all * b_d
    k_rhs = k_all * (b_d * eg_d)
    q_g = q_all * eg_d

    a_off, qk_dec = [], []
    for cb in range(n_chunks):
        rs = slice(cb * c, (cb + 1) * c)
        g_row = jnp.sum(jnp.where(eye, g_c[rs], 0.0), axis=0, keepdims=True)
        decay = jnp.where(lower, jnp.exp(jnp.where(lower, g_c[rs] - g_row, 0.0)), 0.0)
        k_bd = jnp.where(k_mask, jnp.concatenate([k_all[rs]] * nh, axis=0), 0.0)
        kq = _mm_nt(jnp.concatenate([k_all[rs], q_all[rs]], axis=0), k_bd)
        a_off.append(jnp.where(strict, b_c[rs] * kq[:c] * decay, 0.0))
        qk_dec.append(kq[c:] * decay)

    eye_f = jnp.where(eye, 1.0, 0.0).astype(F32)
    x_inv = [eye_f - jnp.where(joins[0], a, 0.0) for a in a_off]
    for lb in range(1, lg):
        xb = [jnp.dot(x.astype(BF16), block_diag(jnp.where(joins[lb], a, 0.0)), preferred_element_type=F32)
              for x, a in zip(x_inv, a_off)]
        xbx = [jnp.dot(t.astype(BF16), block_diag(x), preferred_element_type=F32) for t, x in zip(xb, x_inv)]
        x_inv = [x - t for x, t in zip(x_inv, xbx)]

    u, wk = [], []
    for cb in range(n_chunks):
        rs = slice(cb * c, (cb + 1) * c)
        sol = [_mm(head(x_inv[cb], hd, c),
                   jnp.concatenate([head(v_rhs[rs], hd, HEAD_DIM), head(k_rhs[rs], hd, HEAD_DIM)], axis=1))
               for hd in range(nh)]
        u.append([s[:, :HEAD_DIM] for s in sol])
        wk.append([s[:, HEAD_DIM:] for s in sol])

    state = [s_ref[0, hd] for hd in range(nh)]
    o_blocks = []
    for cb in range(n_chunks):
        rs = slice(cb * c, (cb + 1) * c)
        g_last = g_d[(cb + 1) * c - 1:(cb + 1) * c, :]
        k_dec = k_all[rs] * jnp.exp(g_last - g_d[rs])
        eg_last = jnp.exp(g_last)
        r = [_mm(jnp.concatenate([wk[cb][hd], head(q_g[rs], hd, HEAD_DIM)], axis=0), state[hd]) for hd in range(nh)]
        v_new = [u[cb][hd] - r[hd][:c] for hd in range(nh)]
        o_blocks.append(jnp.concatenate(
            [r[hd][c:] + _mm(head(qk_dec[cb], hd, c), v_new[hd]) for hd in range(nh)], axis=1))
        state = [state[hd] * head(eg_last, hd, HEAD_DIM) + _mm_tn(head(k_dec, hd, HEAD_DIM), v_new[hd])
                 for hd in range(nh)]
    for hd in range(nh):
        s_ref[0, hd] = state[hd]
    return jnp.concatenate(o_blocks, axis=0)


def _shifted_history(buf, new, rows, n_taps, hist_buf, seg_len):
    buf[SUBLANES:SUBLANES + rows, :] = new
    out = []
    for s in range(1, n_taps):
        raw = buf[pl.ds(SUBLANES - s, rows), :]
        if hist_buf is not None:
            t = jnp.bitwise_and(lax.broadcasted_iota(jnp.int32, (rows, 1), 0), seg_len - 1)
            raw = jnp.where(t >= s, raw, hist_buf[pl.ds(n_taps - 1 - s, rows), :])
        out.append(raw)
    return out


def _mixer_kernel(rows, seg_len, per_seq_hist,
                  x_ref, hq_ref, hs_ref, s_in_ref, nmix_ref, w1_ref, cwq_ref, gate_ref, gnorm_ref, cws_ref, wout_ref,
                  xmid_ref, qtail_ref, stail_ref, s_out_ref, qbuf, sbuf, *hist_bufs):
    first = pl.program_id(1) == 0
    n_chunks = rows // CHUNK
    x = x_ref[...]
    h = _rmsnorm(x, nmix_ref[...])
    proj = _mm(h, w1_ref[...])

    @pl.when(first)
    def _():
        s_out_ref[...] = s_in_ref[...]

    qkv_pre = proj[:, C_QKV:C_QKV + QKV_DIM]
    sc_pre = proj[:, C_SCC:C_SCC + SC_WIDTH] * proj[:, C_SCH:C_SCH + SC_WIDTH]
    if per_seq_hist:
        hq_buf, hs_buf = hist_bufs
        zq = jnp.zeros((SUBLANES, QKV_DIM), F32)
        zs = jnp.zeros((SUBLANES, SC_WIDTH), F32)
        qbuf[0:SUBLANES, :] = zq
        sbuf[0:SUBLANES, :] = zs
        hq_buf[0:rows, :] = hq_ref[...]
        hq_buf[rows:rows + SUBLANES, :] = zq
        hs_buf[0:rows, :] = hs_ref[...]
        hs_buf[rows:rows + SUBLANES, :] = zs
    else:
        hq_buf = hs_buf = None

        @pl.when(first)
        def _():
            qbuf[0:SUBLANES, :] = hq_ref[...]
            sbuf[0:SUBLANES, :] = hs_ref[...]

    q1, q2, q3 = _shifted_history(qbuf, qkv_pre, rows, GDN_CONV, hq_buf, seg_len)
    cwq = cwq_ref[...]
    qkv = q3 * cwq[0:1, :] + q2 * cwq[1:2, :] + q1 * cwq[2:3, :] + qkv_pre * cwq[3:4, :]
    qkv = _silu(qkv)
    s1, s2 = _shifted_history(sbuf, sc_pre, rows, SC_CONV, hs_buf, seg_len)
    cws = cws_ref[...]
    y_sc = proj[:, C_SCB:C_SCB + SC_WIDTH] * (s2 * cws[0:1, :] + s1 * cws[1:2, :] + sc_pre * cws[2:3, :])
    if per_seq_hist:
        qtail_ref[...] = qkv_pre
        stail_ref[...] = sc_pre
    else:
        q_tail = qbuf[rows:rows + SUBLANES, :]
        s_tail = sbuf[rows:rows + SUBLANES, :]
        qtail_ref[...] = q_tail
        stail_ref[...] = s_tail
        qbuf[0:SUBLANES, :] = q_tail
        sbuf[0:SUBLANES, :] = s_tail

    ab = proj[:, C_AB:C_AB + LANES]
    gate = gate_ref[...]
    log_a = -jnp.exp(gate[0:1, :]) * _softplus(ab + gate[1:2, :])
    beta = _sigmoid(ab)
    pos = jnp.bitwise_and(lax.broadcasted_iota(jnp.int32, (rows, 1), 0), seg_len - 1)
    g = log_a
    shift = 1
    while shift < seg_len:
        g = g + jnp.where(pos >= shift, pltpu.roll(g, shift, 0), 0.0)
        shift *= 2

    q_n, k_n, v_n = [], [], []
    for hd in range(GDN_HEADS):
        lo = hd * HEAD_DIM
        q_h = qkv[:, lo:lo + HEAD_DIM]
        k_h = qkv[:, GDN_WIDTH + lo:GDN_WIDTH + lo + HEAD_DIM]
        q_n.append(q_h * lax.rsqrt(jnp.sum(q_h * q_h, axis=-1, keepdims=True) + EPS) * (HEAD_DIM ** -0.5))
        k_n.append(k_h * lax.rsqrt(jnp.sum(k_h * k_h, axis=-1, keepdims=True) + EPS))
        v_n.append(qkv[:, 2 * GDN_WIDTH + lo:2 * GDN_WIDTH + lo + HEAD_DIM])
    if per_seq_hist:
        g_t = g.T
        n_seg = CHUNK // seg_len
        o_h = []
        for hd in range(GDN_HEADS):
            o_blocks = []
            for cb in range(n_chunks):
                r0 = cb * CHUNK

                def load_state(s, hd=hd, base=cb * n_seg):
                    return s_out_ref[base + s, hd]

                def store_state(s, val, hd=hd, base=cb * n_seg):
                    s_out_ref[base + s, hd] = val

                o_blocks.append(_wy_block(
                    q_n[hd][r0:r0 + CHUNK], k_n[hd][r0:r0 + CHUNK], v_n[hd][r0:r0 + CHUNK],
                    g[r0:r0 + CHUNK, hd:hd + 1], g_t[hd:hd + 1, r0:r0 + CHUNK],
                    beta[r0:r0 + CHUNK, GDN_HEADS + hd:GDN_HEADS + hd + 1],
                    seg_len, load_state, store_state))
            o_h.append(o_blocks[0] if n_chunks == 1 else jnp.concatenate(o_blocks, axis=0))
    else:
        o_all = _wy_tile_packed(jnp.concatenate(q_n, axis=1), jnp.concatenate(k_n, axis=1),
                                jnp.concatenate(v_n, axis=1), g, beta, s_out_ref, rows)
        o_h = [o_all[:, hd * HEAD_DIM:(hd + 1) * HEAD_DIM] for hd in range(GDN_HEADS)]
    gnorm = gnorm_ref[...]
    o_heads = []
    for hd in range(GDN_HEADS):
        z_h = proj[:, C_Z + hd * HEAD_DIM:C_Z + (hd + 1) * HEAD_DIM]
        o_heads.append(_rmsnorm(o_h[hd], gnorm) * _silu(z_h))

    mix = jnp.concatenate(o_heads + [y_sc], axis=1)
    xmid_ref[...] = x + _mm(mix, wout_ref[...])


def _ffn_kernel(x_ref, p_ref, nmlp_ref, wup_ref, wdown_ref, nple_ref, wg_ref, wp_ref, nf_ref, y_ref):
    x = x_ref[...]
    hn = _rmsnorm(x, nmlp_ref[...]).astype(BF16)
    acc = x
    for j in range(D_FF // D_MODEL):
        u = jnp.maximum(_mm(hn, wup_ref[:, j * D_MODEL:(j + 1) * D_MODEL]), 0.0)
        acc = acc + _mm(u * u, wdown_ref[j * D_MODEL:(j + 1) * D_MODEL, :])
    gate = _sigmoid(_mm(_rmsnorm(acc, nple_ref[...]), wg_ref[...]))
    x3 = acc + gate * _mm(p_ref[...], wp_ref[...])
    y_ref[...] = _rmsnorm(x3, nf_ref[...])


def _const_spec(shape):
    nd = len(shape)
    return pl.BlockSpec(shape, lambda *_: (0,) * nd, pipeline_mode=pl.Buffered(1))


def _mixer_call(x2d, hist_q, hist_s, s_in, weights, *, n_seq, seq_rows, rows, seg_len, per_seq_hist):
    nmix, w1, cwq, gate, gnorm, cws, wout = weights
    tiles = seq_rows // rows
    n_state = rows // seg_len if per_seq_hist else 1
    row_map = lambda b, i: (b * tiles + i, 0)
    if per_seq_hist:
        hq_spec = pl.BlockSpec((rows, QKV_DIM), row_map)
        hs_spec = pl.BlockSpec((rows, SC_WIDTH), row_map)
        qtail = jax.ShapeDtypeStruct((n_seq * seq_rows, QKV_DIM), F32)
        stail = jax.ShapeDtypeStruct((n_seq * seq_rows, SC_WIDTH), F32)
        qtail_spec, stail_spec = hq_spec, hs_spec
        hist_scratch = [pltpu.VMEM((rows + SUBLANES, QKV_DIM), F32), pltpu.VMEM((rows + SUBLANES, SC_WIDTH), F32)]
    else:
        hq_spec = pl.BlockSpec((SUBLANES, QKV_DIM), lambda b, i: (b, 0))
        hs_spec = pl.BlockSpec((SUBLANES, SC_WIDTH), lambda b, i: (b, 0))
        qtail = jax.ShapeDtypeStruct((n_seq * SUBLANES, QKV_DIM), F32)
        stail = jax.ShapeDtypeStruct((n_seq * SUBLANES, SC_WIDTH), F32)
        qtail_spec, stail_spec = hq_spec, hs_spec
        hist_scratch = []
    state_spec = pl.BlockSpec((n_state, GDN_HEADS, HEAD_DIM, HEAD_DIM), lambda b, i: (b, 0, 0, 0))
    kernel = functools.partial(_mixer_kernel, rows, seg_len, per_seq_hist)
    return pl.pallas_call(
        kernel,
        grid=(n_seq, tiles),
        in_specs=[
            pl.BlockSpec((rows, D_MODEL), row_map), hq_spec, hs_spec, state_spec,
            _const_spec(nmix.shape), _const_spec(w1.shape), _const_spec(cwq.shape), _const_spec(gate.shape),
            _const_spec(gnorm.shape), _const_spec(cws.shape), _const_spec(wout.shape),
        ],
        out_specs=[pl.BlockSpec((rows, D_MODEL), row_map), qtail_spec, stail_spec, state_spec],
        out_shape=[jax.ShapeDtypeStruct(x2d.shape, F32), qtail, stail, jax.ShapeDtypeStruct(s_in.shape, F32)],
        scratch_shapes=[pltpu.VMEM((rows + SUBLANES, QKV_DIM), F32), pltpu.VMEM((rows + SUBLANES, SC_WIDTH), F32)]
        + hist_scratch,
        compiler_params=pltpu.CompilerParams(
            dimension_semantics=("arbitrary", "arbitrary"), vmem_limit_bytes=VMEM_LIMIT_BYTES),
        name="mixer_seq" if not per_seq_hist else "mixer_step",
    )(x2d, hist_q, hist_s, s_in, nmix, w1, cwq, gate, gnorm, cws, wout)


def _ffn_call(x2d, p2d, weights, *, rows, name):
    nmlp, wup, wdown, nple, wg, wp, nf = weights
    n = x2d.shape[0]
    row_map = lambda i: (i, 0)
    return pl.pallas_call(
        _ffn_kernel,
        grid=(n // rows,),
        in_specs=[pl.BlockSpec((rows, D_MODEL), row_map), pl.BlockSpec((rows, PLE_DIM), row_map)]
        + [_const_spec(w.shape) for w in weights],
        out_specs=pl.BlockSpec((rows, D_MODEL), row_map),
        out_shape=jax.ShapeDtypeStruct(x2d.shape, F32),
        compiler_params=pltpu.CompilerParams(dimension_semantics=("arbitrary",), vmem_limit_bytes=VMEM_LIMIT_BYTES),
        name=name,
    )(x2d, p2d, *weights)


def _pad_rows_front(a, rows):
    b, r, c = a.shape
    return jnp.pad(a, ((0, 0), (rows - r, 0), (0, 0))).reshape(b * rows, c)


def _pad_rows_back(a, rows):
    b, r, c = a.shape
    return jnp.pad(a, ((0, 0), (0, rows - r), (0, 0))).reshape(b * rows, c)


def _layer(x_prompt, x_sample, conv_qkv, s_gdn, conv_sc, p_prompt, p_sample, norm_mix, w_in, w_conv_qkv, a_log,
           dt_bias, w_gdn_norm, w_conv_sc, w_out, norm_mlp, w_up, w_down, norm_ple, w_ple_gate, w_ple_proj, norm_f):
    bp, tp, _ = x_prompt.shape
    bs, ts, _ = x_sample.shape
    o_a = QKV_DIM
    o_z = o_a + 2 * GDN_HEADS
    ab_cols = jnp.pad(w_in[:, o_a:o_z], ((0, 0), (0, LANES - 2 * GDN_HEADS)))
    w1 = jnp.concatenate([w_in[:, :QKV_DIM], w_in[:, o_z:], ab_cols], axis=1).astype(BF16)
    gate = jnp.zeros((SUBLANES, LANES), F32)
    gate = gate.at[0, :GDN_HEADS].set(a_log.astype(F32)).at[1, :GDN_HEADS].set(dt_bias.astype(F32))
    mixer_w = (norm_mix.reshape(1, D_MODEL), w1, w_conv_qkv, gate, w_gdn_norm.reshape(1, HEAD_DIM), w_conv_sc,
               w_out.astype(BF16))
    ffn_w = (norm_mlp.reshape(1, D_MODEL), w_up.astype(BF16), w_down.astype(BF16), norm_ple.reshape(1, D_MODEL),
             w_ple_gate.astype(BF16), w_ple_proj.astype(BF16), norm_f.reshape(1, D_MODEL))

    zq = jnp.zeros((bp * SUBLANES, QKV_DIM), F32)
    zs = jnp.zeros((bp * SUBLANES, SC_WIDTH), F32)
    s0 = jnp.zeros((bp, GDN_HEADS, HEAD_DIM, HEAD_DIM), F32)
    xm_p, qt_p, st_p, s_p = _mixer_call(
        x_prompt.reshape(bp * tp, D_MODEL), zq, zs, s0, mixer_w,
        n_seq=bp, seq_rows=tp, rows=256, seg_len=CHUNK, per_seq_hist=False)
    y_p = _ffn_call(xm_p, p_prompt.reshape(bp * tp, PLE_DIM), ffn_w, rows=512, name="ffn_prompt")

    seq_per_tile = CHUNK // ts
    xm_s, qt_s, st_s, s_s = _mixer_call(
        x_sample.reshape(bs * ts, D_MODEL), _pad_rows_back(conv_qkv, ts), _pad_rows_back(conv_sc, ts), s_gdn,
        mixer_w, n_seq=bs // seq_per_tile, seq_rows=CHUNK, rows=CHUNK, seg_len=ts, per_seq_hist=True)
    y_s = _ffn_call(xm_s, p_sample.reshape(bs * ts, PLE_DIM), ffn_w, rows=bs * ts, name="ffn_sample")

    new_conv_p = qt_p.reshape(bp, SUBLANES, QKV_DIM)[:, SUBLANES - (GDN_CONV - 1):]
    new_sc_p = st_p.reshape(bp, SUBLANES, SC_WIDTH)[:, SUBLANES - (SC_CONV - 1):]
    new_conv_s = qt_s.reshape(bs, ts, QKV_DIM)[:, ts - (GDN_CONV - 1):]
    new_sc_s = st_s.reshape(bs, ts, SC_WIDTH)[:, ts - (SC_CONV - 1):]
    return (y_p.reshape(bp, tp, D_MODEL), y_s.reshape(bs, ts, D_MODEL), new_conv_p, s_p, new_sc_p,
            new_conv_s, s_s, new_sc_s)


def kernel(x_prompt, x_sample, state_gdn_conv, state_gdn, state_sc_conv, p_prompt, p_sample, norm_mix, w_in, w_conv_qkv, a_log, dt_bias, w_gdn_norm, w_conv_sc, w_out, norm_mlp, w_up, w_down, norm_ple, w_ple_gate, w_ple_proj, norm_f):
    depth = w_in.shape[0]
    assert depth == 1, "one layer per call"
    assert x_sample.shape[1] >= GDN_CONV - 1 and CHUNK % x_sample.shape[1] == 0
    assert x_prompt.shape[1] % 256 == 0
    outs = _layer(x_prompt, x_sample, state_gdn_conv[0], state_gdn[0], state_sc_conv[0], p_prompt[0], p_sample[0],
                  norm_mix[0], w_in[0], w_conv_qkv[0], a_log[0], dt_bias[0], w_gdn_norm[0], w_conv_sc[0], w_out[0],
                  norm_mlp[0], w_up[0], w_down[0], norm_ple[0], w_ple_gate[0], w_ple_proj[0], norm_f)
    y_p, y_s, c_p, s_p, sc_p, c_s, s_s, sc_s = outs
    return (y_p, y_s, c_p[None], s_p[None], sc_p[None], c_s[None], s_s[None], sc_s[None])
```

```python
import functools

import jax
import jax.numpy as jnp
from jax import lax
from jax.experimental import pallas as pl
from jax.experimental.pallas import tpu as pltpu

F32 = jnp.float32
BF16 = jnp.bfloat16

D_MODEL = 1024
PLE_DIM = 256
GDN_HEADS = 4
HEAD_DIM = 128
GDN_WIDTH = GDN_HEADS * HEAD_DIM
QKV_DIM = 3 * GDN_WIDTH
GDN_CONV = 4
SC_WIDTH = D_MODEL - GDN_WIDTH
SC_CONV = 3
D_FF = 4 * D_MODEL
EPS = 1e-6
CHUNK = 64
LANES = 128
SUBLANES = 8
VMEM_LIMIT_BYTES = 56 * 1024 * 1024
PROMPT_ROWS = 256
PROMPT_SEQS_PER_STEP = 2
FFN_ROWS = 512

C_QKV = 0
C_Z = C_QKV + QKV_DIM
C_SCB = C_Z + GDN_WIDTH
C_SCC = C_SCB + SC_WIDTH
C_SCH = C_SCC + SC_WIDTH
C_AB = C_SCH + SC_WIDTH
W1_COLS = C_AB + LANES


def _mm(a, b):
    return jnp.dot(a.astype(BF16), b.astype(BF16), preferred_element_type=F32)


def _mm_nt(a, b):
    return lax.dot_general(a.astype(BF16), b.astype(BF16), (((1,), (1,)), ((), ())), preferred_element_type=F32)


def _mm_tn(a, b):
    return lax.dot_general(a.astype(BF16), b.astype(BF16), (((0,), (0,)), ((), ())), preferred_element_type=F32)


def _rmsnorm(x, w_row):
    return x * lax.rsqrt(jnp.mean(x * x, axis=-1, keepdims=True) + EPS) * w_row


def _sigmoid(x):
    return 1.0 / (1.0 + jnp.exp(-x))


def _silu(x):
    return x * _sigmoid(x)


def _softplus(x):
    return jnp.maximum(x, 0.0) + jnp.log1p(jnp.exp(-jnp.abs(x)))


def _log2(n):
    k = n.bit_length() - 1
    assert (1 << k) == n, n
    return k


def _wy_block(q, k, v, g_col, g_row, beta_col, seg_len, load_state, store_state):
    c = CHUNK
    n_seg = c // seg_len
    lg = _log2(seg_len)
    ri = lax.broadcasted_iota(jnp.int32, (c, c), 0)
    ci = lax.broadcasted_iota(jnp.int32, (c, c), 1)
    same = jnp.right_shift(ri, lg) == jnp.right_shift(ci, lg)
    lower = same & (ri >= ci)
    strict = same & (ri > ci)
    diff = g_col - g_row
    decay = jnp.where(lower, jnp.exp(jnp.where(lower, diff, 0.0)), 0.0)
    a_off = jnp.where(strict, beta_col * _mm_nt(k, k) * decay, 0.0)
    eye = jnp.where(ri == ci, 1.0, 0.0).astype(F32)
    x_inv = eye
    for lb in range(lg):
        rb = jnp.right_shift(ri, lb)
        cb = jnp.right_shift(ci, lb)
        join = (jnp.bitwise_and(rb, 1) == 1) & (cb == rb - 1)
        b_lvl = jnp.where(join, a_off, 0.0)
        if lb == 0:
            x_inv = eye - b_lvl
        else:
            x_inv = x_inv - _mm(_mm(x_inv, b_lvl), x_inv)
    rhs = jnp.concatenate([v * beta_col, k * (beta_col * jnp.exp(g_col))], axis=1)
    sol = _mm(x_inv, rhs)
    u = sol[:, :HEAD_DIM]
    w = sol[:, HEAD_DIM:]
    qk = _mm_nt(q, k) * decay
    stacked = jnp.concatenate([w, q * jnp.exp(g_col)], axis=0)
    row = lax.broadcasted_iota(jnp.int32, (c, 1), 0)
    states = [load_state(s) for s in range(n_seg)]
    w_s = None
    q_s = None
    for s in range(n_seg):
        r = _mm(stacked, states[s])
        if n_seg == 1:
            w_s, q_s = r[:c], r[c:]
        else:
            in_seg = jnp.right_shift(row, lg) == s
            w_s = jnp.where(in_seg, r[:c], 0.0 if w_s is None else w_s)
            q_s = jnp.where(in_seg, r[c:], 0.0 if q_s is None else q_s)
    v_new = u - w_s
    o = q_s + _mm(qk, v_new)
    for s in range(n_seg):
        last = s * seg_len + seg_len - 1
        g_last = g_col[last:last + 1, :]
        if n_seg == 1:
            k_dec = k * jnp.exp(g_last - g_col)
        else:
            in_seg = jnp.right_shift(row, lg) == s
            k_dec = jnp.where(in_seg, k * jnp.exp(jnp.where(in_seg, g_last - g_col, 0.0)), 0.0)
        store_state(s, states[s] * jnp.exp(g_last) + _mm_tn(k_dec, v_new))
    return o


def _expand_heads(slab, first_lane, width):
    rows = slab.shape[0]
    cols = [slab[:, first_lane + hd:first_lane + hd + 1] for hd in range(GDN_HEADS)]
    if width % LANES == 0:
        return jnp.concatenate([jnp.broadcast_to(c, (rows, width)) for c in cols], axis=1)
    total = GDN_HEADS * width
    lane_head = jnp.right_shift(lax.broadcasted_iota(jnp.int32, (rows, total), 1), _log2(width))
    out = jnp.broadcast_to(cols[-1], (rows, total))
    for hd in range(GDN_HEADS - 2, -1, -1):
        out = jnp.where(lane_head == hd, jnp.broadcast_to(cols[hd], (rows, total)), out)
    return out


def _packed_masks():
    c, nh = CHUNK, GDN_HEADS
    w = nh * c
    lg = _log2(c)
    ri = lax.broadcasted_iota(jnp.int32, (c, w), 0)
    cj = jnp.bitwise_and(lax.broadcasted_iota(jnp.int32, (c, w), 1), c - 1)
    joins = [(jnp.bitwise_and(jnp.right_shift(ri, lb), 1) == 1)
             & (jnp.right_shift(cj, lb) == jnp.right_shift(ri, lb) - 1) for lb in range(lg)]
    bd = (jnp.right_shift(lax.broadcasted_iota(jnp.int32, (w, w), 0), lg)
          == jnp.right_shift(lax.broadcasted_iota(jnp.int32, (w, w), 1), lg))
    kbd = (jnp.right_shift(lax.broadcasted_iota(jnp.int32, (w, nh * HEAD_DIM), 0), lg)
           == jnp.right_shift(lax.broadcasted_iota(jnp.int32, (w, nh * HEAD_DIM), 1), _log2(HEAD_DIM)))
    return dict(lower=ri >= cj, strict=ri > cj, eye=ri == cj, joins=joins, bd=bd, kbd=kbd)


def _wy_tile_packed(q_all, k_all, v_all, g, beta, state_ref, rows, masks):
    c, nh = CHUNK, GDN_HEADS
    lg = _log2(c)
    n_chunks = rows // c
    lower, strict, eye = masks["lower"], masks["strict"], masks["eye"]

    def block_diag(x):
        return jnp.where(masks["bd"], jnp.concatenate([x] * nh, axis=0), 0.0).astype(BF16)

    def head(x, hd, width):
        return x[:, hd * width:(hd + 1) * width]

    g_c = _expand_heads(g, 0, c)
    b_c = _expand_heads(beta, nh, c)
    g_d = _expand_heads(g, 0, HEAD_DIM)
    b_d = _expand_heads(beta, nh, HEAD_DIM)
    eg_d = jnp.exp(g_d)
    v_rhs = v_all * b_d
    k_rhs = k_all * (b_d * eg_d)
    q_g = q_all * eg_d

    a_off, qk_dec = [], []
    for cb in range(n_chunks):
        rs = slice(cb * c, (cb + 1) * c)
        g_row = jnp.sum(jnp.where(eye, g_c[rs], 0.0), axis=0, keepdims=True)
        decay = jnp.where(lower, jnp.exp(jnp.where(lower, g_c[rs] - g_row, 0.0)), 0.0)
        k_bd = jnp.where(masks["kbd"], jnp.concatenate([k_all[rs]] * nh, axis=0), 0.0)
        kq = _mm_nt(jnp.concatenate([k_all[rs], q_all[rs]], axis=0), k_bd)
        a_off.append(jnp.where(strict, b_c[rs] * kq[:c] * decay, 0.0))
        qk_dec.append(kq[c:] * decay)
    yield

    eye_f = jnp.where(eye, 1.0, 0.0).astype(F32)
    x_inv = [eye_f - jnp.where(masks["joins"][0], a, 0.0) for a in a_off]
    for lb in range(1, lg):
        xb = [jnp.dot(x.astype(BF16), block_diag(jnp.where(masks["joins"][lb], a, 0.0)),
                      preferred_element_type=F32) for x, a in zip(x_inv, a_off)]
        yield
        xbx = [jnp.dot(t.astype(BF16), block_diag(x), preferred_element_type=F32) for t, x in zip(xb, x_inv)]
        x_inv = [x - t for x, t in zip(x_inv, xbx)]
        yield

    u, wk = [], []
    for cb in range(n_chunks):
        rs = slice(cb * c, (cb + 1) * c)
        sol = [_mm(head(x_inv[cb], hd, c),
                   jnp.concatenate([head(v_rhs[rs], hd, HEAD_DIM), head(k_rhs[rs], hd, HEAD_DIM)], axis=1))
               for hd in range(nh)]
        u.append([s[:, :HEAD_DIM] for s in sol])
        wk.append([s[:, HEAD_DIM:] for s in sol])
    yield

    state = [state_ref[0, hd] for hd in range(nh)]
    o_blocks = []
    for cb in range(n_chunks):
        rs = slice(cb * c, (cb + 1) * c)
        g_last = g_d[(cb + 1) * c - 1:(cb + 1) * c, :]
        k_dec = k_all[rs] * jnp.exp(g_last - g_d[rs])
        eg_last = jnp.exp(g_last)
        r = [_mm(jnp.concatenate([wk[cb][hd], head(q_g[rs], hd, HEAD_DIM)], axis=0), state[hd]) for hd in range(nh)]
        v_new = [u[cb][hd] - r[hd][:c] for hd in range(nh)]
        yield
        o_blocks.append(jnp.concatenate(
            [r[hd][c:] + _mm(head(qk_dec[cb], hd, c), v_new[hd]) for hd in range(nh)], axis=1))
        state = [state[hd] * head(eg_last, hd, HEAD_DIM) + _mm_tn(head(k_dec, hd, HEAD_DIM), v_new[hd])
                 for hd in range(nh)]
        yield
    for hd in range(nh):
        state_ref[0, hd] = state[hd]
    return jnp.concatenate(o_blocks, axis=0)


def _shifted_history(buf, rows, n_taps, hist_buf, seg_len):
    out = []
    for s in range(1, n_taps):
        raw = buf[pl.ds(SUBLANES - s, rows), :]
        if hist_buf is not None:
            t = jnp.bitwise_and(lax.broadcasted_iota(jnp.int32, (rows, 1), 0), seg_len - 1)
            raw = jnp.where(t >= s, raw, hist_buf[pl.ds(n_taps - 1 - s, rows), :])
        out.append(raw)
    return out


def _mixer_tile(ti, rows, seg_len, per_seq_hist, masks,
                x_ref, hq_ref, hs_ref, nmix_ref, w1_ref, cwq_ref, gate_ref, gnorm_ref, cws_ref, wout_ref,
                xmid_ref, qtail_ref, stail_ref, s_out_ref, qbuf, sbuf, hist_bufs):
    n_chunks = rows // CHUNK
    new_rows = slice(SUBLANES, SUBLANES + rows)
    qb, sb = qbuf.at[ti], sbuf.at[ti]
    x = x_ref[ti]
    h = _rmsnorm(x, nmix_ref[...]).astype(BF16)
    qb[new_rows, :] = jnp.dot(h, w1_ref[:, C_QKV:C_QKV + QKV_DIM], preferred_element_type=F32)
    yield
    rest = jnp.dot(h, w1_ref[:, C_Z:], preferred_element_type=F32)
    off = lambda col: col - C_Z

    sc_pre = rest[:, off(C_SCC):off(C_SCC) + SC_WIDTH] * rest[:, off(C_SCH):off(C_SCH) + SC_WIDTH]
    sb[new_rows, :] = sc_pre
    if per_seq_hist:
        hq_buf, hs_buf = hist_bufs[0].at[ti], hist_bufs[1].at[ti]
        zq = jnp.zeros((SUBLANES, QKV_DIM), F32)
        zs = jnp.zeros((SUBLANES, SC_WIDTH), F32)
        qb[0:SUBLANES, :] = zq
        sb[0:SUBLANES, :] = zs
        hq_buf[0:rows, :] = hq_ref[ti]
        hq_buf[rows:rows + SUBLANES, :] = zq
        hs_buf[0:rows, :] = hs_ref[ti]
        hs_buf[rows:rows + SUBLANES, :] = zs
    else:
        hq_buf = hs_buf = None
    qkv_pre = qb[new_rows, :]
    q1, q2, q3 = _shifted_history(qb, rows, GDN_CONV, hq_buf, seg_len)
    cwq = cwq_ref[...]
    qkv = _silu(q3 * cwq[0:1, :] + q2 * cwq[1:2, :] + q1 * cwq[2:3, :] + qkv_pre * cwq[3:4, :])
    if per_seq_hist:
        qtail_ref[ti] = qkv_pre
        stail_ref[ti] = sc_pre
    else:
        q_tail = qb[rows:rows + SUBLANES, :]
        s_tail = sb[rows:rows + SUBLANES, :]
        qtail_ref[ti] = q_tail
        stail_ref[ti] = s_tail
    q_n, k_n, v_n = [], [], []
    for hd in range(GDN_HEADS):
        lo = hd * HEAD_DIM
        q_h = qkv[:, lo:lo + HEAD_DIM]
        k_h = qkv[:, GDN_WIDTH + lo:GDN_WIDTH + lo + HEAD_DIM]
        q_n.append(q_h * lax.rsqrt(jnp.sum(q_h * q_h, axis=-1, keepdims=True) + EPS) * (HEAD_DIM ** -0.5))
        k_n.append(k_h * lax.rsqrt(jnp.sum(k_h * k_h, axis=-1, keepdims=True) + EPS))
        v_n.append(qkv[:, 2 * GDN_WIDTH + lo:2 * GDN_WIDTH + lo + HEAD_DIM])
    yield

    ab = rest[:, off(C_AB):off(C_AB) + LANES]
    gate = gate_ref[...]
    log_a = -jnp.exp(gate[0:1, :]) * _softplus(ab + gate[1:2, :])
    beta = _sigmoid(ab)
    pos = jnp.bitwise_and(lax.broadcasted_iota(jnp.int32, (rows, 1), 0), seg_len - 1)
    g = log_a
    shift = 1
    while shift < seg_len:
        g = g + jnp.where(pos >= shift, pltpu.roll(g, shift, 0), 0.0)
        shift *= 2

    state_ref = s_out_ref.at[ti]
    if per_seq_hist:
        g_t = g.T
        n_seg = CHUNK // seg_len
        o_h = []
        for hd in range(GDN_HEADS):
            o_blocks = []
            for cb in range(n_chunks):
                r0 = cb * CHUNK

                def load_state(s, hd=hd, base=cb * n_seg):
                    return state_ref[base + s, hd]

                def store_state(s, val, hd=hd, base=cb * n_seg):
                    state_ref[base + s, hd] = val

                o_blocks.append(_wy_block(
                    q_n[hd][r0:r0 + CHUNK], k_n[hd][r0:r0 + CHUNK], v_n[hd][r0:r0 + CHUNK],
                    g[r0:r0 + CHUNK, hd:hd + 1], g_t[hd:hd + 1, r0:r0 + CHUNK],
                    beta[r0:r0 + CHUNK, GDN_HEADS + hd:GDN_HEADS + hd + 1],
                    seg_len, load_state, store_state))
            o_h.append(o_blocks[0] if n_chunks == 1 else jnp.concatenate(o_blocks, axis=0))
    else:
        o_all = yield from _wy_tile_packed(jnp.concatenate(q_n, axis=1), jnp.concatenate(k_n, axis=1),
                                           jnp.concatenate(v_n, axis=1), g, beta, state_ref, rows, masks)
        o_h = [o_all[:, hd * HEAD_DIM:(hd + 1) * HEAD_DIM] for hd in range(GDN_HEADS)]
    yield

    s1, s2 = _shifted_history(sb, rows, SC_CONV, hs_buf, seg_len)
    cws = cws_ref[...]
    y_sc = rest[:, off(C_SCB):off(C_SCB) + SC_WIDTH] * (s2 * cws[0:1, :] + s1 * cws[1:2, :] + sc_pre * cws[2:3, :])
    if not per_seq_hist:
        qb[0:SUBLANES, :] = qb[rows:rows + SUBLANES, :]
        sb[0:SUBLANES, :] = sb[rows:rows + SUBLANES, :]
    gnorm = gnorm_ref[...]
    o_heads = []
    for hd in range(GDN_HEADS):
        z_h = rest[:, off(C_Z) + hd * HEAD_DIM:off(C_Z) + (hd + 1) * HEAD_DIM]
        o_heads.append(_rmsnorm(o_h[hd], gnorm) * _silu(z_h))
    mix = jnp.concatenate(o_heads + [y_sc], axis=1)
    xmid_ref[ti] = x + _mm(mix, wout_ref[...])


def _mixer_kernel(n_tiles, rows, seg_len, per_seq_hist,
                  x_ref, hq_ref, hs_ref, s_in_ref, nmix_ref, w1_ref, cwq_ref, gate_ref, gnorm_ref, cws_ref, wout_ref,
                  xmid_ref, qtail_ref, stail_ref, s_out_ref, qbuf, sbuf, *hist_bufs):
    @pl.when(pl.program_id(1) == 0)
    def _():
        s_out_ref[...] = s_in_ref[...]
        if not per_seq_hist:
            qbuf[:, 0:SUBLANES, :] = hq_ref[...]
            sbuf[:, 0:SUBLANES, :] = hs_ref[...]

    masks = None if per_seq_hist else _packed_masks()
    programs = [_mixer_tile(ti, rows, seg_len, per_seq_hist, masks,
                            x_ref, hq_ref, hs_ref, nmix_ref, w1_ref, cwq_ref, gate_ref, gnorm_ref, cws_ref,
                            wout_ref, xmid_ref, qtail_ref, stail_ref, s_out_ref, qbuf, sbuf, hist_bufs)
                for ti in range(n_tiles)]
    while programs:
        for prog in list(programs):
            try:
                next(prog)
            except StopIteration:
                programs.remove(prog)


def _ffn_kernel(x_ref, p_ref, nmlp_ref, wup_ref, wdown_ref, nple_ref, wg_ref, wp_ref, nf_ref, y_ref):
    x = x_ref[...]
    hn = _rmsnorm(x, nmlp_ref[...]).astype(BF16)
    acc = x
    for j in range(D_FF // D_MODEL):
        u = jnp.maximum(_mm(hn, wup_ref[:, j * D_MODEL:(j + 1) * D_MODEL]), 0.0)
        acc = acc + _mm(u * u, wdown_ref[j * D_MODEL:(j + 1) * D_MODEL, :])
    gate = _sigmoid(_mm(_rmsnorm(acc, nple_ref[...]), wg_ref[...]))
    x3 = acc + gate * _mm(p_ref[...], wp_ref[...])
    y_ref[...] = _rmsnorm(x3, nf_ref[...])


def _const_spec(shape):
    nd = len(shape)
    return pl.BlockSpec(shape, lambda *_: (0,) * nd, pipeline_mode=pl.Buffered(1))


def _mixer_call(x3, hist_q, hist_s, s_in, weights, *, n_tiles, rows, seg_len, per_seq_hist, name):
    groups, seq_rows, _ = x3.shape
    n_state = s_in.shape[1]
    row_map = lambda b, i: (b, i, 0)
    hist_rows = rows if per_seq_hist else SUBLANES
    hist_map = row_map if per_seq_hist else (lambda b, i: (b, 0, 0))
    hq_spec = pl.BlockSpec((n_tiles, hist_rows, QKV_DIM), hist_map)
    hs_spec = pl.BlockSpec((n_tiles, hist_rows, SC_WIDTH), hist_map)
    tail_rows = seq_rows if per_seq_hist else SUBLANES
    hist_scratch = []
    if per_seq_hist:
        hist_scratch = [pltpu.VMEM((n_tiles, rows + SUBLANES, QKV_DIM), F32),
                        pltpu.VMEM((n_tiles, rows + SUBLANES, SC_WIDTH), F32)]
    state_spec = pl.BlockSpec((n_tiles, n_state, GDN_HEADS, HEAD_DIM, HEAD_DIM), lambda b, i: (b, 0, 0, 0, 0))
    x_spec = pl.BlockSpec((n_tiles, rows, D_MODEL), row_map)
    kernel = functools.partial(_mixer_kernel, n_tiles, rows, seg_len, per_seq_hist)
    return pl.pallas_call(
        kernel,
        grid=(groups // n_tiles, seq_rows // rows),
        in_specs=[x_spec, hq_spec, hs_spec, state_spec] + [_const_spec(w.shape) for w in weights],
        out_specs=[x_spec, hq_spec, hs_spec, state_spec],
        out_shape=[jax.ShapeDtypeStruct(x3.shape, F32),
                   jax.ShapeDtypeStruct((groups, tail_rows, QKV_DIM), F32),
                   jax.ShapeDtypeStruct((groups, tail_rows, SC_WIDTH), F32),
                   jax.ShapeDtypeStruct(s_in.shape, F32)],
        scratch_shapes=[pltpu.VMEM((n_tiles, rows + SUBLANES, QKV_DIM), F32),
                        pltpu.VMEM((n_tiles, rows + SUBLANES, SC_WIDTH), F32)] + hist_scratch,
        compiler_params=pltpu.CompilerParams(
            dimension_semantics=("arbitrary", "arbitrary"), vmem_limit_bytes=VMEM_LIMIT_BYTES),
        name=name,
    )(x3, hist_q, hist_s, s_in, *weights)


def _ffn_call(x2d, p2d, weights, *, rows, name):
    n = x2d.shape[0]
    row_map = lambda i: (i, 0)
    return pl.pallas_call(
        _ffn_kernel,
        grid=(n // rows,),
        in_specs=[pl.BlockSpec((rows, D_MODEL), row_map), pl.BlockSpec((rows, PLE_DIM), row_map)]
        + [_const_spec(w.shape) for w in weights],
        out_specs=pl.BlockSpec((rows, D_MODEL), row_map),
        out_shape=jax.ShapeDtypeStruct(x2d.shape, F32),
        compiler_params=pltpu.CompilerParams(dimension_semantics=("arbitrary",), vmem_limit_bytes=VMEM_LIMIT_BYTES),
        name=name,
    )(x2d, p2d, *weights)


def _pad_rows(a, rows, front):
    pad = rows - a.shape[1]
    return jnp.pad(a, ((0, 0), (pad, 0) if front else (0, pad), (0, 0)))


def _layer(x_prompt, x_sample, conv_qkv, s_gdn, conv_sc, p_prompt, p_sample, norm_mix, w_in, w_conv_qkv, a_log,
           dt_bias, w_gdn_norm, w_conv_sc, w_out, norm_mlp, w_up, w_down, norm_ple, w_ple_gate, w_ple_proj, norm_f):
    bp, tp, _ = x_prompt.shape
    bs, ts, _ = x_sample.shape
    o_a = QKV_DIM
    o_z = o_a + 2 * GDN_HEADS
    ab_cols = jnp.pad(w_in[:, o_a:o_z], ((0, 0), (0, LANES - 2 * GDN_HEADS)))
    w1 = jnp.concatenate([w_in[:, :QKV_DIM], w_in[:, o_z:], ab_cols], axis=1).astype(BF16)
    gate = jnp.zeros((SUBLANES, LANES), F32)
    gate = gate.at[0, :GDN_HEADS].set(a_log.astype(F32)).at[1, :GDN_HEADS].set(dt_bias.astype(F32))
    mixer_w = (norm_mix.reshape(1, D_MODEL), w1, w_conv_qkv, gate, w_gdn_norm.reshape(1, HEAD_DIM), w_conv_sc,
               w_out.astype(BF16))
    ffn_w = (norm_mlp.reshape(1, D_MODEL), w_up.astype(BF16), w_down.astype(BF16), norm_ple.reshape(1, D_MODEL),
             w_ple_gate.astype(BF16), w_ple_proj.astype(BF16), norm_f.reshape(1, D_MODEL))

    zq = jnp.zeros((bp, SUBLANES, QKV_DIM), F32)
    zs = jnp.zeros((bp, SUBLANES, SC_WIDTH), F32)
    s0 = jnp.zeros((bp, 1, GDN_HEADS, HEAD_DIM, HEAD_DIM), F32)
    xm_p, qt_p, st_p, s_p = _mixer_call(
        x_prompt, zq, zs, s0, mixer_w, n_tiles=PROMPT_SEQS_PER_STEP, rows=PROMPT_ROWS, seg_len=CHUNK,
        per_seq_hist=False, name="mixer_seq")
    y_p = _ffn_call(xm_p.reshape(bp * tp, D_MODEL), p_prompt.reshape(bp * tp, PLE_DIM), ffn_w, rows=FFN_ROWS,
                    name="ffn_prompt")

    seq_per_tile = CHUNK // ts
    tiles = bs // seq_per_tile
    xm_s, qt_s, st_s, s_s = _mixer_call(
        x_sample.reshape(tiles, CHUNK, D_MODEL),
        _pad_rows(conv_qkv, ts, front=False).reshape(tiles, CHUNK, QKV_DIM),
        _pad_rows(conv_sc, ts, front=False).reshape(tiles, CHUNK, SC_WIDTH),
        s_gdn.reshape(tiles, seq_per_tile, GDN_HEADS, HEAD_DIM, HEAD_DIM),
        mixer_w, n_tiles=1, rows=CHUNK, seg_len=ts, per_seq_hist=True, name="mixer_step")
    y_s = _ffn_call(xm_s.reshape(bs * ts, D_MODEL), p_sample.reshape(bs * ts, PLE_DIM), ffn_w, rows=bs * ts,
                    name="ffn_sample")

    new_conv_p = qt_p[:, SUBLANES - (GDN_CONV - 1):]
    new_sc_p = st_p[:, SUBLANES - (SC_CONV - 1):]
    new_conv_s = qt_s.reshape(bs, ts, QKV_DIM)[:, ts - (GDN_CONV - 1):]
    new_sc_s = st_s.reshape(bs, ts, SC_WIDTH)[:, ts - (SC_CONV - 1):]
    return (y_p.reshape(bp, tp, D_MODEL), y_s.reshape(bs, ts, D_MODEL), new_conv_p,
            s_p.reshape(bp, GDN_HEADS, HEAD_DIM, HEAD_DIM), new_sc_p, new_conv_s,
            s_s.reshape(bs, GDN_HEADS, HEAD_DIM, HEAD_DIM), new_sc_s)


def kernel(x_prompt, x_sample, state_gdn_conv, state_gdn, state_sc_conv, p_prompt, p_sample, norm_mix, w_in, w_conv_qkv, a_log, dt_bias, w_gdn_norm, w_conv_sc, w_out, norm_mlp, w_up, w_down, norm_ple, w_ple_gate, w_ple_proj, norm_f):
    depth = w_in.shape[0]
    assert depth == 1, "one layer per call"
    assert x_sample.shape[1] >= GDN_CONV - 1 and CHUNK % x_sample.shape[1] == 0
    assert x_prompt.shape[1] % PROMPT_ROWS == 0 and x_prompt.shape[0] % PROMPT_SEQS_PER_STEP == 0
    outs = _layer(x_prompt, x_sample, state_gdn_conv[0], state_gdn[0], state_sc_conv[0], p_prompt[0], p_sample[0],
                  norm_mix[0], w_in[0], w_conv_qkv[0], a_log[0], dt_bias[0], w_gdn_norm[0], w_conv_sc[0], w_out[0],
                  norm_mlp[0], w_up[0], w_down[0], norm_ple[0], w_ple_gate[0], w_ple_proj[0], norm_f)
    y_p, y_s, c_p, s_p, sc_p, c_s, s_s, sc_s = outs
    return (y_p, y_s, c_p[None], s_p[None], sc_p[None], c_s[None], s_s[None], sc_s[None])
```

```python
import functools

import jax
import jax.numpy as jnp
from jax import lax
from jax.experimental import pallas as pl
from jax.experimental.pallas import tpu as pltpu

F32 = jnp.float32
BF16 = jnp.bfloat16

D_MODEL = 1024
PLE_DIM = 256
GDN_HEADS = 4
HEAD_DIM = 128
GDN_WIDTH = GDN_HEADS * HEAD_DIM
QKV_DIM = 3 * GDN_WIDTH
GDN_CONV = 4
SC_WIDTH = D_MODEL - GDN_WIDTH
SC_CONV = 3
D_FF = 4 * D_MODEL
EPS = 1e-6
CHUNK = 64
LANES = 128
SUBLANES = 8
VMEM_LIMIT_BYTES = 56 * 1024 * 1024
PROMPT_ROWS = 256
PROMPT_SEQS_PER_STEP = 2
FFN_ROWS = 512

C_AB = 0
C_QKV = C_AB + LANES
C_Z = C_QKV + QKV_DIM
C_SCB = C_Z + GDN_WIDTH
C_SCC = C_SCB + SC_WIDTH
C_SCH = C_SCC + SC_WIDTH
W1_COLS = C_SCH + SC_WIDTH


def _mm(a, b):
    return jnp.dot(a.astype(BF16), b.astype(BF16), preferred_element_type=F32)


def _mm_nt(a, b):
    return lax.dot_general(a.astype(BF16), b.astype(BF16), (((1,), (1,)), ((), ())), preferred_element_type=F32)


def _mm_tn(a, b):
    return lax.dot_general(a.astype(BF16), b.astype(BF16), (((0,), (0,)), ((), ())), preferred_element_type=F32)


def _rmsnorm(x, w_row):
    return x * lax.rsqrt(jnp.mean(x * x, axis=-1, keepdims=True) + EPS) * w_row


def _sigmoid(x):
    return 1.0 / (1.0 + jnp.exp(-x))


def _silu(x):
    return x * _sigmoid(x)


def _softplus(x):
    return jnp.maximum(x, 0.0) + jnp.log1p(jnp.exp(-jnp.abs(x)))


def _log2(n):
    k = n.bit_length() - 1
    assert (1 << k) == n, n
    return k


def _wy_block(q, k, v, g_col, g_row, beta_col, seg_len, load_state, store_state):
    c = CHUNK
    n_seg = c // seg_len
    lg = _log2(seg_len)
    ri = lax.broadcasted_iota(jnp.int32, (c, c), 0)
    ci = lax.broadcasted_iota(jnp.int32, (c, c), 1)
    same = jnp.right_shift(ri, lg) == jnp.right_shift(ci, lg)
    lower = same & (ri >= ci)
    strict = same & (ri > ci)
    diff = g_col - g_row
    decay = jnp.where(lower, jnp.exp(jnp.where(lower, diff, 0.0)), 0.0)
    a_off = jnp.where(strict, beta_col * _mm_nt(k, k) * decay, 0.0)
    eye = jnp.where(ri == ci, 1.0, 0.0).astype(F32)
    x_inv = eye
    for lb in range(lg):
        rb = jnp.right_shift(ri, lb)
        cb = jnp.right_shift(ci, lb)
        join = (jnp.bitwise_and(rb, 1) == 1) & (cb == rb - 1)
        b_lvl = jnp.where(join, a_off, 0.0)
        if lb == 0:
            x_inv = eye - b_lvl
        else:
            x_inv = x_inv - _mm(_mm(x_inv, b_lvl), x_inv)
    rhs = jnp.concatenate([v * beta_col, k * (beta_col * jnp.exp(g_col))], axis=1)
    sol = _mm(x_inv, rhs)
    u = sol[:, :HEAD_DIM]
    w = sol[:, HEAD_DIM:]
    qk = _mm_nt(q, k) * decay
    stacked = jnp.concatenate([w, q * jnp.exp(g_col)], axis=0)
    row = lax.broadcasted_iota(jnp.int32, (c, 1), 0)
    states = [load_state(s) for s in range(n_seg)]
    w_s = None
    q_s = None
    for s in range(n_seg):
        r = _mm(stacked, states[s])
        if n_seg == 1:
            w_s, q_s = r[:c], r[c:]
        else:
            in_seg = jnp.right_shift(row, lg) == s
            w_s = jnp.where(in_seg, r[:c], 0.0 if w_s is None else w_s)
            q_s = jnp.where(in_seg, r[c:], 0.0 if q_s is None else q_s)
    v_new = u - w_s
    o = q_s + _mm(qk, v_new)
    for s in range(n_seg):
        last = s * seg_len + seg_len - 1
        g_last = g_col[last:last + 1, :]
        if n_seg == 1:
            k_dec = k * jnp.exp(g_last - g_col)
        else:
            in_seg = jnp.right_shift(row, lg) == s
            k_dec = jnp.where(in_seg, k * jnp.exp(jnp.where(in_seg, g_last - g_col, 0.0)), 0.0)
        store_state(s, states[s] * jnp.exp(g_last) + _mm_tn(k_dec, v_new))
    return o


def _expand_heads(slab, first_lane, width):
    rows = slab.shape[0]
    cols = [slab[:, first_lane + hd:first_lane + hd + 1] for hd in range(GDN_HEADS)]
    if width % LANES == 0:
        return jnp.concatenate([jnp.broadcast_to(c, (rows, width)) for c in cols], axis=1)
    total = GDN_HEADS * width
    lane_head = jnp.right_shift(lax.broadcasted_iota(jnp.int32, (rows, total), 1), _log2(width))
    out = jnp.broadcast_to(cols[-1], (rows, total))
    for hd in range(GDN_HEADS - 2, -1, -1):
        out = jnp.where(lane_head == hd, jnp.broadcast_to(cols[hd], (rows, total)), out)
    return out


def _packed_masks():
    c, nh = CHUNK, GDN_HEADS
    w = nh * c
    lg = _log2(c)
    ri = lax.broadcasted_iota(jnp.int32, (c, w), 0)
    cj = jnp.bitwise_and(lax.broadcasted_iota(jnp.int32, (c, w), 1), c - 1)
    join0 = (jnp.bitwise_and(ri, 1) == 1) & (cj == ri - 1)
    rw = lax.broadcasted_iota(jnp.int32, (w, w), 0)
    cw = lax.broadcasted_iota(jnp.int32, (w, w), 1)
    same_head = jnp.right_shift(rw, lg) == jnp.right_shift(cw, lg)
    rw, cw = jnp.bitwise_and(rw, c - 1), jnp.bitwise_and(cw, c - 1)
    one_zero = lambda m: jnp.where(m, 1.0, 0.0).astype(BF16)
    joins_bd = [one_zero(same_head & (jnp.bitwise_and(jnp.right_shift(rw, lb), 1) == 1)
                         & (jnp.right_shift(cw, lb) == jnp.right_shift(rw, lb) - 1)) for lb in range(lg)]
    kbd = (jnp.right_shift(lax.broadcasted_iota(jnp.int32, (w, nh * HEAD_DIM), 0), lg)
           == jnp.right_shift(lax.broadcasted_iota(jnp.int32, (w, nh * HEAD_DIM), 1), _log2(HEAD_DIM)))
    return dict(lower=ri >= cj, strict=ri > cj, eye=ri == cj, join0=join0, joins_bd=joins_bd,
                bd=one_zero(same_head), kbd=one_zero(kbd))


def _wy_tile_packed(q_all, k_all, v_all, g, beta, state_ref, rows, masks):
    c, nh = CHUNK, GDN_HEADS
    lg = _log2(c)
    n_chunks = rows // c
    lower, strict, eye = masks["lower"], masks["strict"], masks["eye"]

    def block_diag(x, factor):
        return jnp.concatenate([x.astype(BF16)] * nh, axis=0) * factor

    def head(x, hd, width):
        return x[:, hd * width:(hd + 1) * width]

    g_c = _expand_heads(g, 0, c)
    b_c = _expand_heads(beta, nh, c)
    g_d = _expand_heads(g, 0, HEAD_DIM)
    b_d = _expand_heads(beta, nh, HEAD_DIM)
    eg_d = jnp.exp(g_d)
    v_rhs = v_all * b_d
    k_rhs = k_all * (b_d * eg_d)
    q_g = q_all * eg_d

    a_off, qk_dec = [], []
    for cb in range(n_chunks):
        rs = slice(cb * c, (cb + 1) * c)
        g_row = jnp.sum(jnp.where(eye, g_c[rs], 0.0), axis=0, keepdims=True)
        decay = jnp.where(lower, jnp.exp(jnp.where(lower, g_c[rs] - g_row, 0.0)), 0.0)
        kq = _mm_nt(jnp.concatenate([k_all[rs], q_all[rs]], axis=0), block_diag(k_all[rs], masks["kbd"]))
        a_off.append(jnp.where(strict, b_c[rs] * kq[:c] * decay, 0.0))
        qk_dec.append(kq[c:] * decay)
    yield

    eye_f = jnp.where(eye, 1.0, 0.0).astype(F32)
    x_inv = [eye_f - jnp.where(masks["join0"], a, 0.0) for a in a_off]
    for lb in range(1, lg):
        xb = [jnp.dot(x.astype(BF16), block_diag(a, masks["joins_bd"][lb]), preferred_element_type=F32)
              for x, a in zip(x_inv, a_off)]
        yield
        xbx = [jnp.dot(t.astype(BF16), block_diag(x, masks["bd"]), preferred_element_type=F32)
               for t, x in zip(xb, x_inv)]
        x_inv = [x - t for x, t in zip(x_inv, xbx)]
        yield

    u, wk = [], []
    for cb in range(n_chunks):
        rs = slice(cb * c, (cb + 1) * c)
        sol = [_mm(head(x_inv[cb], hd, c),
                   jnp.concatenate([head(v_rhs[rs], hd, HEAD_DIM), head(k_rhs[rs], hd, HEAD_DIM)], axis=1))
               for hd in range(nh)]
        u.append([s[:, :HEAD_DIM] for s in sol])
        wk.append([s[:, HEAD_DIM:] for s in sol])
    yield

    state = [state_ref[0, hd] for hd in range(nh)]
    o_blocks = []
    for cb in range(n_chunks):
        rs = slice(cb * c, (cb + 1) * c)
        g_last = g_d[(cb + 1) * c - 1:(cb + 1) * c, :]
        k_dec = k_all[rs] * jnp.exp(g_last - g_d[rs])
        eg_last = jnp.exp(g_last)
        r = [_mm(jnp.concatenate([wk[cb][hd], head(q_g[rs], hd, HEAD_DIM)], axis=0), state[hd]) for hd in range(nh)]
        v_new = [u[cb][hd] - r[hd][:c] for hd in range(nh)]
        yield
        o_blocks.append(jnp.concatenate(
            [r[hd][c:] + _mm(head(qk_dec[cb], hd, c), v_new[hd]) for hd in range(nh)], axis=1))
        state = [state[hd] * head(eg_last, hd, HEAD_DIM) + _mm_tn(head(k_dec, hd, HEAD_DIM), v_new[hd])
                 for hd in range(nh)]
        yield
    for hd in range(nh):
        state_ref[0, hd] = state[hd]
    return jnp.concatenate(o_blocks, axis=0)


def _shifted_history(buf, rows, n_taps, hist_buf, seg_len):
    out = []
    with_carry = buf[0:SUBLANES + rows, :]
    for s in range(1, n_taps):
        raw = pltpu.roll(with_carry, s, 0)[SUBLANES:, :]
        if hist_buf is not None:
            t = jnp.bitwise_and(lax.broadcasted_iota(jnp.int32, (rows, 1), 0), seg_len - 1)
            raw = jnp.where(t >= s, raw, hist_buf[pl.ds(n_taps - 1 - s, rows), :])
        out.append(raw)
    return out


def _mixer_tile(ti, rows, seg_len, per_seq_hist, masks,
                x_ref, hq_ref, hs_ref, nmix_ref, w1_ref, cwq_ref, gate_ref, gnorm_ref, cws_ref, wout_ref,
                xmid_ref, qtail_ref, stail_ref, s_out_ref, qbuf, sbuf, hist_bufs):
    n_chunks = rows // CHUNK
    new_rows = slice(SUBLANES, SUBLANES + rows)
    qb, sb = qbuf.at[ti], sbuf.at[ti]
    x = x_ref[ti]
    h = _rmsnorm(x, nmix_ref[...]).astype(BF16)
    ab = jnp.dot(h, w1_ref[:, C_AB:C_AB + LANES], preferred_element_type=F32)
    qb[new_rows, :] = jnp.dot(h, w1_ref[:, C_QKV:C_QKV + QKV_DIM], preferred_element_type=F32)
    yield

    gate = gate_ref[...]
    log_a = -jnp.exp(gate[0:1, :]) * _softplus(ab + gate[1:2, :])
    beta = _sigmoid(ab)
    pos = jnp.bitwise_and(lax.broadcasted_iota(jnp.int32, (rows, 1), 0), seg_len - 1)
    g = log_a
    shift = 1
    while shift < seg_len:
        g = g + jnp.where(pos >= shift, pltpu.roll(g, shift, 0), 0.0)
        shift *= 2

    rest = jnp.dot(h, w1_ref[:, C_Z:], preferred_element_type=F32)
    off = lambda col: col - C_Z

    sc_pre = rest[:, off(C_SCC):off(C_SCC) + SC_WIDTH] * rest[:, off(C_SCH):off(C_SCH) + SC_WIDTH]
    sb[new_rows, :] = sc_pre
    if per_seq_hist:
        hq_buf, hs_buf = hist_bufs[0].at[ti], hist_bufs[1].at[ti]
        zq = jnp.zeros((SUBLANES, QKV_DIM), F32)
        zs = jnp.zeros((SUBLANES, SC_WIDTH), F32)
        qb[0:SUBLANES, :] = zq
        sb[0:SUBLANES, :] = zs
        hq_buf[0:rows, :] = hq_ref[ti]
        hq_buf[rows:rows + SUBLANES, :] = zq
        hs_buf[0:rows, :] = hs_ref[ti]
        hs_buf[rows:rows + SUBLANES, :] = zs
    else:
        hq_buf = hs_buf = None
    qkv_pre = qb[new_rows, :]
    q1, q2, q3 = _shifted_history(qb, rows, GDN_CONV, hq_buf, seg_len)
    cwq = cwq_ref[...]
    qkv = _silu(q3 * cwq[0:1, :] + q2 * cwq[1:2, :] + q1 * cwq[2:3, :] + qkv_pre * cwq[3:4, :])
    if per_seq_hist:
        qtail_ref[ti] = qkv_pre
        stail_ref[ti] = sc_pre
    else:
        q_tail = qb[rows:rows + SUBLANES, :]
        s_tail = sb[rows:rows + SUBLANES, :]
        qtail_ref[ti] = q_tail
        stail_ref[ti] = s_tail
    q_n, k_n, v_n = [], [], []
    for hd in range(GDN_HEADS):
        lo = hd * HEAD_DIM
        q_h = qkv[:, lo:lo + HEAD_DIM]
        k_h = qkv[:, GDN_WIDTH + lo:GDN_WIDTH + lo + HEAD_DIM]
        q_n.append(q_h * lax.rsqrt(jnp.sum(q_h * q_h, axis=-1, keepdims=True) + EPS) * (HEAD_DIM ** -0.5))
        k_n.append(k_h * lax.rsqrt(jnp.sum(k_h * k_h, axis=-1, keepdims=True) + EPS))
        v_n.append(qkv[:, 2 * GDN_WIDTH + lo:2 * GDN_WIDTH + lo + HEAD_DIM])
    yield

    state_ref = s_out_ref.at[ti]
    if per_seq_hist:
        g_t = g.T
        n_seg = CHUNK // seg_len
        o_h = []
        for hd in range(GDN_HEADS):
            o_blocks = []
            for cb in range(n_chunks):
                r0 = cb * CHUNK

                def load_state(s, hd=hd, base=cb * n_seg):
                    return state_ref[base + s, hd]

                def store_state(s, val, hd=hd, base=cb * n_seg):
                    state_ref[base + s, hd] = val

                o_blocks.append(_wy_block(
                    q_n[hd][r0:r0 + CHUNK], k_n[hd][r0:r0 + CHUNK], v_n[hd][r0:r0 + CHUNK],
                    g[r0:r0 + CHUNK, hd:hd + 1], g_t[hd:hd + 1, r0:r0 + CHUNK],
                    beta[r0:r0 + CHUNK, GDN_HEADS + hd:GDN_HEADS + hd + 1],
                    seg_len, load_state, store_state))
            o_h.append(o_blocks[0] if n_chunks == 1 else jnp.concatenate(o_blocks, axis=0))
    else:
        o_all = yield from _wy_tile_packed(jnp.concatenate(q_n, axis=1), jnp.concatenate(k_n, axis=1),
                                           jnp.concatenate(v_n, axis=1), g, beta, state_ref, rows, masks)
        o_h = [o_all[:, hd * HEAD_DIM:(hd + 1) * HEAD_DIM] for hd in range(GDN_HEADS)]
    yield

    s1, s2 = _shifted_history(sb, rows, SC_CONV, hs_buf, seg_len)
    cws = cws_ref[...]
    y_sc = rest[:, off(C_SCB):off(C_SCB) + SC_WIDTH] * (s2 * cws[0:1, :] + s1 * cws[1:2, :] + sc_pre * cws[2:3, :])
    if not per_seq_hist:
        qb[0:SUBLANES, :] = qb[rows:rows + SUBLANES, :]
        sb[0:SUBLANES, :] = sb[rows:rows + SUBLANES, :]
    gnorm = gnorm_ref[...]
    o_heads = []
    for hd in range(GDN_HEADS):
        z_h = rest[:, off(C_Z) + hd * HEAD_DIM:off(C_Z) + (hd + 1) * HEAD_DIM]
        o_heads.append(_rmsnorm(o_h[hd], gnorm) * _silu(z_h))
    mix = jnp.concatenate(o_heads + [y_sc], axis=1)
    xmid_ref[ti] = x + _mm(mix, wout_ref[...])


def _mixer_kernel(n_tiles, rows, seg_len, per_seq_hist,
                  x_ref, hq_ref, hs_ref, s_in_ref, nmix_ref, w1_ref, cwq_ref, gate_ref, gnorm_ref, cws_ref, wout_ref,
                  xmid_ref, qtail_ref, stail_ref, s_out_ref, qbuf, sbuf, *hist_bufs):
    @pl.when(pl.program_id(1) == 0)
    def _():
        s_out_ref[...] = s_in_ref[...]
        if not per_seq_hist:
            qbuf[:, 0:SUBLANES, :] = hq_ref[...]
            sbuf[:, 0:SUBLANES, :] = hs_ref[...]

    masks = None if per_seq_hist else _packed_masks()
    programs = [_mixer_tile(ti, rows, seg_len, per_seq_hist, masks,
                            x_ref, hq_ref, hs_ref, nmix_ref, w1_ref, cwq_ref, gate_ref, gnorm_ref, cws_ref,
                            wout_ref, xmid_ref, qtail_ref, stail_ref, s_out_ref, qbuf, sbuf, hist_bufs)
                for ti in range(n_tiles)]
    while programs:
        for prog in list(programs):
            try:
                next(prog)
            except StopIteration:
                programs.remove(prog)


def _ffn_kernel(x_ref, p_ref, nmlp_ref, wup_ref, wdown_ref, nple_ref, wg_ref, wp_ref, nf_ref, y_ref):
    x = x_ref[...]
    hn = _rmsnorm(x, nmlp_ref[...]).astype(BF16)
    acc = x
    for j in range(D_FF // D_MODEL):
        u = jnp.maximum(_mm(hn, wup_ref[:, j * D_MODEL:(j + 1) * D_MODEL]), 0.0)
        acc = acc + _mm(u * u, wdown_ref[j * D_MODEL:(j + 1) * D_MODEL, :])
    gate = _sigmoid(_mm(_rmsnorm(acc, nple_ref[...]), wg_ref[...]))
    x3 = acc + gate * _mm(p_ref[...], wp_ref[...])
    y_ref[...] = _rmsnorm(x3, nf_ref[...])


def _const_spec(shape):
    nd = len(shape)
    return pl.BlockSpec(shape, lambda *_: (0,) * nd, pipeline_mode=pl.Buffered(1))


def _mixer_call(x3, hist_q, hist_s, s_in, weights, *, n_tiles, rows, seg_len, per_seq_hist, name):
    groups, seq_rows, _ = x3.shape
    n_state = s_in.shape[1]
    row_map = lambda b, i: (b, i, 0)
    hist_rows = rows if per_seq_hist else SUBLANES
    hist_map = row_map if per_seq_hist else (lambda b, i: (b, 0, 0))
    hq_spec = pl.BlockSpec((n_tiles, hist_rows, QKV_DIM), hist_map)
    hs_spec = pl.BlockSpec((n_tiles, hist_rows, SC_WIDTH), hist_map)
    tail_rows = seq_rows if per_seq_hist else SUBLANES
    hist_scratch = []
    if per_seq_hist:
        hist_scratch = [pltpu.VMEM((n_tiles, rows + SUBLANES, QKV_DIM), F32),
                        pltpu.VMEM((n_tiles, rows + SUBLANES, SC_WIDTH), F32)]
    state_spec = pl.BlockSpec((n_tiles, n_state, GDN_HEADS, HEAD_DIM, HEAD_DIM), lambda b, i: (b, 0, 0, 0, 0))
    x_spec = pl.BlockSpec((n_tiles, rows, D_MODEL), row_map)
    kernel = functools.partial(_mixer_kernel, n_tiles, rows, seg_len, per_seq_hist)
    return pl.pallas_call(
        kernel,
        grid=(groups // n_tiles, seq_rows // rows),
        in_specs=[x_spec, hq_spec, hs_spec, state_spec] + [_const_spec(w.shape) for w in weights],
        out_specs=[x_spec, hq_spec, hs_spec, state_spec],
        out_shape=[jax.ShapeDtypeStruct(x3.shape, F32),
                   jax.ShapeDtypeStruct((groups, tail_rows, QKV_DIM), F32),
                   jax.ShapeDtypeStruct((groups, tail_rows, SC_WIDTH), F32),
                   jax.ShapeDtypeStruct(s_in.shape, F32)],
        scratch_shapes=[pltpu.VMEM((n_tiles, rows + SUBLANES, QKV_DIM), F32),
                        pltpu.VMEM((n_tiles, rows + SUBLANES, SC_WIDTH), F32)] + hist_scratch,
        compiler_params=pltpu.CompilerParams(
            dimension_semantics=("arbitrary", "arbitrary"), vmem_limit_bytes=VMEM_LIMIT_BYTES),
        name=name,
    )(x3, hist_q, hist_s, s_in, *weights)


def _ffn_call(x2d, p2d, weights, *, rows, name):
    n = x2d.shape[0]
    row_map = lambda i: (i, 0)
    return pl.pallas_call(
        _ffn_kernel,
        grid=(n // rows,),
        in_specs=[pl.BlockSpec((rows, D_MODEL), row_map), pl.BlockSpec((rows, PLE_DIM), row_map)]
        + [_const_spec(w.shape) for w in weights],
        out_specs=pl.BlockSpec((rows, D_MODEL), row_map),
        out_shape=jax.ShapeDtypeStruct(x2d.shape, F32),
        compiler_params=pltpu.CompilerParams(dimension_semantics=("arbitrary",), vmem_limit_bytes=VMEM_LIMIT_BYTES),
        name=name,
    )(x2d, p2d, *weights)


def _pad_rows(a, rows, front):
    pad = rows - a.shape[1]
    return jnp.pad(a, ((0, 0), (pad, 0) if front else (0, pad), (0, 0)))


def _layer(x_prompt, x_sample, conv_qkv, s_gdn, conv_sc, p_prompt, p_sample, norm_mix, w_in, w_conv_qkv, a_log,
           dt_bias, w_gdn_norm, w_conv_sc, w_out, norm_mlp, w_up, w_down, norm_ple, w_ple_gate, w_ple_proj, norm_f):
    bp, tp, _ = x_prompt.shape
    bs, ts, _ = x_sample.shape
    o_a = QKV_DIM
    o_z = o_a + 2 * GDN_HEADS
    ab_cols = jnp.pad(w_in[:, o_a:o_z], ((0, 0), (0, LANES - 2 * GDN_HEADS)))
    w1 = jnp.concatenate([ab_cols, w_in[:, :QKV_DIM], w_in[:, o_z:]], axis=1).astype(BF16)
    gate = jnp.zeros((SUBLANES, LANES), F32)
    gate = gate.at[0, :GDN_HEADS].set(a_log.astype(F32)).at[1, :GDN_HEADS].set(dt_bias.astype(F32))
    mixer_w = (norm_mix.reshape(1, D_MODEL), w1, w_conv_qkv, gate, w_gdn_norm.reshape(1, HEAD_DIM), w_conv_sc,
               w_out.astype(BF16))
    ffn_w = (norm_mlp.reshape(1, D_MODEL), w_up.astype(BF16), w_down.astype(BF16), norm_ple.reshape(1, D_MODEL),
             w_ple_gate.astype(BF16), w_ple_proj.astype(BF16), norm_f.reshape(1, D_MODEL))

    zq = jnp.zeros((bp, SUBLANES, QKV_DIM), F32)
    zs = jnp.zeros((bp, SUBLANES, SC_WIDTH), F32)
    s0 = jnp.zeros((bp, 1, GDN_HEADS, HEAD_DIM, HEAD_DIM), F32)
    xm_p, qt_p, st_p, s_p = _mixer_call(
        x_prompt, zq, zs, s0, mixer_w, n_tiles=PROMPT_SEQS_PER_STEP, rows=PROMPT_ROWS, seg_len=CHUNK,
        per_seq_hist=False, name="mixer_seq")
    y_p = _ffn_call(xm_p.reshape(bp * tp, D_MODEL), p_prompt.reshape(bp * tp, PLE_DIM), ffn_w, rows=FFN_ROWS,
                    name="ffn_prompt")

    seq_per_tile = CHUNK // ts
    tiles = bs // seq_per_tile
    xm_s, qt_s, st_s, s_s = _mixer_call(
        x_sample.reshape(tiles, CHUNK, D_MODEL),
        _pad_rows(conv_qkv, ts, front=False).reshape(tiles, CHUNK, QKV_DIM),
        _pad_rows(conv_sc, ts, front=False).reshape(tiles, CHUNK, SC_WIDTH),
        s_gdn.reshape(tiles, seq_per_tile, GDN_HEADS, HEAD_DIM, HEAD_DIM),
        mixer_w, n_tiles=1, rows=CHUNK, seg_len=ts, per_seq_hist=True, name="mixer_step")
    y_s = _ffn_call(xm_s.reshape(bs * ts, D_MODEL), p_sample.reshape(bs * ts, PLE_DIM), ffn_w, rows=bs * ts,
                    name="ffn_sample")

    new_conv_p = qt_p[:, SUBLANES - (GDN_CONV - 1):]
    new_sc_p = st_p[:, SUBLANES - (SC_CONV - 1):]
    new_conv_s = qt_s.reshape(bs, ts, QKV_DIM)[:, ts - (GDN_CONV - 1):]
    new_sc_s = st_s.reshape(bs, ts, SC_WIDTH)[:, ts - (SC_CONV - 1):]
    return (y_p.reshape(bp, tp, D_MODEL), y_s.reshape(bs, ts, D_MODEL), new_conv_p,
            s_p.reshape(bp, GDN_HEADS, HEAD_DIM, HEAD_DIM), new_sc_p, new_conv_s,
            s_s.reshape(bs, GDN_HEADS, HEAD_DIM, HEAD_DIM), new_sc_s)


def kernel(x_prompt, x_sample, state_gdn_conv, state_gdn, state_sc_conv, p_prompt, p_sample, norm_mix, w_in, w_conv_qkv, a_log, dt_bias, w_gdn_norm, w_conv_sc, w_out, norm_mlp, w_up, w_down, norm_ple, w_ple_gate, w_ple_proj, norm_f):
    depth = w_in.shape[0]
    assert depth == 1, "one layer per call"
    assert x_sample.shape[1] >= GDN_CONV - 1 and CHUNK % x_sample.shape[1] == 0
    assert x_prompt.shape[1] % PROMPT_ROWS == 0 and x_prompt.shape[0] % PROMPT_SEQS_PER_STEP == 0
    outs = _layer(x_prompt, x_sample, state_gdn_conv[0], state_gdn[0], state_sc_conv[0], p_prompt[0], p_sample[0],
                  norm_mix[0], w_in[0], w_conv_qkv[0], a_log[0], dt_bias[0], w_gdn_norm[0], w_conv_sc[0], w_out[0],
                  norm_mlp[0], w_up[0], w_down[0], norm_ple[0], w_ple_gate[0], w_ple_proj[0], norm_f)
    y_p, y_s, c_p, s_p, sc_p, c_s, s_s, sc_s = outs
    return (y_p, y_s, c_p[None], s_p[None], sc_p[None], c_s[None], s_s[None], sc_s[None])
```

```python
import functools

import jax
import jax.numpy as jnp
from jax import lax
from jax.experimental import pallas as pl
from jax.experimental.pallas import tpu as pltpu

F32 = jnp.float32
BF16 = jnp.bfloat16

D_MODEL = 1024
PLE_DIM = 256
GDN_HEADS = 4
HEAD_DIM = 128
GDN_WIDTH = GDN_HEADS * HEAD_DIM
QKV_DIM = 3 * GDN_WIDTH
GDN_CONV = 4
SC_WIDTH = D_MODEL - GDN_WIDTH
SC_CONV = 3
D_FF = 4 * D_MODEL
EPS = 1e-6
NEG_LOG2_E = -1.4426950408889634
CHUNK = 64
LANES = 128
SUBLANES = 8
VMEM_LIMIT_BYTES = 56 * 1024 * 1024
PROMPT_ROWS = 256
PROMPT_SEQS_PER_STEP = 2
FFN_ROWS = 1024

R_Z = 0
R_SCB = R_Z + GDN_WIDTH
R_SCC = R_SCB + SC_WIDTH
R_SCH = R_SCC + SC_WIDTH


def _mm(a, b):
    return jnp.dot(a.astype(BF16), b.astype(BF16), preferred_element_type=F32)


def _mm_nt(a, b):
    return lax.dot_general(a.astype(BF16), b.astype(BF16), (((1,), (1,)), ((), ())), preferred_element_type=F32)


def _mm_tn(a, b):
    return lax.dot_general(a.astype(BF16), b.astype(BF16), (((0,), (0,)), ((), ())), preferred_element_type=F32)


def _rmsnorm(x, w_row):
    return x * lax.rsqrt(jnp.mean(x * x, axis=-1, keepdims=True) + EPS) * w_row


def _sigmoid(x):
    return 1.0 / (1.0 + jnp.exp2(x * NEG_LOG2_E))


def _silu(x):
    return x * _sigmoid(x)


def _softplus(x):
    return jnp.maximum(x, 0.0) + jnp.log1p(jnp.exp(-jnp.abs(x)))


def _log2(n):
    k = n.bit_length() - 1
    assert (1 << k) == n, n
    return k


def _wy_block(q, k, v, g_col, g_row, beta_col, seg_len, load_state, store_state):
    c = CHUNK
    n_seg = c // seg_len
    lg = _log2(seg_len)
    ri = lax.broadcasted_iota(jnp.int32, (c, c), 0)
    ci = lax.broadcasted_iota(jnp.int32, (c, c), 1)
    same = jnp.right_shift(ri, lg) == jnp.right_shift(ci, lg)
    lower = same & (ri >= ci)
    strict = same & (ri > ci)
    diff = g_col - g_row
    decay = jnp.where(lower, jnp.exp(jnp.where(lower, diff, 0.0)), 0.0)
    a_off = jnp.where(strict, beta_col * _mm_nt(k, k) * decay, 0.0)
    qk = _mm_nt(q, k) * decay
    yield
    eye = jnp.where(ri == ci, 1.0, 0.0).astype(F32)
    x_inv = eye
    for lb in range(lg):
        rb = jnp.right_shift(ri, lb)
        cb = jnp.right_shift(ci, lb)
        join = (jnp.bitwise_and(rb, 1) == 1) & (cb == rb - 1)
        b_lvl = jnp.where(join, a_off, 0.0)
        if lb == 0:
            x_inv = eye - b_lvl
        else:
            xb = _mm(x_inv, b_lvl)
            yield
            x_inv = x_inv - _mm(xb, x_inv)
            yield
    rhs = jnp.concatenate([v * beta_col, k * (beta_col * jnp.exp(g_col))], axis=1)
    sol = _mm(x_inv, rhs)
    yield
    u = sol[:, :HEAD_DIM]
    w = sol[:, HEAD_DIM:]
    stacked = jnp.concatenate([w, q * jnp.exp(g_col)], axis=0)
    row = lax.broadcasted_iota(jnp.int32, (c, 1), 0)
    states = [load_state(s) for s in range(n_seg)]
    w_s = None
    q_s = None
    for s in range(n_seg):
        r = _mm(stacked, states[s])
        if n_seg == 1:
            w_s, q_s = r[:c], r[c:]
        else:
            in_seg = jnp.right_shift(row, lg) == s
            w_s = jnp.where(in_seg, r[:c], 0.0 if w_s is None else w_s)
            q_s = jnp.where(in_seg, r[c:], 0.0 if q_s is None else q_s)
    v_new = u - w_s
    yield
    o = q_s + _mm(qk, v_new)
    for s in range(n_seg):
        last = s * seg_len + seg_len - 1
        g_last = g_col[last:last + 1, :]
        if n_seg == 1:
            k_dec = k * jnp.exp(g_last - g_col)
        else:
            in_seg = jnp.right_shift(row, lg) == s
            k_dec = jnp.where(in_seg, k * jnp.exp(jnp.where(in_seg, g_last - g_col, 0.0)), 0.0)
        store_state(s, states[s] * jnp.exp(g_last) + _mm_tn(k_dec, v_new))
    return o


def _expand_heads(slab, first_lane, width):
    rows = slab.shape[0]
    cols = [slab[:, first_lane + hd:first_lane + hd + 1] for hd in range(GDN_HEADS)]
    if width % LANES == 0:
        return jnp.concatenate([jnp.broadcast_to(c, (rows, width)) for c in cols], axis=1)
    total = GDN_HEADS * width
    lane_head = jnp.right_shift(lax.broadcasted_iota(jnp.int32, (rows, total), 1), _log2(width))
    out = jnp.broadcast_to(cols[-1], (rows, total))
    for hd in range(GDN_HEADS - 2, -1, -1):
        out = jnp.where(lane_head == hd, jnp.broadcast_to(cols[hd], (rows, total)), out)
    return out


def _packed_masks():
    c, nh = CHUNK, GDN_HEADS
    ri = lax.broadcasted_iota(jnp.int32, (c, nh * c), 0)
    cj = jnp.bitwise_and(lax.broadcasted_iota(jnp.int32, (c, nh * c), 1), c - 1)
    return dict(lower=ri >= cj, strict=ri > cj, eye=ri == cj, join0=(jnp.bitwise_and(ri, 1) == 1) & (cj == ri - 1))


def _fill_block_factors(bd_ref, kbd_ref):
    c, nh = CHUNK, GDN_HEADS
    w = nh * c
    lg = _log2(c)
    rw = lax.broadcasted_iota(jnp.int32, (w, w), 0)
    cw = lax.broadcasted_iota(jnp.int32, (w, w), 1)
    same_head = jnp.right_shift(rw, lg) == jnp.right_shift(cw, lg)
    rw, cw = jnp.bitwise_and(rw, c - 1), jnp.bitwise_and(cw, c - 1)
    one_zero = lambda m: jnp.where(m, 1.0, 0.0).astype(BF16)
    bd_ref[0] = one_zero(same_head)
    for lb in range(1, lg):
        bd_ref[lb] = one_zero(same_head & (jnp.bitwise_and(jnp.right_shift(rw, lb), 1) == 1)
                              & (jnp.right_shift(cw, lb) == jnp.right_shift(rw, lb) - 1))
    kbd_ref[...] = one_zero(jnp.right_shift(lax.broadcasted_iota(jnp.int32, (w, nh * HEAD_DIM), 0), lg)
                            == jnp.right_shift(lax.broadcasted_iota(jnp.int32, (w, nh * HEAD_DIM), 1),
                                               _log2(HEAD_DIM)))


def _wy_tile_packed(q_all, k_all, v_all, g, beta, state_ref, rows, masks, bd_ref, kbd_ref):
    c, nh = CHUNK, GDN_HEADS
    lg = _log2(c)
    n_chunks = rows // c
    lower, strict, eye = masks["lower"], masks["strict"], masks["eye"]

    def block_diag(x, factor):
        return jnp.concatenate([x.astype(BF16)] * nh, axis=0) * factor

    def head(x, hd, width):
        return x[:, hd * width:(hd + 1) * width]

    g_c = _expand_heads(g, 0, c)
    b_c = _expand_heads(beta, nh, c)
    g_d = _expand_heads(g, 0, HEAD_DIM)
    b_d = _expand_heads(beta, nh, HEAD_DIM)
    eg_d = jnp.exp(g_d)
    v_rhs = v_all * b_d
    k_rhs = k_all * (b_d * eg_d)
    q_g = q_all * eg_d

    a_off, qk_dec = [], []
    for cb in range(n_chunks):
        rs = slice(cb * c, (cb + 1) * c)
        g_row = jnp.sum(jnp.where(eye, g_c[rs], 0.0), axis=0, keepdims=True)
        decay = jnp.where(lower, jnp.exp(jnp.where(lower, g_c[rs] - g_row, 0.0)), 0.0)
        kq = _mm_nt(jnp.concatenate([k_all[rs], q_all[rs]], axis=0), block_diag(k_all[rs], kbd_ref[...]))
        a_off.append(jnp.where(strict, b_c[rs] * kq[:c] * decay, 0.0))
        qk_dec.append(kq[c:] * decay)
    yield

    eye_f = jnp.where(eye, 1.0, 0.0).astype(F32)
    x_inv = [eye_f - jnp.where(masks["join0"], a, 0.0) for a in a_off]
    for lb in range(1, lg):
        xb = [jnp.dot(x.astype(BF16), block_diag(a, bd_ref[lb]), preferred_element_type=F32)
              for x, a in zip(x_inv, a_off)]
        yield
        xbx = [jnp.dot(t.astype(BF16), block_diag(x, bd_ref[0]), preferred_element_type=F32)
               for t, x in zip(xb, x_inv)]
        x_inv = [x - t for x, t in zip(x_inv, xbx)]
        yield

    u, wk = [], []
    for cb in range(n_chunks):
        rs = slice(cb * c, (cb + 1) * c)
        sol = [_mm(head(x_inv[cb], hd, c),
                   jnp.concatenate([head(v_rhs[rs], hd, HEAD_DIM), head(k_rhs[rs], hd, HEAD_DIM)], axis=1))
               for hd in range(nh)]
        u.append([s[:, :HEAD_DIM] for s in sol])
        wk.append([s[:, HEAD_DIM:] for s in sol])
    yield

    state = [state_ref[0, hd] for hd in range(nh)]
    o_blocks = []
    for cb in range(n_chunks):
        rs = slice(cb * c, (cb + 1) * c)
        g_last = g_d[(cb + 1) * c - 1:(cb + 1) * c, :]
        k_dec = k_all[rs] * jnp.exp(g_last - g_d[rs])
        eg_last = jnp.exp(g_last)
        r = [_mm(jnp.concatenate([wk[cb][hd], head(q_g[rs], hd, HEAD_DIM)], axis=0), state[hd]) for hd in range(nh)]
        v_new = [u[cb][hd] - r[hd][:c] for hd in range(nh)]
        yield
        o_blocks.append(jnp.concatenate(
            [r[hd][c:] + _mm(head(qk_dec[cb], hd, c), v_new[hd]) for hd in range(nh)], axis=1))
        state = [state[hd] * head(eg_last, hd, HEAD_DIM) + _mm_tn(head(k_dec, hd, HEAD_DIM), v_new[hd])
                 for hd in range(nh)]
        yield
    for hd in range(nh):
        state_ref[0, hd] = state[hd]
    return jnp.concatenate(o_blocks, axis=0)


def _run_tiles(programs):
    programs = list(programs)
    results = [None] * len(programs)
    live = list(range(len(programs)))
    while live:
        for idx in list(live):
            try:
                next(programs[idx])
            except StopIteration as done:
                results[idx] = done.value
                live.remove(idx)
    return results


def _shifted_history(buf, rows, n_taps, hist_buf, seg_len):
    out = []
    with_carry = buf[0:SUBLANES + rows, :]
    for s in range(1, n_taps):
        raw = pltpu.roll(with_carry, s, 0)[SUBLANES:, :]
        if hist_buf is not None:
            t = jnp.bitwise_and(lax.broadcasted_iota(jnp.int32, (rows, 1), 0), seg_len - 1)
            raw = jnp.where(t >= s, raw, hist_buf[pl.ds(n_taps - 1 - s, rows), :])
        out.append(raw)
    return out


def _mixer_tile(ti, rows, seg_len, fused, masks, r):
    n_chunks = rows // CHUNK
    new_rows = slice(SUBLANES, SUBLANES + rows)
    qb, sb = r.qbuf.at[ti], r.sbuf.at[ti]
    if fused:
        x = r.x[ti]
        h = _rmsnorm(x, r.nmix[...]).astype(BF16)
        ab = jnp.dot(h, r.w_ab[...], preferred_element_type=F32)
        qb[new_rows, :] = jnp.dot(h, r.w_qkv[...], preferred_element_type=F32)
    else:
        ab = r.ab[ti]
        qb[new_rows, :] = r.qkv[ti]
    yield

    gate = r.gate[...]
    log_a = -jnp.exp(gate[0:1, :]) * _softplus(ab + gate[1:2, :])
    beta = _sigmoid(ab)
    pos = jnp.bitwise_and(lax.broadcasted_iota(jnp.int32, (rows, 1), 0), seg_len - 1)
    g = log_a
    shift = 1
    while shift < seg_len:
        g = g + jnp.where(pos >= shift, pltpu.roll(g, shift, 0), 0.0)
        shift *= 2

    rest = jnp.dot(h, r.w_rest[...], preferred_element_type=F32) if fused else r.rest[ti]

    sc_pre = rest[:, R_SCC:R_SCC + SC_WIDTH] * rest[:, R_SCH:R_SCH + SC_WIDTH]
    sb[new_rows, :] = sc_pre
    if fused:
        hq_buf = hs_buf = None
    else:
        hq_buf, hs_buf = r.hq_buf.at[ti], r.hs_buf.at[ti]
        zq = jnp.zeros((SUBLANES, QKV_DIM), F32)
        zs = jnp.zeros((SUBLANES, SC_WIDTH), F32)
        qb[0:SUBLANES, :] = zq
        sb[0:SUBLANES, :] = zs
        hq_buf[0:rows, :] = r.hq[ti]
        hq_buf[rows:rows + SUBLANES, :] = zq
        hs_buf[0:rows, :] = r.hs[ti]
        hs_buf[rows:rows + SUBLANES, :] = zs
    qkv_pre = qb[new_rows, :]
    q1, q2, q3 = _shifted_history(qb, rows, GDN_CONV, hq_buf, seg_len)
    cwq = r.cwq[...]
    qkv = _silu(q3 * cwq[0:1, :] + q2 * cwq[1:2, :] + q1 * cwq[2:3, :] + qkv_pre * cwq[3:4, :])
    if fused:
        r.qtail[ti] = qb[rows:rows + SUBLANES, :]
        r.stail[ti] = sb[rows:rows + SUBLANES, :]
    else:
        r.stail[ti] = sc_pre
    q_n, k_n, v_n = [], [], []
    for hd in range(GDN_HEADS):
        lo = hd * HEAD_DIM
        q_h = qkv[:, lo:lo + HEAD_DIM]
        k_h = qkv[:, GDN_WIDTH + lo:GDN_WIDTH + lo + HEAD_DIM]
        q_n.append(q_h * (lax.rsqrt(jnp.sum(q_h * q_h, axis=-1, keepdims=True) + EPS) * (HEAD_DIM ** -0.5)))
        k_n.append(k_h * lax.rsqrt(jnp.sum(k_h * k_h, axis=-1, keepdims=True) + EPS))
        v_n.append(qkv[:, 2 * GDN_WIDTH + lo:2 * GDN_WIDTH + lo + HEAD_DIM])
    yield

    state_ref = r.s_out.at[ti]
    if fused:
        o_all = yield from _wy_tile_packed(jnp.concatenate(q_n, axis=1), jnp.concatenate(k_n, axis=1),
                                           jnp.concatenate(v_n, axis=1), g, beta, state_ref, rows, masks,
                                           r.bd, r.kbd)
        o_h = [o_all[:, hd * HEAD_DIM:(hd + 1) * HEAD_DIM] for hd in range(GDN_HEADS)]
    else:
        g_t = g.T
        n_seg = CHUNK // seg_len
        blocks = []
        for cb in range(n_chunks):
            r0 = cb * CHUNK
            for hd in range(GDN_HEADS):

                def load_state(s, hd=hd, base=cb * n_seg):
                    return state_ref[base + s, hd]

                def store_state(s, val, hd=hd, base=cb * n_seg):
                    state_ref[base + s, hd] = val

                blocks.append(_wy_block(
                    q_n[hd][r0:r0 + CHUNK], k_n[hd][r0:r0 + CHUNK], v_n[hd][r0:r0 + CHUNK],
                    g[r0:r0 + CHUNK, hd:hd + 1], g_t[hd:hd + 1, r0:r0 + CHUNK],
                    beta[r0:r0 + CHUNK, GDN_HEADS + hd:GDN_HEADS + hd + 1],
                    seg_len, load_state, store_state))
        o_blocks = _run_tiles(blocks)
        o_h = [jnp.concatenate(o_blocks[hd::GDN_HEADS], axis=0) for hd in range(GDN_HEADS)]
    yield

    s1, s2 = _shifted_history(sb, rows, SC_CONV, hs_buf, seg_len)
    cws = r.cws[...]
    y_sc = rest[:, R_SCB:R_SCB + SC_WIDTH] * (s2 * cws[0:1, :] + s1 * cws[1:2, :] + sc_pre * cws[2:3, :])
    if fused:
        qb[0:SUBLANES, :] = qb[rows:rows + SUBLANES, :]
        sb[0:SUBLANES, :] = sb[rows:rows + SUBLANES, :]
    gnorm = r.gnorm[...]
    o_heads = []
    for hd in range(GDN_HEADS):
        z_h = rest[:, R_Z + hd * HEAD_DIM:R_Z + (hd + 1) * HEAD_DIM]
        o_heads.append(_rmsnorm(o_h[hd], gnorm) * _silu(z_h))
    mix = jnp.concatenate(o_heads + [y_sc], axis=1)
    if fused:
        r.xmid[ti] = x + _mm(mix, r.wout[...])
    else:
        r.mix[ti] = mix.astype(BF16)


class _Refs:
    def __init__(self, **refs):
        self.__dict__.update(refs)


def _mixer_seq_kernel(n_tiles, rows,
                      x, hq, hs, s_in, nmix, w_ab, w_qkv, w_rest, cwq, gate, gnorm, cws, wout,
                      xmid, qtail, stail, s_out, qbuf, sbuf, bd, kbd):
    @pl.when(pl.program_id(1) == 0)
    def _():
        s_out[...] = s_in[...]
        qbuf[:, 0:SUBLANES, :] = hq[...]
        sbuf[:, 0:SUBLANES, :] = hs[...]
        _fill_block_factors(bd, kbd)

    r = _Refs(x=x, nmix=nmix, w_ab=w_ab, w_qkv=w_qkv, w_rest=w_rest, cwq=cwq, gate=gate, gnorm=gnorm, cws=cws,
              wout=wout, xmid=xmid, qtail=qtail, stail=stail, s_out=s_out, qbuf=qbuf, sbuf=sbuf, bd=bd, kbd=kbd)
    masks = _packed_masks()
    _run_tiles(_mixer_tile(ti, rows, CHUNK, True, masks, r) for ti in range(n_tiles))


def _mixer_step_kernel(rows, seg_len,
                       ab, qkv, rest, hq, hs, s_in, cwq, gate, gnorm, cws,
                       mix, stail, s_out, qbuf, sbuf, hq_buf, hs_buf):
    s_out[...] = s_in[...]
    r = _Refs(ab=ab, qkv=qkv, rest=rest, hq=hq, hs=hs, cwq=cwq, gate=gate, gnorm=gnorm, cws=cws,
              mix=mix, stail=stail, s_out=s_out, qbuf=qbuf, sbuf=sbuf, hq_buf=hq_buf, hs_buf=hs_buf)
    _run_tiles([_mixer_tile(0, rows, seg_len, False, None, r)])


def _proj_kernel(x_ref, nmix_ref, w_ab_ref, w_qkv_ref, w_rest_ref, ab_ref, qkv_ref, rest_ref):
    h = _rmsnorm(x_ref[...], nmix_ref[...]).astype(BF16)
    ab_ref[...] = jnp.dot(h, w_ab_ref[...], preferred_element_type=F32)
    qkv_ref[...] = jnp.dot(h, w_qkv_ref[...], preferred_element_type=F32)
    rest_ref[...] = jnp.dot(h, w_rest_ref[...], preferred_element_type=F32)


def _ffn_body(x, p_ref, nmlp_ref, wup_ref, wdown_ref, nple_ref, wg_ref, wp_ref, nf_ref, y_ref):
    hn = _rmsnorm(x, nmlp_ref[...]).astype(BF16)
    acc = x
    for j in range(D_FF // D_MODEL):
        u = jnp.maximum(_mm(hn, wup_ref[:, j * D_MODEL:(j + 1) * D_MODEL]), 0.0)
        acc = acc + _mm(u * u, wdown_ref[j * D_MODEL:(j + 1) * D_MODEL, :])
    gate = _sigmoid(_mm(_rmsnorm(acc, nple_ref[...]), wg_ref[...]))
    x3 = acc + gate * _mm(p_ref[...], wp_ref[...])
    y_ref[...] = _rmsnorm(x3, nf_ref[...])


def _ffn_kernel(x_ref, *rest):
    _ffn_body(x_ref[...], *rest)


def _out_ffn_kernel(x_ref, mix_ref, wout_ref, *rest):
    _ffn_body(x_ref[...] + jnp.dot(mix_ref[...], wout_ref[...], preferred_element_type=F32), *rest)


def _const_spec(shape):
    nd = len(shape)
    return pl.BlockSpec(shape, lambda *_: (0,) * nd, pipeline_mode=pl.Buffered(1))


def _state_spec(n_tiles, n_state):
    return pl.BlockSpec((n_tiles, n_state, GDN_HEADS, HEAD_DIM, HEAD_DIM), lambda b, i: (b, 0, 0, 0, 0))


def _mixer_seq_call(x3, hist_q, hist_s, s_in, weights, *, n_tiles, rows):
    seqs, seq_rows, _ = x3.shape
    row_map = lambda b, i: (b, i, 0)
    seq_map = lambda b, i: (b, 0, 0)
    x_spec = pl.BlockSpec((n_tiles, rows, D_MODEL), row_map)
    hq_spec = pl.BlockSpec((n_tiles, SUBLANES, QKV_DIM), seq_map)
    hs_spec = pl.BlockSpec((n_tiles, SUBLANES, SC_WIDTH), seq_map)
    packed_w = GDN_HEADS * CHUNK
    return pl.pallas_call(
        functools.partial(_mixer_seq_kernel, n_tiles, rows),
        grid=(seqs // n_tiles, seq_rows // rows),
        in_specs=[x_spec, hq_spec, hs_spec, _state_spec(n_tiles, 1)] + [_const_spec(w.shape) for w in weights],
        out_specs=[x_spec, hq_spec, hs_spec, _state_spec(n_tiles, 1)],
        out_shape=[jax.ShapeDtypeStruct(x3.shape, F32), jax.ShapeDtypeStruct(hist_q.shape, F32),
                   jax.ShapeDtypeStruct(hist_s.shape, F32), jax.ShapeDtypeStruct(s_in.shape, F32)],
        scratch_shapes=[pltpu.VMEM((n_tiles, rows + SUBLANES, QKV_DIM), F32),
                        pltpu.VMEM((n_tiles, rows + SUBLANES, SC_WIDTH), F32),
                        pltpu.VMEM((_log2(CHUNK), packed_w, packed_w), BF16),
                        pltpu.VMEM((packed_w, GDN_WIDTH), BF16)],
        compiler_params=pltpu.CompilerParams(
            dimension_semantics=("arbitrary", "arbitrary"), vmem_limit_bytes=VMEM_LIMIT_BYTES),
        name="mixer_seq",
    )(x3, hist_q, hist_s, s_in, *weights)


def _proj_call(x2d, weights):
    n = x2d.shape[0]
    outs = [jax.ShapeDtypeStruct((n, w.shape[1]), F32) for w in weights[1:]]
    full = lambda a: pl.BlockSpec(a.shape, lambda i: (0, 0))
    return pl.pallas_call(
        _proj_kernel,
        grid=(1,),
        in_specs=[full(x2d)] + [full(w) for w in weights],
        out_specs=[full(o) for o in outs],
        out_shape=outs,
        compiler_params=pltpu.CompilerParams(dimension_semantics=("arbitrary",), vmem_limit_bytes=VMEM_LIMIT_BYTES),
        name="proj_step",
    )(x2d, *weights)


def _mixer_step_call(ab, qkv, rest, hist_q, hist_s, s_in, weights, *, seg_len):
    tiles, rows, _ = qkv.shape
    tile_map = lambda b, i: (b, 0, 0)
    slab = lambda a: pl.BlockSpec((1, rows, a.shape[2]), tile_map)
    n_state = s_in.shape[1]
    mix = jax.ShapeDtypeStruct((tiles, rows, D_MODEL), BF16)
    stail = jax.ShapeDtypeStruct((tiles, rows, SC_WIDTH), F32)
    return pl.pallas_call(
        functools.partial(_mixer_step_kernel, rows, seg_len),
        grid=(tiles, 1),
        in_specs=[slab(ab), slab(qkv), slab(rest), slab(hist_q), slab(hist_s), _state_spec(1, n_state)]
        + [_const_spec(w.shape) for w in weights],
        out_specs=[slab(mix), slab(stail), _state_spec(1, n_state)],
        out_shape=[mix, stail, jax.ShapeDtypeStruct(s_in.shape, F32)],
        scratch_shapes=[pltpu.VMEM((1, rows + SUBLANES, QKV_DIM), F32), pltpu.VMEM((1, rows + SUBLANES, SC_WIDTH), F32),
                        pltpu.VMEM((1, rows + SUBLANES, QKV_DIM), F32), pltpu.VMEM((1, rows + SUBLANES, SC_WIDTH), F32)],
        compiler_params=pltpu.CompilerParams(
            dimension_semantics=("arbitrary", "arbitrary"), vmem_limit_bytes=VMEM_LIMIT_BYTES),
        name="mixer_step",
    )(ab, qkv, rest, hist_q, hist_s, s_in, *weights)


def _ffn_call(x2d, p2d, weights, *, rows, name, mix=None, wout=None):
    n = x2d.shape[0]
    row_map = lambda i: (i, 0)
    rows_spec = lambda width: pl.BlockSpec((rows, width), row_map)
    pre_args, pre_specs, body = [], [], _ffn_kernel
    if mix is not None:
        pre_args, pre_specs, body = [mix, wout], [rows_spec(D_MODEL), _const_spec(wout.shape)], _out_ffn_kernel
    return pl.pallas_call(
        body,
        grid=(n // rows,),
        in_specs=[rows_spec(D_MODEL)] + pre_specs + [rows_spec(PLE_DIM)] + [_const_spec(w.shape) for w in weights],
        out_specs=rows_spec(D_MODEL),
        out_shape=jax.ShapeDtypeStruct(x2d.shape, F32),
        compiler_params=pltpu.CompilerParams(dimension_semantics=("arbitrary",), vmem_limit_bytes=VMEM_LIMIT_BYTES),
        name=name,
    )(x2d, *pre_args, p2d, *weights)


def _pad_rows(a, rows):
    return jnp.pad(a, ((0, 0), (0, rows - a.shape[1]), (0, 0)))


def _layer(x_prompt, x_sample, conv_qkv, s_gdn, conv_sc, p_prompt, p_sample, norm_mix, w_in, w_conv_qkv, a_log,
           dt_bias, w_gdn_norm, w_conv_sc, w_out, norm_mlp, w_up, w_down, norm_ple, w_ple_gate, w_ple_proj, norm_f):
    bp, tp, _ = x_prompt.shape
    bs, ts, _ = x_sample.shape
    o_a = QKV_DIM
    o_z = o_a + 2 * GDN_HEADS
    w_ab = jnp.pad(w_in[:, o_a:o_z], ((0, 0), (0, LANES - 2 * GDN_HEADS))).astype(BF16)
    w_qkv = w_in[:, :QKV_DIM].astype(BF16)
    w_rest = w_in[:, o_z:].astype(BF16)
    wout = w_out.astype(BF16)
    gate = jnp.zeros((SUBLANES, LANES), F32)
    gate = gate.at[0, :GDN_HEADS].set(a_log.astype(F32)).at[1, :GDN_HEADS].set(dt_bias.astype(F32))
    nmix = norm_mix.reshape(1, D_MODEL)
    core_w = (w_conv_qkv, gate, w_gdn_norm.reshape(1, HEAD_DIM), w_conv_sc)
    ffn_w = (norm_mlp.reshape(1, D_MODEL), w_up.astype(BF16), w_down.astype(BF16), norm_ple.reshape(1, D_MODEL),
             w_ple_gate.astype(BF16), w_ple_proj.astype(BF16), norm_f.reshape(1, D_MODEL))

    zq = jnp.zeros((bp, SUBLANES, QKV_DIM), F32)
    zs = jnp.zeros((bp, SUBLANES, SC_WIDTH), F32)
    s0 = jnp.zeros((bp, 1, GDN_HEADS, HEAD_DIM, HEAD_DIM), F32)
    xm_p, qt_p, st_p, s_p = _mixer_seq_call(
        x_prompt, zq, zs, s0, (nmix, w_ab, w_qkv, w_rest) + core_w + (wout,),
        n_tiles=PROMPT_SEQS_PER_STEP, rows=PROMPT_ROWS)
    y_p = _ffn_call(xm_p.reshape(bp * tp, D_MODEL), p_prompt.reshape(bp * tp, PLE_DIM), ffn_w, rows=FFN_ROWS,
                    name="ffn_prompt")

    seq_per_tile = CHUNK // ts
    tiles = bs // seq_per_tile
    x_s = x_sample.reshape(bs * ts, D_MODEL)
    ab_s, qkv_s, rest_s = _proj_call(x_s, (nmix, w_ab, w_qkv, w_rest))
    tiled = lambda a: a.reshape(tiles, CHUNK, a.shape[-1])
    mix_s, st_s, s_s = _mixer_step_call(
        tiled(ab_s), tiled(qkv_s), tiled(rest_s), tiled(_pad_rows(conv_qkv, ts)), tiled(_pad_rows(conv_sc, ts)),
        s_gdn.reshape(tiles, seq_per_tile, GDN_HEADS, HEAD_DIM, HEAD_DIM), core_w, seg_len=ts)
    y_s = _ffn_call(x_s, p_sample.reshape(bs * ts, PLE_DIM), ffn_w, rows=bs * ts, name="ffn_sample",
                    mix=mix_s.reshape(bs * ts, D_MODEL), wout=wout)

    new_conv_p = qt_p[:, SUBLANES - (GDN_CONV - 1):]
    new_sc_p = st_p[:, SUBLANES - (SC_CONV - 1):]
    new_conv_s = qkv_s.reshape(bs, ts, QKV_DIM)[:, ts - (GDN_CONV - 1):]
    new_sc_s = st_s.reshape(bs, ts, SC_WIDTH)[:, ts - (SC_CONV - 1):]
    return (y_p.reshape(bp, tp, D_MODEL), y_s.reshape(bs, ts, D_MODEL), new_conv_p,
            s_p.reshape(bp, GDN_HEADS, HEAD_DIM, HEAD_DIM), new_sc_p, new_conv_s,
            s_s.reshape(bs, GDN_HEADS, HEAD_DIM, HEAD_DIM), new_sc_s)


def kernel(x_prompt, x_sample, state_gdn_conv, state_gdn, state_sc_conv, p_prompt, p_sample, norm_mix, w_in, w_conv_qkv, a_log, dt_bias, w_gdn_norm, w_conv_sc, w_out, norm_mlp, w_up, w_down, norm_ple, w_ple_gate, w_ple_proj, norm_f):
    depth = w_in.shape[0]
    assert depth == 1, "one layer per call"
    assert x_sample.shape[1] >= GDN_CONV - 1 and CHUNK % x_sample.shape[1] == 0
    assert x_prompt.shape[1] % PROMPT_ROWS == 0 and x_prompt.shape[0] % PROMPT_SEQS_PER_STEP == 0
    outs = _layer(x_prompt, x_sample, state_gdn_conv[0], state_gdn[0], state_sc_conv[0], p_prompt[0], p_sample[0],
                  norm_mix[0], w_in[0], w_conv_qkv[0], a_log[0], dt_bias[0], w_gdn_norm[0], w_conv_sc[0], w_out[0],
                  norm_mlp[0], w_up[0], w_down[0], norm_ple[0], w_ple_gate[0], w_ple_proj[0], norm_f)
    y_p, y_s, c_p, s_p, sc_p, c_s, s_s, sc_s = outs
    return (y_p, y_s, c_p[None], s_p[None], sc_p[None], c_s[None], s_s[None], sc_s[None])
```

```python
import functools

import jax
import jax.numpy as jnp
from jax import lax
from jax.experimental import pallas as pl
from jax.experimental.pallas import tpu as pltpu

F32 = jnp.float32
BF16 = jnp.bfloat16

D_MODEL = 1024
PLE_DIM = 256
GDN_HEADS = 4
HEAD_DIM = 128
GDN_WIDTH = GDN_HEADS * HEAD_DIM
QKV_DIM = 3 * GDN_WIDTH
GDN_CONV = 4
SC_WIDTH = D_MODEL - GDN_WIDTH
SC_CONV = 3
D_FF = 4 * D_MODEL
EPS = 1e-6
NEG_LOG2_E = -1.4426950408889634
CHUNK = 64
LANES = 128
SUBLANES = 8
VMEM_LIMIT_BYTES = 56 * 1024 * 1024
PROMPT_ROWS = 128
PROMPT_SEQS_PER_STEP = 4
FFN_ROWS = 1024

R_Z = 0
R_SCB = R_Z + GDN_WIDTH
R_SCC = R_SCB + SC_WIDTH
R_SCH = R_SCC + SC_WIDTH


def _mm(a, b):
    return jnp.dot(a.astype(BF16), b.astype(BF16), preferred_element_type=F32)


def _mm_nt(a, b):
    return lax.dot_general(a.astype(BF16), b.astype(BF16), (((1,), (1,)), ((), ())), preferred_element_type=F32)


def _mm_tn(a, b):
    return lax.dot_general(a.astype(BF16), b.astype(BF16), (((0,), (0,)), ((), ())), preferred_element_type=F32)


def _rmsnorm(x, w_row):
    return x * lax.rsqrt(jnp.mean(x * x, axis=-1, keepdims=True) + EPS) * w_row


def _sigmoid(x):
    return 1.0 / (1.0 + jnp.exp2(x * NEG_LOG2_E))


def _silu(x):
    return x * _sigmoid(x)


def _softplus(x):
    return jnp.maximum(x, 0.0) + jnp.log1p(jnp.exp(-jnp.abs(x)))


def _log2(n):
    k = n.bit_length() - 1
    assert (1 << k) == n, n
    return k


def _wy_block(q, k, v, g_col, g_row, beta_col, seg_len, load_state, store_state):
    c = CHUNK
    n_seg = c // seg_len
    lg = _log2(seg_len)
    ri = lax.broadcasted_iota(jnp.int32, (c, c), 0)
    ci = lax.broadcasted_iota(jnp.int32, (c, c), 1)
    same = jnp.right_shift(ri, lg) == jnp.right_shift(ci, lg)
    lower = same & (ri >= ci)
    strict = same & (ri > ci)
    diff = g_col - g_row
    decay = jnp.where(lower, jnp.exp(jnp.where(lower, diff, 0.0)), 0.0)
    a_off = jnp.where(strict, beta_col * _mm_nt(k, k) * decay, 0.0)
    qk = _mm_nt(q, k) * decay
    yield
    eye = jnp.where(ri == ci, 1.0, 0.0).astype(F32)
    x_inv = eye
    for lb in range(lg):
        rb = jnp.right_shift(ri, lb)
        cb = jnp.right_shift(ci, lb)
        join = (jnp.bitwise_and(rb, 1) == 1) & (cb == rb - 1)
        b_lvl = jnp.where(join, a_off, 0.0)
        if lb == 0:
            x_inv = eye - b_lvl
        else:
            xb = _mm(x_inv, b_lvl)
            yield
            x_inv = x_inv - _mm(xb, x_inv)
            yield
    rhs = jnp.concatenate([v * beta_col, k * (beta_col * jnp.exp(g_col))], axis=1)
    sol = _mm(x_inv, rhs)
    yield
    u = sol[:, :HEAD_DIM]
    w = sol[:, HEAD_DIM:]
    stacked = jnp.concatenate([w, q * jnp.exp(g_col)], axis=0)
    row = lax.broadcasted_iota(jnp.int32, (c, 1), 0)
    states = [load_state(s) for s in range(n_seg)]
    w_s = None
    q_s = None
    for s in range(n_seg):
        r = _mm(stacked, states[s])
        if n_seg == 1:
            w_s, q_s = r[:c], r[c:]
        else:
            in_seg = jnp.right_shift(row, lg) == s
            w_s = jnp.where(in_seg, r[:c], 0.0 if w_s is None else w_s)
            q_s = jnp.where(in_seg, r[c:], 0.0 if q_s is None else q_s)
    v_new = u - w_s
    yield
    o = q_s + _mm(qk, v_new)
    for s in range(n_seg):
        last = s * seg_len + seg_len - 1
        g_last = g_col[last:last + 1, :]
        if n_seg == 1:
            k_dec = k * jnp.exp(g_last - g_col)
        else:
            in_seg = jnp.right_shift(row, lg) == s
            k_dec = jnp.where(in_seg, k * jnp.exp(jnp.where(in_seg, g_last - g_col, 0.0)), 0.0)
        store_state(s, states[s] * jnp.exp(g_last) + _mm_tn(k_dec, v_new))
    return o


def _expand_heads(slab, first_lane, width):
    rows = slab.shape[0]
    cols = [slab[:, first_lane + hd:first_lane + hd + 1] for hd in range(GDN_HEADS)]
    if width % LANES == 0:
        return jnp.concatenate([jnp.broadcast_to(c, (rows, width)) for c in cols], axis=1)
    total = GDN_HEADS * width
    lane_head = jnp.right_shift(lax.broadcasted_iota(jnp.int32, (rows, total), 1), _log2(width))
    out = jnp.broadcast_to(cols[-1], (rows, total))
    for hd in range(GDN_HEADS - 2, -1, -1):
        out = jnp.where(lane_head == hd, jnp.broadcast_to(cols[hd], (rows, total)), out)
    return out


def _packed_masks():
    c, nh = CHUNK, GDN_HEADS
    ri = lax.broadcasted_iota(jnp.int32, (c, nh * c), 0)
    cj = jnp.bitwise_and(lax.broadcasted_iota(jnp.int32, (c, nh * c), 1), c - 1)
    return dict(lower=ri >= cj, strict=ri > cj, eye=ri == cj, join0=(jnp.bitwise_and(ri, 1) == 1) & (cj == ri - 1))


def _fill_block_factors(bd_ref, kbd_ref):
    c, nh = CHUNK, GDN_HEADS
    w = nh * c
    lg = _log2(c)
    rw = lax.broadcasted_iota(jnp.int32, (w, w), 0)
    cw = lax.broadcasted_iota(jnp.int32, (w, w), 1)
    same_head = jnp.right_shift(rw, lg) == jnp.right_shift(cw, lg)
    rw, cw = jnp.bitwise_and(rw, c - 1), jnp.bitwise_and(cw, c - 1)
    one_zero = lambda m: jnp.where(m, 1.0, 0.0).astype(BF16)
    bd_ref[0] = one_zero(same_head)
    for lb in range(1, lg):
        bd_ref[lb] = one_zero(same_head & (jnp.bitwise_and(jnp.right_shift(rw, lb), 1) == 1)
                              & (jnp.right_shift(cw, lb) == jnp.right_shift(rw, lb) - 1))
    kbd_ref[...] = one_zero(jnp.right_shift(lax.broadcasted_iota(jnp.int32, (w, nh * HEAD_DIM), 0), lg)
                            == jnp.right_shift(lax.broadcasted_iota(jnp.int32, (w, nh * HEAD_DIM), 1),
                                               _log2(HEAD_DIM)))


def _wy_tile_packed(q_all, k_all, v_all, g, beta, state_ref, rows, masks, bd_ref, kbd_ref):
    c, nh = CHUNK, GDN_HEADS
    lg = _log2(c)
    n_chunks = rows // c
    lower, strict, eye = masks["lower"], masks["strict"], masks["eye"]

    def block_diag(x, factor):
        return jnp.concatenate([x.astype(BF16)] * nh, axis=0) * factor

    def head(x, hd, width):
        return x[:, hd * width:(hd + 1) * width]

    g_c = _expand_heads(g, 0, c)
    b_c = _expand_heads(beta, nh, c)
    g_d = _expand_heads(g, 0, HEAD_DIM)
    b_d = _expand_heads(beta, nh, HEAD_DIM)
    eg_d = jnp.exp(g_d)
    v_rhs = v_all * b_d
    k_rhs = k_all * (b_d * eg_d)
    q_g = q_all * eg_d

    a_off, qk_dec = [], []
    for cb in range(n_chunks):
        rs = slice(cb * c, (cb + 1) * c)
        g_row = jnp.sum(jnp.where(eye, g_c[rs], 0.0), axis=0, keepdims=True)
        decay = jnp.where(lower, jnp.exp(jnp.where(lower, g_c[rs] - g_row, 0.0)), 0.0)
        kq = _mm_nt(jnp.concatenate([k_all[rs], q_all[rs]], axis=0), block_diag(k_all[rs], kbd_ref[...]))
        a_off.append(jnp.where(strict, b_c[rs] * kq[:c] * decay, 0.0))
        qk_dec.append(kq[c:] * decay)
    yield

    eye_f = jnp.where(eye, 1.0, 0.0).astype(F32)
    x_inv = [eye_f - jnp.where(masks["join0"], a, 0.0) for a in a_off]
    for lb in range(1, lg):
        xb = [jnp.dot(x.astype(BF16), block_diag(a, bd_ref[lb]), preferred_element_type=F32)
              for x, a in zip(x_inv, a_off)]
        yield
        xbx = [jnp.dot(t.astype(BF16), block_diag(x, bd_ref[0]), preferred_element_type=F32)
               for t, x in zip(xb, x_inv)]
        x_inv = [x - t for x, t in zip(x_inv, xbx)]
        yield

    u, wk = [], []
    for cb in range(n_chunks):
        rs = slice(cb * c, (cb + 1) * c)
        sol = [_mm(head(x_inv[cb], hd, c),
                   jnp.concatenate([head(v_rhs[rs], hd, HEAD_DIM), head(k_rhs[rs], hd, HEAD_DIM)], axis=1))
               for hd in range(nh)]
        u.append([s[:, :HEAD_DIM] for s in sol])
        wk.append([s[:, HEAD_DIM:] for s in sol])
    yield

    state = [state_ref[0, hd] for hd in range(nh)]
    o_blocks = []
    for cb in range(n_chunks):
        rs = slice(cb * c, (cb + 1) * c)
        g_last = g_d[(cb + 1) * c - 1:(cb + 1) * c, :]
        k_dec = k_all[rs] * jnp.exp(g_last - g_d[rs])
        eg_last = jnp.exp(g_last)
        r = [_mm(jnp.concatenate([wk[cb][hd], head(q_g[rs], hd, HEAD_DIM)], axis=0), state[hd]) for hd in range(nh)]
        v_new = [u[cb][hd] - r[hd][:c] for hd in range(nh)]
        yield
        o_blocks.append(jnp.concatenate(
            [r[hd][c:] + _mm(head(qk_dec[cb], hd, c), v_new[hd]) for hd in range(nh)], axis=1))
        state = [state[hd] * head(eg_last, hd, HEAD_DIM) + _mm_tn(head(k_dec, hd, HEAD_DIM), v_new[hd])
                 for hd in range(nh)]
        yield
    for hd in range(nh):
        state_ref[0, hd] = state[hd]
    return jnp.concatenate(o_blocks, axis=0)


def _run_tiles(programs):
    programs = list(programs)
    results = [None] * len(programs)
    live = list(range(len(programs)))
    while live:
        for idx in list(live):
            try:
                next(programs[idx])
            except StopIteration as done:
                results[idx] = done.value
                live.remove(idx)
    return results


def _shifted_history(buf, rows, n_taps, hist_buf, seg_len):
    out = []
    with_carry = buf[0:SUBLANES + rows, :]
    for s in range(1, n_taps):
        raw = pltpu.roll(with_carry, s, 0)[SUBLANES:, :]
        if hist_buf is not None:
            t = jnp.bitwise_and(lax.broadcasted_iota(jnp.int32, (rows, 1), 0), seg_len - 1)
            raw = jnp.where(t >= s, raw, hist_buf[pl.ds(n_taps - 1 - s, rows), :])
        out.append(raw)
    return out


def _mixer_tile(ti, rows, seg_len, long_seq, masks, r, ab, rest):
    n_chunks = rows // CHUNK
    new_rows = slice(SUBLANES, SUBLANES + rows)
    qb, sb = r.qbuf.at[ti], r.sbuf.at[ti]

    gate = r.gate[...]
    log_a = -jnp.exp(gate[0:1, :]) * _softplus(ab + gate[1:2, :])
    beta = _sigmoid(ab)
    pos = jnp.bitwise_and(lax.broadcasted_iota(jnp.int32, (rows, 1), 0), seg_len - 1)
    g = log_a
    shift = 1
    while shift < seg_len:
        g = g + jnp.where(pos >= shift, pltpu.roll(g, shift, 0), 0.0)
        shift *= 2

    sc_pre = rest[:, R_SCC:R_SCC + SC_WIDTH] * rest[:, R_SCH:R_SCH + SC_WIDTH]
    sb[new_rows, :] = sc_pre
    if long_seq:
        hq_buf = hs_buf = None
    else:
        hq_buf, hs_buf = r.hq_buf.at[ti], r.hs_buf.at[ti]
        zq = jnp.zeros((SUBLANES, QKV_DIM), F32)
        zs = jnp.zeros((SUBLANES, SC_WIDTH), F32)
        qb[0:SUBLANES, :] = zq
        sb[0:SUBLANES, :] = zs
        hq_buf[0:rows, :] = r.hq[ti]
        hq_buf[rows:rows + SUBLANES, :] = zq
        hs_buf[0:rows, :] = r.hs[ti]
        hs_buf[rows:rows + SUBLANES, :] = zs
    qkv_pre = qb[new_rows, :]
    q1, q2, q3 = _shifted_history(qb, rows, GDN_CONV, hq_buf, seg_len)
    cwq = r.cwq[...]
    qkv = _silu(q3 * cwq[0:1, :] + q2 * cwq[1:2, :] + q1 * cwq[2:3, :] + qkv_pre * cwq[3:4, :])
    if long_seq:
        r.qtail[ti] = qb[rows:rows + SUBLANES, :]
        r.stail[ti] = sb[rows:rows + SUBLANES, :]
    else:
        r.stail[ti] = sc_pre
    q_n, k_n, v_n = [], [], []
    for hd in range(GDN_HEADS):
        lo = hd * HEAD_DIM
        q_h = qkv[:, lo:lo + HEAD_DIM]
        k_h = qkv[:, GDN_WIDTH + lo:GDN_WIDTH + lo + HEAD_DIM]
        q_n.append(q_h * (lax.rsqrt(jnp.sum(q_h * q_h, axis=-1, keepdims=True) + EPS) * (HEAD_DIM ** -0.5)))
        k_n.append(k_h * lax.rsqrt(jnp.sum(k_h * k_h, axis=-1, keepdims=True) + EPS))
        v_n.append(qkv[:, 2 * GDN_WIDTH + lo:2 * GDN_WIDTH + lo + HEAD_DIM])
    yield

    state_ref = r.s_out.at[ti]
    if long_seq:
        o_all = yield from _wy_tile_packed(jnp.concatenate(q_n, axis=1), jnp.concatenate(k_n, axis=1),
                                           jnp.concatenate(v_n, axis=1), g, beta, state_ref, rows, masks,
                                           r.bd, r.kbd)
        o_h = [o_all[:, hd * HEAD_DIM:(hd + 1) * HEAD_DIM] for hd in range(GDN_HEADS)]
    else:
        g_t = g.T
        n_seg = CHUNK // seg_len
        blocks = []
        for cb in range(n_chunks):
            r0 = cb * CHUNK
            for hd in range(GDN_HEADS):

                def load_state(s, hd=hd, base=cb * n_seg):
                    return state_ref[base + s, hd]

                def store_state(s, val, hd=hd, base=cb * n_seg):
                    state_ref[base + s, hd] = val

                blocks.append(_wy_block(
                    q_n[hd][r0:r0 + CHUNK], k_n[hd][r0:r0 + CHUNK], v_n[hd][r0:r0 + CHUNK],
                    g[r0:r0 + CHUNK, hd:hd + 1], g_t[hd:hd + 1, r0:r0 + CHUNK],
                    beta[r0:r0 + CHUNK, GDN_HEADS + hd:GDN_HEADS + hd + 1],
                    seg_len, load_state, store_state))
        o_blocks = _run_tiles(blocks)
        o_h = [jnp.concatenate(o_blocks[hd::GDN_HEADS], axis=0) for hd in range(GDN_HEADS)]
    yield

    s1, s2 = _shifted_history(sb, rows, SC_CONV, hs_buf, seg_len)
    cws = r.cws[...]
    y_sc = rest[:, R_SCB:R_SCB + SC_WIDTH] * (s2 * cws[0:1, :] + s1 * cws[1:2, :] + sc_pre * cws[2:3, :])
    if long_seq:
        qb[0:SUBLANES, :] = qb[rows:rows + SUBLANES, :]
        sb[0:SUBLANES, :] = sb[rows:rows + SUBLANES, :]
    gnorm = r.gnorm[...]
    o_heads = []
    for hd in range(GDN_HEADS):
        z_h = rest[:, R_Z + hd * HEAD_DIM:R_Z + (hd + 1) * HEAD_DIM]
        o_heads.append(_rmsnorm(o_h[hd], gnorm) * _silu(z_h))
    return jnp.concatenate(o_heads + [y_sc], axis=1)


class _Refs:
    def __init__(self, **refs):
        self.__dict__.update(refs)


def _mixer_seq_kernel(n_tiles, rows,
                      x, hq, hs, s_in, nmix, w_ab, w_qkv, w_rest, cwq, gate, gnorm, cws, wout,
                      xmid, qtail, stail, s_out, qbuf, sbuf, bd, kbd):
    @pl.when(pl.program_id(1) == 0)
    def _():
        s_out[...] = s_in[...]
        qbuf[:, 0:SUBLANES, :] = hq[...]
        sbuf[:, 0:SUBLANES, :] = hs[...]
        _fill_block_factors(bd, kbd)

    x_all = x[...].reshape(n_tiles * rows, D_MODEL)
    h = _rmsnorm(x_all, nmix[...]).astype(BF16)
    ab = jnp.dot(h, w_ab[...], preferred_element_type=F32)
    qbuf[:, SUBLANES:SUBLANES + rows, :] = jnp.dot(h, w_qkv[...], preferred_element_type=F32).reshape(
        n_tiles, rows, QKV_DIM)
    rest = jnp.dot(h, w_rest[...], preferred_element_type=F32)
    r = _Refs(cwq=cwq, gate=gate, gnorm=gnorm, cws=cws, qtail=qtail, stail=stail, s_out=s_out, qbuf=qbuf, sbuf=sbuf,
              bd=bd, kbd=kbd)
    masks = _packed_masks()
    tile = lambda a, ti: a[ti * rows:(ti + 1) * rows]
    mix = _run_tiles(_mixer_tile(ti, rows, CHUNK, True, masks, r, tile(ab, ti), tile(rest, ti))
                     for ti in range(n_tiles))
    xmid[...] = (x_all + _mm(jnp.concatenate(mix, axis=0), wout[...])).reshape(n_tiles, rows, D_MODEL)


def _mixer_step_kernel(rows, seg_len,
                       ab, qkv, rest, hq, hs, s_in, cwq, gate, gnorm, cws,
                       mix, stail, s_out, qbuf, sbuf, hq_buf, hs_buf):
    s_out[...] = s_in[...]
    qbuf[0, SUBLANES:SUBLANES + rows, :] = qkv[0]
    r = _Refs(hq=hq, hs=hs, cwq=cwq, gate=gate, gnorm=gnorm, cws=cws, stail=stail, s_out=s_out, qbuf=qbuf, sbuf=sbuf,
              hq_buf=hq_buf, hs_buf=hs_buf)
    mix[0] = _run_tiles([_mixer_tile(0, rows, seg_len, False, None, r, ab[0], rest[0])])[0].astype(BF16)


def _proj_kernel(x_ref, nmix_ref, w_ab_ref, w_qkv_ref, w_rest_ref, ab_ref, qkv_ref, rest_ref):
    h = _rmsnorm(x_ref[...], nmix_ref[...]).astype(BF16)
    ab_ref[...] = jnp.dot(h, w_ab_ref[...], preferred_element_type=F32)
    qkv_ref[...] = jnp.dot(h, w_qkv_ref[...], preferred_element_type=F32)
    rest_ref[...] = jnp.dot(h, w_rest_ref[...], preferred_element_type=F32)


def _ffn_body(x, p_ref, nmlp_ref, wup_ref, wdown_ref, nple_ref, wg_ref, wp_ref, nf_ref, y_ref):
    hn = _rmsnorm(x, nmlp_ref[...]).astype(BF16)
    acc = x
    for j in range(D_FF // D_MODEL):
        u = jnp.maximum(_mm(hn, wup_ref[:, j * D_MODEL:(j + 1) * D_MODEL]), 0.0)
        acc = acc + _mm(u * u, wdown_ref[j * D_MODEL:(j + 1) * D_MODEL, :])
    gate = _sigmoid(_mm(_rmsnorm(acc, nple_ref[...]), wg_ref[...]))
    x3 = acc + gate * _mm(p_ref[...], wp_ref[...])
    y_ref[...] = _rmsnorm(x3, nf_ref[...])


def _ffn_kernel(x_ref, *rest):
    _ffn_body(x_ref[...], *rest)


def _out_ffn_kernel(x_ref, mix_ref, wout_ref, *rest):
    _ffn_body(x_ref[...] + jnp.dot(mix_ref[...], wout_ref[...], preferred_element_type=F32), *rest)


def _const_spec(shape):
    nd = len(shape)
    return pl.BlockSpec(shape, lambda *_: (0,) * nd, pipeline_mode=pl.Buffered(1))


def _state_spec(n_tiles, n_state):
    return pl.BlockSpec((n_tiles, n_state, GDN_HEADS, HEAD_DIM, HEAD_DIM), lambda b, i: (b, 0, 0, 0, 0))


def _mixer_seq_call(x3, hist_q, hist_s, s_in, weights, *, n_tiles, rows):
    seqs, seq_rows, _ = x3.shape
    row_map = lambda b, i: (b, i, 0)
    seq_map = lambda b, i: (b, 0, 0)
    x_spec = pl.BlockSpec((n_tiles, rows, D_MODEL), row_map)
    hq_spec = pl.BlockSpec((n_tiles, SUBLANES, QKV_DIM), seq_map)
    hs_spec = pl.BlockSpec((n_tiles, SUBLANES, SC_WIDTH), seq_map)
    packed_w = GDN_HEADS * CHUNK
    return pl.pallas_call(
        functools.partial(_mixer_seq_kernel, n_tiles, rows),
        grid=(seqs // n_tiles, seq_rows // rows),
        in_specs=[x_spec, hq_spec, hs_spec, _state_spec(n_tiles, 1)] + [_const_spec(w.shape) for w in weights],
        out_specs=[x_spec, hq_spec, hs_spec, _state_spec(n_tiles, 1)],
        out_shape=[jax.ShapeDtypeStruct(x3.shape, F32), jax.ShapeDtypeStruct(hist_q.shape, F32),
                   jax.ShapeDtypeStruct(hist_s.shape, F32), jax.ShapeDtypeStruct(s_in.shape, F32)],
        scratch_shapes=[pltpu.VMEM((n_tiles, rows + SUBLANES, QKV_DIM), F32),
                        pltpu.VMEM((n_tiles, rows + SUBLANES, SC_WIDTH), F32),
                        pltpu.VMEM((_log2(CHUNK), packed_w, packed_w), BF16),
                        pltpu.VMEM((packed_w, GDN_WIDTH), BF16)],
        compiler_params=pltpu.CompilerParams(
            dimension_semantics=("arbitrary", "arbitrary"), vmem_limit_bytes=VMEM_LIMIT_BYTES),
        name="mixer_seq",
    )(x3, hist_q, hist_s, s_in, *weights)


def _proj_call(x2d, weights):
    n = x2d.shape[0]
    outs = [jax.ShapeDtypeStruct((n, w.shape[1]), F32) for w in weights[1:]]
    full = lambda a: pl.BlockSpec(a.shape, lambda i: (0, 0))
    return pl.pallas_call(
        _proj_kernel,
        grid=(1,),
        in_specs=[full(x2d)] + [full(w) for w in weights],
        out_specs=[full(o) for o in outs],
        out_shape=outs,
        compiler_params=pltpu.CompilerParams(dimension_semantics=("arbitrary",), vmem_limit_bytes=VMEM_LIMIT_BYTES),
        name="proj_step",
    )(x2d, *weights)


def _mixer_step_call(ab, qkv, rest, hist_q, hist_s, s_in, weights, *, seg_len):
    tiles, rows, _ = qkv.shape
    tile_map = lambda b, i: (b, 0, 0)
    slab = lambda a: pl.BlockSpec((1, rows, a.shape[2]), tile_map)
    n_state = s_in.shape[1]
    mix = jax.ShapeDtypeStruct((tiles, rows, D_MODEL), BF16)
    stail = jax.ShapeDtypeStruct((tiles, rows, SC_WIDTH), F32)
    return pl.pallas_call(
        functools.partial(_mixer_step_kernel, rows, seg_len),
        grid=(tiles, 1),
        in_specs=[slab(ab), slab(qkv), slab(rest), slab(hist_q), slab(hist_s), _state_spec(1, n_state)]
        + [_const_spec(w.shape) for w in weights],
        out_specs=[slab(mix), slab(stail), _state_spec(1, n_state)],
        out_shape=[mix, stail, jax.ShapeDtypeStruct(s_in.shape, F32)],
        scratch_shapes=[pltpu.VMEM((1, rows + SUBLANES, QKV_DIM), F32), pltpu.VMEM((1, rows + SUBLANES, SC_WIDTH), F32),
                        pltpu.VMEM((1, rows + SUBLANES, QKV_DIM), F32), pltpu.VMEM((1, rows + SUBLANES, SC_WIDTH), F32)],
        compiler_params=pltpu.CompilerParams(
            dimension_semantics=("arbitrary", "arbitrary"), vmem_limit_bytes=VMEM_LIMIT_BYTES),
        name="mixer_step",
    )(ab, qkv, rest, hist_q, hist_s, s_in, *weights)


def _ffn_call(x2d, p2d, weights, *, rows, name, mix=None, wout=None):
    n = x2d.shape[0]
    row_map = lambda i: (i, 0)
    rows_spec = lambda width: pl.BlockSpec((rows, width), row_map)
    pre_args, pre_specs, body = [], [], _ffn_kernel
    if mix is not None:
        pre_args, pre_specs, body = [mix, wout], [rows_spec(D_MODEL), _const_spec(wout.shape)], _out_ffn_kernel
    return pl.pallas_call(
        body,
        grid=(n // rows,),
        in_specs=[rows_spec(D_MODEL)] + pre_specs + [rows_spec(PLE_DIM)] + [_const_spec(w.shape) for w in weights],
        out_specs=rows_spec(D_MODEL),
        out_shape=jax.ShapeDtypeStruct(x2d.shape, F32),
        compiler_params=pltpu.CompilerParams(dimension_semantics=("arbitrary",), vmem_limit_bytes=VMEM_LIMIT_BYTES),
        name=name,
    )(x2d, *pre_args, p2d, *weights)


def _pad_rows(a, rows):
    return jnp.pad(a, ((0, 0), (0, rows - a.shape[1]), (0, 0)))


def _layer(x_prompt, x_sample, conv_qkv, s_gdn, conv_sc, p_prompt, p_sample, norm_mix, w_in, w_conv_qkv, a_log,
           dt_bias, w_gdn_norm, w_conv_sc, w_out, norm_mlp, w_up, w_down, norm_ple, w_ple_gate, w_ple_proj, norm_f):
    bp, tp, _ = x_prompt.shape
    bs, ts, _ = x_sample.shape
    o_a = QKV_DIM
    o_z = o_a + 2 * GDN_HEADS
    w_ab = jnp.pad(w_in[:, o_a:o_z], ((0, 0), (0, LANES - 2 * GDN_HEADS))).astype(BF16)
    w_qkv = w_in[:, :QKV_DIM].astype(BF16)
    w_rest = w_in[:, o_z:].astype(BF16)
    wout = w_out.astype(BF16)
    gate = jnp.zeros((SUBLANES, LANES), F32)
    gate = gate.at[0, :GDN_HEADS].set(a_log.astype(F32)).at[1, :GDN_HEADS].set(dt_bias.astype(F32))
    nmix = norm_mix.reshape(1, D_MODEL)
    core_w = (w_conv_qkv, gate, w_gdn_norm.reshape(1, HEAD_DIM), w_conv_sc)
    ffn_w = (norm_mlp.reshape(1, D_MODEL), w_up.astype(BF16), w_down.astype(BF16), norm_ple.reshape(1, D_MODEL),
             w_ple_gate.astype(BF16), w_ple_proj.astype(BF16), norm_f.reshape(1, D_MODEL))

    zq = jnp.zeros((bp, SUBLANES, QKV_DIM), F32)
    zs = jnp.zeros((bp, SUBLANES, SC_WIDTH), F32)
    s0 = jnp.zeros((bp, 1, GDN_HEADS, HEAD_DIM, HEAD_DIM), F32)
    xm_p, qt_p, st_p, s_p = _mixer_seq_call(
        x_prompt, zq, zs, s0, (nmix, w_ab, w_qkv, w_rest) + core_w + (wout,),
        n_tiles=PROMPT_SEQS_PER_STEP, rows=PROMPT_ROWS)
    y_p = _ffn_call(xm_p.reshape(bp * tp, D_MODEL), p_prompt.reshape(bp * tp, PLE_DIM), ffn_w, rows=FFN_ROWS,
                    name="ffn_prompt")

    seq_per_tile = CHUNK // ts
    tiles = bs // seq_per_tile
    x_s = x_sample.reshape(bs * ts, D_MODEL)
    ab_s, qkv_s, rest_s = _proj_call(x_s, (nmix, w_ab, w_qkv, w_rest))
    tiled = lambda a: a.reshape(tiles, CHUNK, a.shape[-1])
    mix_s, st_s, s_s = _mixer_step_call(
        tiled(ab_s), tiled(qkv_s), tiled(rest_s), tiled(_pad_rows(conv_qkv, ts)), tiled(_pad_rows(conv_sc, ts)),
        s_gdn.reshape(tiles, seq_per_tile, GDN_HEADS, HEAD_DIM, HEAD_DIM), core_w, seg_len=ts)
    y_s = _ffn_call(x_s, p_sample.reshape(bs * ts, PLE_DIM), ffn_w, rows=bs * ts, name="ffn_sample",
                    mix=mix_s.reshape(bs * ts, D_MODEL), wout=wout)

    new_conv_p = qt_p[:, SUBLANES - (GDN_CONV - 1):]
    new_sc_p = st_p[:, SUBLANES - (SC_CONV - 1):]
    new_conv_s = qkv_s.reshape(bs, ts, QKV_DIM)[:, ts - (GDN_CONV - 1):]
    new_sc_s = st_s.reshape(bs, ts, SC_WIDTH)[:, ts - (SC_CONV - 1):]
    return (y_p.reshape(bp, tp, D_MODEL), y_s.reshape(bs, ts, D_MODEL), new_conv_p,
            s_p.reshape(bp, GDN_HEADS, HEAD_DIM, HEAD_DIM), new_sc_p, new_conv_s,
            s_s.reshape(bs, GDN_HEADS, HEAD_DIM, HEAD_DIM), new_sc_s)


def kernel(x_prompt, x_sample, state_gdn_conv, state_gdn, state_sc_conv, p_prompt, p_sample, norm_mix, w_in, w_conv_qkv, a_log, dt_bias, w_gdn_norm, w_conv_sc, w_out, norm_mlp, w_up, w_down, norm_ple, w_ple_gate, w_ple_proj, norm_f):
    depth = w_in.shape[0]
    assert depth == 1, "one layer per call"
    assert x_sample.shape[1] >= GDN_CONV - 1 and CHUNK % x_sample.shape[1] == 0
    assert x_prompt.shape[1] % PROMPT_ROWS == 0 and x_prompt.shape[0] % PROMPT_SEQS_PER_STEP == 0
    outs = _layer(x_prompt, x_sample, state_gdn_conv[0], state_gdn[0], state_sc_conv[0], p_prompt[0], p_sample[0],
                  norm_mix[0], w_in[0], w_conv_qkv[0], a_log[0], dt_bias[0], w_gdn_norm[0], w_conv_sc[0], w_out[0],
                  norm_mlp[0], w_up[0], w_down[0], norm_ple[0], w_ple_gate[0], w_ple_proj[0], norm_f)
    y_p, y_s, c_p, s_p, sc_p, c_s, s_s, sc_s = outs
    return (y_p, y_s, c_p[None], s_p[None], sc_p[None], c_s[None], s_s[None], sc_s[None])
```

```python
import functools

import jax
import jax.numpy as jnp
from jax import lax
from jax.experimental import pallas as pl
from jax.experimental.pallas import tpu as pltpu

F32 = jnp.float32
BF16 = jnp.bfloat16

D_MODEL = 1024
PLE_DIM = 256
GDN_HEADS = 4
HEAD_DIM = 128
GDN_WIDTH = GDN_HEADS * HEAD_DIM
QKV_DIM = 3 * GDN_WIDTH
GDN_CONV = 4
SC_WIDTH = D_MODEL - GDN_WIDTH
SC_CONV = 3
D_FF = 4 * D_MODEL
EPS = 1e-6
NEG_LOG2_E = -1.4426950408889634
CHUNK = 64
LANES = 128
SUBLANES = 8
VMEM_LIMIT_BYTES = 56 * 1024 * 1024
PROMPT_ROWS = 128
PROMPT_SEQS_PER_STEP = 4
FFN_ROWS = 1024
SPLIT_ROWS = 256

R_Z = 0
R_SCB = R_Z + GDN_WIDTH
R_SCC = R_SCB + SC_WIDTH
R_SCH = R_SCC + SC_WIDTH


def _mm(a, b):
    return jnp.dot(a.astype(BF16), b.astype(BF16), preferred_element_type=F32)


def _mm_nt(a, b):
    return lax.dot_general(a.astype(BF16), b.astype(BF16), (((1,), (1,)), ((), ())), preferred_element_type=F32)


def _mm_tn(a, b):
    return lax.dot_general(a.astype(BF16), b.astype(BF16), (((0,), (0,)), ((), ())), preferred_element_type=F32)


def _rmsnorm(x, w_row):
    return x * lax.rsqrt(jnp.mean(x * x, axis=-1, keepdims=True) + EPS) * w_row


def _sigmoid(x):
    return 1.0 / (1.0 + jnp.exp2(x * NEG_LOG2_E))


def _silu(x):
    return x * _sigmoid(x)


def _softplus(x):
    return jnp.maximum(x, 0.0) + jnp.log1p(jnp.exp(-jnp.abs(x)))


def _log2(n):
    k = n.bit_length() - 1
    assert (1 << k) == n, n
    return k


def _wy_block(q, k, v, g_col, g_row, beta_col, seg_len, load_state, store_state):
    c = CHUNK
    n_seg = c // seg_len
    lg = _log2(seg_len)
    ri = lax.broadcasted_iota(jnp.int32, (c, c), 0)
    ci = lax.broadcasted_iota(jnp.int32, (c, c), 1)
    same = jnp.right_shift(ri, lg) == jnp.right_shift(ci, lg)
    lower = same & (ri >= ci)
    strict = same & (ri > ci)
    diff = g_col - g_row
    decay = jnp.where(lower, jnp.exp(jnp.where(lower, diff, 0.0)), 0.0)
    a_off = jnp.where(strict, beta_col * _mm_nt(k, k) * decay, 0.0)
    qk = _mm_nt(q, k) * decay
    yield
    eye = jnp.where(ri == ci, 1.0, 0.0).astype(F32)
    x_inv = eye
    for lb in range(lg):
        rb = jnp.right_shift(ri, lb)
        cb = jnp.right_shift(ci, lb)
        join = (jnp.bitwise_and(rb, 1) == 1) & (cb == rb - 1)
        b_lvl = jnp.where(join, a_off, 0.0)
        if lb == 0:
            x_inv = eye - b_lvl
        else:
            xb = _mm(x_inv, b_lvl)
            yield
            x_inv = x_inv - _mm(xb, x_inv)
            yield
    rhs = jnp.concatenate([v * beta_col, k * (beta_col * jnp.exp(g_col))], axis=1)
    sol = _mm(x_inv, rhs)
    yield
    u = sol[:, :HEAD_DIM]
    w = sol[:, HEAD_DIM:]
    stacked = jnp.concatenate([w, q * jnp.exp(g_col)], axis=0)
    row = lax.broadcasted_iota(jnp.int32, (c, 1), 0)
    states = [load_state(s) for s in range(n_seg)]
    w_s = None
    q_s = None
    for s in range(n_seg):
        r = _mm(stacked, states[s])
        if n_seg == 1:
            w_s, q_s = r[:c], r[c:]
        else:
            in_seg = jnp.right_shift(row, lg) == s
            w_s = jnp.where(in_seg, r[:c], 0.0 if w_s is None else w_s)
            q_s = jnp.where(in_seg, r[c:], 0.0 if q_s is None else q_s)
    v_new = u - w_s
    yield
    o = q_s + _mm(qk, v_new)
    for s in range(n_seg):
        last = s * seg_len + seg_len - 1
        g_last = g_col[last:last + 1, :]
        if n_seg == 1:
            k_dec = k * jnp.exp(g_last - g_col)
        else:
            in_seg = jnp.right_shift(row, lg) == s
            k_dec = jnp.where(in_seg, k * jnp.exp(jnp.where(in_seg, g_last - g_col, 0.0)), 0.0)
        store_state(s, states[s] * jnp.exp(g_last) + _mm_tn(k_dec, v_new))
    return o


def _expand_heads(slab, first_lane, width):
    rows = slab.shape[0]
    cols = [slab[:, first_lane + hd:first_lane + hd + 1] for hd in range(GDN_HEADS)]
    if width % LANES == 0:
        return jnp.concatenate([jnp.broadcast_to(c, (rows, width)) for c in cols], axis=1)
    total = GDN_HEADS * width
    lane_head = jnp.right_shift(lax.broadcasted_iota(jnp.int32, (rows, total), 1), _log2(width))
    out = jnp.broadcast_to(cols[-1], (rows, total))
    for hd in range(GDN_HEADS - 2, -1, -1):
        out = jnp.where(lane_head == hd, jnp.broadcast_to(cols[hd], (rows, total)), out)
    return out


def _packed_masks():
    c, nh = CHUNK, GDN_HEADS
    ri = lax.broadcasted_iota(jnp.int32, (c, nh * c), 0)
    cj = jnp.bitwise_and(lax.broadcasted_iota(jnp.int32, (c, nh * c), 1), c - 1)
    return dict(lower=ri >= cj, strict=ri > cj, eye=ri == cj, join0=(jnp.bitwise_and(ri, 1) == 1) & (cj == ri - 1))


def _fill_block_factors(bd_ref, kbd_ref):
    c, nh = CHUNK, GDN_HEADS
    w = nh * c
    lg = _log2(c)
    rw = lax.broadcasted_iota(jnp.int32, (w, w), 0)
    cw = lax.broadcasted_iota(jnp.int32, (w, w), 1)
    same_head = jnp.right_shift(rw, lg) == jnp.right_shift(cw, lg)
    rw, cw = jnp.bitwise_and(rw, c - 1), jnp.bitwise_and(cw, c - 1)
    one_zero = lambda m: jnp.where(m, 1.0, 0.0).astype(BF16)
    bd_ref[0] = one_zero(same_head)
    for lb in range(1, lg):
        bd_ref[lb] = one_zero(same_head & (jnp.bitwise_and(jnp.right_shift(rw, lb), 1) == 1)
                              & (jnp.right_shift(cw, lb) == jnp.right_shift(rw, lb) - 1))
    kbd_ref[...] = one_zero(jnp.right_shift(lax.broadcasted_iota(jnp.int32, (w, nh * HEAD_DIM), 0), lg)
                            == jnp.right_shift(lax.broadcasted_iota(jnp.int32, (w, nh * HEAD_DIM), 1),
                                               _log2(HEAD_DIM)))


def _wy_tile_packed(q_all, k_all, v_all, g, beta, state_ref, rows, masks, bd_ref, kbd_ref):
    c, nh = CHUNK, GDN_HEADS
    lg = _log2(c)
    n_chunks = rows // c
    lower, strict, eye = masks["lower"], masks["strict"], masks["eye"]

    def block_diag(x, factor):
        return jnp.concatenate([x.astype(BF16)] * nh, axis=0) * factor

    def head(x, hd, width):
        return x[:, hd * width:(hd + 1) * width]

    g_c = _expand_heads(g, 0, c)
    b_c = _expand_heads(beta, nh, c)
    g_d = _expand_heads(g, 0, HEAD_DIM)
    b_d = _expand_heads(beta, nh, HEAD_DIM)
    eg_d = jnp.exp(g_d)
    v_rhs = v_all * b_d
    k_rhs = k_all * (b_d * eg_d)
    q_g = q_all * eg_d

    a_off, qk_dec = [], []
    for cb in range(n_chunks):
        rs = slice(cb * c, (cb + 1) * c)
        g_row = jnp.sum(jnp.where(eye, g_c[rs], 0.0), axis=0, keepdims=True)
        decay = jnp.where(lower, jnp.exp(jnp.where(lower, g_c[rs] - g_row, 0.0)), 0.0)
        kq = _mm_nt(jnp.concatenate([k_all[rs], q_all[rs]], axis=0), block_diag(k_all[rs], kbd_ref[...]))
        a_off.append(jnp.where(strict, b_c[rs] * kq[:c] * decay, 0.0))
        qk_dec.append(kq[c:] * decay)
    yield

    eye_f = jnp.where(eye, 1.0, 0.0).astype(F32)
    x_inv = [eye_f - jnp.where(masks["join0"], a, 0.0) for a in a_off]
    for lb in range(1, lg):
        xb = [jnp.dot(x.astype(BF16), block_diag(a, bd_ref[lb]), preferred_element_type=F32)
              for x, a in zip(x_inv, a_off)]
        yield
        xbx = [jnp.dot(t.astype(BF16), block_diag(x, bd_ref[0]), preferred_element_type=F32)
               for t, x in zip(xb, x_inv)]
        x_inv = [x - t for x, t in zip(x_inv, xbx)]
        yield

    u, wk = [], []
    for cb in range(n_chunks):
        rs = slice(cb * c, (cb + 1) * c)
        sol = [_mm(head(x_inv[cb], hd, c),
                   jnp.concatenate([head(v_rhs[rs], hd, HEAD_DIM), head(k_rhs[rs], hd, HEAD_DIM)], axis=1))
               for hd in range(nh)]
        u.append([s[:, :HEAD_DIM] for s in sol])
        wk.append([s[:, HEAD_DIM:] for s in sol])
    yield

    state = [state_ref[0, hd] for hd in range(nh)]
    o_blocks = []
    for cb in range(n_chunks):
        rs = slice(cb * c, (cb + 1) * c)
        g_last = g_d[(cb + 1) * c - 1:(cb + 1) * c, :]
        k_dec = k_all[rs] * jnp.exp(g_last - g_d[rs])
        eg_last = jnp.exp(g_last)
        r = [_mm(jnp.concatenate([wk[cb][hd], head(q_g[rs], hd, HEAD_DIM)], axis=0), state[hd]) for hd in range(nh)]
        v_new = [u[cb][hd] - r[hd][:c] for hd in range(nh)]
        yield
        o_blocks.append(jnp.concatenate(
            [r[hd][c:] + _mm(head(qk_dec[cb], hd, c), v_new[hd]) for hd in range(nh)], axis=1))
        state = [state[hd] * head(eg_last, hd, HEAD_DIM) + _mm_tn(head(k_dec, hd, HEAD_DIM), v_new[hd])
                 for hd in range(nh)]
        yield
    for hd in range(nh):
        state_ref[0, hd] = state[hd]
    return jnp.concatenate(o_blocks, axis=0)


def _run_tiles(programs):
    programs = list(programs)
    results = [None] * len(programs)
    live = list(range(len(programs)))
    while live:
        for idx in list(live):
            try:
                next(programs[idx])
            except StopIteration as done:
                results[idx] = done.value
                live.remove(idx)
    return results


def _shifted_history(buf, rows, n_taps, hist_buf, seg_len):
    out = []
    with_carry = buf[0:SUBLANES + rows, :]
    for s in range(1, n_taps):
        raw = pltpu.roll(with_carry, s, 0)[SUBLANES:, :]
        if hist_buf is not None:
            t = jnp.bitwise_and(lax.broadcasted_iota(jnp.int32, (rows, 1), 0), seg_len - 1)
            raw = jnp.where(t >= s, raw, hist_buf[pl.ds(n_taps - 1 - s, rows), :])
        out.append(raw)
    return out


def _mixer_tile(ti, rows, seg_len, long_seq, masks, r, ab, rest):
    n_chunks = rows // CHUNK
    new_rows = slice(SUBLANES, SUBLANES + rows)
    qb, sb = r.qbuf.at[ti], r.sbuf.at[ti]

    gate = r.gate[...]
    log_a = -jnp.exp(gate[0:1, :]) * _softplus(ab + gate[1:2, :])
    beta = _sigmoid(ab)
    pos = jnp.bitwise_and(lax.broadcasted_iota(jnp.int32, (rows, 1), 0), seg_len - 1)
    g = log_a
    shift = 1
    while shift < seg_len:
        g = g + jnp.where(pos >= shift, pltpu.roll(g, shift, 0), 0.0)
        shift *= 2

    sc_pre = rest[:, R_SCC:R_SCC + SC_WIDTH] * rest[:, R_SCH:R_SCH + SC_WIDTH]
    sb[new_rows, :] = sc_pre
    if long_seq:
        hq_buf = hs_buf = None
    else:
        hq_buf, hs_buf = r.hq_buf.at[ti], r.hs_buf.at[ti]
        zq = jnp.zeros((SUBLANES, QKV_DIM), F32)
        zs = jnp.zeros((SUBLANES, SC_WIDTH), F32)
        qb[0:SUBLANES, :] = zq
        sb[0:SUBLANES, :] = zs
        hq_buf[0:rows, :] = r.hq[ti]
        hq_buf[rows:rows + SUBLANES, :] = zq
        hs_buf[0:rows, :] = r.hs[ti]
        hs_buf[rows:rows + SUBLANES, :] = zs
    qkv_pre = qb[new_rows, :]
    cwq = r.cwq[...]
    if long_seq:
        with_carry = qb[0:SUBLANES + rows, :]
        back1 = pltpu.roll(with_carry, 1, 0)
        pair = with_carry * cwq[1:2, :] + back1 * cwq[0:1, :]
        qkv = _silu(pltpu.roll(pair, 2, 0)[SUBLANES:, :] + back1[SUBLANES:, :] * cwq[2:3, :]
                    + qkv_pre * cwq[3:4, :])
    else:
        q1, q2, q3 = _shifted_history(qb, rows, GDN_CONV, hq_buf, seg_len)
        qkv = _silu(q3 * cwq[0:1, :] + q2 * cwq[1:2, :] + q1 * cwq[2:3, :] + qkv_pre * cwq[3:4, :])
    if long_seq:
        r.qtail[ti] = qb[rows:rows + SUBLANES, :]
        r.stail[ti] = sb[rows:rows + SUBLANES, :]
    else:
        r.stail[ti] = sc_pre
    q_n, k_n, v_n = [], [], []
    for hd in range(GDN_HEADS):
        lo = hd * HEAD_DIM
        q_h = qkv[:, lo:lo + HEAD_DIM]
        k_h = qkv[:, GDN_WIDTH + lo:GDN_WIDTH + lo + HEAD_DIM]
        q_n.append(q_h * (lax.rsqrt(jnp.sum(q_h * q_h, axis=-1, keepdims=True) + EPS) * (HEAD_DIM ** -0.5)))
        k_n.append(k_h * lax.rsqrt(jnp.sum(k_h * k_h, axis=-1, keepdims=True) + EPS))
        v_n.append(qkv[:, 2 * GDN_WIDTH + lo:2 * GDN_WIDTH + lo + HEAD_DIM])
    yield

    state_ref = r.s_out.at[ti]
    if long_seq:
        o_all = yield from _wy_tile_packed(jnp.concatenate(q_n, axis=1), jnp.concatenate(k_n, axis=1),
                                           jnp.concatenate(v_n, axis=1), g, beta, state_ref, rows, masks,
                                           r.bd, r.kbd)
        o_h = [o_all[:, hd * HEAD_DIM:(hd + 1) * HEAD_DIM] for hd in range(GDN_HEADS)]
    else:
        g_t = g.T
        n_seg = CHUNK // seg_len
        blocks = []
        for cb in range(n_chunks):
            r0 = cb * CHUNK
            for hd in range(GDN_HEADS):

                def load_state(s, hd=hd, base=cb * n_seg):
                    return state_ref[base + s, hd]

                def store_state(s, val, hd=hd, base=cb * n_seg):
                    state_ref[base + s, hd] = val

                blocks.append(_wy_block(
                    q_n[hd][r0:r0 + CHUNK], k_n[hd][r0:r0 + CHUNK], v_n[hd][r0:r0 + CHUNK],
                    g[r0:r0 + CHUNK, hd:hd + 1], g_t[hd:hd + 1, r0:r0 + CHUNK],
                    beta[r0:r0 + CHUNK, GDN_HEADS + hd:GDN_HEADS + hd + 1],
                    seg_len, load_state, store_state))
        o_blocks = _run_tiles(blocks)
        o_h = [jnp.concatenate(o_blocks[hd::GDN_HEADS], axis=0) for hd in range(GDN_HEADS)]
    yield

    s1, s2 = _shifted_history(sb, rows, SC_CONV, hs_buf, seg_len)
    cws = r.cws[...]
    y_sc = rest[:, R_SCB:R_SCB + SC_WIDTH] * (s2 * cws[0:1, :] + s1 * cws[1:2, :] + sc_pre * cws[2:3, :])
    if long_seq:
        qb[0:SUBLANES, :] = qb[rows:rows + SUBLANES, :]
        sb[0:SUBLANES, :] = sb[rows:rows + SUBLANES, :]
    gnorm = r.gnorm[...]
    o_heads = []
    for hd in range(GDN_HEADS):
        z_h = rest[:, R_Z + hd * HEAD_DIM:R_Z + (hd + 1) * HEAD_DIM]
        o_heads.append(_rmsnorm(o_h[hd], gnorm) * _silu(z_h))
    return jnp.concatenate(o_heads + [y_sc], axis=1)


class _Refs:
    def __init__(self, **refs):
        self.__dict__.update(refs)


def _mixer_seq_kernel(n_tiles, rows,
                      x, hq, hs, s_in, nmix, w_ab, w_qkv, w_rest, cwq, gate, gnorm, cws, wout,
                      xmid, qtail, stail, s_out, qbuf, sbuf, bd, kbd):
    @pl.when(pl.program_id(1) == 0)
    def _():
        s_out[...] = s_in[...]
        qbuf[:, 0:SUBLANES, :] = hq[...]
        sbuf[:, 0:SUBLANES, :] = hs[...]
        _fill_block_factors(bd, kbd)

    x_all = x[...].reshape(n_tiles * rows, D_MODEL)
    h = _rmsnorm(x_all, nmix[...]).astype(BF16)
    ab = jnp.dot(h, w_ab[...], preferred_element_type=F32)
    qbuf[:, SUBLANES:SUBLANES + rows, :] = jnp.dot(h, w_qkv[...], preferred_element_type=F32).reshape(
        n_tiles, rows, QKV_DIM)
    rest = jnp.dot(h, w_rest[...], preferred_element_type=F32)
    r = _Refs(cwq=cwq, gate=gate, gnorm=gnorm, cws=cws, qtail=qtail, stail=stail, s_out=s_out, qbuf=qbuf, sbuf=sbuf,
              bd=bd, kbd=kbd)
    masks = _packed_masks()
    tile = lambda a, ti: a[ti * rows:(ti + 1) * rows]
    mix = _run_tiles(_mixer_tile(ti, rows, CHUNK, True, masks, r, tile(ab, ti), tile(rest, ti))
                     for ti in range(n_tiles))
    xmid[...] = (x_all + _mm(jnp.concatenate(mix, axis=0), wout[...])).reshape(n_tiles, rows, D_MODEL)


def _mixer_step_kernel(rows, seg_len,
                       ab, qkv, rest, hq, hs, s_in, cwq, gate, gnorm, cws,
                       mix, stail, s_out, qbuf, sbuf, hq_buf, hs_buf):
    s_out[...] = s_in[...]
    qbuf[0, SUBLANES:SUBLANES + rows, :] = qkv[0]
    r = _Refs(hq=hq, hs=hs, cwq=cwq, gate=gate, gnorm=gnorm, cws=cws, stail=stail, s_out=s_out, qbuf=qbuf, sbuf=sbuf,
              hq_buf=hq_buf, hs_buf=hs_buf)
    mix[0] = _run_tiles([_mixer_tile(0, rows, seg_len, False, None, r, ab[0], rest[0])])[0].astype(BF16)


def _proj_kernel(x_ref, nmix_ref, w_ab_ref, w_qkv_ref, w_rest_ref, ab_ref, qkv_ref, rest_ref):
    h = _rmsnorm(x_ref[...], nmix_ref[...]).astype(BF16)
    ab_ref[...] = jnp.dot(h, w_ab_ref[...], preferred_element_type=F32)
    qkv_ref[...] = jnp.dot(h, w_qkv_ref[...], preferred_element_type=F32)
    rest_ref[...] = jnp.dot(h, w_rest_ref[...], preferred_element_type=F32)


def _ffn_body(x, p_ref, nmlp_ref, wup_ref, wdown_ref, nple_ref, wg_ref, wp_ref, nf_ref, y_ref):
    hn = _rmsnorm(x, nmlp_ref[...]).astype(BF16)
    acc = x
    for j in range(D_FF // D_MODEL):
        u = jnp.maximum(_mm(hn, wup_ref[:, j * D_MODEL:(j + 1) * D_MODEL]), 0.0)
        acc = acc + _mm(u * u, wdown_ref[j * D_MODEL:(j + 1) * D_MODEL, :])
    gate = _sigmoid(_mm(_rmsnorm(acc, nple_ref[...]), wg_ref[...]))
    x3 = acc + gate * _mm(p_ref[...], wp_ref[...])
    y_ref[...] = _rmsnorm(x3, nf_ref[...])


def _ffn_kernel(x_ref, *rest):
    _ffn_body(x_ref[...], *rest)


def _out_ffn_kernel(x_ref, mix_ref, wout_ref, *rest):
    _ffn_body(x_ref[...] + jnp.dot(mix_ref[...], wout_ref[...], preferred_element_type=F32), *rest)


def _const_spec(shape):
    nd = len(shape)
    return pl.BlockSpec(shape, lambda *_: (0,) * nd, pipeline_mode=pl.Buffered(1))


def _state_spec(n_tiles, n_state):
    return pl.BlockSpec((n_tiles, n_state, GDN_HEADS, HEAD_DIM, HEAD_DIM), lambda b, i: (b, 0, 0, 0, 0))


def _mixer_seq_call(x3, hist_q, hist_s, s_in, weights, *, n_tiles, rows):
    seqs, seq_rows, _ = x3.shape
    row_map = lambda b, i: (b, i, 0)
    seq_map = lambda b, i: (b, 0, 0)
    x_spec = pl.BlockSpec((n_tiles, rows, D_MODEL), row_map)
    hq_spec = pl.BlockSpec((n_tiles, SUBLANES, QKV_DIM), seq_map)
    hs_spec = pl.BlockSpec((n_tiles, SUBLANES, SC_WIDTH), seq_map)
    packed_w = GDN_HEADS * CHUNK
    return pl.pallas_call(
        functools.partial(_mixer_seq_kernel, n_tiles, rows),
        grid=(seqs // n_tiles, seq_rows // rows),
        in_specs=[x_spec, hq_spec, hs_spec, _state_spec(n_tiles, 1)] + [_const_spec(w.shape) for w in weights],
        out_specs=[x_spec, hq_spec, hs_spec, _state_spec(n_tiles, 1)],
        out_shape=[jax.ShapeDtypeStruct(x3.shape, F32), jax.ShapeDtypeStruct(hist_q.shape, F32),
                   jax.ShapeDtypeStruct(hist_s.shape, F32), jax.ShapeDtypeStruct(s_in.shape, F32)],
        scratch_shapes=[pltpu.VMEM((n_tiles, rows + SUBLANES, QKV_DIM), F32),
                        pltpu.VMEM((n_tiles, rows + SUBLANES, SC_WIDTH), F32),
                        pltpu.VMEM((_log2(CHUNK), packed_w, packed_w), BF16),
                        pltpu.VMEM((packed_w, GDN_WIDTH), BF16)],
        compiler_params=pltpu.CompilerParams(
            dimension_semantics=("arbitrary", "arbitrary"), vmem_limit_bytes=VMEM_LIMIT_BYTES),
        name="mixer_seq",
    )(x3, hist_q, hist_s, s_in, *weights)


def _split_in_proj_kernel(w_ref, w_ab_ref, w_qkv_ref, w_rest_ref):
    o_ab = QKV_DIM
    n_ab = 2 * GDN_HEADS
    w_qkv_ref[...] = w_ref[:, 0:QKV_DIM].astype(BF16)
    tail = w_ref[:, o_ab:]
    lane = lax.broadcasted_iota(jnp.int32, (tail.shape[0], LANES), 1)
    w_ab_ref[...] = jnp.where(lane < n_ab, tail[:, 0:LANES], 0.0).astype(BF16)
    w_rest_ref[...] = tail[:, n_ab:].astype(BF16)


def _split_in_proj_call(w_in):
    rows, cols = w_in.shape
    tile = SPLIT_ROWS
    n_rest = cols - QKV_DIM - 2 * GDN_HEADS
    row_map = lambda i: (i, 0)
    outs = [jax.ShapeDtypeStruct((rows, n), BF16) for n in (LANES, QKV_DIM, n_rest)]
    return pl.pallas_call(
        _split_in_proj_kernel,
        grid=(rows // tile,),
        in_specs=[pl.BlockSpec((tile, cols), row_map)],
        out_specs=[pl.BlockSpec((tile, o.shape[1]), row_map) for o in outs],
        out_shape=outs,
        compiler_params=pltpu.CompilerParams(dimension_semantics=("arbitrary",), vmem_limit_bytes=VMEM_LIMIT_BYTES),
        name="split_in_proj",
    )(w_in)


def _proj_call(x2d, weights):
    n = x2d.shape[0]
    outs = [jax.ShapeDtypeStruct((n, w.shape[1]), F32) for w in weights[1:]]
    full = lambda a: pl.BlockSpec(a.shape, lambda i: (0, 0))
    return pl.pallas_call(
        _proj_kernel,
        grid=(1,),
        in_specs=[full(x2d)] + [full(w) for w in weights],
        out_specs=[full(o) for o in outs],
        out_shape=outs,
        compiler_params=pltpu.CompilerParams(dimension_semantics=("arbitrary",), vmem_limit_bytes=VMEM_LIMIT_BYTES),
        name="proj_step",
    )(x2d, *weights)


def _mixer_step_call(ab, qkv, rest, hist_q, hist_s, s_in, weights, *, seg_len):
    tiles, rows, _ = qkv.shape
    tile_map = lambda b, i: (b, 0, 0)
    slab = lambda a: pl.BlockSpec((1, rows, a.shape[2]), tile_map)
    n_state = s_in.shape[1]
    mix = jax.ShapeDtypeStruct((tiles, rows, D_MODEL), BF16)
    stail = jax.ShapeDtypeStruct((tiles, rows, SC_WIDTH), F32)
    return pl.pallas_call(
        functools.partial(_mixer_step_kernel, rows, seg_len),
        grid=(tiles, 1),
        in_specs=[slab(ab), slab(qkv), slab(rest), slab(hist_q), slab(hist_s), _state_spec(1, n_state)]
        + [_const_spec(w.shape) for w in weights],
        out_specs=[slab(mix), slab(stail), _state_spec(1, n_state)],
        out_shape=[mix, stail, jax.ShapeDtypeStruct(s_in.shape, F32)],
        scratch_shapes=[pltpu.VMEM((1, rows + SUBLANES, QKV_DIM), F32), pltpu.VMEM((1, rows + SUBLANES, SC_WIDTH), F32),
                        pltpu.VMEM((1, rows + SUBLANES, QKV_DIM), F32), pltpu.VMEM((1, rows + SUBLANES, SC_WIDTH), F32)],
        compiler_params=pltpu.CompilerParams(
            dimension_semantics=("arbitrary", "arbitrary"), vmem_limit_bytes=VMEM_LIMIT_BYTES),
        name="mixer_step",
    )(ab, qkv, rest, hist_q, hist_s, s_in, *weights)


def _ffn_call(x2d, p2d, weights, *, rows, name, mix=None, wout=None):
    n = x2d.shape[0]
    row_map = lambda i: (i, 0)
    rows_spec = lambda width: pl.BlockSpec((rows, width), row_map)
    pre_args, pre_specs, body = [], [], _ffn_kernel
    if mix is not None:
        pre_args, pre_specs, body = [mix, wout], [rows_spec(D_MODEL), _const_spec(wout.shape)], _out_ffn_kernel
    return pl.pallas_call(
        body,
        grid=(n // rows,),
        in_specs=[rows_spec(D_MODEL)] + pre_specs + [rows_spec(PLE_DIM)] + [_const_spec(w.shape) for w in weights],
        out_specs=rows_spec(D_MODEL),
        out_shape=jax.ShapeDtypeStruct(x2d.shape, F32),
        compiler_params=pltpu.CompilerParams(dimension_semantics=("arbitrary",), vmem_limit_bytes=VMEM_LIMIT_BYTES),
        name=name,
    )(x2d, *pre_args, p2d, *weights)


def _pad_rows(a, rows):
    return jnp.pad(a, ((0, 0), (0, rows - a.shape[1]), (0, 0)))


def _layer(x_prompt, x_sample, conv_qkv, s_gdn, conv_sc, p_prompt, p_sample, norm_mix, w_in, w_conv_qkv, a_log,
           dt_bias, w_gdn_norm, w_conv_sc, w_out, norm_mlp, w_up, w_down, norm_ple, w_ple_gate, w_ple_proj, norm_f):
    bp, tp, _ = x_prompt.shape
    bs, ts, _ = x_sample.shape
    w_ab, w_qkv, w_rest = _split_in_proj_call(w_in)
    wout = w_out.astype(BF16)
    gate = jnp.zeros((SUBLANES, LANES), F32)
    gate = gate.at[0, :GDN_HEADS].set(a_log.astype(F32)).at[1, :GDN_HEADS].set(dt_bias.astype(F32))
    nmix = norm_mix.reshape(1, D_MODEL)
    core_w = (w_conv_qkv, gate, w_gdn_norm.reshape(1, HEAD_DIM), w_conv_sc)
    ffn_w = (norm_mlp.reshape(1, D_MODEL), w_up.astype(BF16), w_down.astype(BF16), norm_ple.reshape(1, D_MODEL),
             w_ple_gate.astype(BF16), w_ple_proj.astype(BF16), norm_f.reshape(1, D_MODEL))

    zq = jnp.zeros((bp, SUBLANES, QKV_DIM), F32)
    zs = jnp.zeros((bp, SUBLANES, SC_WIDTH), F32)
    s0 = jnp.zeros((bp, 1, GDN_HEADS, HEAD_DIM, HEAD_DIM), F32)
    xm_p, qt_p, st_p, s_p = _mixer_seq_call(
        x_prompt, zq, zs, s0, (nmix, w_ab, w_qkv, w_rest) + core_w + (wout,),
        n_tiles=PROMPT_SEQS_PER_STEP, rows=PROMPT_ROWS)
    y_p = _ffn_call(xm_p.reshape(bp * tp, D_MODEL), p_prompt.reshape(bp * tp, PLE_DIM), ffn_w, rows=FFN_ROWS,
                    name="ffn_prompt")

    seq_per_tile = CHUNK // ts
    tiles = bs // seq_per_tile
    x_s = x_sample.reshape(bs * ts, D_MODEL)
    ab_s, qkv_s, rest_s = _proj_call(x_s, (nmix, w_ab, w_qkv, w_rest))
    tiled = lambda a: a.reshape(tiles, CHUNK, a.shape[-1])
    mix_s, st_s, s_s = _mixer_step_call(
        tiled(ab_s), tiled(qkv_s), tiled(rest_s), tiled(_pad_rows(conv_qkv, ts)), tiled(_pad_rows(conv_sc, ts)),
        s_gdn.reshape(tiles, seq_per_tile, GDN_HEADS, HEAD_DIM, HEAD_DIM), core_w, seg_len=ts)
    y_s = _ffn_call(x_s, p_sample.reshape(bs * ts, PLE_DIM), ffn_w, rows=bs * ts, name="ffn_sample",
                    mix=mix_s.reshape(bs * ts, D_MODEL), wout=wout)

    new_conv_p = qt_p[:, SUBLANES - (GDN_CONV - 1):]
    new_sc_p = st_p[:, SUBLANES - (SC_CONV - 1):]
    new_conv_s = qkv_s.reshape(bs, ts, QKV_DIM)[:, ts - (GDN_CONV - 1):]
    new_sc_s = st_s.reshape(bs, ts, SC_WIDTH)[:, ts - (SC_CONV - 1):]
    return (y_p.reshape(bp, tp, D_MODEL), y_s.reshape(bs, ts, D_MODEL), new_conv_p,
            s_p.reshape(bp, GDN_HEADS, HEAD_DIM, HEAD_DIM), new_sc_p, new_conv_s,
            s_s.reshape(bs, GDN_HEADS, HEAD_DIM, HEAD_DIM), new_sc_s)


def kernel(x_prompt, x_sample, state_gdn_conv, state_gdn, state_sc_conv, p_prompt, p_sample, norm_mix, w_in, w_conv_qkv, a_log, dt_bias, w_gdn_norm, w_conv_sc, w_out, norm_mlp, w_up, w_down, norm_ple, w_ple_gate, w_ple_proj, norm_f):
    depth = w_in.shape[0]
    assert depth == 1, "one layer per call"
    assert x_sample.shape[1] >= GDN_CONV - 1 and CHUNK % x_sample.shape[1] == 0
    assert x_prompt.shape[1] % PROMPT_ROWS == 0 and x_prompt.shape[0] % PROMPT_SEQS_PER_STEP == 0
    outs = _layer(x_prompt, x_sample, state_gdn_conv[0], state_gdn[0], state_sc_conv[0], p_prompt[0], p_sample[0],
                  norm_mix[0], w_in[0], w_conv_qkv[0], a_log[0], dt_bias[0], w_gdn_norm[0], w_conv_sc[0], w_out[0],
                  norm_mlp[0], w_up[0], w_down[0], norm_ple[0], w_ple_gate[0], w_ple_proj[0], norm_f)
    y_p, y_s, c_p, s_p, sc_p, c_s, s_s, sc_s = outs
    return (y_p, y_s, c_p[None], s_p[None], sc_p[None], c_s[None], s_s[None], sc_s[None])
```

```python
import functools

import jax
import jax.numpy as jnp
from jax import lax
from jax.experimental import pallas as pl
from jax.experimental.pallas import tpu as pltpu

F32 = jnp.float32
BF16 = jnp.bfloat16

D_MODEL = 1024
PLE_DIM = 256
GDN_HEADS = 4
HEAD_DIM = 128
GDN_WIDTH = GDN_HEADS * HEAD_DIM
QKV_DIM = 3 * GDN_WIDTH
GDN_CONV = 4
SC_WIDTH = D_MODEL - GDN_WIDTH
SC_CONV = 3
D_FF = 4 * D_MODEL
EPS = 1e-6
NEG_LOG2_E = -1.4426950408889634
CHUNK = 64
LANES = 128
SUBLANES = 8
VMEM_LIMIT_BYTES = 56 * 1024 * 1024
PROMPT_ROWS = 128
PROMPT_SEQS_PER_STEP = 4
FFN_ROWS = 1024
SPLIT_COLS = 256

R_Z = 0
R_SCB = R_Z + GDN_WIDTH
R_SCC = R_SCB + SC_WIDTH
R_SCH = R_SCC + SC_WIDTH


def _mm(a, b):
    return jnp.dot(a.astype(BF16), b.astype(BF16), preferred_element_type=F32)


def _mm_nt(a, b):
    return lax.dot_general(a.astype(BF16), b.astype(BF16), (((1,), (1,)), ((), ())), preferred_element_type=F32)


def _mm_tn(a, b):
    return lax.dot_general(a.astype(BF16), b.astype(BF16), (((0,), (0,)), ((), ())), preferred_element_type=F32)


def _rmsnorm(x, w_row):
    return x * lax.rsqrt(jnp.mean(x * x, axis=-1, keepdims=True) + EPS) * w_row


def _sigmoid(x):
    return 1.0 / (1.0 + jnp.exp2(x * NEG_LOG2_E))


def _silu(x):
    return x * _sigmoid(x)


def _softplus(x):
    return jnp.maximum(x, 0.0) + jnp.log1p(jnp.exp(-jnp.abs(x)))


def _log2(n):
    k = n.bit_length() - 1
    assert (1 << k) == n, n
    return k


def _wy_block(q, k, v, g_col, g_row, beta_col, seg_len, load_state, store_state):
    c = CHUNK
    n_seg = c // seg_len
    lg = _log2(seg_len)
    ri = lax.broadcasted_iota(jnp.int32, (c, c), 0)
    ci = lax.broadcasted_iota(jnp.int32, (c, c), 1)
    same = jnp.right_shift(ri, lg) == jnp.right_shift(ci, lg)
    lower = same & (ri >= ci)
    strict = same & (ri > ci)
    diff = g_col - g_row
    decay = jnp.where(lower, jnp.exp(jnp.where(lower, diff, 0.0)), 0.0)
    a_off = jnp.where(strict, beta_col * _mm_nt(k, k) * decay, 0.0)
    qk = _mm_nt(q, k) * decay
    yield
    eye = jnp.where(ri == ci, 1.0, 0.0).astype(F32)
    x_inv = eye
    for lb in range(lg):
        rb = jnp.right_shift(ri, lb)
        cb = jnp.right_shift(ci, lb)
        join = (jnp.bitwise_and(rb, 1) == 1) & (cb == rb - 1)
        b_lvl = jnp.where(join, a_off, 0.0)
        if lb == 0:
            x_inv = eye - b_lvl
        else:
            xb = _mm(x_inv, b_lvl)
            yield
            x_inv = x_inv - _mm(xb, x_inv)
            yield
    rhs = jnp.concatenate([v * beta_col, k * (beta_col * jnp.exp(g_col))], axis=1)
    sol = _mm(x_inv, rhs)
    yield
    u = sol[:, :HEAD_DIM]
    w = sol[:, HEAD_DIM:]
    stacked = jnp.concatenate([w, q * jnp.exp(g_col)], axis=0)
    row = lax.broadcasted_iota(jnp.int32, (c, 1), 0)
    states = [load_state(s) for s in range(n_seg)]
    w_s = None
    q_s = None
    for s in range(n_seg):
        r = _mm(stacked, states[s])
        if n_seg == 1:
            w_s, q_s = r[:c], r[c:]
        else:
            in_seg = jnp.right_shift(row, lg) == s
            w_s = jnp.where(in_seg, r[:c], 0.0 if w_s is None else w_s)
            q_s = jnp.where(in_seg, r[c:], 0.0 if q_s is None else q_s)
    v_new = u - w_s
    yield
    o = q_s + _mm(qk, v_new)
    for s in range(n_seg):
        last = s * seg_len + seg_len - 1
        g_last = g_col[last:last + 1, :]
        if n_seg == 1:
            k_dec = k * jnp.exp(g_last - g_col)
        else:
            in_seg = jnp.right_shift(row, lg) == s
            k_dec = jnp.where(in_seg, k * jnp.exp(jnp.where(in_seg, g_last - g_col, 0.0)), 0.0)
        store_state(s, states[s] * jnp.exp(g_last) + _mm_tn(k_dec, v_new))
    return o


def _expand_heads(slab, first_lane, width):
    rows = slab.shape[0]
    cols = [slab[:, first_lane + hd:first_lane + hd + 1] for hd in range(GDN_HEADS)]
    if width % LANES == 0:
        return jnp.concatenate([jnp.broadcast_to(c, (rows, width)) for c in cols], axis=1)
    total = GDN_HEADS * width
    lane_head = jnp.right_shift(lax.broadcasted_iota(jnp.int32, (rows, total), 1), _log2(width))
    out = jnp.broadcast_to(cols[-1], (rows, total))
    for hd in range(GDN_HEADS - 2, -1, -1):
        out = jnp.where(lane_head == hd, jnp.broadcast_to(cols[hd], (rows, total)), out)
    return out


def _packed_masks():
    c, nh = CHUNK, GDN_HEADS
    ri = lax.broadcasted_iota(jnp.int32, (c, nh * c), 0)
    cj = jnp.bitwise_and(lax.broadcasted_iota(jnp.int32, (c, nh * c), 1), c - 1)
    return dict(lower=ri >= cj, strict=ri > cj, eye=ri == cj, join0=(jnp.bitwise_and(ri, 1) == 1) & (cj == ri - 1))


def _fill_block_factors(bd_ref, kbd_ref):
    c, nh = CHUNK, GDN_HEADS
    w = nh * c
    lg = _log2(c)
    rw = lax.broadcasted_iota(jnp.int32, (w, w), 0)
    cw = lax.broadcasted_iota(jnp.int32, (w, w), 1)
    same_head = jnp.right_shift(rw, lg) == jnp.right_shift(cw, lg)
    rw, cw = jnp.bitwise_and(rw, c - 1), jnp.bitwise_and(cw, c - 1)
    one_zero = lambda m: jnp.where(m, 1.0, 0.0).astype(BF16)
    bd_ref[0] = one_zero(same_head)
    for lb in range(1, lg):
        bd_ref[lb] = one_zero(same_head & (jnp.bitwise_and(jnp.right_shift(rw, lb), 1) == 1)
                              & (jnp.right_shift(cw, lb) == jnp.right_shift(rw, lb) - 1))
    kbd_ref[...] = one_zero(jnp.right_shift(lax.broadcasted_iota(jnp.int32, (w, nh * HEAD_DIM), 0), lg)
                            == jnp.right_shift(lax.broadcasted_iota(jnp.int32, (w, nh * HEAD_DIM), 1),
                                               _log2(HEAD_DIM)))


def _wy_tile_packed(q_all, k_all, v_all, g, beta, state_ref, rows, masks, bd_ref, kbd_ref):
    c, nh = CHUNK, GDN_HEADS
    lg = _log2(c)
    n_chunks = rows // c
    lower, strict, eye = masks["lower"], masks["strict"], masks["eye"]

    def block_diag(x, factor):
        return jnp.concatenate([x.astype(BF16)] * nh, axis=0) * factor

    def head(x, hd, width):
        return x[:, hd * width:(hd + 1) * width]

    g_c = _expand_heads(g, 0, c)
    b_c = _expand_heads(beta, nh, c)
    g_d = _expand_heads(g, 0, HEAD_DIM)
    b_d = _expand_heads(beta, nh, HEAD_DIM)
    eg_d = jnp.exp(g_d)
    v_rhs = v_all * b_d
    k_rhs = k_all * (b_d * eg_d)
    q_g = q_all * eg_d

    a_off, qk_dec = [], []
    for cb in range(n_chunks):
        rs = slice(cb * c, (cb + 1) * c)
        g_row = jnp.sum(jnp.where(eye, g_c[rs], 0.0), axis=0, keepdims=True)
        decay = jnp.where(lower, jnp.exp(jnp.where(lower, g_c[rs] - g_row, 0.0)), 0.0)
        kq = _mm_nt(jnp.concatenate([k_all[rs], q_all[rs]], axis=0), block_diag(k_all[rs], kbd_ref[...]))
        a_off.append(jnp.where(strict, b_c[rs] * kq[:c] * decay, 0.0))
        qk_dec.append(kq[c:] * decay)
    yield

    eye_f = jnp.where(eye, 1.0, 0.0).astype(F32)
    x_inv = [eye_f - jnp.where(masks["join0"], a, 0.0) for a in a_off]
    for lb in range(1, lg):
        xb = [jnp.dot(x.astype(BF16), block_diag(a, bd_ref[lb]), preferred_element_type=F32)
              for x, a in zip(x_inv, a_off)]
        yield
        xbx = [jnp.dot(t.astype(BF16), block_diag(x, bd_ref[0]), preferred_element_type=F32)
               for t, x in zip(xb, x_inv)]
        x_inv = [x - t for x, t in zip(x_inv, xbx)]
        yield

    u, wk = [], []
    for cb in range(n_chunks):
        rs = slice(cb * c, (cb + 1) * c)
        sol = [_mm(head(x_inv[cb], hd, c),
                   jnp.concatenate([head(v_rhs[rs], hd, HEAD_DIM), head(k_rhs[rs], hd, HEAD_DIM)], axis=1))
               for hd in range(nh)]
        u.append([s[:, :HEAD_DIM] for s in sol])
        wk.append([s[:, HEAD_DIM:] for s in sol])
    yield

    state = [state_ref[0, hd] for hd in range(nh)]
    o_blocks = []
    for cb in range(n_chunks):
        rs = slice(cb * c, (cb + 1) * c)
        g_last = g_d[(cb + 1) * c - 1:(cb + 1) * c, :]
        k_dec = k_all[rs] * jnp.exp(g_last - g_d[rs])
        eg_last = jnp.exp(g_last)
        r = [_mm(jnp.concatenate([wk[cb][hd], head(q_g[rs], hd, HEAD_DIM)], axis=0), state[hd]) for hd in range(nh)]
        v_new = [u[cb][hd] - r[hd][:c] for hd in range(nh)]
        yield
        o_blocks.append(jnp.concatenate(
            [r[hd][c:] + _mm(head(qk_dec[cb], hd, c), v_new[hd]) for hd in range(nh)], axis=1))
        state = [state[hd] * head(eg_last, hd, HEAD_DIM) + _mm_tn(head(k_dec, hd, HEAD_DIM), v_new[hd])
                 for hd in range(nh)]
        yield
    for hd in range(nh):
        state_ref[0, hd] = state[hd]
    return jnp.concatenate(o_blocks, axis=0)


def _run_tiles(programs):
    programs = list(programs)
    results = [None] * len(programs)
    live = list(range(len(programs)))
    while live:
        for idx in list(live):
            try:
                next(programs[idx])
            except StopIteration as done:
                results[idx] = done.value
                live.remove(idx)
    return results


def _shifted_history(buf, rows, n_taps, hist_buf, seg_len):
    out = []
    with_carry = buf[0:SUBLANES + rows, :]
    for s in range(1, n_taps):
        raw = pltpu.roll(with_carry, s, 0)[SUBLANES:, :]
        if hist_buf is not None:
            t = jnp.bitwise_and(lax.broadcasted_iota(jnp.int32, (rows, 1), 0), seg_len - 1)
            raw = jnp.where(t >= s, raw, hist_buf[pl.ds(n_taps - 1 - s, rows), :])
        out.append(raw)
    return out


def _mixer_tile(ti, rows, seg_len, long_seq, masks, r, ab, rest):
    n_chunks = rows // CHUNK
    new_rows = slice(SUBLANES, SUBLANES + rows)
    qb, sb = r.qbuf.at[ti], r.sbuf.at[ti]

    gate = r.gate[...]
    log_a = -jnp.exp(gate[0:1, :]) * _softplus(ab + gate[1:2, :])
    beta = _sigmoid(ab)
    pos = jnp.bitwise_and(lax.broadcasted_iota(jnp.int32, (rows, 1), 0), seg_len - 1)
    g = log_a
    shift = 1
    while shift < seg_len:
        g = g + jnp.where(pos >= shift, pltpu.roll(g, shift, 0), 0.0)
        shift *= 2

    sc_pre = rest[:, R_SCC:R_SCC + SC_WIDTH] * rest[:, R_SCH:R_SCH + SC_WIDTH]
    sb[new_rows, :] = sc_pre
    if long_seq:
        hq_buf = hs_buf = None
    else:
        hq_buf, hs_buf = r.hq_buf.at[ti], r.hs_buf.at[ti]
        zq = jnp.zeros((SUBLANES, QKV_DIM), F32)
        zs = jnp.zeros((SUBLANES, SC_WIDTH), F32)
        qb[0:SUBLANES, :] = zq
        sb[0:SUBLANES, :] = zs
        hq_buf[0:rows, :] = r.hq[ti]
        hq_buf[rows:rows + SUBLANES, :] = zq
        hs_buf[0:rows, :] = r.hs[ti]
        hs_buf[rows:rows + SUBLANES, :] = zs
    qkv_pre = qb[new_rows, :]
    cwq = r.cwq[...]
    if long_seq:
        with_carry = qb[0:SUBLANES + rows, :]
        back1 = pltpu.roll(with_carry, 1, 0)
        pair = with_carry * cwq[1:2, :] + back1 * cwq[0:1, :]
        qkv = _silu(pltpu.roll(pair, 2, 0)[SUBLANES:, :] + back1[SUBLANES:, :] * cwq[2:3, :]
                    + qkv_pre * cwq[3:4, :])
    else:
        q1, q2, q3 = _shifted_history(qb, rows, GDN_CONV, hq_buf, seg_len)
        qkv = _silu(q3 * cwq[0:1, :] + q2 * cwq[1:2, :] + q1 * cwq[2:3, :] + qkv_pre * cwq[3:4, :])
    if long_seq:
        r.qtail[ti] = qb[rows:rows + SUBLANES, :]
        r.stail[ti] = sb[rows:rows + SUBLANES, :]
    else:
        r.stail[ti] = sc_pre
    q_n, k_n, v_n = [], [], []
    for hd in range(GDN_HEADS):
        lo = hd * HEAD_DIM
        q_h = qkv[:, lo:lo + HEAD_DIM]
        k_h = qkv[:, GDN_WIDTH + lo:GDN_WIDTH + lo + HEAD_DIM]
        q_n.append(q_h * (lax.rsqrt(jnp.sum(q_h * q_h, axis=-1, keepdims=True) + EPS) * (HEAD_DIM ** -0.5)))
        k_n.append(k_h * lax.rsqrt(jnp.sum(k_h * k_h, axis=-1, keepdims=True) + EPS))
        v_n.append(qkv[:, 2 * GDN_WIDTH + lo:2 * GDN_WIDTH + lo + HEAD_DIM])
    yield

    state_ref = r.s_out.at[ti]
    if long_seq:
        o_all = yield from _wy_tile_packed(jnp.concatenate(q_n, axis=1), jnp.concatenate(k_n, axis=1),
                                           jnp.concatenate(v_n, axis=1), g, beta, state_ref, rows, masks,
                                           r.bd, r.kbd)
        o_h = [o_all[:, hd * HEAD_DIM:(hd + 1) * HEAD_DIM] for hd in range(GDN_HEADS)]
    else:
        g_t = g.T
        n_seg = CHUNK // seg_len
        blocks = []
        for cb in range(n_chunks):
            r0 = cb * CHUNK
            for hd in range(GDN_HEADS):

                def load_state(s, hd=hd, base=cb * n_seg):
                    return state_ref[base + s, hd]

                def store_state(s, val, hd=hd, base=cb * n_seg):
                    state_ref[base + s, hd] = val

                blocks.append(_wy_block(
                    q_n[hd][r0:r0 + CHUNK], k_n[hd][r0:r0 + CHUNK], v_n[hd][r0:r0 + CHUNK],
                    g[r0:r0 + CHUNK, hd:hd + 1], g_t[hd:hd + 1, r0:r0 + CHUNK],
                    beta[r0:r0 + CHUNK, GDN_HEADS + hd:GDN_HEADS + hd + 1],
                    seg_len, load_state, store_state))
        o_blocks = _run_tiles(blocks)
        o_h = [jnp.concatenate(o_blocks[hd::GDN_HEADS], axis=0) for hd in range(GDN_HEADS)]
    yield

    s1, s2 = _shifted_history(sb, rows, SC_CONV, hs_buf, seg_len)
    cws = r.cws[...]
    y_sc = rest[:, R_SCB:R_SCB + SC_WIDTH] * (s2 * cws[0:1, :] + s1 * cws[1:2, :] + sc_pre * cws[2:3, :])
    if long_seq:
        qb[0:SUBLANES, :] = qb[rows:rows + SUBLANES, :]
        sb[0:SUBLANES, :] = sb[rows:rows + SUBLANES, :]
    gnorm = r.gnorm[...]
    o_heads = []
    for hd in range(GDN_HEADS):
        z_h = rest[:, R_Z + hd * HEAD_DIM:R_Z + (hd + 1) * HEAD_DIM]
        o_heads.append(_rmsnorm(o_h[hd], gnorm) * _silu(z_h))
    return jnp.concatenate(o_heads + [y_sc], axis=1)


class _Refs:
    def __init__(self, **refs):
        self.__dict__.update(refs)


def _mixer_seq_kernel(n_tiles, rows,
                      x, hq, hs, s_in, nmix, w_ab, w_qkv, w_rest, cwq, gate, gnorm, cws, wout,
                      xmid, qtail, stail, s_out, qbuf, sbuf, bd, kbd):
    @pl.when(pl.program_id(1) == 0)
    def _():
        s_out[...] = s_in[...]
        qbuf[:, 0:SUBLANES, :] = hq[...]
        sbuf[:, 0:SUBLANES, :] = hs[...]
        _fill_block_factors(bd, kbd)

    x_all = x[...].reshape(n_tiles * rows, D_MODEL)
    h = _rmsnorm(x_all, nmix[...]).astype(BF16)
    ab = jnp.dot(h, w_ab[...], preferred_element_type=F32)
    qbuf[:, SUBLANES:SUBLANES + rows, :] = jnp.dot(h, w_qkv[...], preferred_element_type=F32).reshape(
        n_tiles, rows, QKV_DIM)
    rest = jnp.dot(h, w_rest[...], preferred_element_type=F32)
    r = _Refs(cwq=cwq, gate=gate, gnorm=gnorm, cws=cws, qtail=qtail, stail=stail, s_out=s_out, qbuf=qbuf, sbuf=sbuf,
              bd=bd, kbd=kbd)
    masks = _packed_masks()
    tile = lambda a, ti: a[ti * rows:(ti + 1) * rows]
    mix = _run_tiles(_mixer_tile(ti, rows, CHUNK, True, masks, r, tile(ab, ti), tile(rest, ti))
                     for ti in range(n_tiles))
    xmid[...] = (x_all + _mm(jnp.concatenate(mix, axis=0), wout[...])).reshape(n_tiles, rows, D_MODEL)


def _mixer_step_kernel(rows, seg_len,
                       ab, qkv, rest, hq, hs, s_in, cwq, gate, gnorm, cws,
                       mix, stail, s_out, qbuf, sbuf, hq_buf, hs_buf):
    s_out[...] = s_in[...]
    qbuf[0, SUBLANES:SUBLANES + rows, :] = qkv[0]
    r = _Refs(hq=hq, hs=hs, cwq=cwq, gate=gate, gnorm=gnorm, cws=cws, stail=stail, s_out=s_out, qbuf=qbuf, sbuf=sbuf,
              hq_buf=hq_buf, hs_buf=hs_buf)
    mix[0] = _run_tiles([_mixer_tile(0, rows, seg_len, False, None, r, ab[0], rest[0])])[0].astype(BF16)


def _proj_kernel(x_ref, nmix_ref, w_ab_ref, w_qkv_ref, w_rest_ref, ab_ref, qkv_ref, rest_ref):
    h = _rmsnorm(x_ref[...], nmix_ref[...]).astype(BF16)
    ab_ref[...] = jnp.dot(h, w_ab_ref[...], preferred_element_type=F32)
    qkv_ref[...] = jnp.dot(h, w_qkv_ref[...], preferred_element_type=F32)
    rest_ref[...] = jnp.dot(h, w_rest_ref[...], preferred_element_type=F32)


def _ffn_body(x, p_ref, nmlp_ref, wup_ref, wdown_ref, nple_ref, wg_ref, wp_ref, nf_ref, y_ref):
    hn = _rmsnorm(x, nmlp_ref[...]).astype(BF16)
    acc = x
    for j in range(D_FF // D_MODEL):
        u = jnp.maximum(_mm(hn, wup_ref[:, j * D_MODEL:(j + 1) * D_MODEL]), 0.0)
        acc = acc + _mm(u * u, wdown_ref[j * D_MODEL:(j + 1) * D_MODEL, :])
    gate = _sigmoid(_mm(_rmsnorm(acc, nple_ref[...]), wg_ref[...]))
    x3 = acc + gate * _mm(p_ref[...], wp_ref[...])
    y_ref[...] = _rmsnorm(x3, nf_ref[...])


def _ffn_kernel(x_ref, *rest):
    _ffn_body(x_ref[...], *rest)


def _out_ffn_kernel(x_ref, mix_ref, wout_ref, *rest):
    _ffn_body(x_ref[...] + jnp.dot(mix_ref[...], wout_ref[...], preferred_element_type=F32), *rest)


def _const_spec(shape):
    nd = len(shape)
    return pl.BlockSpec(shape, lambda *_: (0,) * nd, pipeline_mode=pl.Buffered(1))


def _state_spec(n_tiles, n_state):
    return pl.BlockSpec((n_tiles, n_state, GDN_HEADS, HEAD_DIM, HEAD_DIM), lambda b, i: (b, 0, 0, 0, 0))


def _mixer_seq_call(x3, hist_q, hist_s, s_in, weights, *, n_tiles, rows):
    seqs, seq_rows, _ = x3.shape
    row_map = lambda b, i: (b, i, 0)
    seq_map = lambda b, i: (b, 0, 0)
    x_spec = pl.BlockSpec((n_tiles, rows, D_MODEL), row_map)
    hq_spec = pl.BlockSpec((n_tiles, SUBLANES, QKV_DIM), seq_map)
    hs_spec = pl.BlockSpec((n_tiles, SUBLANES, SC_WIDTH), seq_map)
    packed_w = GDN_HEADS * CHUNK
    return pl.pallas_call(
        functools.partial(_mixer_seq_kernel, n_tiles, rows),
        grid=(seqs // n_tiles, seq_rows // rows),
        in_specs=[x_spec, hq_spec, hs_spec, _state_spec(n_tiles, 1)] + [_const_spec(w.shape) for w in weights],
        out_specs=[x_spec, hq_spec, hs_spec, _state_spec(n_tiles, 1)],
        out_shape=[jax.ShapeDtypeStruct(x3.shape, F32), jax.ShapeDtypeStruct(hist_q.shape, F32),
                   jax.ShapeDtypeStruct(hist_s.shape, F32), jax.ShapeDtypeStruct(s_in.shape, F32)],
        scratch_shapes=[pltpu.VMEM((n_tiles, rows + SUBLANES, QKV_DIM), F32),
                        pltpu.VMEM((n_tiles, rows + SUBLANES, SC_WIDTH), F32),
                        pltpu.VMEM((_log2(CHUNK), packed_w, packed_w), BF16),
                        pltpu.VMEM((packed_w, GDN_WIDTH), BF16)],
        compiler_params=pltpu.CompilerParams(
            dimension_semantics=("arbitrary", "arbitrary"), vmem_limit_bytes=VMEM_LIMIT_BYTES),
        name="mixer_seq",
    )(x3, hist_q, hist_s, s_in, *weights)


def _split_in_proj_kernel(wt_ref, w_ab_ref, w_qkv_ref, w_rest_ref):
    n_ab = 2 * GDN_HEADS

    def group(out_ref, first_col):
        for c0 in range(0, out_ref.shape[1], SPLIT_COLS):
            out_ref[:, c0:c0 + SPLIT_COLS] = wt_ref[first_col + c0:first_col + c0 + SPLIT_COLS, :].T.astype(BF16)

    group(w_qkv_ref, 0)
    group(w_rest_ref, QKV_DIM + n_ab)
    lane = lax.broadcasted_iota(jnp.int32, w_ab_ref.shape, 1)
    w_ab_ref[...] = jnp.where(lane < n_ab, wt_ref[QKV_DIM:QKV_DIM + LANES, :].T, 0.0).astype(BF16)


def _split_in_proj_call(w_in):
    rows, cols = w_in.shape
    n_rest = cols - QKV_DIM - 2 * GDN_HEADS
    outs = [jax.ShapeDtypeStruct((rows, n), BF16) for n in (LANES, QKV_DIM, n_rest)]
    full = lambda a: pl.BlockSpec(a.shape, lambda i: (0, 0))
    w_t = w_in.T
    return pl.pallas_call(
        _split_in_proj_kernel,
        grid=(1,),
        in_specs=[full(w_t)],
        out_specs=[full(o) for o in outs],
        out_shape=outs,
        compiler_params=pltpu.CompilerParams(dimension_semantics=("arbitrary",), vmem_limit_bytes=VMEM_LIMIT_BYTES),
        name="split_in_proj",
    )(w_t)


def _proj_call(x2d, weights):
    n = x2d.shape[0]
    outs = [jax.ShapeDtypeStruct((n, w.shape[1]), F32) for w in weights[1:]]
    full = lambda a: pl.BlockSpec(a.shape, lambda i: (0, 0))
    return pl.pallas_call(
        _proj_kernel,
        grid=(1,),
        in_specs=[full(x2d)] + [full(w) for w in weights],
        out_specs=[full(o) for o in outs],
        out_shape=outs,
        compiler_params=pltpu.CompilerParams(dimension_semantics=("arbitrary",), vmem_limit_bytes=VMEM_LIMIT_BYTES),
        name="proj_step",
    )(x2d, *weights)


def _mixer_step_call(ab, qkv, rest, hist_q, hist_s, s_in, weights, *, seg_len):
    tiles, rows, _ = qkv.shape
    tile_map = lambda b, i: (b, 0, 0)
    slab = lambda a: pl.BlockSpec((1, rows, a.shape[2]), tile_map)
    n_state = s_in.shape[1]
    mix = jax.ShapeDtypeStruct((tiles, rows, D_MODEL), BF16)
    stail = jax.ShapeDtypeStruct((tiles, rows, SC_WIDTH), F32)
    return pl.pallas_call(
        functools.partial(_mixer_step_kernel, rows, seg_len),
        grid=(tiles, 1),
        in_specs=[slab(ab), slab(qkv), slab(rest), slab(hist_q), slab(hist_s), _state_spec(1, n_state)]
        + [_const_spec(w.shape) for w in weights],
        out_specs=[slab(mix), slab(stail), _state_spec(1, n_state)],
        out_shape=[mix, stail, jax.ShapeDtypeStruct(s_in.shape, F32)],
        scratch_shapes=[pltpu.VMEM((1, rows + SUBLANES, QKV_DIM), F32), pltpu.VMEM((1, rows + SUBLANES, SC_WIDTH), F32),
                        pltpu.VMEM((1, rows + SUBLANES, QKV_DIM), F32), pltpu.VMEM((1, rows + SUBLANES, SC_WIDTH), F32)],
        compiler_params=pltpu.CompilerParams(
            dimension_semantics=("arbitrary", "arbitrary"), vmem_limit_bytes=VMEM_LIMIT_BYTES),
        name="mixer_step",
    )(ab, qkv, rest, hist_q, hist_s, s_in, *weights)


def _ffn_call(x2d, p2d, weights, *, rows, name, mix=None, wout=None):
    n = x2d.shape[0]
    row_map = lambda i: (i, 0)
    rows_spec = lambda width: pl.BlockSpec((rows, width), row_map)
    pre_args, pre_specs, body = [], [], _ffn_kernel
    if mix is not None:
        pre_args, pre_specs, body = [mix, wout], [rows_spec(D_MODEL), _const_spec(wout.shape)], _out_ffn_kernel
    return pl.pallas_call(
        body,
        grid=(n // rows,),
        in_specs=[rows_spec(D_MODEL)] + pre_specs + [rows_spec(PLE_DIM)] + [_const_spec(w.shape) for w in weights],
        out_specs=rows_spec(D_MODEL),
        out_shape=jax.ShapeDtypeStruct(x2d.shape, F32),
        compiler_params=pltpu.CompilerParams(dimension_semantics=("arbitrary",), vmem_limit_bytes=VMEM_LIMIT_BYTES),
        name=name,
    )(x2d, *pre_args, p2d, *weights)


def _pad_rows(a, rows):
    return jnp.pad(a, ((0, 0), (0, rows - a.shape[1]), (0, 0)))


def _layer(x_prompt, x_sample, conv_qkv, s_gdn, conv_sc, p_prompt, p_sample, norm_mix, w_in, w_conv_qkv, a_log,
           dt_bias, w_gdn_norm, w_conv_sc, w_out, norm_mlp, w_up, w_down, norm_ple, w_ple_gate, w_ple_proj, norm_f):
    bp, tp, _ = x_prompt.shape
    bs, ts, _ = x_sample.shape
    w_ab, w_qkv, w_rest = _split_in_proj_call(w_in)
    wout = w_out.astype(BF16)
    gate = jnp.zeros((SUBLANES, LANES), F32)
    gate = gate.at[0, :GDN_HEADS].set(a_log.astype(F32)).at[1, :GDN_HEADS].set(dt_bias.astype(F32))
    nmix = norm_mix.reshape(1, D_MODEL)
    core_w = (w_conv_qkv, gate, w_gdn_norm.reshape(1, HEAD_DIM), w_conv_sc)
    ffn_w = (norm_mlp.reshape(1, D_MODEL), w_up.astype(BF16), w_down.astype(BF16), norm_ple.reshape(1, D_MODEL),
             w_ple_gate.astype(BF16), w_ple_proj.astype(BF16), norm_f.reshape(1, D_MODEL))

    zq = jnp.zeros((bp, SUBLANES, QKV_DIM), F32)
    zs = jnp.zeros((bp, SUBLANES, SC_WIDTH), F32)
    s0 = jnp.zeros((bp, 1, GDN_HEADS, HEAD_DIM, HEAD_DIM), F32)
    xm_p, qt_p, st_p, s_p = _mixer_seq_call(
        x_prompt, zq, zs, s0, (nmix, w_ab, w_qkv, w_rest) + core_w + (wout,),
        n_tiles=PROMPT_SEQS_PER_STEP, rows=PROMPT_ROWS)
    y_p = _ffn_call(xm_p.reshape(bp * tp, D_MODEL), p_prompt.reshape(bp * tp, PLE_DIM), ffn_w, rows=FFN_ROWS,
                    name="ffn_prompt")

    seq_per_tile = CHUNK // ts
    tiles = bs // seq_per_tile
    x_s = x_sample.reshape(bs * ts, D_MODEL)
    ab_s, qkv_s, rest_s = _proj_call(x_s, (nmix, w_ab, w_qkv, w_rest))
    tiled = lambda a: a.reshape(tiles, CHUNK, a.shape[-1])
    mix_s, st_s, s_s = _mixer_step_call(
        tiled(ab_s), tiled(qkv_s), tiled(rest_s), tiled(_pad_rows(conv_qkv, ts)), tiled(_pad_rows(conv_sc, ts)),
        s_gdn.reshape(tiles, seq_per_tile, GDN_HEADS, HEAD_DIM, HEAD_DIM), core_w, seg_len=ts)
    y_s = _ffn_call(x_s, p_sample.reshape(bs * ts, PLE_DIM), ffn_w, rows=bs * ts, name="ffn_sample",
                    mix=mix_s.reshape(bs * ts, D_MODEL), wout=wout)

    new_conv_p = qt_p[:, SUBLANES - (GDN_CONV - 1):]
    new_sc_p = st_p[:, SUBLANES - (SC_CONV - 1):]
    new_conv_s = qkv_s.reshape(bs, ts, QKV_DIM)[:, ts - (GDN_CONV - 1):]
    new_sc_s = st_s.reshape(bs, ts, SC_WIDTH)[:, ts - (SC_CONV - 1):]
    return (y_p.reshape(bp, tp, D_MODEL), y_s.reshape(bs, ts, D_MODEL), new_conv_p,
            s_p.reshape(bp, GDN_HEADS, HEAD_DIM, HEAD_DIM), new_sc_p, new_conv_s,
            s_s.reshape(bs, GDN_HEADS, HEAD_DIM, HEAD_DIM), new_sc_s)


def kernel(x_prompt, x_sample, state_gdn_conv, state_gdn, state_sc_conv, p_prompt, p_sample, norm_mix, w_in, w_conv_qkv, a_log, dt_bias, w_gdn_norm, w_conv_sc, w_out, norm_mlp, w_up, w_down, norm_ple, w_ple_gate, w_ple_proj, norm_f):
    depth = w_in.shape[0]
    assert depth == 1, "one layer per call"
    assert x_sample.shape[1] >= GDN_CONV - 1 and CHUNK % x_sample.shape[1] == 0
    assert x_prompt.shape[1] % PROMPT_ROWS == 0 and x_prompt.shape[0] % PROMPT_SEQS_PER_STEP == 0
    outs = _layer(x_prompt, x_sample, state_gdn_conv[0], state_gdn[0], state_sc_conv[0], p_prompt[0], p_sample[0],
                  norm_mix[0], w_in[0], w_conv_qkv[0], a_log[0], dt_bias[0], w_gdn_norm[0], w_conv_sc[0], w_out[0],
                  norm_mlp[0], w_up[0], w_down[0], norm_ple[0], w_ple_gate[0], w_ple_proj[0], norm_f)
    y_p, y_s, c_p, s_p, sc_p, c_s, s_s, sc_s = outs
    return (y_p, y_s, c_p[None], s_p[None], sc_p[None], c_s[None], s_s[None], sc_s[None])
```

```python
import functools

import jax
import jax.numpy as jnp
from jax import lax
from jax.experimental import pallas as pl
from jax.experimental.pallas import tpu as pltpu

F32 = jnp.float32
BF16 = jnp.bfloat16

D_MODEL = 1024
PLE_DIM = 256
GDN_HEADS = 4
HEAD_DIM = 128
GDN_WIDTH = GDN_HEADS * HEAD_DIM
QKV_DIM = 3 * GDN_WIDTH
GDN_CONV = 4
SC_WIDTH = D_MODEL - GDN_WIDTH
SC_CONV = 3
D_FF = 4 * D_MODEL
EPS = 1e-6
NEG_LOG2_E = -1.4426950408889634
CHUNK = 64
LANES = 128
SUBLANES = 8
MIB = 1024 * 1024
MIXER_SEQ_VMEM_BYTES = 46 * MIB
FFN_VMEM_BYTES = 52 * MIB
SMALL_CALL_VMEM_BYTES = 40 * MIB
PROMPT_ROWS = 128
PROMPT_SEQS_PER_STEP = 4
FFN_ROWS = 1024
SPLIT_COLS = 256

R_Z = 0
R_SCB = R_Z + GDN_WIDTH
R_SCC = R_SCB + SC_WIDTH
R_SCH = R_SCC + SC_WIDTH


def _mm(a, b):
    return jnp.dot(a.astype(BF16), b.astype(BF16), preferred_element_type=F32)


def _mm_nt(a, b):
    return lax.dot_general(a.astype(BF16), b.astype(BF16), (((1,), (1,)), ((), ())), preferred_element_type=F32)


def _mm_tn(a, b):
    return lax.dot_general(a.astype(BF16), b.astype(BF16), (((0,), (0,)), ((), ())), preferred_element_type=F32)


def _rmsnorm(x, w_row):
    return x * lax.rsqrt(jnp.mean(x * x, axis=-1, keepdims=True) + EPS) * w_row


def _sigmoid(x):
    return 1.0 / (1.0 + jnp.exp2(x * NEG_LOG2_E))


def _silu(x):
    return x * _sigmoid(x)


def _softplus(x):
    return jnp.maximum(x, 0.0) + jnp.log1p(jnp.exp(-jnp.abs(x)))


def _log2(n):
    k = n.bit_length() - 1
    assert (1 << k) == n, n
    return k


def _wy_block(q, k, v, g_col, g_row, beta_col, seg_len, load_state, store_state):
    c = CHUNK
    n_seg = c // seg_len
    lg = _log2(seg_len)
    ri = lax.broadcasted_iota(jnp.int32, (c, c), 0)
    ci = lax.broadcasted_iota(jnp.int32, (c, c), 1)
    same = jnp.right_shift(ri, lg) == jnp.right_shift(ci, lg)
    lower = same & (ri >= ci)
    strict = same & (ri > ci)
    diff = g_col - g_row
    decay = jnp.where(lower, jnp.exp(jnp.where(lower, diff, 0.0)), 0.0)
    a_off = jnp.where(strict, beta_col * _mm_nt(k, k) * decay, 0.0)
    qk = _mm_nt(q, k) * decay
    yield
    eye = jnp.where(ri == ci, 1.0, 0.0).astype(F32)
    x_inv = eye
    for lb in range(lg):
        rb = jnp.right_shift(ri, lb)
        cb = jnp.right_shift(ci, lb)
        join = (jnp.bitwise_and(rb, 1) == 1) & (cb == rb - 1)
        b_lvl = jnp.where(join, a_off, 0.0)
        if lb == 0:
            x_inv = eye - b_lvl
        else:
            xb = _mm(x_inv, b_lvl)
            yield
            x_inv = x_inv - _mm(xb, x_inv)
            yield
    rhs = jnp.concatenate([v * beta_col, k * (beta_col * jnp.exp(g_col))], axis=1)
    sol = _mm(x_inv, rhs)
    yield
    u = sol[:, :HEAD_DIM]
    w = sol[:, HEAD_DIM:]
    stacked = jnp.concatenate([w, q * jnp.exp(g_col)], axis=0)
    row = lax.broadcasted_iota(jnp.int32, (c, 1), 0)
    states = [load_state(s) for s in range(n_seg)]
    w_s = None
    q_s = None
    for s in range(n_seg):
        r = _mm(stacked, states[s])
        if n_seg == 1:
            w_s, q_s = r[:c], r[c:]
        else:
            in_seg = jnp.right_shift(row, lg) == s
            w_s = jnp.where(in_seg, r[:c], 0.0 if w_s is None else w_s)
            q_s = jnp.where(in_seg, r[c:], 0.0 if q_s is None else q_s)
    v_new = u - w_s
    yield
    o = q_s + _mm(qk, v_new)
    for s in range(n_seg):
        last = s * seg_len + seg_len - 1
        g_last = g_col[last:last + 1, :]
        if n_seg == 1:
            k_dec = k * jnp.exp(g_last - g_col)
        else:
            in_seg = jnp.right_shift(row, lg) == s
            k_dec = jnp.where(in_seg, k * jnp.exp(jnp.where(in_seg, g_last - g_col, 0.0)), 0.0)
        store_state(s, states[s] * jnp.exp(g_last) + _mm_tn(k_dec, v_new))
    return o


def _expand_heads(slab, first_lane, width):
    rows = slab.shape[0]
    cols = [slab[:, first_lane + hd:first_lane + hd + 1] for hd in range(GDN_HEADS)]
    if width % LANES == 0:
        return jnp.concatenate([jnp.broadcast_to(c, (rows, width)) for c in cols], axis=1)
    total = GDN_HEADS * width
    lane_head = jnp.right_shift(lax.broadcasted_iota(jnp.int32, (rows, total), 1), _log2(width))
    out = jnp.broadcast_to(cols[-1], (rows, total))
    for hd in range(GDN_HEADS - 2, -1, -1):
        out = jnp.where(lane_head == hd, jnp.broadcast_to(cols[hd], (rows, total)), out)
    return out


def _packed_masks():
    c, nh = CHUNK, GDN_HEADS
    ri = lax.broadcasted_iota(jnp.int32, (c, nh * c), 0)
    cj = jnp.bitwise_and(lax.broadcasted_iota(jnp.int32, (c, nh * c), 1), c - 1)
    return dict(lower=ri >= cj, strict=ri > cj, eye=ri == cj, join0=(jnp.bitwise_and(ri, 1) == 1) & (cj == ri - 1))


def _fill_block_factors(bd_ref, kbd_ref):
    c, nh = CHUNK, GDN_HEADS
    w = nh * c
    lg = _log2(c)
    rw = lax.broadcasted_iota(jnp.int32, (w, w), 0)
    cw = lax.broadcasted_iota(jnp.int32, (w, w), 1)
    same_head = jnp.right_shift(rw, lg) == jnp.right_shift(cw, lg)
    rw, cw = jnp.bitwise_and(rw, c - 1), jnp.bitwise_and(cw, c - 1)
    one_zero = lambda m: jnp.where(m, 1.0, 0.0).astype(BF16)
    bd_ref[0] = one_zero(same_head)
    for lb in range(1, lg):
        bd_ref[lb] = one_zero(same_head & (jnp.bitwise_and(jnp.right_shift(rw, lb), 1) == 1)
                              & (jnp.right_shift(cw, lb) == jnp.right_shift(rw, lb) - 1))
    kbd_ref[...] = one_zero(jnp.right_shift(lax.broadcasted_iota(jnp.int32, (w, nh * HEAD_DIM), 0), lg)
                            == jnp.right_shift(lax.broadcasted_iota(jnp.int32, (w, nh * HEAD_DIM), 1),
                                               _log2(HEAD_DIM)))


def _wy_tile_packed(q_all, k_all, v_all, g, beta, state_ref, rows, masks, bd_ref, kbd_ref):
    c, nh = CHUNK, GDN_HEADS
    lg = _log2(c)
    n_chunks = rows // c
    lower, strict, eye = masks["lower"], masks["strict"], masks["eye"]

    def block_diag(x, factor):
        return jnp.concatenate([x.astype(BF16)] * nh, axis=0) * factor

    def head(x, hd, width):
        return x[:, hd * width:(hd + 1) * width]

    g_c = _expand_heads(g, 0, c)
    b_c = _expand_heads(beta, nh, c)
    g_d = _expand_heads(g, 0, HEAD_DIM)
    b_d = _expand_heads(beta, nh, HEAD_DIM)
    eg_d = jnp.exp(g_d)
    v_rhs = v_all * b_d
    k_rhs = k_all * (b_d * eg_d)
    q_g = q_all * eg_d

    a_off, qk_dec = [], []
    for cb in range(n_chunks):
        rs = slice(cb * c, (cb + 1) * c)
        g_row = jnp.sum(jnp.where(eye, g_c[rs], 0.0), axis=0, keepdims=True)
        decay = jnp.where(lower, jnp.exp(jnp.where(lower, g_c[rs] - g_row, 0.0)), 0.0)
        kq = _mm_nt(jnp.concatenate([k_all[rs], q_all[rs]], axis=0), block_diag(k_all[rs], kbd_ref[...]))
        a_off.append(jnp.where(strict, b_c[rs] * kq[:c] * decay, 0.0))
        qk_dec.append(kq[c:] * decay)
    yield

    eye_f = jnp.where(eye, 1.0, 0.0).astype(F32)
    x_inv = [eye_f - jnp.where(masks["join0"], a, 0.0) for a in a_off]
    for lb in range(1, lg):
        xb = [jnp.dot(x.astype(BF16), block_diag(a, bd_ref[lb]), preferred_element_type=F32)
              for x, a in zip(x_inv, a_off)]
        yield
        xbx = [jnp.dot(t.astype(BF16), block_diag(x, bd_ref[0]), preferred_element_type=F32)
               for t, x in zip(xb, x_inv)]
        x_inv = [x - t for x, t in zip(x_inv, xbx)]
        yield

    u, wk = [], []
    for cb in range(n_chunks):
        rs = slice(cb * c, (cb + 1) * c)
        sol = [_mm(head(x_inv[cb], hd, c),
                   jnp.concatenate([head(v_rhs[rs], hd, HEAD_DIM), head(k_rhs[rs], hd, HEAD_DIM)], axis=1))
               for hd in range(nh)]
        u.append([s[:, :HEAD_DIM] for s in sol])
        wk.append([s[:, HEAD_DIM:] for s in sol])
    yield

    state = [state_ref[0, hd] for hd in range(nh)]
    o_blocks = []
    for cb in range(n_chunks):
        rs = slice(cb * c, (cb + 1) * c)
        g_last = g_d[(cb + 1) * c - 1:(cb + 1) * c, :]
        k_dec = k_all[rs] * jnp.exp(g_last - g_d[rs])
        eg_last = jnp.exp(g_last)
        r = [_mm(jnp.concatenate([wk[cb][hd], head(q_g[rs], hd, HEAD_DIM)], axis=0), state[hd]) for hd in range(nh)]
        v_new = [u[cb][hd] - r[hd][:c] for hd in range(nh)]
        yield
        o_blocks.append(jnp.concatenate(
            [r[hd][c:] + _mm(head(qk_dec[cb], hd, c), v_new[hd]) for hd in range(nh)], axis=1))
        state = [state[hd] * head(eg_last, hd, HEAD_DIM) + _mm_tn(head(k_dec, hd, HEAD_DIM), v_new[hd])
                 for hd in range(nh)]
        yield
    for hd in range(nh):
        state_ref[0, hd] = state[hd]
    return jnp.concatenate(o_blocks, axis=0)


def _run_tiles(programs):
    programs = list(programs)
    results = [None] * len(programs)
    live = list(range(len(programs)))
    while live:
        for idx in list(live):
            try:
                next(programs[idx])
            except StopIteration as done:
                results[idx] = done.value
                live.remove(idx)
    return results


def _shifted_history(buf, rows, n_taps, hist_buf, seg_len):
    out = []
    with_carry = buf[0:SUBLANES + rows, :]
    for s in range(1, n_taps):
        raw = pltpu.roll(with_carry, s, 0)[SUBLANES:, :]
        if hist_buf is not None:
            t = jnp.bitwise_and(lax.broadcasted_iota(jnp.int32, (rows, 1), 0), seg_len - 1)
            raw = jnp.where(t >= s, raw, hist_buf[pl.ds(n_taps - 1 - s, rows), :])
        out.append(raw)
    return out


def _mixer_tile(ti, rows, seg_len, long_seq, masks, r, ab, rest):
    n_chunks = rows // CHUNK
    new_rows = slice(SUBLANES, SUBLANES + rows)
    qb, sb = r.qbuf.at[ti], r.sbuf.at[ti]

    gate = r.gate[...]
    log_a = -jnp.exp(gate[0:1, :]) * _softplus(ab + gate[1:2, :])
    beta = _sigmoid(ab)
    pos = jnp.bitwise_and(lax.broadcasted_iota(jnp.int32, (rows, 1), 0), seg_len - 1)
    g = log_a
    shift = 1
    while shift < seg_len:
        g = g + jnp.where(pos >= shift, pltpu.roll(g, shift, 0), 0.0)
        shift *= 2

    sc_pre = rest[:, R_SCC:R_SCC + SC_WIDTH] * rest[:, R_SCH:R_SCH + SC_WIDTH]
    sb[new_rows, :] = sc_pre
    if long_seq:
        hq_buf = hs_buf = None
    else:
        hq_buf, hs_buf = r.hq_buf.at[ti], r.hs_buf.at[ti]
        zq = jnp.zeros((SUBLANES, QKV_DIM), F32)
        zs = jnp.zeros((SUBLANES, SC_WIDTH), F32)
        qb[0:SUBLANES, :] = zq
        sb[0:SUBLANES, :] = zs
        hq_buf[0:rows, :] = r.hq[ti]
        hq_buf[rows:rows + SUBLANES, :] = zq
        hs_buf[0:rows, :] = r.hs[ti]
        hs_buf[rows:rows + SUBLANES, :] = zs
    qkv_pre = qb[new_rows, :]
    cwq = r.cwq[...]
    if long_seq:
        with_carry = qb[0:SUBLANES + rows, :]
        back1 = pltpu.roll(with_carry, 1, 0)
        pair = with_carry * cwq[1:2, :] + back1 * cwq[0:1, :]
        qkv = _silu(pltpu.roll(pair, 2, 0)[SUBLANES:, :] + back1[SUBLANES:, :] * cwq[2:3, :]
                    + qkv_pre * cwq[3:4, :])
    else:
        q1, q2, q3 = _shifted_history(qb, rows, GDN_CONV, hq_buf, seg_len)
        qkv = _silu(q3 * cwq[0:1, :] + q2 * cwq[1:2, :] + q1 * cwq[2:3, :] + qkv_pre * cwq[3:4, :])
    if long_seq:
        r.qtail[ti] = qb[rows:rows + SUBLANES, :]
        r.stail[ti] = sb[rows:rows + SUBLANES, :]
    else:
        r.stail[ti] = sc_pre
    q_n, k_n, v_n = [], [], []
    for hd in range(GDN_HEADS):
        lo = hd * HEAD_DIM
        q_h = qkv[:, lo:lo + HEAD_DIM]
        k_h = qkv[:, GDN_WIDTH + lo:GDN_WIDTH + lo + HEAD_DIM]
        q_n.append(q_h * (lax.rsqrt(jnp.sum(q_h * q_h, axis=-1, keepdims=True) + EPS) * (HEAD_DIM ** -0.5)))
        k_n.append(k_h * lax.rsqrt(jnp.sum(k_h * k_h, axis=-1, keepdims=True) + EPS))
        v_n.append(qkv[:, 2 * GDN_WIDTH + lo:2 * GDN_WIDTH + lo + HEAD_DIM])
    yield

    state_ref = r.s_out.at[ti]
    if long_seq:
        o_all = yield from _wy_tile_packed(jnp.concatenate(q_n, axis=1), jnp.concatenate(k_n, axis=1),
                                           jnp.concatenate(v_n, axis=1), g, beta, state_ref, rows, masks,
                                           r.bd, r.kbd)
        o_h = [o_all[:, hd * HEAD_DIM:(hd + 1) * HEAD_DIM] for hd in range(GDN_HEADS)]
    else:
        g_t = g.T
        n_seg = CHUNK // seg_len
        blocks = []
        for cb in range(n_chunks):
            r0 = cb * CHUNK
            for hd in range(GDN_HEADS):

                def load_state(s, hd=hd, base=cb * n_seg):
                    return state_ref[base + s, hd]

                def store_state(s, val, hd=hd, base=cb * n_seg):
                    state_ref[base + s, hd] = val

                blocks.append(_wy_block(
                    q_n[hd][r0:r0 + CHUNK], k_n[hd][r0:r0 + CHUNK], v_n[hd][r0:r0 + CHUNK],
                    g[r0:r0 + CHUNK, hd:hd + 1], g_t[hd:hd + 1, r0:r0 + CHUNK],
                    beta[r0:r0 + CHUNK, GDN_HEADS + hd:GDN_HEADS + hd + 1],
                    seg_len, load_state, store_state))
        o_blocks = _run_tiles(blocks)
        o_h = [jnp.concatenate(o_blocks[hd::GDN_HEADS], axis=0) for hd in range(GDN_HEADS)]
    yield

    s1, s2 = _shifted_history(sb, rows, SC_CONV, hs_buf, seg_len)
    cws = r.cws[...]
    y_sc = rest[:, R_SCB:R_SCB + SC_WIDTH] * (s2 * cws[0:1, :] + s1 * cws[1:2, :] + sc_pre * cws[2:3, :])
    if long_seq:
        qb[0:SUBLANES, :] = qb[rows:rows + SUBLANES, :]
        sb[0:SUBLANES, :] = sb[rows:rows + SUBLANES, :]
    gnorm = r.gnorm[...]
    o_heads = []
    for hd in range(GDN_HEADS):
        z_h = rest[:, R_Z + hd * HEAD_DIM:R_Z + (hd + 1) * HEAD_DIM]
        o_heads.append(_rmsnorm(o_h[hd], gnorm) * _silu(z_h))
    return jnp.concatenate(o_heads + [y_sc], axis=1)


class _Refs:
    def __init__(self, **refs):
        self.__dict__.update(refs)


def _mixer_seq_kernel(n_tiles, rows,
                      x, hq, hs, s_in, nmix, w_ab, w_qkv, w_rest, cwq, gate, gnorm, cws, wout,
                      xmid, qtail, stail, s_out, qbuf, sbuf, bd, kbd):
    @pl.when(pl.program_id(1) == 0)
    def _():
        s_out[...] = s_in[...]
        qbuf[:, 0:SUBLANES, :] = hq[...]
        sbuf[:, 0:SUBLANES, :] = hs[...]
        _fill_block_factors(bd, kbd)

    x_all = x[...].reshape(n_tiles * rows, D_MODEL)
    h = _rmsnorm(x_all, nmix[...]).astype(BF16)
    ab = jnp.dot(h, w_ab[...], preferred_element_type=F32)
    qbuf[:, SUBLANES:SUBLANES + rows, :] = jnp.dot(h, w_qkv[...], preferred_element_type=F32).reshape(
        n_tiles, rows, QKV_DIM)
    rest = jnp.dot(h, w_rest[...], preferred_element_type=F32)
    r = _Refs(cwq=cwq, gate=gate, gnorm=gnorm, cws=cws, qtail=qtail, stail=stail, s_out=s_out, qbuf=qbuf, sbuf=sbuf,
              bd=bd, kbd=kbd)
    masks = _packed_masks()
    tile = lambda a, ti: a[ti * rows:(ti + 1) * rows]
    mix = _run_tiles(_mixer_tile(ti, rows, CHUNK, True, masks, r, tile(ab, ti), tile(rest, ti))
                     for ti in range(n_tiles))
    xmid[...] = (x_all + _mm(jnp.concatenate(mix, axis=0), wout[...])).reshape(n_tiles, rows, D_MODEL)


def _mixer_step_kernel(rows, seg_len,
                       ab, qkv, rest, hq, hs, s_in, cwq, gate, gnorm, cws,
                       mix, stail, s_out, qbuf, sbuf, hq_buf, hs_buf):
    s_out[...] = s_in[...]
    qbuf[0, SUBLANES:SUBLANES + rows, :] = qkv[0]
    r = _Refs(hq=hq, hs=hs, cwq=cwq, gate=gate, gnorm=gnorm, cws=cws, stail=stail, s_out=s_out, qbuf=qbuf, sbuf=sbuf,
              hq_buf=hq_buf, hs_buf=hs_buf)
    mix[0] = _run_tiles([_mixer_tile(0, rows, seg_len, False, None, r, ab[0], rest[0])])[0].astype(BF16)


def _proj_kernel(x_ref, nmix_ref, w_ab_ref, w_qkv_ref, w_rest_ref, ab_ref, qkv_ref, rest_ref):
    h = _rmsnorm(x_ref[...], nmix_ref[...]).astype(BF16)
    ab_ref[...] = jnp.dot(h, w_ab_ref[...], preferred_element_type=F32)
    qkv_ref[...] = jnp.dot(h, w_qkv_ref[...], preferred_element_type=F32)
    rest_ref[...] = jnp.dot(h, w_rest_ref[...], preferred_element_type=F32)


def _ffn_body(x, p_ref, nmlp_ref, wup_ref, wdown_ref, nple_ref, wg_ref, wp_ref, nf_ref, y_ref):
    hn = _rmsnorm(x, nmlp_ref[...]).astype(BF16)
    acc = x
    for j in range(D_FF // D_MODEL):
        u = jnp.maximum(_mm(hn, wup_ref[:, j * D_MODEL:(j + 1) * D_MODEL]), 0.0)
        acc = acc + _mm(u * u, wdown_ref[j * D_MODEL:(j + 1) * D_MODEL, :])
    gate = _sigmoid(_mm(_rmsnorm(acc, nple_ref[...]), wg_ref[...]))
    x3 = acc + gate * _mm(p_ref[...], wp_ref[...])
    y_ref[...] = _rmsnorm(x3, nf_ref[...])


def _ffn_kernel(x_ref, *rest):
    _ffn_body(x_ref[...], *rest)


def _out_ffn_kernel(x_ref, mix_ref, wout_ref, *rest):
    _ffn_body(x_ref[...] + jnp.dot(mix_ref[...], wout_ref[...], preferred_element_type=F32), *rest)


def _const_spec(shape):
    nd = len(shape)
    return pl.BlockSpec(shape, lambda *_: (0,) * nd, pipeline_mode=pl.Buffered(1))


def _state_spec(n_tiles, n_state):
    return pl.BlockSpec((n_tiles, n_state, GDN_HEADS, HEAD_DIM, HEAD_DIM), lambda b, i: (b, 0, 0, 0, 0))


def _mixer_seq_call(x3, hist_q, hist_s, s_in, weights, *, n_tiles, rows):
    seqs, seq_rows, _ = x3.shape
    row_map = lambda b, i: (b, i, 0)
    seq_map = lambda b, i: (b, 0, 0)
    x_spec = pl.BlockSpec((n_tiles, rows, D_MODEL), row_map)
    hq_spec = pl.BlockSpec((n_tiles, SUBLANES, QKV_DIM), seq_map)
    hs_spec = pl.BlockSpec((n_tiles, SUBLANES, SC_WIDTH), seq_map)
    packed_w = GDN_HEADS * CHUNK
    return pl.pallas_call(
        functools.partial(_mixer_seq_kernel, n_tiles, rows),
        grid=(seqs // n_tiles, seq_rows // rows),
        in_specs=[x_spec, hq_spec, hs_spec, _state_spec(n_tiles, 1)] + [_const_spec(w.shape) for w in weights],
        out_specs=[x_spec, hq_spec, hs_spec, _state_spec(n_tiles, 1)],
        out_shape=[jax.ShapeDtypeStruct(x3.shape, F32), jax.ShapeDtypeStruct(hist_q.shape, F32),
                   jax.ShapeDtypeStruct(hist_s.shape, F32), jax.ShapeDtypeStruct(s_in.shape, F32)],
        scratch_shapes=[pltpu.VMEM((n_tiles, rows + SUBLANES, QKV_DIM), F32),
                        pltpu.VMEM((n_tiles, rows + SUBLANES, SC_WIDTH), F32),
                        pltpu.VMEM((_log2(CHUNK), packed_w, packed_w), BF16),
                        pltpu.VMEM((packed_w, GDN_WIDTH), BF16)],
        compiler_params=pltpu.CompilerParams(
            dimension_semantics=("arbitrary", "arbitrary"), vmem_limit_bytes=MIXER_SEQ_VMEM_BYTES),
        name="mixer_seq",
    )(x3, hist_q, hist_s, s_in, *weights)


def _split_in_proj_kernel(wt_ref, w_ab_ref, w_qkv_ref, w_rest_ref):
    n_ab = 2 * GDN_HEADS

    def group(out_ref, first_col):
        for c0 in range(0, out_ref.shape[1], SPLIT_COLS):
            out_ref[:, c0:c0 + SPLIT_COLS] = wt_ref[first_col + c0:first_col + c0 + SPLIT_COLS, :].T.astype(BF16)

    group(w_qkv_ref, 0)
    group(w_rest_ref, QKV_DIM + n_ab)
    lane = lax.broadcasted_iota(jnp.int32, w_ab_ref.shape, 1)
    w_ab_ref[...] = jnp.where(lane < n_ab, wt_ref[QKV_DIM:QKV_DIM + LANES, :].T, 0.0).astype(BF16)


def _split_in_proj_call(w_in):
    rows, cols = w_in.shape
    n_rest = cols - QKV_DIM - 2 * GDN_HEADS
    outs = [jax.ShapeDtypeStruct((rows, n), BF16) for n in (LANES, QKV_DIM, n_rest)]
    full = lambda a: pl.BlockSpec(a.shape, lambda i: (0, 0))
    w_t = w_in.T
    return pl.pallas_call(
        _split_in_proj_kernel,
        grid=(1,),
        in_specs=[full(w_t)],
        out_specs=[full(o) for o in outs],
        out_shape=outs,
        compiler_params=pltpu.CompilerParams(dimension_semantics=("arbitrary",), vmem_limit_bytes=SMALL_CALL_VMEM_BYTES),
        name="split_in_proj",
    )(w_t)


def _proj_call(x2d, weights):
    n = x2d.shape[0]
    outs = [jax.ShapeDtypeStruct((n, w.shape[1]), F32) for w in weights[1:]]
    full = lambda a: pl.BlockSpec(a.shape, lambda i: (0, 0))
    return pl.pallas_call(
        _proj_kernel,
        grid=(1,),
        in_specs=[full(x2d)] + [full(w) for w in weights],
        out_specs=[full(o) for o in outs],
        out_shape=outs,
        compiler_params=pltpu.CompilerParams(dimension_semantics=("arbitrary",), vmem_limit_bytes=SMALL_CALL_VMEM_BYTES),
        name="proj_step",
    )(x2d, *weights)


def _mixer_step_call(ab, qkv, rest, hist_q, hist_s, s_in, weights, *, seg_len):
    tiles, rows, _ = qkv.shape
    tile_map = lambda b, i: (b, 0, 0)
    slab = lambda a: pl.BlockSpec((1, rows, a.shape[2]), tile_map)
    n_state = s_in.shape[1]
    mix = jax.ShapeDtypeStruct((tiles, rows, D_MODEL), BF16)
    stail = jax.ShapeDtypeStruct((tiles, rows, SC_WIDTH), F32)
    return pl.pallas_call(
        functools.partial(_mixer_step_kernel, rows, seg_len),
        grid=(tiles, 1),
        in_specs=[slab(ab), slab(qkv), slab(rest), slab(hist_q), slab(hist_s), _state_spec(1, n_state)]
        + [_const_spec(w.shape) for w in weights],
        out_specs=[slab(mix), slab(stail), _state_spec(1, n_state)],
        out_shape=[mix, stail, jax.ShapeDtypeStruct(s_in.shape, F32)],
        scratch_shapes=[pltpu.VMEM((1, rows + SUBLANES, QKV_DIM), F32), pltpu.VMEM((1, rows + SUBLANES, SC_WIDTH), F32),
                        pltpu.VMEM((1, rows + SUBLANES, QKV_DIM), F32), pltpu.VMEM((1, rows + SUBLANES, SC_WIDTH), F32)],
        compiler_params=pltpu.CompilerParams(
            dimension_semantics=("arbitrary", "arbitrary"), vmem_limit_bytes=SMALL_CALL_VMEM_BYTES),
        name="mixer_step",
    )(ab, qkv, rest, hist_q, hist_s, s_in, *weights)


def _ffn_call(x2d, p2d, weights, *, rows, name, mix=None, wout=None):
    n = x2d.shape[0]
    row_map = lambda i: (i, 0)
    rows_spec = lambda width: pl.BlockSpec((rows, width), row_map)
    pre_args, pre_specs, body = [], [], _ffn_kernel
    if mix is not None:
        pre_args, pre_specs, body = [mix, wout], [rows_spec(D_MODEL), _const_spec(wout.shape)], _out_ffn_kernel
    return pl.pallas_call(
        body,
        grid=(n // rows,),
        in_specs=[rows_spec(D_MODEL)] + pre_specs + [rows_spec(PLE_DIM)] + [_const_spec(w.shape) for w in weights],
        out_specs=rows_spec(D_MODEL),
        out_shape=jax.ShapeDtypeStruct(x2d.shape, F32),
        compiler_params=pltpu.CompilerParams(dimension_semantics=("arbitrary",), vmem_limit_bytes=FFN_VMEM_BYTES),
        name=name,
    )(x2d, *pre_args, p2d, *weights)


def _pad_rows(a, rows):
    return jnp.pad(a, ((0, 0), (0, rows - a.shape[1]), (0, 0)))


def _layer(x_prompt, x_sample, conv_qkv, s_gdn, conv_sc, p_prompt, p_sample, norm_mix, w_in, w_conv_qkv, a_log,
           dt_bias, w_gdn_norm, w_conv_sc, w_out, norm_mlp, w_up, w_down, norm_ple, w_ple_gate, w_ple_proj, norm_f):
    bp, tp, _ = x_prompt.shape
    bs, ts, _ = x_sample.shape
    w_ab, w_qkv, w_rest = _split_in_proj_call(w_in)
    wout = w_out.astype(BF16)
    gate = jnp.zeros((SUBLANES, LANES), F32)
    gate = gate.at[0, :GDN_HEADS].set(a_log.astype(F32)).at[1, :GDN_HEADS].set(dt_bias.astype(F32))
    nmix = norm_mix.reshape(1, D_MODEL)
    core_w = (w_conv_qkv, gate, w_gdn_norm.reshape(1, HEAD_DIM), w_conv_sc)
    ffn_w = (norm_mlp.reshape(1, D_MODEL), w_up.astype(BF16), w_down.astype(BF16), norm_ple.reshape(1, D_MODEL),
             w_ple_gate.astype(BF16), w_ple_proj.astype(BF16), norm_f.reshape(1, D_MODEL))

    zq = jnp.zeros((bp, SUBLANES, QKV_DIM), F32)
    zs = jnp.zeros((bp, SUBLANES, SC_WIDTH), F32)
    s0 = jnp.zeros((bp, 1, GDN_HEADS, HEAD_DIM, HEAD_DIM), F32)
    xm_p, qt_p, st_p, s_p = _mixer_seq_call(
        x_prompt, zq, zs, s0, (nmix, w_ab, w_qkv, w_rest) + core_w + (wout,),
        n_tiles=PROMPT_SEQS_PER_STEP, rows=PROMPT_ROWS)
    y_p = _ffn_call(xm_p.reshape(bp * tp, D_MODEL), p_prompt.reshape(bp * tp, PLE_DIM), ffn_w, rows=FFN_ROWS,
                    name="ffn_prompt")

    seq_per_tile = CHUNK // ts
    tiles = bs // seq_per_tile
    x_s = x_sample.reshape(bs * ts, D_MODEL)
    ab_s, qkv_s, rest_s = _proj_call(x_s, (nmix, w_ab, w_qkv, w_rest))
    tiled = lambda a: a.reshape(tiles, CHUNK, a.shape[-1])
    mix_s, st_s, s_s = _mixer_step_call(
        tiled(ab_s), tiled(qkv_s), tiled(rest_s), tiled(_pad_rows(conv_qkv, ts)), tiled(_pad_rows(conv_sc, ts)),
        s_gdn.reshape(tiles, seq_per_tile, GDN_HEADS, HEAD_DIM, HEAD_DIM), core_w, seg_len=ts)
    y_s = _ffn_call(x_s, p_sample.reshape(bs * ts, PLE_DIM), ffn_w, rows=bs * ts, name="ffn_sample",
                    mix=mix_s.reshape(bs * ts, D_MODEL), wout=wout)

    new_conv_p = qt_p[:, SUBLANES - (GDN_CONV - 1):]
    new_sc_p = st_p[:, SUBLANES - (SC_CONV - 1):]
    new_conv_s = qkv_s.reshape(bs, ts, QKV_DIM)[:, ts - (GDN_CONV - 1):]
    new_sc_s = st_s.reshape(bs, ts, SC_WIDTH)[:, ts - (SC_CONV - 1):]
    return (y_p.reshape(bp, tp, D_MODEL), y_s.reshape(bs, ts, D_MODEL), new_conv_p,
            s_p.reshape(bp, GDN_HEADS, HEAD_DIM, HEAD_DIM), new_sc_p, new_conv_s,
            s_s.reshape(bs, GDN_HEADS, HEAD_DIM, HEAD_DIM), new_sc_s)


def kernel(x_prompt, x_sample, state_gdn_conv, state_gdn, state_sc_conv, p_prompt, p_sample, norm_mix, w_in, w_conv_qkv, a_log, dt_bias, w_gdn_norm, w_conv_sc, w_out, norm_mlp, w_up, w_down, norm_ple, w_ple_gate, w_ple_proj, norm_f):
    depth = w_in.shape[0]
    assert depth == 1, "one layer per call"
    assert x_sample.shape[1] >= GDN_CONV - 1 and CHUNK % x_sample.shape[1] == 0
    assert x_prompt.shape[1] % PROMPT_ROWS == 0 and x_prompt.shape[0] % PROMPT_SEQS_PER_STEP == 0
    outs = _layer(x_prompt, x_sample, state_gdn_conv[0], state_gdn[0], state_sc_conv[0], p_prompt[0], p_sample[0],
                  norm_mix[0], w_in[0], w_conv_qkv[0], a_log[0], dt_bias[0], w_gdn_norm[0], w_conv_sc[0], w_out[0],
                  norm_mlp[0], w_up[0], w_down[0], norm_ple[0], w_ple_gate[0], w_ple_proj[0], norm_f)
    y_p, y_s, c_p, s_p, sc_p, c_s, s_s, sc_s = outs
    return (y_p, y_s, c_p[None], s_p[None], sc_p[None], c_s[None], s_s[None], sc_s[None])
```

```python
import functools

import jax
import jax.numpy as jnp
from jax import lax
from jax.experimental import pallas as pl
from jax.experimental.pallas import tpu as pltpu

F32 = jnp.float32
BF16 = jnp.bfloat16

D_MODEL = 1024
PLE_DIM = 256
GDN_HEADS = 4
HEAD_DIM = 128
GDN_WIDTH = GDN_HEADS * HEAD_DIM
QKV_DIM = 3 * GDN_WIDTH
GDN_CONV = 4
SC_WIDTH = D_MODEL - GDN_WIDTH
SC_CONV = 3
D_FF = 4 * D_MODEL
EPS = 1e-6
NEG_LOG2_E = -1.4426950408889634
CHUNK = 64
LANES = 128
SUBLANES = 8
MIB = 1024 * 1024
MIXER_SEQ_VMEM_BYTES = 56 * MIB
FFN_VMEM_BYTES = 56 * MIB
SMALL_CALL_VMEM_BYTES = 40 * MIB
PROMPT_ROWS = 128
PROMPT_SEQS_PER_STEP = 4
FFN_ROWS = 1024
SPLIT_COLS = 256

R_Z = 0
R_SCB = R_Z + GDN_WIDTH
R_SCC = R_SCB + SC_WIDTH
R_SCH = R_SCC + SC_WIDTH


def _mm(a, b):
    return jnp.dot(a.astype(BF16), b.astype(BF16), preferred_element_type=F32)


def _mm_nt(a, b):
    return lax.dot_general(a.astype(BF16), b.astype(BF16), (((1,), (1,)), ((), ())), preferred_element_type=F32)


def _mm_tn(a, b):
    return lax.dot_general(a.astype(BF16), b.astype(BF16), (((0,), (0,)), ((), ())), preferred_element_type=F32)


def _rmsnorm(x, w_row):
    return x * lax.rsqrt(jnp.mean(x * x, axis=-1, keepdims=True) + EPS) * w_row


def _sigmoid(x):
    return 1.0 / (1.0 + jnp.exp2(x * NEG_LOG2_E))


def _silu(x):
    return x * _sigmoid(x)


def _softplus(x):
    return jnp.maximum(x, 0.0) + jnp.log1p(jnp.exp(-jnp.abs(x)))


def _log2(n):
    k = n.bit_length() - 1
    assert (1 << k) == n, n
    return k


def _wy_block(q, k, v, g_col, g_row, beta_col, seg_len, load_state, store_state):
    c = CHUNK
    n_seg = c // seg_len
    lg = _log2(seg_len)
    ri = lax.broadcasted_iota(jnp.int32, (c, c), 0)
    ci = lax.broadcasted_iota(jnp.int32, (c, c), 1)
    same = jnp.right_shift(ri, lg) == jnp.right_shift(ci, lg)
    lower = same & (ri >= ci)
    strict = same & (ri > ci)
    diff = g_col - g_row
    decay = jnp.where(lower, jnp.exp(jnp.where(lower, diff, 0.0)), 0.0)
    a_off = jnp.where(strict, beta_col * _mm_nt(k, k) * decay, 0.0)
    qk = _mm_nt(q, k) * decay
    yield
    eye = jnp.where(ri == ci, 1.0, 0.0).astype(F32)
    x_inv = eye
    for lb in range(lg):
        rb = jnp.right_shift(ri, lb)
        cb = jnp.right_shift(ci, lb)
        join = (jnp.bitwise_and(rb, 1) == 1) & (cb == rb - 1)
        b_lvl = jnp.where(join, a_off, 0.0)
        if lb == 0:
            x_inv = eye - b_lvl
        else:
            xb = _mm(x_inv, b_lvl)
            yield
            x_inv = x_inv - _mm(xb, x_inv)
            yield
    rhs = jnp.concatenate([v * beta_col, k * (beta_col * jnp.exp(g_col))], axis=1)
    sol = _mm(x_inv, rhs)
    yield
    u = sol[:, :HEAD_DIM]
    w = sol[:, HEAD_DIM:]
    stacked = jnp.concatenate([w, q * jnp.exp(g_col)], axis=0)
    row = lax.broadcasted_iota(jnp.int32, (c, 1), 0)
    states = [load_state(s) for s in range(n_seg)]
    w_s = None
    q_s = None
    for s in range(n_seg):
        r = _mm(stacked, states[s])
        if n_seg == 1:
            w_s, q_s = r[:c], r[c:]
        else:
            in_seg = jnp.right_shift(row, lg) == s
            w_s = jnp.where(in_seg, r[:c], 0.0 if w_s is None else w_s)
            q_s = jnp.where(in_seg, r[c:], 0.0 if q_s is None else q_s)
    v_new = u - w_s
    yield
    o = q_s + _mm(qk, v_new)
    for s in range(n_seg):
        last = s * seg_len + seg_len - 1
        g_last = g_col[last:last + 1, :]
        if n_seg == 1:
            k_dec = k * jnp.exp(g_last - g_col)
        else:
            in_seg = jnp.right_shift(row, lg) == s
            k_dec = jnp.where(in_seg, k * jnp.exp(jnp.where(in_seg, g_last - g_col, 0.0)), 0.0)
        store_state(s, states[s] * jnp.exp(g_last) + _mm_tn(k_dec, v_new))
    return o


def _expand_heads(slab, first_lane, width):
    rows = slab.shape[0]
    cols = [slab[:, first_lane + hd:first_lane + hd + 1] for hd in range(GDN_HEADS)]
    if width % LANES == 0:
        return jnp.concatenate([jnp.broadcast_to(c, (rows, width)) for c in cols], axis=1)
    total = GDN_HEADS * width
    lane_head = jnp.right_shift(lax.broadcasted_iota(jnp.int32, (rows, total), 1), _log2(width))
    out = jnp.broadcast_to(cols[-1], (rows, total))
    for hd in range(GDN_HEADS - 2, -1, -1):
        out = jnp.where(lane_head == hd, jnp.broadcast_to(cols[hd], (rows, total)), out)
    return out


def _packed_masks():
    c, nh = CHUNK, GDN_HEADS
    ri = lax.broadcasted_iota(jnp.int32, (c, nh * c), 0)
    cj = jnp.bitwise_and(lax.broadcasted_iota(jnp.int32, (c, nh * c), 1), c - 1)
    return dict(lower=ri >= cj, strict=ri > cj, eye=ri == cj, join0=(jnp.bitwise_and(ri, 1) == 1) & (cj == ri - 1))


def _fill_block_factors(bd_ref, kbd_ref):
    c, nh = CHUNK, GDN_HEADS
    w = nh * c
    lg = _log2(c)
    rw = lax.broadcasted_iota(jnp.int32, (w, w), 0)
    cw = lax.broadcasted_iota(jnp.int32, (w, w), 1)
    same_head = jnp.right_shift(rw, lg) == jnp.right_shift(cw, lg)
    rw, cw = jnp.bitwise_and(rw, c - 1), jnp.bitwise_and(cw, c - 1)
    one_zero = lambda m: jnp.where(m, 1.0, 0.0).astype(BF16)
    bd_ref[0] = one_zero(same_head)
    for lb in range(1, lg):
        bd_ref[lb] = one_zero(same_head & (jnp.bitwise_and(jnp.right_shift(rw, lb), 1) == 1)
                              & (jnp.right_shift(cw, lb) == jnp.right_shift(rw, lb) - 1))
    kbd_ref[...] = one_zero(jnp.right_shift(lax.broadcasted_iota(jnp.int32, (w, nh * HEAD_DIM), 0), lg)
                            == jnp.right_shift(lax.broadcasted_iota(jnp.int32, (w, nh * HEAD_DIM), 1),
                                               _log2(HEAD_DIM)))


def _wy_tile_packed(q_all, k_all, v_all, g, beta, state_ref, rows, masks, bd_ref, kbd_ref):
    c, nh = CHUNK, GDN_HEADS
    lg = _log2(c)
    n_chunks = rows // c
    lower, strict, eye = masks["lower"], masks["strict"], masks["eye"]

    def block_diag(x, factor):
        return jnp.concatenate([x.astype(BF16)] * nh, axis=0) * factor

    def head(x, hd, width):
        return x[:, hd * width:(hd + 1) * width]

    g_c = _expand_heads(g, 0, c)
    b_c = _expand_heads(beta, nh, c)
    g_d = _expand_heads(g, 0, HEAD_DIM)
    b_d = _expand_heads(beta, nh, HEAD_DIM)
    eg_d = jnp.exp(g_d)
    v_rhs = v_all * b_d
    k_rhs = k_all * (b_d * eg_d)
    q_g = q_all * eg_d

    a_off, qk_dec = [], []
    for cb in range(n_chunks):
        rs = slice(cb * c, (cb + 1) * c)
        g_row = jnp.sum(jnp.where(eye, g_c[rs], 0.0), axis=0, keepdims=True)
        decay = jnp.where(lower, jnp.exp(jnp.where(lower, g_c[rs] - g_row, 0.0)), 0.0)
        kq = _mm_nt(jnp.concatenate([k_all[rs], q_all[rs]], axis=0), block_diag(k_all[rs], kbd_ref[...]))
        a_off.append(jnp.where(strict, b_c[rs] * kq[:c] * decay, 0.0))
        qk_dec.append(kq[c:] * decay)
    yield

    eye_f = jnp.where(eye, 1.0, 0.0).astype(F32)
    x_inv = [eye_f - jnp.where(masks["join0"], a, 0.0) for a in a_off]
    for lb in range(1, lg):
        xb = [jnp.dot(x.astype(BF16), block_diag(a, bd_ref[lb]), preferred_element_type=F32)
              for x, a in zip(x_inv, a_off)]
        yield
        xbx = [jnp.dot(t.astype(BF16), block_diag(x, bd_ref[0]), preferred_element_type=F32)
               for t, x in zip(xb, x_inv)]
        x_inv = [x - t for x, t in zip(x_inv, xbx)]
        yield

    u, wk = [], []
    for cb in range(n_chunks):
        rs = slice(cb * c, (cb + 1) * c)
        sol = [_mm(head(x_inv[cb], hd, c),
                   jnp.concatenate([head(v_rhs[rs], hd, HEAD_DIM), head(k_rhs[rs], hd, HEAD_DIM)], axis=1))
               for hd in range(nh)]
        u.append([s[:, :HEAD_DIM] for s in sol])
        wk.append([s[:, HEAD_DIM:] for s in sol])
    yield

    state = [state_ref[0, hd] for hd in range(nh)]
    o_blocks = []
    for cb in range(n_chunks):
        rs = slice(cb * c, (cb + 1) * c)
        g_last = g_d[(cb + 1) * c - 1:(cb + 1) * c, :]
        k_dec = k_all[rs] * jnp.exp(g_last - g_d[rs])
        eg_last = jnp.exp(g_last)
        r = [_mm(jnp.concatenate([wk[cb][hd], head(q_g[rs], hd, HEAD_DIM)], axis=0), state[hd]) for hd in range(nh)]
        v_new = [u[cb][hd] - r[hd][:c] for hd in range(nh)]
        yield
        o_blocks.append(jnp.concatenate(
            [r[hd][c:] + _mm(head(qk_dec[cb], hd, c), v_new[hd]) for hd in range(nh)], axis=1))
        state = [state[hd] * head(eg_last, hd, HEAD_DIM) + _mm_tn(head(k_dec, hd, HEAD_DIM), v_new[hd])
                 for hd in range(nh)]
        yield
    for hd in range(nh):
        state_ref[0, hd] = state[hd]
    return jnp.concatenate(o_blocks, axis=0)


def _run_tiles(programs):
    programs = list(programs)
    results = [None] * len(programs)
    live = list(range(len(programs)))
    while live:
        for idx in list(live):
            try:
                next(programs[idx])
            except StopIteration as done:
                results[idx] = done.value
                live.remove(idx)
    return results


def _shifted_history(buf, rows, n_taps, hist_buf, seg_len):
    out = []
    with_carry = buf[0:SUBLANES + rows, :]
    for s in range(1, n_taps):
        raw = pltpu.roll(with_carry, s, 0)[SUBLANES:, :]
        if hist_buf is not None:
            t = jnp.bitwise_and(lax.broadcasted_iota(jnp.int32, (rows, 1), 0), seg_len - 1)
            raw = jnp.where(t >= s, raw, hist_buf[pl.ds(n_taps - 1 - s, rows), :])
        out.append(raw)
    return out


def _mixer_tile(ti, rows, seg_len, long_seq, masks, r, ab, rest):
    n_chunks = rows // CHUNK
    new_rows = slice(SUBLANES, SUBLANES + rows)
    qb, sb = r.qbuf.at[ti], r.sbuf.at[ti]

    gate = r.gate[...]
    log_a = -jnp.exp(gate[0:1, :]) * _softplus(ab + gate[1:2, :])
    beta = _sigmoid(ab)
    pos = jnp.bitwise_and(lax.broadcasted_iota(jnp.int32, (rows, 1), 0), seg_len - 1)
    g = log_a
    shift = 1
    while shift < seg_len:
        g = g + jnp.where(pos >= shift, pltpu.roll(g, shift, 0), 0.0)
        shift *= 2

    sc_pre = rest[:, R_SCC:R_SCC + SC_WIDTH] * rest[:, R_SCH:R_SCH + SC_WIDTH]
    sb[new_rows, :] = sc_pre
    if long_seq:
        hq_buf = hs_buf = None
    else:
        hq_buf, hs_buf = r.hq_buf.at[ti], r.hs_buf.at[ti]
        zq = jnp.zeros((SUBLANES, QKV_DIM), F32)
        zs = jnp.zeros((SUBLANES, SC_WIDTH), F32)
        qb[0:SUBLANES, :] = zq
        sb[0:SUBLANES, :] = zs
        hq_buf[0:rows, :] = r.hq[ti]
        hq_buf[rows:rows + SUBLANES, :] = zq
        hs_buf[0:rows, :] = r.hs[ti]
        hs_buf[rows:rows + SUBLANES, :] = zs
    qkv_pre = qb[new_rows, :]
    cwq = r.cwq[...]
    if long_seq:
        with_carry = qb[0:SUBLANES + rows, :]
        back1 = pltpu.roll(with_carry, 1, 0)
        pair = with_carry * cwq[1:2, :] + back1 * cwq[0:1, :]
        qkv = _silu(pltpu.roll(pair, 2, 0)[SUBLANES:, :] + back1[SUBLANES:, :] * cwq[2:3, :]
                    + qkv_pre * cwq[3:4, :])
    else:
        q1, q2, q3 = _shifted_history(qb, rows, GDN_CONV, hq_buf, seg_len)
        qkv = _silu(q3 * cwq[0:1, :] + q2 * cwq[1:2, :] + q1 * cwq[2:3, :] + qkv_pre * cwq[3:4, :])
    if long_seq:
        r.qtail[ti] = qb[rows:rows + SUBLANES, :]
        r.stail[ti] = sb[rows:rows + SUBLANES, :]
    else:
        r.stail[ti] = sc_pre
    q_n, k_n, v_n = [], [], []
    for hd in range(GDN_HEADS):
        lo = hd * HEAD_DIM
        q_h = qkv[:, lo:lo + HEAD_DIM]
        k_h = qkv[:, GDN_WIDTH + lo:GDN_WIDTH + lo + HEAD_DIM]
        q_n.append(q_h * (lax.rsqrt(jnp.sum(q_h * q_h, axis=-1, keepdims=True) + EPS) * (HEAD_DIM ** -0.5)))
        k_n.append(k_h * lax.rsqrt(jnp.sum(k_h * k_h, axis=-1, keepdims=True) + EPS))
        v_n.append(qkv[:, 2 * GDN_WIDTH + lo:2 * GDN_WIDTH + lo + HEAD_DIM])
    yield

    state_ref = r.s_out.at[ti]
    if long_seq:
        o_all = yield from _wy_tile_packed(jnp.concatenate(q_n, axis=1), jnp.concatenate(k_n, axis=1),
                                           jnp.concatenate(v_n, axis=1), g, beta, state_ref, rows, masks,
                                           r.bd, r.kbd)
        o_h = [o_all[:, hd * HEAD_DIM:(hd + 1) * HEAD_DIM] for hd in range(GDN_HEADS)]
    else:
        g_t = g.T
        n_seg = CHUNK // seg_len
        blocks = []
        for cb in range(n_chunks):
            r0 = cb * CHUNK
            for hd in range(GDN_HEADS):

                def load_state(s, hd=hd, base=cb * n_seg):
                    return state_ref[base + s, hd]

                def store_state(s, val, hd=hd, base=cb * n_seg):
                    state_ref[base + s, hd] = val

                blocks.append(_wy_block(
                    q_n[hd][r0:r0 + CHUNK], k_n[hd][r0:r0 + CHUNK], v_n[hd][r0:r0 + CHUNK],
                    g[r0:r0 + CHUNK, hd:hd + 1], g_t[hd:hd + 1, r0:r0 + CHUNK],
                    beta[r0:r0 + CHUNK, GDN_HEADS + hd:GDN_HEADS + hd + 1],
                    seg_len, load_state, store_state))
        o_blocks = _run_tiles(blocks)
        o_h = [jnp.concatenate(o_blocks[hd::GDN_HEADS], axis=0) for hd in range(GDN_HEADS)]
    yield

    s1, s2 = _shifted_history(sb, rows, SC_CONV, hs_buf, seg_len)
    cws = r.cws[...]
    y_sc = rest[:, R_SCB:R_SCB + SC_WIDTH] * (s2 * cws[0:1, :] + s1 * cws[1:2, :] + sc_pre * cws[2:3, :])
    if long_seq:
        qb[0:SUBLANES, :] = qb[rows:rows + SUBLANES, :]
        sb[0:SUBLANES, :] = sb[rows:rows + SUBLANES, :]
    gnorm = r.gnorm[...]
    o_heads = []
    for hd in range(GDN_HEADS):
        z_h = rest[:, R_Z + hd * HEAD_DIM:R_Z + (hd + 1) * HEAD_DIM]
        o_heads.append(_rmsnorm(o_h[hd], gnorm) * _silu(z_h))
    return jnp.concatenate(o_heads + [y_sc], axis=1)


class _Refs:
    def __init__(self, **refs):
        self.__dict__.update(refs)


def _mixer_seq_kernel(n_tiles, rows,
                      x, hq, hs, s_in, nmix, w_ab, w_qkv, w_rest, cwq, gate, gnorm, cws, wout,
                      xmid, qtail, stail, s_out, qbuf, sbuf, bd, kbd):
    @pl.when(pl.program_id(1) == 0)
    def _():
        s_out[...] = s_in[...]
        qbuf[:, 0:SUBLANES, :] = hq[...]
        sbuf[:, 0:SUBLANES, :] = hs[...]
        _fill_block_factors(bd, kbd)

    x_all = x[...].reshape(n_tiles * rows, D_MODEL)
    h = _rmsnorm(x_all, nmix[...]).astype(BF16)
    ab = jnp.dot(h, w_ab[...], preferred_element_type=F32)
    qbuf[:, SUBLANES:SUBLANES + rows, :] = jnp.dot(h, w_qkv[...], preferred_element_type=F32).reshape(
        n_tiles, rows, QKV_DIM)
    rest = jnp.dot(h, w_rest[...], preferred_element_type=F32)
    r = _Refs(cwq=cwq, gate=gate, gnorm=gnorm, cws=cws, qtail=qtail, stail=stail, s_out=s_out, qbuf=qbuf, sbuf=sbuf,
              bd=bd, kbd=kbd)
    masks = _packed_masks()
    tile = lambda a, ti: a[ti * rows:(ti + 1) * rows]
    mix = _run_tiles(_mixer_tile(ti, rows, CHUNK, True, masks, r, tile(ab, ti), tile(rest, ti))
                     for ti in range(n_tiles))
    xmid[...] = (x_all + _mm(jnp.concatenate(mix, axis=0), wout[...])).reshape(n_tiles, rows, D_MODEL)


def _mixer_step_kernel(rows, seg_len,
                       ab, qkv, rest, hq, hs, s_in, cwq, gate, gnorm, cws,
                       mix, stail, s_out, qbuf, sbuf, hq_buf, hs_buf):
    s_out[...] = s_in[...]
    qbuf[0, SUBLANES:SUBLANES + rows, :] = qkv[0]
    r = _Refs(hq=hq, hs=hs, cwq=cwq, gate=gate, gnorm=gnorm, cws=cws, stail=stail, s_out=s_out, qbuf=qbuf, sbuf=sbuf,
              hq_buf=hq_buf, hs_buf=hs_buf)
    mix[0] = _run_tiles([_mixer_tile(0, rows, seg_len, False, None, r, ab[0], rest[0])])[0].astype(BF16)


def _proj_kernel(x_ref, nmix_ref, w_ab_ref, w_qkv_ref, w_rest_ref, ab_ref, qkv_ref, rest_ref):
    h = _rmsnorm(x_ref[...], nmix_ref[...]).astype(BF16)
    ab_ref[...] = jnp.dot(h, w_ab_ref[...], preferred_element_type=F32)
    qkv_ref[...] = jnp.dot(h, w_qkv_ref[...], preferred_element_type=F32)
    rest_ref[...] = jnp.dot(h, w_rest_ref[...], preferred_element_type=F32)


def _ffn_body(x, p_ref, nmlp_ref, wup_ref, wdown_ref, nple_ref, wg_ref, wp_ref, nf_ref, y_ref):
    hn = _rmsnorm(x, nmlp_ref[...]).astype(BF16)
    acc = x
    for j in range(D_FF // D_MODEL):
        u = jnp.maximum(_mm(hn, wup_ref[:, j * D_MODEL:(j + 1) * D_MODEL]), 0.0)
        acc = acc + _mm(u * u, wdown_ref[j * D_MODEL:(j + 1) * D_MODEL, :])
    gate = _sigmoid(_mm(_rmsnorm(acc, nple_ref[...]), wg_ref[...]))
    x3 = acc + gate * _mm(p_ref[...], wp_ref[...])
    y_ref[...] = _rmsnorm(x3, nf_ref[...])


def _ffn_kernel(x_ref, *rest):
    _ffn_body(x_ref[...], *rest)


def _out_ffn_kernel(x_ref, mix_ref, wout_ref, *rest):
    _ffn_body(x_ref[...] + jnp.dot(mix_ref[...], wout_ref[...], preferred_element_type=F32), *rest)


def _const_spec(shape):
    nd = len(shape)
    return pl.BlockSpec(shape, lambda *_: (0,) * nd, pipeline_mode=pl.Buffered(1))


def _state_spec(n_tiles, n_state):
    return pl.BlockSpec((n_tiles, n_state, GDN_HEADS, HEAD_DIM, HEAD_DIM), lambda b, i: (b, 0, 0, 0, 0))


def _mixer_seq_call(x3, hist_q, hist_s, s_in, weights, *, n_tiles, rows):
    seqs, seq_rows, _ = x3.shape
    row_map = lambda b, i: (b, i, 0)
    seq_map = lambda b, i: (b, 0, 0)
    x_spec = pl.BlockSpec((n_tiles, rows, D_MODEL), row_map)
    hq_spec = pl.BlockSpec((n_tiles, SUBLANES, QKV_DIM), seq_map)
    hs_spec = pl.BlockSpec((n_tiles, SUBLANES, SC_WIDTH), seq_map)
    packed_w = GDN_HEADS * CHUNK
    return pl.pallas_call(
        functools.partial(_mixer_seq_kernel, n_tiles, rows),
        grid=(seqs // n_tiles, seq_rows // rows),
        in_specs=[x_spec, hq_spec, hs_spec, _state_spec(n_tiles, 1)] + [_const_spec(w.shape) for w in weights],
        out_specs=[x_spec, hq_spec, hs_spec, _state_spec(n_tiles, 1)],
        out_shape=[jax.ShapeDtypeStruct(x3.shape, F32), jax.ShapeDtypeStruct(hist_q.shape, F32),
                   jax.ShapeDtypeStruct(hist_s.shape, F32), jax.ShapeDtypeStruct(s_in.shape, F32)],
        scratch_shapes=[pltpu.VMEM((n_tiles, rows + SUBLANES, QKV_DIM), F32),
                        pltpu.VMEM((n_tiles, rows + SUBLANES, SC_WIDTH), F32),
                        pltpu.VMEM((_log2(CHUNK), packed_w, packed_w), BF16),
                        pltpu.VMEM((packed_w, GDN_WIDTH), BF16)],
        compiler_params=pltpu.CompilerParams(
            dimension_semantics=("arbitrary", "arbitrary"), vmem_limit_bytes=MIXER_SEQ_VMEM_BYTES),
        name="mixer_seq",
    )(x3, hist_q, hist_s, s_in, *weights)


def _split_in_proj_kernel(wt_ref, w_ab_ref, w_qkv_ref, w_rest_ref):
    n_ab = 2 * GDN_HEADS

    def group(out_ref, first_col):
        for c0 in range(0, out_ref.shape[1], SPLIT_COLS):
            out_ref[:, c0:c0 + SPLIT_COLS] = wt_ref[first_col + c0:first_col + c0 + SPLIT_COLS, :].T.astype(BF16)

    group(w_qkv_ref, 0)
    group(w_rest_ref, QKV_DIM + n_ab)
    lane = lax.broadcasted_iota(jnp.int32, w_ab_ref.shape, 1)
    w_ab_ref[...] = jnp.where(lane < n_ab, wt_ref[QKV_DIM:QKV_DIM + LANES, :].T, 0.0).astype(BF16)


def _split_in_proj_call(w_in):
    rows, cols = w_in.shape
    n_rest = cols - QKV_DIM - 2 * GDN_HEADS
    outs = [jax.ShapeDtypeStruct((rows, n), BF16) for n in (LANES, QKV_DIM, n_rest)]
    full = lambda a: pl.BlockSpec(a.shape, lambda i: (0, 0))
    w_t = w_in.T
    return pl.pallas_call(
        _split_in_proj_kernel,
        grid=(1,),
        in_specs=[full(w_t)],
        out_specs=[full(o) for o in outs],
        out_shape=outs,
        compiler_params=pltpu.CompilerParams(dimension_semantics=("arbitrary",), vmem_limit_bytes=SMALL_CALL_VMEM_BYTES),
        name="split_in_proj",
    )(w_t)


def _proj_call(x2d, weights):
    n = x2d.shape[0]
    outs = [jax.ShapeDtypeStruct((n, w.shape[1]), F32) for w in weights[1:]]
    full = lambda a: pl.BlockSpec(a.shape, lambda i: (0, 0))
    return pl.pallas_call(
        _proj_kernel,
        grid=(1,),
        in_specs=[full(x2d)] + [full(w) for w in weights],
        out_specs=[full(o) for o in outs],
        out_shape=outs,
        compiler_params=pltpu.CompilerParams(dimension_semantics=("arbitrary",), vmem_limit_bytes=SMALL_CALL_VMEM_BYTES),
        name="proj_step",
    )(x2d, *weights)


def _mixer_step_call(ab, qkv, rest, hist_q, hist_s, s_in, weights, *, seg_len):
    tiles, rows, _ = qkv.shape
    tile_map = lambda b, i: (b, 0, 0)
    slab = lambda a: pl.BlockSpec((1, rows, a.shape[2]), tile_map)
    n_state = s_in.shape[1]
    mix = jax.ShapeDtypeStruct((tiles, rows, D_MODEL), BF16)
    stail = jax.ShapeDtypeStruct((tiles, rows, SC_WIDTH), F32)
    return pl.pallas_call(
        functools.partial(_mixer_step_kernel, rows, seg_len),
        grid=(tiles, 1),
        in_specs=[slab(ab), slab(qkv), slab(rest), slab(hist_q), slab(hist_s), _state_spec(1, n_state)]
        + [_const_spec(w.shape) for w in weights],
        out_specs=[slab(mix), slab(stail), _state_spec(1, n_state)],
        out_shape=[mix, stail, jax.ShapeDtypeStruct(s_in.shape, F32)],
        scratch_shapes=[pltpu.VMEM((1, rows + SUBLANES, QKV_DIM), F32), pltpu.VMEM((1, rows + SUBLANES, SC_WIDTH), F32),
                        pltpu.VMEM((1, rows + SUBLANES, QKV_DIM), F32), pltpu.VMEM((1, rows + SUBLANES, SC_WIDTH), F32)],
        compiler_params=pltpu.CompilerParams(
            dimension_semantics=("arbitrary", "arbitrary"), vmem_limit_bytes=SMALL_CALL_VMEM_BYTES),
        name="mixer_step",
    )(ab, qkv, rest, hist_q, hist_s, s_in, *weights)


def _ffn_call(x2d, p2d, weights, *, rows, name, vmem_bytes, mix=None, wout=None):
    n = x2d.shape[0]
    row_map = lambda i: (i, 0)
    rows_spec = lambda width: pl.BlockSpec((rows, width), row_map)
    pre_args, pre_specs, body = [], [], _ffn_kernel
    if mix is not None:
        pre_args, pre_specs, body = [mix, wout], [rows_spec(D_MODEL), _const_spec(wout.shape)], _out_ffn_kernel
    return pl.pallas_call(
        body,
        grid=(n // rows,),
        in_specs=[rows_spec(D_MODEL)] + pre_specs + [rows_spec(PLE_DIM)] + [_const_spec(w.shape) for w in weights],
        out_specs=rows_spec(D_MODEL),
        out_shape=jax.ShapeDtypeStruct(x2d.shape, F32),
        compiler_params=pltpu.CompilerParams(dimension_semantics=("arbitrary",), vmem_limit_bytes=vmem_bytes),
        name=name,
    )(x2d, *pre_args, p2d, *weights)


def _pad_rows(a, rows):
    return jnp.pad(a, ((0, 0), (0, rows - a.shape[1]), (0, 0)))


def _layer(x_prompt, x_sample, conv_qkv, s_gdn, conv_sc, p_prompt, p_sample, norm_mix, w_in, w_conv_qkv, a_log,
           dt_bias, w_gdn_norm, w_conv_sc, w_out, norm_mlp, w_up, w_down, norm_ple, w_ple_gate, w_ple_proj, norm_f):
    bp, tp, _ = x_prompt.shape
    bs, ts, _ = x_sample.shape
    w_ab, w_qkv, w_rest = _split_in_proj_call(w_in)
    wout = w_out.astype(BF16)
    gate = jnp.zeros((SUBLANES, LANES), F32)
    gate = gate.at[0, :GDN_HEADS].set(a_log.astype(F32)).at[1, :GDN_HEADS].set(dt_bias.astype(F32))
    nmix = norm_mix.reshape(1, D_MODEL)
    core_w = (w_conv_qkv, gate, w_gdn_norm.reshape(1, HEAD_DIM), w_conv_sc)
    ffn_w = (norm_mlp.reshape(1, D_MODEL), w_up.astype(BF16), w_down.astype(BF16), norm_ple.reshape(1, D_MODEL),
             w_ple_gate.astype(BF16), w_ple_proj.astype(BF16), norm_f.reshape(1, D_MODEL))

    zq = jnp.zeros((bp, SUBLANES, QKV_DIM), F32)
    zs = jnp.zeros((bp, SUBLANES, SC_WIDTH), F32)
    s0 = jnp.zeros((bp, 1, GDN_HEADS, HEAD_DIM, HEAD_DIM), F32)
    xm_p, qt_p, st_p, s_p = _mixer_seq_call(
        x_prompt, zq, zs, s0, (nmix, w_ab, w_qkv, w_rest) + core_w + (wout,),
        n_tiles=PROMPT_SEQS_PER_STEP, rows=PROMPT_ROWS)
    y_p = _ffn_call(xm_p.reshape(bp * tp, D_MODEL), p_prompt.reshape(bp * tp, PLE_DIM), ffn_w, rows=FFN_ROWS,
                    name="ffn_prompt", vmem_bytes=FFN_VMEM_BYTES)

    seq_per_tile = CHUNK // ts
    tiles = bs // seq_per_tile
    x_s = x_sample.reshape(bs * ts, D_MODEL)
    ab_s, qkv_s, rest_s = _proj_call(x_s, (nmix, w_ab, w_qkv, w_rest))
    tiled = lambda a: a.reshape(tiles, CHUNK, a.shape[-1])
    mix_s, st_s, s_s = _mixer_step_call(
        tiled(ab_s), tiled(qkv_s), tiled(rest_s), tiled(_pad_rows(conv_qkv, ts)), tiled(_pad_rows(conv_sc, ts)),
        s_gdn.reshape(tiles, seq_per_tile, GDN_HEADS, HEAD_DIM, HEAD_DIM), core_w, seg_len=ts)
    y_s = _ffn_call(x_s, p_sample.reshape(bs * ts, PLE_DIM), ffn_w, rows=bs * ts, name="ffn_sample",
                    vmem_bytes=SMALL_CALL_VMEM_BYTES, mix=mix_s.reshape(bs * ts, D_MODEL), wout=wout)

    new_conv_p = qt_p[:, SUBLANES - (GDN_CONV - 1):]
    new_sc_p = st_p[:, SUBLANES - (SC_CONV - 1):]
    new_conv_s = qkv_s.reshape(bs, ts, QKV_DIM)[:, ts - (GDN_CONV - 1):]
    new_sc_s = st_s.reshape(bs, ts, SC_WIDTH)[:, ts - (SC_CONV - 1):]
    return (y_p.reshape(bp, tp, D_MODEL), y_s.reshape(bs, ts, D_MODEL), new_conv_p,
            s_p.reshape(bp, GDN_HEADS, HEAD_DIM, HEAD_DIM), new_sc_p, new_conv_s,
            s_s.reshape(bs, GDN_HEADS, HEAD_DIM, HEAD_DIM), new_sc_s)


def kernel(x_prompt, x_sample, state_gdn_conv, state_gdn, state_sc_conv, p_prompt, p_sample, norm_mix, w_in, w_conv_qkv, a_log, dt_bias, w_gdn_norm, w_conv_sc, w_out, norm_mlp, w_up, w_down, norm_ple, w_ple_gate, w_ple_proj, norm_f):
    depth = w_in.shape[0]
    assert depth == 1, "one layer per call"
    assert x_sample.shape[1] >= GDN_CONV - 1 and CHUNK % x_sample.shape[1] == 0
    assert x_prompt.shape[1] % PROMPT_ROWS == 0 and x_prompt.shape[0] % PROMPT_SEQS_PER_STEP == 0
    outs = _layer(x_prompt, x_sample, state_gdn_conv[0], state_gdn[0], state_sc_conv[0], p_prompt[0], p_sample[0],
                  norm_mix[0], w_in[0], w_conv_qkv[0], a_log[0], dt_bias[0], w_gdn_norm[0], w_conv_sc[0], w_out[0],
                  norm_mlp[0], w_up[0], w_down[0], norm_ple[0], w_ple_gate[0], w_ple_proj[0], norm_f)
    y_p, y_s, c_p, s_p, sc_p, c_s, s_s, sc_s = outs
    return (y_p, y_s, c_p[None], s_p[None], sc_p[None], c_s[None], s_s[None], sc_s[None])
```

```python
import functools

import jax
import jax.numpy as jnp
from jax import lax
from jax.experimental import pallas as pl
from jax.experimental.pallas import tpu as pltpu

F32 = jnp.float32
BF16 = jnp.bfloat16

D_MODEL = 1024
PLE_DIM = 256
GDN_HEADS = 4
HEAD_DIM = 128
GDN_WIDTH = GDN_HEADS * HEAD_DIM
QKV_DIM = 3 * GDN_WIDTH
GDN_CONV = 4
SC_WIDTH = D_MODEL - GDN_WIDTH
SC_CONV = 3
D_FF = 4 * D_MODEL
EPS = 1e-6
NEG_LOG2_E = -1.4426950408889634
CHUNK = 64
LANES = 128
SUBLANES = 8
VMEM_LIMIT_BYTES = 56 * 1024 * 1024
PROMPT_ROWS = 128
PROMPT_SEQS_PER_STEP = 4
FFN_ROWS = 1024
SPLIT_COLS = 256

R_Z = 0
R_SCB = R_Z + GDN_WIDTH
R_SCC = R_SCB + SC_WIDTH
R_SCH = R_SCC + SC_WIDTH


def _mm(a, b):
    return jnp.dot(a.astype(BF16), b.astype(BF16), preferred_element_type=F32)


def _mm_nt(a, b):
    return lax.dot_general(a.astype(BF16), b.astype(BF16), (((1,), (1,)), ((), ())), preferred_element_type=F32)


def _mm_tn(a, b):
    return lax.dot_general(a.astype(BF16), b.astype(BF16), (((0,), (0,)), ((), ())), preferred_element_type=F32)


def _rmsnorm(x, w_row):
    return x * lax.rsqrt(jnp.mean(x * x, axis=-1, keepdims=True) + EPS) * w_row


def _sigmoid(x):
    return 1.0 / (1.0 + jnp.exp2(x * NEG_LOG2_E))


def _silu(x):
    return x * _sigmoid(x)


def _softplus(x):
    return jnp.maximum(x, 0.0) + jnp.log1p(jnp.exp(-jnp.abs(x)))


def _log2(n):
    k = n.bit_length() - 1
    assert (1 << k) == n, n
    return k


def _wy_block(q, k, v, g_col, g_row, beta_col, seg_len, load_state, store_state):
    c = CHUNK
    n_seg = c // seg_len
    lg = _log2(seg_len)
    ri = lax.broadcasted_iota(jnp.int32, (c, c), 0)
    ci = lax.broadcasted_iota(jnp.int32, (c, c), 1)
    same = jnp.right_shift(ri, lg) == jnp.right_shift(ci, lg)
    lower = same & (ri >= ci)
    strict = same & (ri > ci)
    diff = g_col - g_row
    decay = jnp.where(lower, jnp.exp(jnp.where(lower, diff, 0.0)), 0.0)
    a_off = jnp.where(strict, beta_col * _mm_nt(k, k) * decay, 0.0)
    qk = _mm_nt(q, k) * decay
    yield
    eye = jnp.where(ri == ci, 1.0, 0.0).astype(F32)
    x_inv = eye
    for lb in range(lg):
        rb = jnp.right_shift(ri, lb)
        cb = jnp.right_shift(ci, lb)
        join = (jnp.bitwise_and(rb, 1) == 1) & (cb == rb - 1)
        b_lvl = jnp.where(join, a_off, 0.0)
        if lb == 0:
            x_inv = eye - b_lvl
        else:
            xb = _mm(x_inv, b_lvl)
            yield
            x_inv = x_inv - _mm(xb, x_inv)
            yield
    rhs = jnp.concatenate([v * beta_col, k * (beta_col * jnp.exp(g_col))], axis=1)
    sol = _mm(x_inv, rhs)
    yield
    u = sol[:, :HEAD_DIM]
    w = sol[:, HEAD_DIM:]
    stacked = jnp.concatenate([w, q * jnp.exp(g_col)], axis=0)
    row = lax.broadcasted_iota(jnp.int32, (c, 1), 0)
    states = [load_state(s) for s in range(n_seg)]
    w_s = None
    q_s = None
    for s in range(n_seg):
        r = _mm(stacked, states[s])
        if n_seg == 1:
            w_s, q_s = r[:c], r[c:]
        else:
            in_seg = jnp.right_shift(row, lg) == s
            w_s = jnp.where(in_seg, r[:c], 0.0 if w_s is None else w_s)
            q_s = jnp.where(in_seg, r[c:], 0.0 if q_s is None else q_s)
    v_new = u - w_s
    yield
    o = q_s + _mm(qk, v_new)
    for s in range(n_seg):
        last = s * seg_len + seg_len - 1
        g_last = g_col[last:last + 1, :]
        if n_seg == 1:
            k_dec = k * jnp.exp(g_last - g_col)
        else:
            in_seg = jnp.right_shift(row, lg) == s
            k_dec = jnp.where(in_seg, k * jnp.exp(jnp.where(in_seg, g_last - g_col, 0.0)), 0.0)
        store_state(s, states[s] * jnp.exp(g_last) + _mm_tn(k_dec, v_new))
    return o


def _expand_heads(slab, first_lane, width):
    rows = slab.shape[0]
    cols = [slab[:, first_lane + hd:first_lane + hd + 1] for hd in range(GDN_HEADS)]
    if width % LANES == 0:
        return jnp.concatenate([jnp.broadcast_to(c, (rows, width)) for c in cols], axis=1)
    total = GDN_HEADS * width
    lane_head = jnp.right_shift(lax.broadcasted_iota(jnp.int32, (rows, total), 1), _log2(width))
    out = jnp.broadcast_to(cols[-1], (rows, total))
    for hd in range(GDN_HEADS - 2, -1, -1):
        out = jnp.where(lane_head == hd, jnp.broadcast_to(cols[hd], (rows, total)), out)
    return out


def _packed_masks():
    c, nh = CHUNK, GDN_HEADS
    ri = lax.broadcasted_iota(jnp.int32, (c, nh * c), 0)
    cj = jnp.bitwise_and(lax.broadcasted_iota(jnp.int32, (c, nh * c), 1), c - 1)
    return dict(lower=ri >= cj, strict=ri > cj, eye=ri == cj, join0=(jnp.bitwise_and(ri, 1) == 1) & (cj == ri - 1))


def _fill_block_factors(bd_ref, kbd_ref):
    c, nh = CHUNK, GDN_HEADS
    w = nh * c
    lg = _log2(c)
    rw = lax.broadcasted_iota(jnp.int32, (w, w), 0)
    cw = lax.broadcasted_iota(jnp.int32, (w, w), 1)
    same_head = jnp.right_shift(rw, lg) == jnp.right_shift(cw, lg)
    rw, cw = jnp.bitwise_and(rw, c - 1), jnp.bitwise_and(cw, c - 1)
    one_zero = lambda m: jnp.where(m, 1.0, 0.0).astype(BF16)
    bd_ref[0] = one_zero(same_head)
    for lb in range(1, lg):
        bd_ref[lb] = one_zero(same_head & (jnp.bitwise_and(jnp.right_shift(rw, lb), 1) == 1)
                              & (jnp.right_shift(cw, lb) == jnp.right_shift(rw, lb) - 1))
    kbd_ref[...] = one_zero(jnp.right_shift(lax.broadcasted_iota(jnp.int32, (w, nh * HEAD_DIM), 0), lg)
                            == jnp.right_shift(lax.broadcasted_iota(jnp.int32, (w, nh * HEAD_DIM), 1),
                                               _log2(HEAD_DIM)))


def _wy_tile_packed(q_all, k_all, v_all, g, beta, state_ref, rows, masks, bd_ref, kbd_ref):
    c, nh = CHUNK, GDN_HEADS
    lg = _log2(c)
    n_chunks = rows // c
    lower, strict, eye = masks["lower"], masks["strict"], masks["eye"]

    def block_diag(x, factor):
        return jnp.concatenate([x.astype(BF16)] * nh, axis=0) * factor

    def head(x, hd, width):
        return x[:, hd * width:(hd + 1) * width]

    g_c = _expand_heads(g, 0, c)
    b_c = _expand_heads(beta, nh, c)
    g_d = _expand_heads(g, 0, HEAD_DIM)
    b_d = _expand_heads(beta, nh, HEAD_DIM)
    eg_d = jnp.exp(g_d)
    v_rhs = v_all * b_d
    k_rhs = k_all * (b_d * eg_d)
    q_g = q_all * eg_d

    a_off, qk_dec = [], []
    for cb in range(n_chunks):
        rs = slice(cb * c, (cb + 1) * c)
        g_row = jnp.sum(jnp.where(eye, g_c[rs], 0.0), axis=0, keepdims=True)
        decay = jnp.where(lower, jnp.exp(jnp.where(lower, g_c[rs] - g_row, 0.0)), 0.0)
        kq = _mm_nt(jnp.concatenate([k_all[rs], q_all[rs]], axis=0), block_diag(k_all[rs], kbd_ref[...]))
        a_off.append(jnp.where(strict, b_c[rs] * kq[:c] * decay, 0.0))
        qk_dec.append(kq[c:] * decay)
    yield

    eye_f = jnp.where(eye, 1.0, 0.0).astype(F32)
    x_inv = [eye_f - jnp.where(masks["join0"], a, 0.0) for a in a_off]
    for lb in range(1, lg):
        xb = [jnp.dot(x.astype(BF16), block_diag(a, bd_ref[lb]), preferred_element_type=F32)
              for x, a in zip(x_inv, a_off)]
        yield
        xbx = [jnp.dot(t.astype(BF16), block_diag(x, bd_ref[0]), preferred_element_type=F32)
               for t, x in zip(xb, x_inv)]
        x_inv = [x - t for x, t in zip(x_inv, xbx)]
        yield

    u, wk = [], []
    for cb in range(n_chunks):
        rs = slice(cb * c, (cb + 1) * c)
        sol = [_mm(head(x_inv[cb], hd, c),
                   jnp.concatenate([head(v_rhs[rs], hd, HEAD_DIM), head(k_rhs[rs], hd, HEAD_DIM)], axis=1))
               for hd in range(nh)]
        u.append([s[:, :HEAD_DIM] for s in sol])
        wk.append([s[:, HEAD_DIM:] for s in sol])
    yield

    state = [state_ref[0, hd] for hd in range(nh)]
    o_blocks = []
    for cb in range(n_chunks):
        rs = slice(cb * c, (cb + 1) * c)
        g_last = g_d[(cb + 1) * c - 1:(cb + 1) * c, :]
        k_dec = k_all[rs] * jnp.exp(g_last - g_d[rs])
        eg_last = jnp.exp(g_last)
        r = [_mm(jnp.concatenate([wk[cb][hd], head(q_g[rs], hd, HEAD_DIM)], axis=0), state[hd]) for hd in range(nh)]
        v_new = [u[cb][hd] - r[hd][:c] for hd in range(nh)]
        yield
        o_blocks.append(jnp.concatenate(
            [r[hd][c:] + _mm(head(qk_dec[cb], hd, c), v_new[hd]) for hd in range(nh)], axis=1))
        state = [state[hd] * head(eg_last, hd, HEAD_DIM) + _mm_tn(head(k_dec, hd, HEAD_DIM), v_new[hd])
                 for hd in range(nh)]
        yield
    for hd in range(nh):
        state_ref[0, hd] = state[hd]
    return jnp.concatenate(o_blocks, axis=0)


def _run_tiles(programs):
    programs = list(programs)
    results = [None] * len(programs)
    live = list(range(len(programs)))
    while live:
        for idx in list(live):
            try:
                next(programs[idx])
            except StopIteration as done:
                results[idx] = done.value
                live.remove(idx)
    return results


def _shifted_history(buf, rows, n_taps, hist_buf, seg_len):
    out = []
    with_carry = buf[0:SUBLANES + rows, :]
    for s in range(1, n_taps):
        raw = pltpu.roll(with_carry, s, 0)[SUBLANES:, :]
        if hist_buf is not None:
            t = jnp.bitwise_and(lax.broadcasted_iota(jnp.int32, (rows, 1), 0), seg_len - 1)
            raw = jnp.where(t >= s, raw, hist_buf[pl.ds(n_taps - 1 - s, rows), :])
        out.append(raw)
    return out


def _mixer_tile(ti, rows, seg_len, long_seq, masks, r, ab, rest):
    n_chunks = rows // CHUNK
    new_rows = slice(SUBLANES, SUBLANES + rows)
    qb, sb = r.qbuf.at[ti], r.sbuf.at[ti]

    gate = r.gate[...]
    log_a = -jnp.exp(gate[0:1, :]) * _softplus(ab + gate[1:2, :])
    beta = _sigmoid(ab)
    pos = jnp.bitwise_and(lax.broadcasted_iota(jnp.int32, (rows, 1), 0), seg_len - 1)
    g = log_a
    shift = 1
    while shift < seg_len:
        g = g + jnp.where(pos >= shift, pltpu.roll(g, shift, 0), 0.0)
        shift *= 2

    sc_pre = rest[:, R_SCC:R_SCC + SC_WIDTH] * rest[:, R_SCH:R_SCH + SC_WIDTH]
    sb[new_rows, :] = sc_pre
    if long_seq:
        hq_buf = hs_buf = None
    else:
        hq_buf, hs_buf = r.hq_buf.at[ti], r.hs_buf.at[ti]
        zq = jnp.zeros((SUBLANES, QKV_DIM), F32)
        zs = jnp.zeros((SUBLANES, SC_WIDTH), F32)
        qb[0:SUBLANES, :] = zq
        sb[0:SUBLANES, :] = zs
        hq_buf[0:rows, :] = r.hq[ti]
        hq_buf[rows:rows + SUBLANES, :] = zq
        hs_buf[0:rows, :] = r.hs[ti]
        hs_buf[rows:rows + SUBLANES, :] = zs
    qkv_pre = qb[new_rows, :]
    cwq = r.cwq[...]
    if long_seq:
        with_carry = qb[0:SUBLANES + rows, :]
        back1 = pltpu.roll(with_carry, 1, 0)
        pair = with_carry * cwq[1:2, :] + back1 * cwq[0:1, :]
        qkv = _silu(pltpu.roll(pair, 2, 0)[SUBLANES:, :] + back1[SUBLANES:, :] * cwq[2:3, :]
                    + qkv_pre * cwq[3:4, :])
    else:
        q1, q2, q3 = _shifted_history(qb, rows, GDN_CONV, hq_buf, seg_len)
        qkv = _silu(q3 * cwq[0:1, :] + q2 * cwq[1:2, :] + q1 * cwq[2:3, :] + qkv_pre * cwq[3:4, :])
    if long_seq:
        r.qtail[ti] = qb[rows:rows + SUBLANES, :]
        r.stail[ti] = sb[rows:rows + SUBLANES, :]
    else:
        r.stail[ti] = sc_pre
    q_n, k_n, v_n = [], [], []
    for hd in range(GDN_HEADS):
        lo = hd * HEAD_DIM
        q_h = qkv[:, lo:lo + HEAD_DIM]
        k_h = qkv[:, GDN_WIDTH + lo:GDN_WIDTH + lo + HEAD_DIM]
        q_n.append(q_h * (lax.rsqrt(jnp.sum(q_h * q_h, axis=-1, keepdims=True) + EPS) * (HEAD_DIM ** -0.5)))
        k_n.append(k_h * lax.rsqrt(jnp.sum(k_h * k_h, axis=-1, keepdims=True) + EPS))
        v_n.append(qkv[:, 2 * GDN_WIDTH + lo:2 * GDN_WIDTH + lo + HEAD_DIM])
    yield

    state_ref = r.s_out.at[ti]
    if long_seq:
        o_all = yield from _wy_tile_packed(jnp.concatenate(q_n, axis=1), jnp.concatenate(k_n, axis=1),
                                           jnp.concatenate(v_n, axis=1), g, beta, state_ref, rows, masks,
                                           r.bd, r.kbd)
        o_h = [o_all[:, hd * HEAD_DIM:(hd + 1) * HEAD_DIM] for hd in range(GDN_HEADS)]
    else:
        g_t = g.T
        n_seg = CHUNK // seg_len
        blocks = []
        for cb in range(n_chunks):
            r0 = cb * CHUNK
            for hd in range(GDN_HEADS):

                def load_state(s, hd=hd, base=cb * n_seg):
                    return state_ref[base + s, hd]

                def store_state(s, val, hd=hd, base=cb * n_seg):
                    state_ref[base + s, hd] = val

                blocks.append(_wy_block(
                    q_n[hd][r0:r0 + CHUNK], k_n[hd][r0:r0 + CHUNK], v_n[hd][r0:r0 + CHUNK],
                    g[r0:r0 + CHUNK, hd:hd + 1], g_t[hd:hd + 1, r0:r0 + CHUNK],
                    beta[r0:r0 + CHUNK, GDN_HEADS + hd:GDN_HEADS + hd + 1],
                    seg_len, load_state, store_state))
        o_blocks = _run_tiles(blocks)
        o_h = [jnp.concatenate(o_blocks[hd::GDN_HEADS], axis=0) for hd in range(GDN_HEADS)]
    yield

    s1, s2 = _shifted_history(sb, rows, SC_CONV, hs_buf, seg_len)
    cws = r.cws[...]
    y_sc = rest[:, R_SCB:R_SCB + SC_WIDTH] * (s2 * cws[0:1, :] + s1 * cws[1:2, :] + sc_pre * cws[2:3, :])
    if long_seq:
        qb[0:SUBLANES, :] = qb[rows:rows + SUBLANES, :]
        sb[0:SUBLANES, :] = sb[rows:rows + SUBLANES, :]
    gnorm = r.gnorm[...]
    o_heads = []
    for hd in range(GDN_HEADS):
        z_h = rest[:, R_Z + hd * HEAD_DIM:R_Z + (hd + 1) * HEAD_DIM]
        o_heads.append(_rmsnorm(o_h[hd], gnorm) * _silu(z_h))
    return jnp.concatenate(o_heads + [y_sc], axis=1)


class _Refs:
    def __init__(self, **refs):
        self.__dict__.update(refs)


def _mixer_seq_kernel(n_tiles, rows, n_cast, *refs):
    n_in = 13
    x, hq, hs, s_in, nmix, w_ab, w_qkv, w_rest, cwq, gate, gnorm, cws, wout = refs[:n_in]
    cast_in = refs[n_in:n_in + n_cast]
    xmid, qtail, stail, s_out = refs[n_in + n_cast:n_in + n_cast + 4]
    cast_out = refs[n_in + n_cast + 4:n_in + 2 * n_cast + 4]
    qbuf, sbuf, bd, kbd = refs[n_in + 2 * n_cast + 4:]
    for src, dst in zip(cast_in, cast_out):
        dst[...] = src[...].astype(BF16)

    @pl.when(pl.program_id(1) == 0)
    def _():
        s_out[...] = s_in[...]
        qbuf[:, 0:SUBLANES, :] = hq[...]
        sbuf[:, 0:SUBLANES, :] = hs[...]
        _fill_block_factors(bd, kbd)

    x_all = x[...].reshape(n_tiles * rows, D_MODEL)
    h = _rmsnorm(x_all, nmix[...]).astype(BF16)
    ab = jnp.dot(h, w_ab[...], preferred_element_type=F32)
    qbuf[:, SUBLANES:SUBLANES + rows, :] = jnp.dot(h, w_qkv[...], preferred_element_type=F32).reshape(
        n_tiles, rows, QKV_DIM)
    rest = jnp.dot(h, w_rest[...], preferred_element_type=F32)
    r = _Refs(cwq=cwq, gate=gate, gnorm=gnorm, cws=cws, qtail=qtail, stail=stail, s_out=s_out, qbuf=qbuf, sbuf=sbuf,
              bd=bd, kbd=kbd)
    masks = _packed_masks()
    tile = lambda a, ti: a[ti * rows:(ti + 1) * rows]
    mix = _run_tiles(_mixer_tile(ti, rows, CHUNK, True, masks, r, tile(ab, ti), tile(rest, ti))
                     for ti in range(n_tiles))
    xmid[...] = (x_all + _mm(jnp.concatenate(mix, axis=0), wout[...])).reshape(n_tiles, rows, D_MODEL)


def _mixer_step_kernel(rows, seg_len,
                       ab, qkv, rest, hq, hs, s_in, cwq, gate, gnorm, cws,
                       mix, stail, s_out, qbuf, sbuf, hq_buf, hs_buf):
    s_out[...] = s_in[...]
    qbuf[0, SUBLANES:SUBLANES + rows, :] = qkv[0]
    r = _Refs(hq=hq, hs=hs, cwq=cwq, gate=gate, gnorm=gnorm, cws=cws, stail=stail, s_out=s_out, qbuf=qbuf, sbuf=sbuf,
              hq_buf=hq_buf, hs_buf=hs_buf)
    mix[0] = _run_tiles([_mixer_tile(0, rows, seg_len, False, None, r, ab[0], rest[0])])[0].astype(BF16)


def _proj_kernel(x_ref, nmix_ref, w_ab_ref, w_qkv_ref, w_rest_ref, ab_ref, qkv_ref, rest_ref):
    h = _rmsnorm(x_ref[...], nmix_ref[...]).astype(BF16)
    ab_ref[...] = jnp.dot(h, w_ab_ref[...], preferred_element_type=F32)
    qkv_ref[...] = jnp.dot(h, w_qkv_ref[...], preferred_element_type=F32)
    rest_ref[...] = jnp.dot(h, w_rest_ref[...], preferred_element_type=F32)


def _ffn_body(x, p_ref, nmlp_ref, wup_ref, wdown_ref, nple_ref, wg_ref, wp_ref, nf_ref, y_ref):
    hn = _rmsnorm(x, nmlp_ref[...]).astype(BF16)
    acc = x
    for j in range(D_FF // D_MODEL):
        u = jnp.maximum(_mm(hn, wup_ref[:, j * D_MODEL:(j + 1) * D_MODEL]), 0.0)
        acc = acc + _mm(u * u, wdown_ref[j * D_MODEL:(j + 1) * D_MODEL, :])
    gate = _sigmoid(_mm(_rmsnorm(acc, nple_ref[...]), wg_ref[...]))
    x3 = acc + gate * _mm(p_ref[...], wp_ref[...])
    y_ref[...] = _rmsnorm(x3, nf_ref[...])


def _ffn_kernel(x_ref, *rest):
    _ffn_body(x_ref[...], *rest)


def _out_ffn_kernel(x_ref, mix_ref, wout_ref, *rest):
    _ffn_body(x_ref[...] + jnp.dot(mix_ref[...], wout_ref[...], preferred_element_type=F32), *rest)


def _const_spec(shape):
    nd = len(shape)
    return pl.BlockSpec(shape, lambda *_: (0,) * nd, pipeline_mode=pl.Buffered(1))


def _state_spec(n_tiles, n_state):
    return pl.BlockSpec((n_tiles, n_state, GDN_HEADS, HEAD_DIM, HEAD_DIM), lambda b, i: (b, 0, 0, 0, 0))


def _mixer_seq_call(x3, hist_q, hist_s, s_in, weights, to_cast, *, n_tiles, rows):
    seqs, seq_rows, _ = x3.shape
    tiles = seq_rows // rows
    n_steps = (seqs // n_tiles) * tiles
    cast_specs = [pl.BlockSpec((w.shape[0] // n_steps, w.shape[1]), lambda b, i: (b * tiles + i, 0)) for w in to_cast]
    row_map = lambda b, i: (b, i, 0)
    seq_map = lambda b, i: (b, 0, 0)
    x_spec = pl.BlockSpec((n_tiles, rows, D_MODEL), row_map)
    hq_spec = pl.BlockSpec((n_tiles, SUBLANES, QKV_DIM), seq_map)
    hs_spec = pl.BlockSpec((n_tiles, SUBLANES, SC_WIDTH), seq_map)
    packed_w = GDN_HEADS * CHUNK
    return pl.pallas_call(
        functools.partial(_mixer_seq_kernel, n_tiles, rows, len(to_cast)),
        grid=(seqs // n_tiles, tiles),
        in_specs=[x_spec, hq_spec, hs_spec, _state_spec(n_tiles, 1)] + [_const_spec(w.shape) for w in weights]
        + cast_specs,
        out_specs=[x_spec, hq_spec, hs_spec, _state_spec(n_tiles, 1)] + cast_specs,
        out_shape=[jax.ShapeDtypeStruct(x3.shape, F32), jax.ShapeDtypeStruct(hist_q.shape, F32),
                   jax.ShapeDtypeStruct(hist_s.shape, F32), jax.ShapeDtypeStruct(s_in.shape, F32)]
        + [jax.ShapeDtypeStruct(w.shape, BF16) for w in to_cast],
        scratch_shapes=[pltpu.VMEM((n_tiles, rows + SUBLANES, QKV_DIM), F32),
                        pltpu.VMEM((n_tiles, rows + SUBLANES, SC_WIDTH), F32),
                        pltpu.VMEM((_log2(CHUNK), packed_w, packed_w), BF16),
                        pltpu.VMEM((packed_w, GDN_WIDTH), BF16)],
        compiler_params=pltpu.CompilerParams(
            dimension_semantics=("arbitrary", "arbitrary"), vmem_limit_bytes=VMEM_LIMIT_BYTES),
        name="mixer_seq",
    )(x3, hist_q, hist_s, s_in, *weights, *to_cast)


def _split_in_proj_kernel(wt_ref, w_ab_ref, w_qkv_ref, w_rest_ref):
    n_ab = 2 * GDN_HEADS

    def group(out_ref, first_col):
        for c0 in range(0, out_ref.shape[1], SPLIT_COLS):
            out_ref[:, c0:c0 + SPLIT_COLS] = wt_ref[first_col + c0:first_col + c0 + SPLIT_COLS, :].T.astype(BF16)

    group(w_qkv_ref, 0)
    group(w_rest_ref, QKV_DIM + n_ab)
    lane = lax.broadcasted_iota(jnp.int32, w_ab_ref.shape, 1)
    w_ab_ref[...] = jnp.where(lane < n_ab, wt_ref[QKV_DIM:QKV_DIM + LANES, :].T, 0.0).astype(BF16)


def _split_in_proj_call(w_in):
    rows, cols = w_in.shape
    n_rest = cols - QKV_DIM - 2 * GDN_HEADS
    outs = [jax.ShapeDtypeStruct((rows, n), BF16) for n in (LANES, QKV_DIM, n_rest)]
    full = lambda a: pl.BlockSpec(a.shape, lambda i: (0, 0))
    w_t = w_in.T
    return pl.pallas_call(
        _split_in_proj_kernel,
        grid=(1,),
        in_specs=[full(w_t)],
        out_specs=[full(o) for o in outs],
        out_shape=outs,
        compiler_params=pltpu.CompilerParams(dimension_semantics=("arbitrary",), vmem_limit_bytes=VMEM_LIMIT_BYTES),
        name="split_in_proj",
    )(w_t)


def _proj_call(x2d, weights):
    n = x2d.shape[0]
    outs = [jax.ShapeDtypeStruct((n, w.shape[1]), F32) for w in weights[1:]]
    full = lambda a: pl.BlockSpec(a.shape, lambda i: (0, 0))
    return pl.pallas_call(
        _proj_kernel,
        grid=(1,),
        in_specs=[full(x2d)] + [full(w) for w in weights],
        out_specs=[full(o) for o in outs],
        out_shape=outs,
        compiler_params=pltpu.CompilerParams(dimension_semantics=("arbitrary",), vmem_limit_bytes=VMEM_LIMIT_BYTES),
        name="proj_step",
    )(x2d, *weights)


def _mixer_step_call(ab, qkv, rest, hist_q, hist_s, s_in, weights, *, seg_len):
    tiles, rows, _ = qkv.shape
    tile_map = lambda b, i: (b, 0, 0)
    slab = lambda a: pl.BlockSpec((1, rows, a.shape[2]), tile_map)
    n_state = s_in.shape[1]
    mix = jax.ShapeDtypeStruct((tiles, rows, D_MODEL), BF16)
    stail = jax.ShapeDtypeStruct((tiles, rows, SC_WIDTH), F32)
    return pl.pallas_call(
        functools.partial(_mixer_step_kernel, rows, seg_len),
        grid=(tiles, 1),
        in_specs=[slab(ab), slab(qkv), slab(rest), slab(hist_q), slab(hist_s), _state_spec(1, n_state)]
        + [_const_spec(w.shape) for w in weights],
        out_specs=[slab(mix), slab(stail), _state_spec(1, n_state)],
        out_shape=[mix, stail, jax.ShapeDtypeStruct(s_in.shape, F32)],
        scratch_shapes=[pltpu.VMEM((1, rows + SUBLANES, QKV_DIM), F32), pltpu.VMEM((1, rows + SUBLANES, SC_WIDTH), F32),
                        pltpu.VMEM((1, rows + SUBLANES, QKV_DIM), F32), pltpu.VMEM((1, rows + SUBLANES, SC_WIDTH), F32)],
        compiler_params=pltpu.CompilerParams(
            dimension_semantics=("arbitrary", "arbitrary"), vmem_limit_bytes=VMEM_LIMIT_BYTES),
        name="mixer_step",
    )(ab, qkv, rest, hist_q, hist_s, s_in, *weights)


def _ffn_call(x2d, p2d, weights, *, rows, name, mix=None, wout=None):
    n = x2d.shape[0]
    row_map = lambda i: (i, 0)
    rows_spec = lambda width: pl.BlockSpec((rows, width), row_map)
    pre_args, pre_specs, body = [], [], _ffn_kernel
    if mix is not None:
        pre_args, pre_specs, body = [mix, wout], [rows_spec(D_MODEL), _const_spec(wout.shape)], _out_ffn_kernel
    return pl.pallas_call(
        body,
        grid=(n // rows,),
        in_specs=[rows_spec(D_MODEL)] + pre_specs + [rows_spec(PLE_DIM)] + [_const_spec(w.shape) for w in weights],
        out_specs=rows_spec(D_MODEL),
        out_shape=jax.ShapeDtypeStruct(x2d.shape, F32),
        compiler_params=pltpu.CompilerParams(dimension_semantics=("arbitrary",), vmem_limit_bytes=VMEM_LIMIT_BYTES),
        name=name,
    )(x2d, *pre_args, p2d, *weights)


def _pad_rows(a, rows):
    return jnp.pad(a, ((0, 0), (0, rows - a.shape[1]), (0, 0)))


def _layer(x_prompt, x_sample, conv_qkv, s_gdn, conv_sc, p_prompt, p_sample, norm_mix, w_in, w_conv_qkv, a_log,
           dt_bias, w_gdn_norm, w_conv_sc, w_out, norm_mlp, w_up, w_down, norm_ple, w_ple_gate, w_ple_proj, norm_f):
    bp, tp, _ = x_prompt.shape
    bs, ts, _ = x_sample.shape
    w_ab, w_qkv, w_rest = _split_in_proj_call(w_in)
    wout = w_out.astype(BF16)
    gate = jnp.zeros((SUBLANES, LANES), F32)
    gate = gate.at[0, :GDN_HEADS].set(a_log.astype(F32)).at[1, :GDN_HEADS].set(dt_bias.astype(F32))
    nmix = norm_mix.reshape(1, D_MODEL)
    core_w = (w_conv_qkv, gate, w_gdn_norm.reshape(1, HEAD_DIM), w_conv_sc)

    zq = jnp.zeros((bp, SUBLANES, QKV_DIM), F32)
    zs = jnp.zeros((bp, SUBLANES, SC_WIDTH), F32)
    s0 = jnp.zeros((bp, 1, GDN_HEADS, HEAD_DIM, HEAD_DIM), F32)
    xm_p, qt_p, st_p, s_p, wup, wdown, wgate = _mixer_seq_call(
        x_prompt, zq, zs, s0, (nmix, w_ab, w_qkv, w_rest) + core_w + (wout,), (w_up, w_down, w_ple_gate),
        n_tiles=PROMPT_SEQS_PER_STEP, rows=PROMPT_ROWS)
    ffn_w = (norm_mlp.reshape(1, D_MODEL), wup, wdown, norm_ple.reshape(1, D_MODEL), wgate,
             w_ple_proj.astype(BF16), norm_f.reshape(1, D_MODEL))
    y_p = _ffn_call(xm_p.reshape(bp * tp, D_MODEL), p_prompt.reshape(bp * tp, PLE_DIM), ffn_w, rows=FFN_ROWS,
                    name="ffn_prompt")

    seq_per_tile = CHUNK // ts
    tiles = bs // seq_per_tile
    x_s = x_sample.reshape(bs * ts, D_MODEL)
    ab_s, qkv_s, rest_s = _proj_call(x_s, (nmix, w_ab, w_qkv, w_rest))
    tiled = lambda a: a.reshape(tiles, CHUNK, a.shape[-1])
    mix_s, st_s, s_s = _mixer_step_call(
        tiled(ab_s), tiled(qkv_s), tiled(rest_s), tiled(_pad_rows(conv_qkv, ts)), tiled(_pad_rows(conv_sc, ts)),
        s_gdn.reshape(tiles, seq_per_tile, GDN_HEADS, HEAD_DIM, HEAD_DIM), core_w, seg_len=ts)
    y_s = _ffn_call(x_s, p_sample.reshape(bs * ts, PLE_DIM), ffn_w, rows=bs * ts, name="ffn_sample",
                    mix=mix_s.reshape(bs * ts, D_MODEL), wout=wout)

    new_conv_p = qt_p[:, SUBLANES - (GDN_CONV - 1):]
    new_sc_p = st_p[:, SUBLANES - (SC_CONV - 1):]
    new_conv_s = qkv_s.reshape(bs, ts, QKV_DIM)[:, ts - (GDN_CONV - 1):]
    new_sc_s = st_s.reshape(bs, ts, SC_WIDTH)[:, ts - (SC_CONV - 1):]
    return (y_p.reshape(bp, tp, D_MODEL), y_s.reshape(bs, ts, D_MODEL), new_conv_p,
            s_p.reshape(bp, GDN_HEADS, HEAD_DIM, HEAD_DIM), new_sc_p, new_conv_s,
            s_s.reshape(bs, GDN_HEADS, HEAD_DIM, HEAD_DIM), new_sc_s)


def kernel(x_prompt, x_sample, state_gdn_conv, state_gdn, state_sc_conv, p_prompt, p_sample, norm_mix, w_in, w_conv_qkv, a_log, dt_bias, w_gdn_norm, w_conv_sc, w_out, norm_mlp, w_up, w_down, norm_ple, w_ple_gate, w_ple_proj, norm_f):
    depth = w_in.shape[0]
    assert depth == 1, "one layer per call"
    assert x_sample.shape[1] >= GDN_CONV - 1 and CHUNK % x_sample.shape[1] == 0
    assert x_prompt.shape[1] % PROMPT_ROWS == 0 and x_prompt.shape[0] % PROMPT_SEQS_PER_STEP == 0
    outs = _layer(x_prompt, x_sample, state_gdn_conv[0], state_gdn[0], state_sc_conv[0], p_prompt[0], p_sample[0],
                  norm_mix[0], w_in[0], w_conv_qkv[0], a_log[0], dt_bias[0], w_gdn_norm[0], w_conv_sc[0], w_out[0],
                  norm_mlp[0], w_up[0], w_down[0], norm_ple[0], w_ple_gate[0], w_ple_proj[0], norm_f)
    y_p, y_s, c_p, s_p, sc_p, c_s, s_s, sc_s = outs
    return (y_p, y_s, c_p[None], s_p[None], sc_p[None], c_s[None], s_s[None], sc_s[None])
```

```python
import functools

import jax
import jax.numpy as jnp
from jax import lax
from jax.experimental import pallas as pl
from jax.experimental.pallas import tpu as pltpu

F32 = jnp.float32
BF16 = jnp.bfloat16

D_MODEL = 1024
PLE_DIM = 256
GDN_HEADS = 4
HEAD_DIM = 128
GDN_WIDTH = GDN_HEADS * HEAD_DIM
QKV_DIM = 3 * GDN_WIDTH
GDN_CONV = 4
SC_WIDTH = D_MODEL - GDN_WIDTH
SC_CONV = 3
D_FF = 4 * D_MODEL
EPS = 1e-6
NEG_LOG2_E = -1.4426950408889634
CHUNK = 64
LANES = 128
SUBLANES = 8
VMEM_LIMIT_BYTES = 56 * 1024 * 1024
PROMPT_ROWS = 128
PROMPT_SEQS_PER_STEP = 4
FFN_ROWS = 1024
SPLIT_COLS = 256

R_Z = 0
R_SCB = R_Z + GDN_WIDTH
R_SCC = R_SCB + SC_WIDTH
R_SCH = R_SCC + SC_WIDTH


def _mm(a, b):
    return jnp.dot(a.astype(BF16), b.astype(BF16), preferred_element_type=F32)


def _mm_nt(a, b):
    return lax.dot_general(a.astype(BF16), b.astype(BF16), (((1,), (1,)), ((), ())), preferred_element_type=F32)


def _mm_tn(a, b):
    return lax.dot_general(a.astype(BF16), b.astype(BF16), (((0,), (0,)), ((), ())), preferred_element_type=F32)


def _rmsnorm(x, w_row):
    return x * lax.rsqrt(jnp.mean(x * x, axis=-1, keepdims=True) + EPS) * w_row


def _sigmoid(x):
    return 1.0 / (1.0 + jnp.exp2(x * NEG_LOG2_E))


def _silu(x):
    return x * _sigmoid(x)


def _softplus(x):
    return jnp.maximum(x, 0.0) + jnp.log1p(jnp.exp(-jnp.abs(x)))


def _log2(n):
    k = n.bit_length() - 1
    assert (1 << k) == n, n
    return k


def _wy_block(q, k, v, g_col, g_row, beta_col, seg_len, load_state, store_state):
    c = CHUNK
    n_seg = c // seg_len
    lg = _log2(seg_len)
    ri = lax.broadcasted_iota(jnp.int32, (c, c), 0)
    ci = lax.broadcasted_iota(jnp.int32, (c, c), 1)
    same = jnp.right_shift(ri, lg) == jnp.right_shift(ci, lg)
    lower = same & (ri >= ci)
    strict = same & (ri > ci)
    diff = g_col - g_row
    decay = jnp.where(lower, jnp.exp(jnp.where(lower, diff, 0.0)), 0.0)
    a_off = jnp.where(strict, beta_col * _mm_nt(k, k) * decay, 0.0)
    qk = _mm_nt(q, k) * decay
    yield
    eye = jnp.where(ri == ci, 1.0, 0.0).astype(F32)
    x_inv = eye
    for lb in range(lg):
        rb = jnp.right_shift(ri, lb)
        cb = jnp.right_shift(ci, lb)
        join = (jnp.bitwise_and(rb, 1) == 1) & (cb == rb - 1)
        b_lvl = jnp.where(join, a_off, 0.0)
        if lb == 0:
            x_inv = eye - b_lvl
        else:
            xb = _mm(x_inv, b_lvl)
            yield
            x_inv = x_inv - _mm(xb, x_inv)
            yield
    rhs = jnp.concatenate([v * beta_col, k * (beta_col * jnp.exp(g_col))], axis=1)
    sol = _mm(x_inv, rhs)
    yield
    u = sol[:, :HEAD_DIM]
    w = sol[:, HEAD_DIM:]
    stacked = jnp.concatenate([w, q * jnp.exp(g_col)], axis=0)
    row = lax.broadcasted_iota(jnp.int32, (c, 1), 0)
    states = [load_state(s) for s in range(n_seg)]
    w_s = None
    q_s = None
    for s in range(n_seg):
        r = _mm(stacked, states[s])
        if n_seg == 1:
            w_s, q_s = r[:c], r[c:]
        else:
            in_seg = jnp.right_shift(row, lg) == s
            w_s = jnp.where(in_seg, r[:c], 0.0 if w_s is None else w_s)
            q_s = jnp.where(in_seg, r[c:], 0.0 if q_s is None else q_s)
    v_new = u - w_s
    yield
    o = q_s + _mm(qk, v_new)
    for s in range(n_seg):
        last = s * seg_len + seg_len - 1
        g_last = g_col[last:last + 1, :]
        if n_seg == 1:
            k_dec = k * jnp.exp(g_last - g_col)
        else:
            in_seg = jnp.right_shift(row, lg) == s
            k_dec = jnp.where(in_seg, k * jnp.exp(jnp.where(in_seg, g_last - g_col, 0.0)), 0.0)
        store_state(s, states[s] * jnp.exp(g_last) + _mm_tn(k_dec, v_new))
    return o


def _expand_heads(slab, first_lane, width):
    rows = slab.shape[0]
    cols = [slab[:, first_lane + hd:first_lane + hd + 1] for hd in range(GDN_HEADS)]
    if width % LANES == 0:
        return jnp.concatenate([jnp.broadcast_to(c, (rows, width)) for c in cols], axis=1)
    total = GDN_HEADS * width
    lane_head = jnp.right_shift(lax.broadcasted_iota(jnp.int32, (rows, total), 1), _log2(width))
    out = jnp.broadcast_to(cols[-1], (rows, total))
    for hd in range(GDN_HEADS - 2, -1, -1):
        out = jnp.where(lane_head == hd, jnp.broadcast_to(cols[hd], (rows, total)), out)
    return out


def _packed_masks():
    c, nh = CHUNK, GDN_HEADS
    ri = lax.broadcasted_iota(jnp.int32, (c, nh * c), 0)
    cj = jnp.bitwise_and(lax.broadcasted_iota(jnp.int32, (c, nh * c), 1), c - 1)
    return dict(lower=ri >= cj, strict=ri > cj, eye=ri == cj, join0=(jnp.bitwise_and(ri, 1) == 1) & (cj == ri - 1))


def _fill_block_factors(bd_ref, kbd_ref):
    c, nh = CHUNK, GDN_HEADS
    w = nh * c
    lg = _log2(c)
    rw = lax.broadcasted_iota(jnp.int32, (w, w), 0)
    cw = lax.broadcasted_iota(jnp.int32, (w, w), 1)
    same_head = jnp.right_shift(rw, lg) == jnp.right_shift(cw, lg)
    rw, cw = jnp.bitwise_and(rw, c - 1), jnp.bitwise_and(cw, c - 1)
    one_zero = lambda m: jnp.where(m, 1.0, 0.0).astype(BF16)
    bd_ref[0] = one_zero(same_head)
    for lb in range(1, lg):
        bd_ref[lb] = one_zero(same_head & (jnp.bitwise_and(jnp.right_shift(rw, lb), 1) == 1)
                              & (jnp.right_shift(cw, lb) == jnp.right_shift(rw, lb) - 1))
    kbd_ref[...] = one_zero(jnp.right_shift(lax.broadcasted_iota(jnp.int32, (w, nh * HEAD_DIM), 0), lg)
                            == jnp.right_shift(lax.broadcasted_iota(jnp.int32, (w, nh * HEAD_DIM), 1),
                                               _log2(HEAD_DIM)))


def _wy_tile_packed(q_all, k_all, v_all, g, beta, state_ref, rows, masks, bd_ref, kbd_ref):
    c, nh = CHUNK, GDN_HEADS
    lg = _log2(c)
    n_chunks = rows // c
    lower, strict, eye = masks["lower"], masks["strict"], masks["eye"]

    def block_diag(x, factor):
        return jnp.concatenate([x.astype(BF16)] * nh, axis=0) * factor

    def head(x, hd, width):
        return x[:, hd * width:(hd + 1) * width]

    g_c = _expand_heads(g, 0, c)
    b_c = _expand_heads(beta, nh, c)
    g_d = _expand_heads(g, 0, HEAD_DIM)
    b_d = _expand_heads(beta, nh, HEAD_DIM)
    eg_d = jnp.exp(g_d)
    v_rhs = v_all * b_d
    k_rhs = k_all * (b_d * eg_d)
    q_g = q_all * eg_d

    a_off, qk_dec = [], []
    for cb in range(n_chunks):
        rs = slice(cb * c, (cb + 1) * c)
        g_row = jnp.sum(jnp.where(eye, g_c[rs], 0.0), axis=0, keepdims=True)
        decay = jnp.where(lower, jnp.exp(jnp.where(lower, g_c[rs] - g_row, 0.0)), 0.0)
        kq = _mm_nt(jnp.concatenate([k_all[rs], q_all[rs]], axis=0), block_diag(k_all[rs], kbd_ref[...]))
        a_off.append(jnp.where(strict, b_c[rs] * kq[:c] * decay, 0.0))
        qk_dec.append(kq[c:] * decay)
    yield

    eye_f = jnp.where(eye, 1.0, 0.0).astype(F32)
    x_inv = [eye_f - jnp.where(masks["join0"], a, 0.0) for a in a_off]
    for lb in range(1, lg):
        xb = [jnp.dot(x.astype(BF16), block_diag(a, bd_ref[lb]), preferred_element_type=F32)
              for x, a in zip(x_inv, a_off)]
        yield
        xbx = [jnp.dot(t.astype(BF16), block_diag(x, bd_ref[0]), preferred_element_type=F32)
               for t, x in zip(xb, x_inv)]
        x_inv = [x - t for x, t in zip(x_inv, xbx)]
        yield

    u, wk = [], []
    for cb in range(n_chunks):
        rs = slice(cb * c, (cb + 1) * c)
        sol = [_mm(head(x_inv[cb], hd, c),
                   jnp.concatenate([head(v_rhs[rs], hd, HEAD_DIM), head(k_rhs[rs], hd, HEAD_DIM)], axis=1))
               for hd in range(nh)]
        u.append([s[:, :HEAD_DIM] for s in sol])
        wk.append([s[:, HEAD_DIM:] for s in sol])
    yield

    state = [state_ref[0, hd] for hd in range(nh)]
    o_blocks = []
    for cb in range(n_chunks):
        rs = slice(cb * c, (cb + 1) * c)
        g_last = g_d[(cb + 1) * c - 1:(cb + 1) * c, :]
        k_dec = k_all[rs] * jnp.exp(g_last - g_d[rs])
        eg_last = jnp.exp(g_last)
        r = [_mm(jnp.concatenate([wk[cb][hd], head(q_g[rs], hd, HEAD_DIM)], axis=0), state[hd]) for hd in range(nh)]
        v_new = [u[cb][hd] - r[hd][:c] for hd in range(nh)]
        yield
        o_blocks.append(jnp.concatenate(
            [r[hd][c:] + _mm(head(qk_dec[cb], hd, c), v_new[hd]) for hd in range(nh)], axis=1))
        state = [state[hd] * head(eg_last, hd, HEAD_DIM) + _mm_tn(head(k_dec, hd, HEAD_DIM), v_new[hd])
                 for hd in range(nh)]
        yield
    for hd in range(nh):
        state_ref[0, hd] = state[hd]
    return jnp.concatenate(o_blocks, axis=0)


def _run_tiles(programs):
    programs = list(programs)
    results = [None] * len(programs)
    live = list(range(len(programs)))
    while live:
        for idx in list(live):
            try:
                next(programs[idx])
            except StopIteration as done:
                results[idx] = done.value
                live.remove(idx)
    return results


def _shifted_history(buf, rows, n_taps, hist_buf, seg_len):
    out = []
    with_carry = buf[0:SUBLANES + rows, :]
    for s in range(1, n_taps):
        raw = pltpu.roll(with_carry, s, 0)[SUBLANES:, :]
        if hist_buf is not None:
            t = jnp.bitwise_and(lax.broadcasted_iota(jnp.int32, (rows, 1), 0), seg_len - 1)
            raw = jnp.where(t >= s, raw, hist_buf[pl.ds(n_taps - 1 - s, rows), :])
        out.append(raw)
    return out


def _mixer_tile(ti, rows, seg_len, long_seq, masks, r, ab, rest):
    n_chunks = rows // CHUNK
    new_rows = slice(SUBLANES, SUBLANES + rows)
    qb, sb = r.qbuf.at[ti], r.sbuf.at[ti]

    gate = r.gate[...]
    log_a = -jnp.exp(gate[0:1, :]) * _softplus(ab + gate[1:2, :])
    beta = _sigmoid(ab)
    pos = jnp.bitwise_and(lax.broadcasted_iota(jnp.int32, (rows, 1), 0), seg_len - 1)
    g = log_a
    shift = 1
    while shift < seg_len:
        g = g + jnp.where(pos >= shift, pltpu.roll(g, shift, 0), 0.0)
        shift *= 2

    sc_pre = rest[:, R_SCC:R_SCC + SC_WIDTH] * rest[:, R_SCH:R_SCH + SC_WIDTH]
    sb[new_rows, :] = sc_pre
    if long_seq:
        hq_buf = hs_buf = None
    else:
        hq_buf, hs_buf = r.hq_buf.at[ti], r.hs_buf.at[ti]
        zq = jnp.zeros((SUBLANES, QKV_DIM), F32)
        zs = jnp.zeros((SUBLANES, SC_WIDTH), F32)
        qb[0:SUBLANES, :] = zq
        sb[0:SUBLANES, :] = zs
        hq_buf[0:rows, :] = r.hq[ti]
        hq_buf[rows:rows + SUBLANES, :] = zq
        hs_buf[0:rows, :] = r.hs[ti]
        hs_buf[rows:rows + SUBLANES, :] = zs
    qkv_pre = qb[new_rows, :]
    cwq = r.cwq[...]
    if long_seq:
        with_carry = qb[0:SUBLANES + rows, :]
        back1 = pltpu.roll(with_carry, 1, 0)
        pair = with_carry * cwq[1:2, :] + back1 * cwq[0:1, :]
        qkv = _silu(pltpu.roll(pair, 2, 0)[SUBLANES:, :] + back1[SUBLANES:, :] * cwq[2:3, :]
                    + qkv_pre * cwq[3:4, :])
    else:
        q1, q2, q3 = _shifted_history(qb, rows, GDN_CONV, hq_buf, seg_len)
        qkv = _silu(q3 * cwq[0:1, :] + q2 * cwq[1:2, :] + q1 * cwq[2:3, :] + qkv_pre * cwq[3:4, :])
    if long_seq:
        r.qtail[ti] = qb[rows:rows + SUBLANES, :]
        r.stail[ti] = sb[rows:rows + SUBLANES, :]
    else:
        r.stail[...] = sc_pre.reshape(rows // seg_len, seg_len, SC_WIDTH)[:, seg_len - (SC_CONV - 1):, :]
    q_n, k_n, v_n = [], [], []
    for hd in range(GDN_HEADS):
        lo = hd * HEAD_DIM
        q_h = qkv[:, lo:lo + HEAD_DIM]
        k_h = qkv[:, GDN_WIDTH + lo:GDN_WIDTH + lo + HEAD_DIM]
        q_n.append(q_h * (lax.rsqrt(jnp.sum(q_h * q_h, axis=-1, keepdims=True) + EPS) * (HEAD_DIM ** -0.5)))
        k_n.append(k_h * lax.rsqrt(jnp.sum(k_h * k_h, axis=-1, keepdims=True) + EPS))
        v_n.append(qkv[:, 2 * GDN_WIDTH + lo:2 * GDN_WIDTH + lo + HEAD_DIM])
    yield

    state_ref = r.s_out.at[ti]
    if long_seq:
        o_all = yield from _wy_tile_packed(jnp.concatenate(q_n, axis=1), jnp.concatenate(k_n, axis=1),
                                           jnp.concatenate(v_n, axis=1), g, beta, state_ref, rows, masks,
                                           r.bd, r.kbd)
        o_h = [o_all[:, hd * HEAD_DIM:(hd + 1) * HEAD_DIM] for hd in range(GDN_HEADS)]
    else:
        g_t = g.T
        n_seg = CHUNK // seg_len
        blocks = []
        for cb in range(n_chunks):
            r0 = cb * CHUNK
            for hd in range(GDN_HEADS):

                def load_state(s, hd=hd, base=cb * n_seg):
                    return state_ref[base + s, hd]

                def store_state(s, val, hd=hd, base=cb * n_seg):
                    state_ref[base + s, hd] = val

                blocks.append(_wy_block(
                    q_n[hd][r0:r0 + CHUNK], k_n[hd][r0:r0 + CHUNK], v_n[hd][r0:r0 + CHUNK],
                    g[r0:r0 + CHUNK, hd:hd + 1], g_t[hd:hd + 1, r0:r0 + CHUNK],
                    beta[r0:r0 + CHUNK, GDN_HEADS + hd:GDN_HEADS + hd + 1],
                    seg_len, load_state, store_state))
        o_blocks = _run_tiles(blocks)
        o_h = [jnp.concatenate(o_blocks[hd::GDN_HEADS], axis=0) for hd in range(GDN_HEADS)]
    yield

    s1, s2 = _shifted_history(sb, rows, SC_CONV, hs_buf, seg_len)
    cws = r.cws[...]
    y_sc = rest[:, R_SCB:R_SCB + SC_WIDTH] * (s2 * cws[0:1, :] + s1 * cws[1:2, :] + sc_pre * cws[2:3, :])
    if long_seq:
        qb[0:SUBLANES, :] = qb[rows:rows + SUBLANES, :]
        sb[0:SUBLANES, :] = sb[rows:rows + SUBLANES, :]
    gnorm = r.gnorm[...]
    o_heads = []
    for hd in range(GDN_HEADS):
        z_h = rest[:, R_Z + hd * HEAD_DIM:R_Z + (hd + 1) * HEAD_DIM]
        o_heads.append(_rmsnorm(o_h[hd], gnorm) * _silu(z_h))
    return jnp.concatenate(o_heads + [y_sc], axis=1)


class _Refs:
    def __init__(self, **refs):
        self.__dict__.update(refs)


def _mixer_seq_kernel(n_tiles, rows, n_cast, *refs):
    n_in = 13
    x, hq, hs, s_in, nmix, w_ab, w_qkv, w_rest, cwq, gate, gnorm, cws, wout = refs[:n_in]
    cast_in = refs[n_in:n_in + n_cast]
    xmid, qtail, stail, s_out = refs[n_in + n_cast:n_in + n_cast + 4]
    cast_out = refs[n_in + n_cast + 4:n_in + 2 * n_cast + 4]
    qbuf, sbuf, bd, kbd = refs[n_in + 2 * n_cast + 4:]
    for src, dst in zip(cast_in, cast_out):
        dst[...] = src[...].astype(BF16)

    @pl.when(pl.program_id(1) == 0)
    def _():
        s_out[...] = s_in[...]
        qbuf[:, 0:SUBLANES, :] = hq[...]
        sbuf[:, 0:SUBLANES, :] = hs[...]
        _fill_block_factors(bd, kbd)

    x_all = x[...].reshape(n_tiles * rows, D_MODEL)
    h = _rmsnorm(x_all, nmix[...]).astype(BF16)
    ab = jnp.dot(h, w_ab[...], preferred_element_type=F32)
    qbuf[:, SUBLANES:SUBLANES + rows, :] = jnp.dot(h, w_qkv[...], preferred_element_type=F32).reshape(
        n_tiles, rows, QKV_DIM)
    rest = jnp.dot(h, w_rest[...], preferred_element_type=F32)
    r = _Refs(cwq=cwq, gate=gate, gnorm=gnorm, cws=cws, qtail=qtail, stail=stail, s_out=s_out, qbuf=qbuf, sbuf=sbuf,
              bd=bd, kbd=kbd)
    masks = _packed_masks()
    tile = lambda a, ti: a[ti * rows:(ti + 1) * rows]
    mix = _run_tiles(_mixer_tile(ti, rows, CHUNK, True, masks, r, tile(ab, ti), tile(rest, ti))
                     for ti in range(n_tiles))
    xmid[...] = (x_all + _mm(jnp.concatenate(mix, axis=0), wout[...])).reshape(n_tiles, rows, D_MODEL)


def _mixer_step_kernel(rows, seg_len,
                       ab, qkv, rest, hq, hs, s_in, cwq, gate, gnorm, cws,
                       mix, stail, s_out, qbuf, sbuf, hq_buf, hs_buf):
    n_seq = rows // seg_len
    s_out[...] = s_in[...]
    qbuf[0, SUBLANES:SUBLANES + rows, :] = qkv[0]
    hq_rows = [hq[j][:, None, :] for j in range(GDN_CONV - 1)]
    hq_rows.append(jnp.zeros((n_seq, seg_len - (GDN_CONV - 1), QKV_DIM), F32))
    hq_slab = jnp.concatenate(hq_rows, axis=1).reshape(rows, QKV_DIM)
    hs_slab = jnp.concatenate([hs[...], jnp.zeros((n_seq, seg_len - (SC_CONV - 1), SC_WIDTH), F32)],
                              axis=1).reshape(rows, SC_WIDTH)
    r = _Refs(hq=[hq_slab], hs=[hs_slab], cwq=cwq, gate=gate, gnorm=gnorm, cws=cws, stail=stail, s_out=s_out,
              qbuf=qbuf, sbuf=sbuf, hq_buf=hq_buf, hs_buf=hs_buf)
    mix[0] = _run_tiles([_mixer_tile(0, rows, seg_len, False, None, r, ab[0], rest[0])])[0].astype(BF16)


def _proj_kernel(x_ref, nmix_ref, w_ab_ref, w_qkv_ref, w_rest_ref, ab_ref, qkv_ref, rest_ref, qtail_ref):
    n_seq, seg_len, _ = x_ref.shape
    h = _rmsnorm(x_ref[...].reshape(n_seq * seg_len, D_MODEL), nmix_ref[...]).astype(BF16)
    ab_ref[...] = jnp.dot(h, w_ab_ref[...], preferred_element_type=F32)
    qkv = jnp.dot(h, w_qkv_ref[...], preferred_element_type=F32)
    qkv_ref[...] = qkv
    rest_ref[...] = jnp.dot(h, w_rest_ref[...], preferred_element_type=F32)
    by_seq = qkv.reshape(n_seq, seg_len, QKV_DIM)
    for j in range(GDN_CONV - 1):
        qtail_ref[j] = by_seq[:, seg_len - (GDN_CONV - 1) + j, :]


def _ffn_body(x, p, nmlp_ref, wup_ref, wdown_ref, nple_ref, wg_ref, wp_ref, nf_ref, y_ref):
    hn = _rmsnorm(x, nmlp_ref[...]).astype(BF16)
    acc = x
    for j in range(D_FF // D_MODEL):
        u = jnp.maximum(_mm(hn, wup_ref[:, j * D_MODEL:(j + 1) * D_MODEL]), 0.0)
        acc = acc + _mm(u * u, wdown_ref[j * D_MODEL:(j + 1) * D_MODEL, :])
    gate = _sigmoid(_mm(_rmsnorm(acc, nple_ref[...]), wg_ref[...]))
    x3 = acc + gate * _mm(p, wp_ref[...])
    y_ref[...] = _rmsnorm(x3, nf_ref[...]).reshape(y_ref.shape)


def _ffn_kernel(x_ref, p_ref, *rest):
    _ffn_body(x_ref[...], p_ref[...], *rest)


def _out_ffn_kernel(x_ref, mix_ref, wout_ref, p_ref, *rest):
    n = mix_ref.shape[0]
    x = x_ref[...].reshape(n, D_MODEL) + jnp.dot(mix_ref[...], wout_ref[...], preferred_element_type=F32)
    _ffn_body(x, p_ref[...].reshape(n, PLE_DIM), *rest)


def _const_spec(shape):
    nd = len(shape)
    return pl.BlockSpec(shape, lambda *_: (0,) * nd, pipeline_mode=pl.Buffered(1))


def _state_spec(n_tiles, n_state):
    return pl.BlockSpec((n_tiles, n_state, GDN_HEADS, HEAD_DIM, HEAD_DIM), lambda b, i: (b, 0, 0, 0, 0))


def _mixer_seq_call(x3, hist_q, hist_s, s_in, weights, to_cast, *, n_tiles, rows):
    seqs, seq_rows, _ = x3.shape
    tiles = seq_rows // rows
    n_steps = (seqs // n_tiles) * tiles
    cast_specs = [pl.BlockSpec((w.shape[0] // n_steps, w.shape[1]), lambda b, i: (b * tiles + i, 0)) for w in to_cast]
    row_map = lambda b, i: (b, i, 0)
    seq_map = lambda b, i: (b, 0, 0)
    x_spec = pl.BlockSpec((n_tiles, rows, D_MODEL), row_map)
    hq_spec = pl.BlockSpec((n_tiles, SUBLANES, QKV_DIM), seq_map)
    hs_spec = pl.BlockSpec((n_tiles, SUBLANES, SC_WIDTH), seq_map)
    packed_w = GDN_HEADS * CHUNK
    return pl.pallas_call(
        functools.partial(_mixer_seq_kernel, n_tiles, rows, len(to_cast)),
        grid=(seqs // n_tiles, tiles),
        in_specs=[x_spec, hq_spec, hs_spec, _state_spec(n_tiles, 1)] + [_const_spec(w.shape) for w in weights]
        + cast_specs,
        out_specs=[x_spec, hq_spec, hs_spec, _state_spec(n_tiles, 1)] + cast_specs,
        out_shape=[jax.ShapeDtypeStruct(x3.shape, F32), jax.ShapeDtypeStruct(hist_q.shape, F32),
                   jax.ShapeDtypeStruct(hist_s.shape, F32), jax.ShapeDtypeStruct(s_in.shape, F32)]
        + [jax.ShapeDtypeStruct(w.shape, BF16) for w in to_cast],
        scratch_shapes=[pltpu.VMEM((n_tiles, rows + SUBLANES, QKV_DIM), F32),
                        pltpu.VMEM((n_tiles, rows + SUBLANES, SC_WIDTH), F32),
                        pltpu.VMEM((_log2(CHUNK), packed_w, packed_w), BF16),
                        pltpu.VMEM((packed_w, GDN_WIDTH), BF16)],
        compiler_params=pltpu.CompilerParams(
            dimension_semantics=("arbitrary", "arbitrary"), vmem_limit_bytes=VMEM_LIMIT_BYTES),
        name="mixer_seq",
    )(x3, hist_q, hist_s, s_in, *weights, *to_cast)


def _split_in_proj_kernel(wt_ref, w_ab_ref, w_qkv_ref, w_rest_ref):
    n_ab = 2 * GDN_HEADS

    def group(out_ref, first_col):
        for c0 in range(0, out_ref.shape[1], SPLIT_COLS):
            out_ref[:, c0:c0 + SPLIT_COLS] = wt_ref[first_col + c0:first_col + c0 + SPLIT_COLS, :].T.astype(BF16)

    group(w_qkv_ref, 0)
    group(w_rest_ref, QKV_DIM + n_ab)
    lane = lax.broadcasted_iota(jnp.int32, w_ab_ref.shape, 1)
    w_ab_ref[...] = jnp.where(lane < n_ab, wt_ref[QKV_DIM:QKV_DIM + LANES, :].T, 0.0).astype(BF16)


def _split_in_proj_call(w_in):
    rows, cols = w_in.shape
    n_rest = cols - QKV_DIM - 2 * GDN_HEADS
    outs = [jax.ShapeDtypeStruct((rows, n), BF16) for n in (LANES, QKV_DIM, n_rest)]
    full = lambda a: pl.BlockSpec(a.shape, lambda i: (0, 0))
    w_t = w_in.T
    return pl.pallas_call(
        _split_in_proj_kernel,
        grid=(1,),
        in_specs=[full(w_t)],
        out_specs=[full(o) for o in outs],
        out_shape=outs,
        compiler_params=pltpu.CompilerParams(dimension_semantics=("arbitrary",), vmem_limit_bytes=VMEM_LIMIT_BYTES),
        name="split_in_proj",
    )(w_t)


def _full_spec(a):
    nd = len(a.shape)
    return pl.BlockSpec(a.shape, lambda i: (0,) * nd)


def _proj_call(x3, weights):
    n_seq, seg_len, _ = x3.shape
    outs = [jax.ShapeDtypeStruct((n_seq * seg_len, w.shape[1]), F32) for w in weights[1:]]
    outs.append(jax.ShapeDtypeStruct((GDN_CONV - 1, n_seq, QKV_DIM), F32))
    return pl.pallas_call(
        _proj_kernel,
        grid=(1,),
        in_specs=[_full_spec(x3)] + [_full_spec(w) for w in weights],
        out_specs=[_full_spec(o) for o in outs],
        out_shape=outs,
        compiler_params=pltpu.CompilerParams(dimension_semantics=("arbitrary",), vmem_limit_bytes=VMEM_LIMIT_BYTES),
        name="proj_step",
    )(x3, *weights)


def _mixer_step_call(ab, qkv, rest, hist_q, hist_s, s_in, weights, *, seg_len):
    tiles, rows, _ = qkv.shape
    tile_map = lambda b, i: (b, 0, 0)
    slab = lambda a: pl.BlockSpec((1, rows, a.shape[2]), tile_map)
    n_state = s_in.shape[1]
    hist_q_spec = pl.BlockSpec((GDN_CONV - 1, n_state, QKV_DIM), lambda b, i: (0, b, 0))
    hist_s_spec = pl.BlockSpec((n_state, SC_CONV - 1, SC_WIDTH), tile_map)
    mix = jax.ShapeDtypeStruct((tiles, rows, D_MODEL), BF16)
    stail = jax.ShapeDtypeStruct(hist_s.shape, F32)
    return pl.pallas_call(
        functools.partial(_mixer_step_kernel, rows, seg_len),
        grid=(tiles, 1),
        in_specs=[slab(ab), slab(qkv), slab(rest), hist_q_spec, hist_s_spec, _state_spec(1, n_state)]
        + [_const_spec(w.shape) for w in weights],
        out_specs=[slab(mix), hist_s_spec, _state_spec(1, n_state)],
        out_shape=[mix, stail, jax.ShapeDtypeStruct(s_in.shape, F32)],
        scratch_shapes=[pltpu.VMEM((1, rows + SUBLANES, QKV_DIM), F32), pltpu.VMEM((1, rows + SUBLANES, SC_WIDTH), F32),
                        pltpu.VMEM((1, rows + SUBLANES, QKV_DIM), F32), pltpu.VMEM((1, rows + SUBLANES, SC_WIDTH), F32)],
        compiler_params=pltpu.CompilerParams(
            dimension_semantics=("arbitrary", "arbitrary"), vmem_limit_bytes=VMEM_LIMIT_BYTES),
        name="mixer_step",
    )(ab, qkv, rest, hist_q, hist_s, s_in, *weights)


def _ffn_call(x, p, weights, *, name, rows=None, mix=None, wout=None):
    if mix is None:
        steps = x.shape[0] // rows
        rows_spec = lambda width: pl.BlockSpec((rows, width), lambda i: (i, 0))
        x_spec, p_spec, pre_args, pre_specs, body = rows_spec(D_MODEL), rows_spec(PLE_DIM), [], [], _ffn_kernel
    else:
        steps = 1
        x_spec, p_spec, body = _full_spec(x), _full_spec(p), _out_ffn_kernel
        pre_args, pre_specs = [mix, wout], [_full_spec(mix), _const_spec(wout.shape)]
    return pl.pallas_call(
        body,
        grid=(steps,),
        in_specs=[x_spec] + pre_specs + [p_spec] + [_const_spec(w.shape) for w in weights],
        out_specs=x_spec,
        out_shape=jax.ShapeDtypeStruct(x.shape, F32),
        compiler_params=pltpu.CompilerParams(dimension_semantics=("arbitrary",), vmem_limit_bytes=VMEM_LIMIT_BYTES),
        name=name,
    )(x, *pre_args, p, *weights)


def _layer(x_prompt, x_sample, conv_qkv, s_gdn, conv_sc, p_prompt, p_sample, norm_mix, w_in, w_conv_qkv, a_log,
           dt_bias, w_gdn_norm, w_conv_sc, w_out, norm_mlp, w_up, w_down, norm_ple, w_ple_gate, w_ple_proj, norm_f):
    bp, tp, _ = x_prompt.shape
    bs, ts, _ = x_sample.shape
    w_ab, w_qkv, w_rest = _split_in_proj_call(w_in)
    wout = w_out.astype(BF16)
    gate = jnp.zeros((SUBLANES, LANES), F32)
    gate = gate.at[0, :GDN_HEADS].set(a_log.astype(F32)).at[1, :GDN_HEADS].set(dt_bias.astype(F32))
    nmix = norm_mix.reshape(1, D_MODEL)
    core_w = (w_conv_qkv, gate, w_gdn_norm.reshape(1, HEAD_DIM), w_conv_sc)

    zq = jnp.zeros((bp, SUBLANES, QKV_DIM), F32)
    zs = jnp.zeros((bp, SUBLANES, SC_WIDTH), F32)
    s0 = jnp.zeros((bp, 1, GDN_HEADS, HEAD_DIM, HEAD_DIM), F32)
    xm_p, qt_p, st_p, s_p, wup, wdown, wgate = _mixer_seq_call(
        x_prompt, zq, zs, s0, (nmix, w_ab, w_qkv, w_rest) + core_w + (wout,), (w_up, w_down, w_ple_gate),
        n_tiles=PROMPT_SEQS_PER_STEP, rows=PROMPT_ROWS)
    ffn_w = (norm_mlp.reshape(1, D_MODEL), wup, wdown, norm_ple.reshape(1, D_MODEL), wgate,
             w_ple_proj.astype(BF16), norm_f.reshape(1, D_MODEL))
    y_p = _ffn_call(xm_p.reshape(bp * tp, D_MODEL), p_prompt.reshape(bp * tp, PLE_DIM), ffn_w, rows=FFN_ROWS,
                    name="ffn_prompt")

    seq_per_tile = CHUNK // ts
    tiles = bs // seq_per_tile
    ab_s, qkv_s, rest_s, qt_s = _proj_call(x_sample, (nmix, w_ab, w_qkv, w_rest))
    tiled = lambda a: a.reshape(tiles, CHUNK, a.shape[-1])
    mix_s, new_sc_s, s_s = _mixer_step_call(
        tiled(ab_s), tiled(qkv_s), tiled(rest_s), jnp.transpose(conv_qkv, (1, 0, 2)), conv_sc,
        s_gdn.reshape(tiles, seq_per_tile, GDN_HEADS, HEAD_DIM, HEAD_DIM), core_w, seg_len=ts)
    y_s = _ffn_call(x_sample, p_sample, ffn_w, name="ffn_sample", mix=mix_s.reshape(bs * ts, D_MODEL), wout=wout)

    new_conv_p = qt_p[:, SUBLANES - (GDN_CONV - 1):]
    new_sc_p = st_p[:, SUBLANES - (SC_CONV - 1):]
    new_conv_s = jnp.transpose(qt_s, (1, 0, 2))
    return (y_p.reshape(bp, tp, D_MODEL), y_s, new_conv_p,
            s_p.reshape(bp, GDN_HEADS, HEAD_DIM, HEAD_DIM), new_sc_p, new_conv_s,
            s_s.reshape(bs, GDN_HEADS, HEAD_DIM, HEAD_DIM), new_sc_s)


def kernel(x_prompt, x_sample, state_gdn_conv, state_gdn, state_sc_conv, p_prompt, p_sample, norm_mix, w_in, w_conv_qkv, a_log, dt_bias, w_gdn_norm, w_conv_sc, w_out, norm_mlp, w_up, w_down, norm_ple, w_ple_gate, w_ple_proj, norm_f):
    depth = w_in.shape[0]
    assert depth == 1, "one layer per call"
    assert x_sample.shape[1] >= GDN_CONV - 1 and CHUNK % x_sample.shape[1] == 0
    assert x_prompt.shape[1] % PROMPT_ROWS == 0 and x_prompt.shape[0] % PROMPT_SEQS_PER_STEP == 0
    outs = _layer(x_prompt, x_sample, state_gdn_conv[0], state_gdn[0], state_sc_conv[0], p_prompt[0], p_sample[0],
                  norm_mix[0], w_in[0], w_conv_qkv[0], a_log[0], dt_bias[0], w_gdn_norm[0], w_conv_sc[0], w_out[0],
                  norm_mlp[0], w_up[0], w_down[0], norm_ple[0], w_ple_gate[0], w_ple_proj[0], norm_f)
    y_p, y_s, c_p, s_p, sc_p, c_s, s_s, sc_s = outs
    return (y_p, y_s, c_p[None], s_p[None], sc_p[None], c_s[None], s_s[None], sc_s[None])
```

```python
import functools

import jax
import jax.numpy as jnp
from jax import lax
from jax.experimental import pallas as pl
from jax.experimental.pallas import tpu as pltpu

F32 = jnp.float32
BF16 = jnp.bfloat16

D_MODEL = 1024
PLE_DIM = 256
GDN_HEADS = 4
HEAD_DIM = 128
GDN_WIDTH = GDN_HEADS * HEAD_DIM
QKV_DIM = 3 * GDN_WIDTH
GDN_CONV = 4
SC_WIDTH = D_MODEL - GDN_WIDTH
SC_CONV = 3
D_FF = 4 * D_MODEL
EPS = 1e-6
NEG_LOG2_E = -1.4426950408889634
CHUNK = 64
LANES = 128
SUBLANES = 8
VMEM_LIMIT_BYTES = 56 * 1024 * 1024
PROMPT_ROWS = 128
PROMPT_SEQS_PER_STEP = 4
FFN_ROWS = 1024
IN_PROJ_COLS = 512
SPLIT_COLS = 256

R_Z = 0
R_SCB = R_Z + GDN_WIDTH
R_SCC = R_SCB + SC_WIDTH
R_SCH = R_SCC + SC_WIDTH


def _mm(a, b):
    return jnp.dot(a.astype(BF16), b.astype(BF16), preferred_element_type=F32)


def _mm_nt(a, b):
    return lax.dot_general(a.astype(BF16), b.astype(BF16), (((1,), (1,)), ((), ())), preferred_element_type=F32)


def _mm_tn(a, b):
    return lax.dot_general(a.astype(BF16), b.astype(BF16), (((0,), (0,)), ((), ())), preferred_element_type=F32)


def _rmsnorm(x, w_row):
    return x * lax.rsqrt(jnp.mean(x * x, axis=-1, keepdims=True) + EPS) * w_row


def _sigmoid(x):
    return 1.0 / (1.0 + jnp.exp2(x * NEG_LOG2_E))


def _silu(x):
    return x * _sigmoid(x)


def _softplus(x):
    return jnp.maximum(x, 0.0) + jnp.log1p(jnp.exp(-jnp.abs(x)))


def _log2(n):
    k = n.bit_length() - 1
    assert (1 << k) == n, n
    return k


def _wy_block(q, k, v, g_col, g_row, beta_col, seg_len, load_state, store_state):
    c = CHUNK
    n_seg = c // seg_len
    lg = _log2(seg_len)
    ri = lax.broadcasted_iota(jnp.int32, (c, c), 0)
    ci = lax.broadcasted_iota(jnp.int32, (c, c), 1)
    same = jnp.right_shift(ri, lg) == jnp.right_shift(ci, lg)
    lower = same & (ri >= ci)
    strict = same & (ri > ci)
    diff = g_col - g_row
    decay = jnp.where(lower, jnp.exp(jnp.where(lower, diff, 0.0)), 0.0)
    a_off = jnp.where(strict, beta_col * _mm_nt(k, k) * decay, 0.0)
    qk = _mm_nt(q, k) * decay
    yield
    eye = jnp.where(ri == ci, 1.0, 0.0).astype(F32)
    x_inv = eye
    for lb in range(lg):
        rb = jnp.right_shift(ri, lb)
        cb = jnp.right_shift(ci, lb)
        join = (jnp.bitwise_and(rb, 1) == 1) & (cb == rb - 1)
        b_lvl = jnp.where(join, a_off, 0.0)
        if lb == 0:
            x_inv = eye - b_lvl
        else:
            xb = _mm(x_inv, b_lvl)
            yield
            x_inv = x_inv - _mm(xb, x_inv)
            yield
    rhs = jnp.concatenate([v * beta_col, k * (beta_col * jnp.exp(g_col))], axis=1)
    sol = _mm(x_inv, rhs)
    yield
    u = sol[:, :HEAD_DIM]
    w = sol[:, HEAD_DIM:]
    stacked = jnp.concatenate([w, q * jnp.exp(g_col)], axis=0)
    row = lax.broadcasted_iota(jnp.int32, (c, 1), 0)
    states = [load_state(s) for s in range(n_seg)]
    w_s = None
    q_s = None
    for s in range(n_seg):
        r = _mm(stacked, states[s])
        if n_seg == 1:
            w_s, q_s = r[:c], r[c:]
        else:
            in_seg = jnp.right_shift(row, lg) == s
            w_s = jnp.where(in_seg, r[:c], 0.0 if w_s is None else w_s)
            q_s = jnp.where(in_seg, r[c:], 0.0 if q_s is None else q_s)
    v_new = u - w_s
    yield
    o = q_s + _mm(qk, v_new)
    for s in range(n_seg):
        last = s * seg_len + seg_len - 1
        g_last = g_col[last:last + 1, :]
        if n_seg == 1:
            k_dec = k * jnp.exp(g_last - g_col)
        else:
            in_seg = jnp.right_shift(row, lg) == s
            k_dec = jnp.where(in_seg, k * jnp.exp(jnp.where(in_seg, g_last - g_col, 0.0)), 0.0)
        store_state(s, states[s] * jnp.exp(g_last) + _mm_tn(k_dec, v_new))
    return o


def _expand_heads(slab, first_lane, width):
    rows = slab.shape[0]
    cols = [slab[:, first_lane + hd:first_lane + hd + 1] for hd in range(GDN_HEADS)]
    if width % LANES == 0:
        return jnp.concatenate([jnp.broadcast_to(c, (rows, width)) for c in cols], axis=1)
    total = GDN_HEADS * width
    lane_head = jnp.right_shift(lax.broadcasted_iota(jnp.int32, (rows, total), 1), _log2(width))
    out = jnp.broadcast_to(cols[-1], (rows, total))
    for hd in range(GDN_HEADS - 2, -1, -1):
        out = jnp.where(lane_head == hd, jnp.broadcast_to(cols[hd], (rows, total)), out)
    return out


def _packed_masks():
    c, nh = CHUNK, GDN_HEADS
    ri = lax.broadcasted_iota(jnp.int32, (c, nh * c), 0)
    cj = jnp.bitwise_and(lax.broadcasted_iota(jnp.int32, (c, nh * c), 1), c - 1)
    return dict(lower=ri >= cj, strict=ri > cj, eye=ri == cj, join0=(jnp.bitwise_and(ri, 1) == 1) & (cj == ri - 1))


def _fill_block_factors(bd_ref, kbd_ref):
    c, nh = CHUNK, GDN_HEADS
    w = nh * c
    lg = _log2(c)
    rw = lax.broadcasted_iota(jnp.int32, (w, w), 0)
    cw = lax.broadcasted_iota(jnp.int32, (w, w), 1)
    same_head = jnp.right_shift(rw, lg) == jnp.right_shift(cw, lg)
    rw, cw = jnp.bitwise_and(rw, c - 1), jnp.bitwise_and(cw, c - 1)
    one_zero = lambda m: jnp.where(m, 1.0, 0.0).astype(BF16)
    bd_ref[0] = one_zero(same_head)
    for lb in range(1, lg):
        bd_ref[lb] = one_zero(same_head & (jnp.bitwise_and(jnp.right_shift(rw, lb), 1) == 1)
                              & (jnp.right_shift(cw, lb) == jnp.right_shift(rw, lb) - 1))
    kbd_ref[...] = one_zero(jnp.right_shift(lax.broadcasted_iota(jnp.int32, (w, nh * HEAD_DIM), 0), lg)
                            == jnp.right_shift(lax.broadcasted_iota(jnp.int32, (w, nh * HEAD_DIM), 1),
                                               _log2(HEAD_DIM)))


def _wy_tile_packed(q_all, k_all, v_all, g, beta, state_ref, rows, masks, bd_ref, kbd_ref):
    c, nh = CHUNK, GDN_HEADS
    lg = _log2(c)
    n_chunks = rows // c
    lower, strict, eye = masks["lower"], masks["strict"], masks["eye"]

    def block_diag(x, factor):
        return jnp.concatenate([x.astype(BF16)] * nh, axis=0) * factor

    def head(x, hd, width):
        return x[:, hd * width:(hd + 1) * width]

    g_c = _expand_heads(g, 0, c)
    b_c = _expand_heads(beta, nh, c)
    g_d = _expand_heads(g, 0, HEAD_DIM)
    b_d = _expand_heads(beta, nh, HEAD_DIM)
    eg_d = jnp.exp(g_d)
    v_rhs = v_all * b_d
    k_rhs = k_all * (b_d * eg_d)
    q_g = q_all * eg_d

    a_off, qk_dec = [], []
    for cb in range(n_chunks):
        rs = slice(cb * c, (cb + 1) * c)
        g_row = jnp.sum(jnp.where(eye, g_c[rs], 0.0), axis=0, keepdims=True)
        decay = jnp.where(lower, jnp.exp(jnp.where(lower, g_c[rs] - g_row, 0.0)), 0.0)
        kq = _mm_nt(jnp.concatenate([k_all[rs], q_all[rs]], axis=0), block_diag(k_all[rs], kbd_ref[...]))
        a_off.append(jnp.where(strict, b_c[rs] * kq[:c] * decay, 0.0))
        qk_dec.append(kq[c:] * decay)
    yield

    eye_f = jnp.where(eye, 1.0, 0.0).astype(F32)
    x_inv = [eye_f - jnp.where(masks["join0"], a, 0.0) for a in a_off]
    for lb in range(1, lg):
        xb = [jnp.dot(x.astype(BF16), block_diag(a, bd_ref[lb]), preferred_element_type=F32)
              for x, a in zip(x_inv, a_off)]
        yield
        xbx = [jnp.dot(t.astype(BF16), block_diag(x, bd_ref[0]), preferred_element_type=F32)
               for t, x in zip(xb, x_inv)]
        x_inv = [x - t for x, t in zip(x_inv, xbx)]
        yield

    u, wk = [], []
    for cb in range(n_chunks):
        rs = slice(cb * c, (cb + 1) * c)
        sol = [_mm(head(x_inv[cb], hd, c),
                   jnp.concatenate([head(v_rhs[rs], hd, HEAD_DIM), head(k_rhs[rs], hd, HEAD_DIM)], axis=1))
               for hd in range(nh)]
        u.append([s[:, :HEAD_DIM] for s in sol])
        wk.append([s[:, HEAD_DIM:] for s in sol])
    yield

    state = [state_ref[0, hd] for hd in range(nh)]
    o_blocks = []
    for cb in range(n_chunks):
        rs = slice(cb * c, (cb + 1) * c)
        g_last = g_d[(cb + 1) * c - 1:(cb + 1) * c, :]
        k_dec = k_all[rs] * jnp.exp(g_last - g_d[rs])
        eg_last = jnp.exp(g_last)
        r = [_mm(jnp.concatenate([wk[cb][hd], head(q_g[rs], hd, HEAD_DIM)], axis=0), state[hd]) for hd in range(nh)]
        v_new = [u[cb][hd] - r[hd][:c] for hd in range(nh)]
        yield
        o_blocks.append(jnp.concatenate(
            [r[hd][c:] + _mm(head(qk_dec[cb], hd, c), v_new[hd]) for hd in range(nh)], axis=1))
        state = [state[hd] * head(eg_last, hd, HEAD_DIM) + _mm_tn(head(k_dec, hd, HEAD_DIM), v_new[hd])
                 for hd in range(nh)]
        yield
    for hd in range(nh):
        state_ref[0, hd] = state[hd]
    return jnp.concatenate(o_blocks, axis=0)


def _run_tiles(programs):
    programs = list(programs)
    results = [None] * len(programs)
    live = list(range(len(programs)))
    while live:
        for idx in list(live):
            try:
                next(programs[idx])
            except StopIteration as done:
                results[idx] = done.value
                live.remove(idx)
    return results


def _shifted_history(buf, rows, n_taps, hist_buf, seg_len):
    out = []
    with_carry = buf[0:SUBLANES + rows, :]
    for s in range(1, n_taps):
        raw = pltpu.roll(with_carry, s, 0)[SUBLANES:, :]
        if hist_buf is not None:
            t = jnp.bitwise_and(lax.broadcasted_iota(jnp.int32, (rows, 1), 0), seg_len - 1)
            raw = jnp.where(t >= s, raw, hist_buf[pl.ds(n_taps - 1 - s, rows), :])
        out.append(raw)
    return out


def _mixer_tile(ti, rows, seg_len, long_seq, masks, r, ab, rest):
    n_chunks = rows // CHUNK
    new_rows = slice(SUBLANES, SUBLANES + rows)
    qb, sb = r.qbuf.at[ti], r.sbuf.at[ti]

    gate = r.gate[...]
    log_a = -jnp.exp(gate[0:1, :]) * _softplus(ab + gate[1:2, :])
    beta = _sigmoid(ab)
    pos = jnp.bitwise_and(lax.broadcasted_iota(jnp.int32, (rows, 1), 0), seg_len - 1)
    g = log_a
    shift = 1
    while shift < seg_len:
        g = g + jnp.where(pos >= shift, pltpu.roll(g, shift, 0), 0.0)
        shift *= 2

    sc_pre = rest[:, R_SCC:R_SCC + SC_WIDTH] * rest[:, R_SCH:R_SCH + SC_WIDTH]
    sb[new_rows, :] = sc_pre
    if long_seq:
        hq_buf = hs_buf = None
    else:
        hq_buf, hs_buf = r.hq_buf.at[ti], r.hs_buf.at[ti]
        zq = jnp.zeros((SUBLANES, QKV_DIM), F32)
        zs = jnp.zeros((SUBLANES, SC_WIDTH), F32)
        qb[0:SUBLANES, :] = zq
        sb[0:SUBLANES, :] = zs
        hq_buf[0:rows, :] = r.hq[ti]
        hq_buf[rows:rows + SUBLANES, :] = zq
        hs_buf[0:rows, :] = r.hs[ti]
        hs_buf[rows:rows + SUBLANES, :] = zs
    qkv_pre = qb[new_rows, :]
    cwq = r.cwq[...]
    if long_seq:
        with_carry = qb[0:SUBLANES + rows, :]
        back1 = pltpu.roll(with_carry, 1, 0)
        pair = with_carry * cwq[1:2, :] + back1 * cwq[0:1, :]
        qkv = _silu(pltpu.roll(pair, 2, 0)[SUBLANES:, :] + back1[SUBLANES:, :] * cwq[2:3, :]
                    + qkv_pre * cwq[3:4, :])
    else:
        q1, q2, q3 = _shifted_history(qb, rows, GDN_CONV, hq_buf, seg_len)
        qkv = _silu(q3 * cwq[0:1, :] + q2 * cwq[1:2, :] + q1 * cwq[2:3, :] + qkv_pre * cwq[3:4, :])
    if long_seq:
        r.qtail[ti] = qb[rows:rows + SUBLANES, :]
        r.stail[ti] = sb[rows:rows + SUBLANES, :]
    else:
        r.stail[...] = sc_pre.reshape(rows // seg_len, seg_len, SC_WIDTH)[:, seg_len - (SC_CONV - 1):, :]
    q_n, k_n, v_n = [], [], []
    for hd in range(GDN_HEADS):
        lo = hd * HEAD_DIM
        q_h = qkv[:, lo:lo + HEAD_DIM]
        k_h = qkv[:, GDN_WIDTH + lo:GDN_WIDTH + lo + HEAD_DIM]
        q_n.append(q_h * (lax.rsqrt(jnp.sum(q_h * q_h, axis=-1, keepdims=True) + EPS) * (HEAD_DIM ** -0.5)))
        k_n.append(k_h * lax.rsqrt(jnp.sum(k_h * k_h, axis=-1, keepdims=True) + EPS))
        v_n.append(qkv[:, 2 * GDN_WIDTH + lo:2 * GDN_WIDTH + lo + HEAD_DIM])
    yield

    state_ref = r.s_out.at[ti]
    if long_seq:
        o_all = yield from _wy_tile_packed(jnp.concatenate(q_n, axis=1), jnp.concatenate(k_n, axis=1),
                                           jnp.concatenate(v_n, axis=1), g, beta, state_ref, rows, masks,
                                           r.bd, r.kbd)
        o_h = [o_all[:, hd * HEAD_DIM:(hd + 1) * HEAD_DIM] for hd in range(GDN_HEADS)]
    else:
        g_t = g.T
        n_seg = CHUNK // seg_len
        blocks = []
        for cb in range(n_chunks):
            r0 = cb * CHUNK
            for hd in range(GDN_HEADS):

                def load_state(s, hd=hd, base=cb * n_seg):
                    return state_ref[base + s, hd]

                def store_state(s, val, hd=hd, base=cb * n_seg):
                    state_ref[base + s, hd] = val

                blocks.append(_wy_block(
                    q_n[hd][r0:r0 + CHUNK], k_n[hd][r0:r0 + CHUNK], v_n[hd][r0:r0 + CHUNK],
                    g[r0:r0 + CHUNK, hd:hd + 1], g_t[hd:hd + 1, r0:r0 + CHUNK],
                    beta[r0:r0 + CHUNK, GDN_HEADS + hd:GDN_HEADS + hd + 1],
                    seg_len, load_state, store_state))
        o_blocks = _run_tiles(blocks)
        o_h = [jnp.concatenate(o_blocks[hd::GDN_HEADS], axis=0) for hd in range(GDN_HEADS)]
    yield

    s1, s2 = _shifted_history(sb, rows, SC_CONV, hs_buf, seg_len)
    cws = r.cws[...]
    y_sc = rest[:, R_SCB:R_SCB + SC_WIDTH] * (s2 * cws[0:1, :] + s1 * cws[1:2, :] + sc_pre * cws[2:3, :])
    if long_seq:
        qb[0:SUBLANES, :] = qb[rows:rows + SUBLANES, :]
        sb[0:SUBLANES, :] = sb[rows:rows + SUBLANES, :]
    gnorm = r.gnorm[...]
    o_heads = []
    for hd in range(GDN_HEADS):
        z_h = rest[:, R_Z + hd * HEAD_DIM:R_Z + (hd + 1) * HEAD_DIM]
        o_heads.append(_rmsnorm(o_h[hd], gnorm) * _silu(z_h))
    return jnp.concatenate(o_heads + [y_sc], axis=1)


class _Refs:
    def __init__(self, **refs):
        self.__dict__.update(refs)


def _mixer_seq_kernel(n_tiles, rows, n_cast, *refs):
    n_in = 12
    x, hq, hs, s_in, nmix, w_ab, w_main, cwq, gate, gnorm, cws, wout = refs[:n_in]
    cast_in = refs[n_in:n_in + n_cast]
    xmid, qtail, stail, s_out = refs[n_in + n_cast:n_in + n_cast + 4]
    cast_out = refs[n_in + n_cast + 4:n_in + 2 * n_cast + 4]
    qbuf, sbuf, bd, kbd = refs[n_in + 2 * n_cast + 4:]
    for src, dst in zip(cast_in, cast_out):
        dst[...] = src[...].astype(BF16)

    @pl.when(pl.program_id(1) == 0)
    def _():
        s_out[...] = s_in[...]
        qbuf[:, 0:SUBLANES, :] = hq[...]
        sbuf[:, 0:SUBLANES, :] = hs[...]
        _fill_block_factors(bd, kbd)

    x_all = x[...].reshape(n_tiles * rows, D_MODEL)
    h = _rmsnorm(x_all, nmix[...]).astype(BF16)
    ab = jnp.dot(h, w_ab[...], preferred_element_type=F32)
    qbuf[:, SUBLANES:SUBLANES + rows, :] = jnp.dot(h, w_main[:, 0:QKV_DIM], preferred_element_type=F32).reshape(
        n_tiles, rows, QKV_DIM)
    rest = jnp.dot(h, w_main[:, QKV_DIM:], preferred_element_type=F32)
    r = _Refs(cwq=cwq, gate=gate, gnorm=gnorm, cws=cws, qtail=qtail, stail=stail, s_out=s_out, qbuf=qbuf, sbuf=sbuf,
              bd=bd, kbd=kbd)
    masks = _packed_masks()
    tile = lambda a, ti: a[ti * rows:(ti + 1) * rows]
    mix = _run_tiles(_mixer_tile(ti, rows, CHUNK, True, masks, r, tile(ab, ti), tile(rest, ti))
                     for ti in range(n_tiles))
    xmid[...] = (x_all + _mm(jnp.concatenate(mix, axis=0), wout[...])).reshape(n_tiles, rows, D_MODEL)


def _mixer_step_kernel(rows, seg_len,
                       ab, proj, hq, hs, s_in, cwq, gate, gnorm, cws,
                       mix, qtail, stail, s_out, qbuf, sbuf, hq_buf, hs_buf):
    n_seq = rows // seg_len
    s_out[...] = s_in[...]
    qkv = proj[0, :, 0:QKV_DIM]
    qbuf[0, SUBLANES:SUBLANES + rows, :] = qkv
    by_seq = qkv.reshape(n_seq, seg_len, QKV_DIM)
    for j in range(GDN_CONV - 1):
        qtail[j] = by_seq[:, seg_len - (GDN_CONV - 1) + j, :]
    hq_rows = [hq[j][:, None, :] for j in range(GDN_CONV - 1)]
    hq_rows.append(jnp.zeros((n_seq, seg_len - (GDN_CONV - 1), QKV_DIM), F32))
    hq_slab = jnp.concatenate(hq_rows, axis=1).reshape(rows, QKV_DIM)
    hs_slab = jnp.concatenate([hs[...], jnp.zeros((n_seq, seg_len - (SC_CONV - 1), SC_WIDTH), F32)],
                              axis=1).reshape(rows, SC_WIDTH)
    r = _Refs(hq=[hq_slab], hs=[hs_slab], cwq=cwq, gate=gate, gnorm=gnorm, cws=cws, stail=stail, s_out=s_out,
              qbuf=qbuf, sbuf=sbuf, hq_buf=hq_buf, hs_buf=hs_buf)
    mix[0] = _run_tiles([_mixer_tile(0, rows, seg_len, False, None, r, ab[0], proj[0, :, QKV_DIM:])])[0].astype(BF16)


def _ffn_body(x, p, nmlp_ref, wup_ref, wdown_ref, nple_ref, wg_ref, wp_ref, nf_ref, y_ref):
    hn = _rmsnorm(x, nmlp_ref[...]).astype(BF16)
    acc = x
    for j in range(D_FF // D_MODEL):
        u = jnp.maximum(_mm(hn, wup_ref[:, j * D_MODEL:(j + 1) * D_MODEL]), 0.0)
        acc = acc + _mm(u * u, wdown_ref[j * D_MODEL:(j + 1) * D_MODEL, :])
    gate = _sigmoid(_mm(_rmsnorm(acc, nple_ref[...]), wg_ref[...]))
    x3 = acc + gate * _mm(p, wp_ref[...])
    y_ref[...] = _rmsnorm(x3, nf_ref[...]).reshape(y_ref.shape)


def _ffn_kernel(x_ref, p_ref, *rest):
    _ffn_body(x_ref[...], p_ref[...], *rest)


def _out_ffn_kernel(x_ref, mix_ref, wout_ref, p_ref, *rest):
    n = mix_ref.shape[0]
    x = x_ref[...].reshape(n, D_MODEL) + jnp.dot(mix_ref[...], wout_ref[...], preferred_element_type=F32)
    _ffn_body(x, p_ref[...].reshape(n, PLE_DIM), *rest)


def _const_spec(shape):
    nd = len(shape)
    return pl.BlockSpec(shape, lambda *_: (0,) * nd, pipeline_mode=pl.Buffered(1))


def _state_spec(n_tiles, n_state):
    return pl.BlockSpec((n_tiles, n_state, GDN_HEADS, HEAD_DIM, HEAD_DIM), lambda b, i: (b, 0, 0, 0, 0))


def _mixer_seq_call(x3, hist_q, hist_s, s_in, weights, to_cast, *, n_tiles, rows):
    seqs, seq_rows, _ = x3.shape
    tiles = seq_rows // rows
    n_steps = (seqs // n_tiles) * tiles
    cast_specs = [pl.BlockSpec((w.shape[0] // n_steps, w.shape[1]), lambda b, i: (b * tiles + i, 0)) for w in to_cast]
    row_map = lambda b, i: (b, i, 0)
    seq_map = lambda b, i: (b, 0, 0)
    x_spec = pl.BlockSpec((n_tiles, rows, D_MODEL), row_map)
    hq_spec = pl.BlockSpec((n_tiles, SUBLANES, QKV_DIM), seq_map)
    hs_spec = pl.BlockSpec((n_tiles, SUBLANES, SC_WIDTH), seq_map)
    packed_w = GDN_HEADS * CHUNK
    return pl.pallas_call(
        functools.partial(_mixer_seq_kernel, n_tiles, rows, len(to_cast)),
        grid=(seqs // n_tiles, tiles),
        in_specs=[x_spec, hq_spec, hs_spec, _state_spec(n_tiles, 1)] + [_const_spec(w.shape) for w in weights]
        + cast_specs,
        out_specs=[x_spec, hq_spec, hs_spec, _state_spec(n_tiles, 1)] + cast_specs,
        out_shape=[jax.ShapeDtypeStruct(x3.shape, F32), jax.ShapeDtypeStruct(hist_q.shape, F32),
                   jax.ShapeDtypeStruct(hist_s.shape, F32), jax.ShapeDtypeStruct(s_in.shape, F32)]
        + [jax.ShapeDtypeStruct(w.shape, BF16) for w in to_cast],
        scratch_shapes=[pltpu.VMEM((n_tiles, rows + SUBLANES, QKV_DIM), F32),
                        pltpu.VMEM((n_tiles, rows + SUBLANES, SC_WIDTH), F32),
                        pltpu.VMEM((_log2(CHUNK), packed_w, packed_w), BF16),
                        pltpu.VMEM((packed_w, GDN_WIDTH), BF16)],
        compiler_params=pltpu.CompilerParams(
            dimension_semantics=("arbitrary", "arbitrary"), vmem_limit_bytes=VMEM_LIMIT_BYTES),
        name="mixer_seq",
    )(x3, hist_q, hist_s, s_in, *weights, *to_cast)


def _in_proj_kernel(n_qkv_steps, wt_ref, behind_ref, x_ref, nmix_ref, w_ab_ref, w_main_ref, ab_ref, proj_ref, h_ref):
    i = pl.program_id(0)

    @pl.when(i == 0)
    def _():
        n_seq, seg_len, _ = x_ref.shape
        h_ref[...] = _rmsnorm(x_ref[...].reshape(n_seq * seg_len, D_MODEL), nmix_ref[...]).astype(BF16)

    on_grid = wt_ref[...]
    behind = behind_ref[...]
    group = jnp.where(i < n_qkv_steps, on_grid, jnp.concatenate([on_grid[SUBLANES:], behind], axis=0))
    for c0 in range(0, IN_PROJ_COLS, SPLIT_COLS):
        w_main_ref[:, c0:c0 + SPLIT_COLS] = group[c0:c0 + SPLIT_COLS].T.astype(BF16)
    proj_ref[...] = jnp.dot(h_ref[...], w_main_ref[...], preferred_element_type=F32)

    @pl.when(i == n_qkv_steps - 1)
    def _():
        w_ab = jnp.concatenate([behind, jnp.zeros((LANES - SUBLANES, D_MODEL), F32)], axis=0).T.astype(BF16)
        w_ab_ref[...] = w_ab
        ab_ref[...] = jnp.dot(h_ref[...], w_ab, preferred_element_type=F32)


def _full_spec(a):
    nd = len(a.shape)
    return pl.BlockSpec(a.shape, lambda i: (0,) * nd)


def _in_proj_call(w_in, x3, nmix):
    rows, cols = w_in.shape
    n_seq, seg_len, _ = x3.shape
    n = n_seq * seg_len
    n_main = cols - 2 * GDN_HEADS
    assert 2 * GDN_HEADS == SUBLANES and QKV_DIM % IN_PROJ_COLS == 0 and n_main % IN_PROJ_COLS == 0
    w_t = w_in.T
    col_block = lambda height: pl.BlockSpec((height, IN_PROJ_COLS), lambda i: (0, i))
    outs = [jax.ShapeDtypeStruct((rows, LANES), BF16), jax.ShapeDtypeStruct((rows, n_main), BF16),
            jax.ShapeDtypeStruct((n, LANES), F32), jax.ShapeDtypeStruct((n, n_main), F32)]
    return pl.pallas_call(
        functools.partial(_in_proj_kernel, QKV_DIM // IN_PROJ_COLS),
        grid=(n_main // IN_PROJ_COLS,),
        in_specs=[pl.BlockSpec((IN_PROJ_COLS, rows), lambda i: (i, 0)),
                  pl.BlockSpec((SUBLANES, rows), lambda i: ((i + 1) * (IN_PROJ_COLS // SUBLANES), 0)),
                  _full_spec(x3), _full_spec(nmix)],
        out_specs=[_full_spec(outs[0]), col_block(rows), _full_spec(outs[2]), col_block(n)],
        out_shape=outs,
        scratch_shapes=[pltpu.VMEM((n, rows), BF16)],
        compiler_params=pltpu.CompilerParams(dimension_semantics=("arbitrary",), vmem_limit_bytes=VMEM_LIMIT_BYTES),
        name="in_proj",
    )(w_t, w_t, x3, nmix)


def _mixer_step_call(ab, proj, hist_q, hist_s, s_in, weights, *, seg_len):
    tiles, rows, _ = proj.shape
    tile_map = lambda b, i: (b, 0, 0)
    slab = lambda a: pl.BlockSpec((1, rows, a.shape[2]), tile_map)
    n_state = s_in.shape[1]
    hist_q_spec = pl.BlockSpec((GDN_CONV - 1, n_state, QKV_DIM), lambda b, i: (0, b, 0))
    hist_s_spec = pl.BlockSpec((n_state, SC_CONV - 1, SC_WIDTH), tile_map)
    mix = jax.ShapeDtypeStruct((tiles, rows, D_MODEL), BF16)
    like = lambda a: jax.ShapeDtypeStruct(a.shape, F32)
    return pl.pallas_call(
        functools.partial(_mixer_step_kernel, rows, seg_len),
        grid=(tiles, 1),
        in_specs=[slab(ab), slab(proj), hist_q_spec, hist_s_spec, _state_spec(1, n_state)]
        + [_const_spec(w.shape) for w in weights],
        out_specs=[slab(mix), hist_q_spec, hist_s_spec, _state_spec(1, n_state)],
        out_shape=[mix, like(hist_q), like(hist_s), like(s_in)],
        scratch_shapes=[pltpu.VMEM((1, rows + SUBLANES, QKV_DIM), F32), pltpu.VMEM((1, rows + SUBLANES, SC_WIDTH), F32),
                        pltpu.VMEM((1, rows + SUBLANES, QKV_DIM), F32), pltpu.VMEM((1, rows + SUBLANES, SC_WIDTH), F32)],
        compiler_params=pltpu.CompilerParams(
            dimension_semantics=("arbitrary", "arbitrary"), vmem_limit_bytes=VMEM_LIMIT_BYTES),
        name="mixer_step",
    )(ab, proj, hist_q, hist_s, s_in, *weights)


def _ffn_call(x, p, weights, *, name, rows=None, mix=None, wout=None):
    if mix is None:
        steps = x.shape[0] // rows
        rows_spec = lambda width: pl.BlockSpec((rows, width), lambda i: (i, 0))
        x_spec, p_spec, pre_args, pre_specs, body = rows_spec(D_MODEL), rows_spec(PLE_DIM), [], [], _ffn_kernel
    else:
        steps = 1
        x_spec, p_spec, body = _full_spec(x), _full_spec(p), _out_ffn_kernel
        pre_args, pre_specs = [mix, wout], [_full_spec(mix), _const_spec(wout.shape)]
    return pl.pallas_call(
        body,
        grid=(steps,),
        in_specs=[x_spec] + pre_specs + [p_spec] + [_const_spec(w.shape) for w in weights],
        out_specs=x_spec,
        out_shape=jax.ShapeDtypeStruct(x.shape, F32),
        compiler_params=pltpu.CompilerParams(dimension_semantics=("arbitrary",), vmem_limit_bytes=VMEM_LIMIT_BYTES),
        name=name,
    )(x, *pre_args, p, *weights)


def _layer(x_prompt, x_sample, conv_qkv, s_gdn, conv_sc, p_prompt, p_sample, norm_mix, w_in, w_conv_qkv, a_log,
           dt_bias, w_gdn_norm, w_conv_sc, w_out, norm_mlp, w_up, w_down, norm_ple, w_ple_gate, w_ple_proj, norm_f):
    bp, tp, _ = x_prompt.shape
    bs, ts, _ = x_sample.shape
    nmix = norm_mix.reshape(1, D_MODEL)
    w_ab, w_main, ab_s, proj_s = _in_proj_call(w_in, x_sample, nmix)
    wout = w_out.astype(BF16)
    gate = jnp.zeros((SUBLANES, LANES), F32)
    gate = gate.at[0, :GDN_HEADS].set(a_log.astype(F32)).at[1, :GDN_HEADS].set(dt_bias.astype(F32))
    core_w = (w_conv_qkv, gate, w_gdn_norm.reshape(1, HEAD_DIM), w_conv_sc)

    zq = jnp.zeros((bp, SUBLANES, QKV_DIM), F32)
    zs = jnp.zeros((bp, SUBLANES, SC_WIDTH), F32)
    s0 = jnp.zeros((bp, 1, GDN_HEADS, HEAD_DIM, HEAD_DIM), F32)
    xm_p, qt_p, st_p, s_p, wup, wdown, wgate = _mixer_seq_call(
        x_prompt, zq, zs, s0, (nmix, w_ab, w_main) + core_w + (wout,), (w_up, w_down, w_ple_gate),
        n_tiles=PROMPT_SEQS_PER_STEP, rows=PROMPT_ROWS)
    ffn_w = (norm_mlp.reshape(1, D_MODEL), wup, wdown, norm_ple.reshape(1, D_MODEL), wgate,
             w_ple_proj.astype(BF16), norm_f.reshape(1, D_MODEL))
    y_p = _ffn_call(xm_p.reshape(bp * tp, D_MODEL), p_prompt.reshape(bp * tp, PLE_DIM), ffn_w, rows=FFN_ROWS,
                    name="ffn_prompt")

    seq_per_tile = CHUNK // ts
    tiles = bs // seq_per_tile
    tiled = lambda a: a.reshape(tiles, CHUNK, a.shape[-1])
    mix_s, qt_s, new_sc_s, s_s = _mixer_step_call(
        tiled(ab_s), tiled(proj_s), jnp.transpose(conv_qkv, (1, 0, 2)), conv_sc,
        s_gdn.reshape(tiles, seq_per_tile, GDN_HEADS, HEAD_DIM, HEAD_DIM), core_w, seg_len=ts)
    y_s = _ffn_call(x_sample, p_sample, ffn_w, name="ffn_sample", mix=mix_s.reshape(bs * ts, D_MODEL), wout=wout)

    new_conv_p = qt_p[:, SUBLANES - (GDN_CONV - 1):]
    new_sc_p = st_p[:, SUBLANES - (SC_CONV - 1):]
    new_conv_s = jnp.transpose(qt_s, (1, 0, 2))
    return (y_p.reshape(bp, tp, D_MODEL), y_s, new_conv_p,
            s_p.reshape(bp, GDN_HEADS, HEAD_DIM, HEAD_DIM), new_sc_p, new_conv_s,
            s_s.reshape(bs, GDN_HEADS, HEAD_DIM, HEAD_DIM), new_sc_s)


def kernel(x_prompt, x_sample, state_gdn_conv, state_gdn, state_sc_conv, p_prompt, p_sample, norm_mix, w_in, w_conv_qkv, a_log, dt_bias, w_gdn_norm, w_conv_sc, w_out, norm_mlp, w_up, w_down, norm_ple, w_ple_gate, w_ple_proj, norm_f):
    depth = w_in.shape[0]
    assert depth == 1, "one layer per call"
    assert x_sample.shape[1] >= GDN_CONV - 1 and CHUNK % x_sample.shape[1] == 0
    assert x_prompt.shape[1] % PROMPT_ROWS == 0 and x_prompt.shape[0] % PROMPT_SEQS_PER_STEP == 0
    outs = _layer(x_prompt, x_sample, state_gdn_conv[0], state_gdn[0], state_sc_conv[0], p_prompt[0], p_sample[0],
                  norm_mix[0], w_in[0], w_conv_qkv[0], a_log[0], dt_bias[0], w_gdn_norm[0], w_conv_sc[0], w_out[0],
                  norm_mlp[0], w_up[0], w_down[0], norm_ple[0], w_ple_gate[0], w_ple_proj[0], norm_f)
    y_p, y_s, c_p, s_p, sc_p, c_s, s_s, sc_s = outs
    return (y_p, y_s, c_p[None], s_p[None], sc_p[None], c_s[None], s_s[None], sc_s[None])
```

```python
import functools

import jax
import jax.numpy as jnp
from jax import lax
from jax.experimental import pallas as pl
from jax.experimental.pallas import tpu as pltpu

F32 = jnp.float32
BF16 = jnp.bfloat16

D_MODEL = 1024
PLE_DIM = 256
GDN_HEADS = 4
HEAD_DIM = 128
GDN_WIDTH = GDN_HEADS * HEAD_DIM
QKV_DIM = 3 * GDN_WIDTH
GDN_CONV = 4
SC_WIDTH = D_MODEL - GDN_WIDTH
SC_CONV = 3
D_FF = 4 * D_MODEL
EPS = 1e-6
NEG_LOG2_E = -1.4426950408889634
CHUNK = 64
LANES = 128
SUBLANES = 8
VMEM_LIMIT_BYTES = 56 * 1024 * 1024
PROMPT_ROWS = 128
PROMPT_SEQS_PER_STEP = 4
FFN_ROWS = 1024
IN_PROJ_COLS = 512
SPLIT_COLS = 256

R_Z = 0
R_SCB = R_Z + GDN_WIDTH
R_SCC = R_SCB + SC_WIDTH
R_SCH = R_SCC + SC_WIDTH


def _mm(a, b):
    return jnp.dot(a.astype(BF16), b.astype(BF16), preferred_element_type=F32)


def _mm_nt(a, b):
    return lax.dot_general(a.astype(BF16), b.astype(BF16), (((1,), (1,)), ((), ())), preferred_element_type=F32)


def _mm_tn(a, b):
    return lax.dot_general(a.astype(BF16), b.astype(BF16), (((0,), (0,)), ((), ())), preferred_element_type=F32)


def _rmsnorm(x, w_row):
    return x * lax.rsqrt(jnp.mean(x * x, axis=-1, keepdims=True) + EPS) * w_row


def _sigmoid(x):
    return 1.0 / (1.0 + jnp.exp2(x * NEG_LOG2_E))


def _silu(x):
    return x * _sigmoid(x)


def _softplus(x):
    return jnp.maximum(x, 0.0) + jnp.log1p(jnp.exp(-jnp.abs(x)))


def _log2(n):
    k = n.bit_length() - 1
    assert (1 << k) == n, n
    return k


def _wy_block(q, k, v, g_col, g_row, beta_col, seg_len, load_state, store_state):
    c = CHUNK
    n_seg = c // seg_len
    lg = _log2(seg_len)
    ls = _log2(n_seg)
    ri = lax.broadcasted_iota(jnp.int32, (c, c), 0)
    ci = lax.broadcasted_iota(jnp.int32, (c, c), 1)
    same = jnp.bitwise_and(ri, n_seg - 1) == jnp.bitwise_and(ci, n_seg - 1)
    lower = same & (ri >= ci)
    strict = same & (ri > ci)
    diff = g_col - g_row
    decay = jnp.where(lower, jnp.exp(jnp.where(lower, diff, 0.0)), 0.0)
    a_off = jnp.where(strict, beta_col * _mm_nt(k, k) * decay, 0.0)
    qk = _mm_nt(q, k) * decay
    yield
    eye = jnp.where(ri == ci, 1.0, 0.0).astype(F32)
    x_inv = eye
    for lb in range(lg):
        rb = jnp.right_shift(ri, ls + lb)
        cb = jnp.right_shift(ci, ls + lb)
        join = (jnp.bitwise_and(rb, 1) == 1) & (cb == rb - 1)
        b_lvl = jnp.where(join, a_off, 0.0)
        if lb == 0:
            x_inv = eye - b_lvl
        else:
            xb = _mm(x_inv, b_lvl)
            yield
            x_inv = x_inv - _mm(xb, x_inv)
            yield
    rhs = jnp.concatenate([v * beta_col, k * (beta_col * jnp.exp(g_col))], axis=1)
    sol = _mm(x_inv, rhs)
    yield
    u = sol[:, :HEAD_DIM]
    w = sol[:, HEAD_DIM:]
    stacked = jnp.concatenate([w, q * jnp.exp(g_col)], axis=0)
    row = lax.broadcasted_iota(jnp.int32, (c, 1), 0)
    states = [load_state(s) for s in range(n_seg)]
    w_s = None
    q_s = None
    for s in range(n_seg):
        r = _mm(stacked, states[s])
        if n_seg == 1:
            w_s, q_s = r[:c], r[c:]
        else:
            in_seg = jnp.bitwise_and(row, n_seg - 1) == s
            w_s = jnp.where(in_seg, r[:c], 0.0 if w_s is None else w_s)
            q_s = jnp.where(in_seg, r[c:], 0.0 if q_s is None else q_s)
    v_new = u - w_s
    yield
    o = q_s + _mm(qk, v_new)
    for s in range(n_seg):
        last = (seg_len - 1) * n_seg + s
        g_last = g_col[last:last + 1, :]
        if n_seg == 1:
            k_dec = k * jnp.exp(g_last - g_col)
        else:
            in_seg = jnp.bitwise_and(row, n_seg - 1) == s
            k_dec = jnp.where(in_seg, k * jnp.exp(jnp.where(in_seg, g_last - g_col, 0.0)), 0.0)
        store_state(s, states[s] * jnp.exp(g_last) + _mm_tn(k_dec, v_new))
    return o


def _expand_heads(slab, first_lane, width):
    rows = slab.shape[0]
    cols = [slab[:, first_lane + hd:first_lane + hd + 1] for hd in range(GDN_HEADS)]
    if width % LANES == 0:
        return jnp.concatenate([jnp.broadcast_to(c, (rows, width)) for c in cols], axis=1)
    total = GDN_HEADS * width
    lane_head = jnp.right_shift(lax.broadcasted_iota(jnp.int32, (rows, total), 1), _log2(width))
    out = jnp.broadcast_to(cols[-1], (rows, total))
    for hd in range(GDN_HEADS - 2, -1, -1):
        out = jnp.where(lane_head == hd, jnp.broadcast_to(cols[hd], (rows, total)), out)
    return out


def _packed_masks():
    c, nh = CHUNK, GDN_HEADS
    ri = lax.broadcasted_iota(jnp.int32, (c, nh * c), 0)
    cj = jnp.bitwise_and(lax.broadcasted_iota(jnp.int32, (c, nh * c), 1), c - 1)
    return dict(lower=ri >= cj, strict=ri > cj, eye=ri == cj, join0=(jnp.bitwise_and(ri, 1) == 1) & (cj == ri - 1))


def _fill_block_factors(bd_ref, kbd_ref):
    c, nh = CHUNK, GDN_HEADS
    w = nh * c
    lg = _log2(c)
    rw = lax.broadcasted_iota(jnp.int32, (w, w), 0)
    cw = lax.broadcasted_iota(jnp.int32, (w, w), 1)
    same_head = jnp.right_shift(rw, lg) == jnp.right_shift(cw, lg)
    rw, cw = jnp.bitwise_and(rw, c - 1), jnp.bitwise_and(cw, c - 1)
    one_zero = lambda m: jnp.where(m, 1.0, 0.0).astype(BF16)
    bd_ref[0] = one_zero(same_head)
    for lb in range(1, lg):
        bd_ref[lb] = one_zero(same_head & (jnp.bitwise_and(jnp.right_shift(rw, lb), 1) == 1)
                              & (jnp.right_shift(cw, lb) == jnp.right_shift(rw, lb) - 1))
    kbd_ref[...] = one_zero(jnp.right_shift(lax.broadcasted_iota(jnp.int32, (w, nh * HEAD_DIM), 0), lg)
                            == jnp.right_shift(lax.broadcasted_iota(jnp.int32, (w, nh * HEAD_DIM), 1),
                                               _log2(HEAD_DIM)))


def _wy_tile_packed(q_all, k_all, v_all, g, beta, state_ref, rows, masks, bd_ref, kbd_ref):
    c, nh = CHUNK, GDN_HEADS
    lg = _log2(c)
    n_chunks = rows // c
    lower, strict, eye = masks["lower"], masks["strict"], masks["eye"]

    def block_diag(x, factor):
        return jnp.concatenate([x.astype(BF16)] * nh, axis=0) * factor

    def head(x, hd, width):
        return x[:, hd * width:(hd + 1) * width]

    g_c = _expand_heads(g, 0, c)
    b_c = _expand_heads(beta, nh, c)
    g_d = _expand_heads(g, 0, HEAD_DIM)
    b_d = _expand_heads(beta, nh, HEAD_DIM)
    eg_d = jnp.exp(g_d)
    v_rhs = v_all * b_d
    k_rhs = k_all * (b_d * eg_d)
    q_g = q_all * eg_d

    a_off, qk_dec = [], []
    for cb in range(n_chunks):
        rs = slice(cb * c, (cb + 1) * c)
        g_row = jnp.sum(jnp.where(eye, g_c[rs], 0.0), axis=0, keepdims=True)
        decay = jnp.where(lower, jnp.exp(jnp.where(lower, g_c[rs] - g_row, 0.0)), 0.0)
        kq = _mm_nt(jnp.concatenate([k_all[rs], q_all[rs]], axis=0), block_diag(k_all[rs], kbd_ref[...]))
        a_off.append(jnp.where(strict, b_c[rs] * kq[:c] * decay, 0.0))
        qk_dec.append(kq[c:] * decay)
    yield

    eye_f = jnp.where(eye, 1.0, 0.0).astype(F32)
    x_inv = [eye_f - jnp.where(masks["join0"], a, 0.0) for a in a_off]
    for lb in range(1, lg):
        xb = [jnp.dot(x.astype(BF16), block_diag(a, bd_ref[lb]), preferred_element_type=F32)
              for x, a in zip(x_inv, a_off)]
        yield
        xbx = [jnp.dot(t.astype(BF16), block_diag(x, bd_ref[0]), preferred_element_type=F32)
               for t, x in zip(xb, x_inv)]
        x_inv = [x - t for x, t in zip(x_inv, xbx)]
        yield

    u, wk = [], []
    for cb in range(n_chunks):
        rs = slice(cb * c, (cb + 1) * c)
        sol = [_mm(head(x_inv[cb], hd, c),
                   jnp.concatenate([head(v_rhs[rs], hd, HEAD_DIM), head(k_rhs[rs], hd, HEAD_DIM)], axis=1))
               for hd in range(nh)]
        u.append([s[:, :HEAD_DIM] for s in sol])
        wk.append([s[:, HEAD_DIM:] for s in sol])
    yield

    state = [state_ref[0, hd] for hd in range(nh)]
    o_blocks = []
    for cb in range(n_chunks):
        rs = slice(cb * c, (cb + 1) * c)
        g_last = g_d[(cb + 1) * c - 1:(cb + 1) * c, :]
        k_dec = k_all[rs] * jnp.exp(g_last - g_d[rs])
        eg_last = jnp.exp(g_last)
        r = [_mm(jnp.concatenate([wk[cb][hd], head(q_g[rs], hd, HEAD_DIM)], axis=0), state[hd]) for hd in range(nh)]
        v_new = [u[cb][hd] - r[hd][:c] for hd in range(nh)]
        yield
        o_blocks.append(jnp.concatenate(
            [r[hd][c:] + _mm(head(qk_dec[cb], hd, c), v_new[hd]) for hd in range(nh)], axis=1))
        state = [state[hd] * head(eg_last, hd, HEAD_DIM) + _mm_tn(head(k_dec, hd, HEAD_DIM), v_new[hd])
                 for hd in range(nh)]
        yield
    for hd in range(nh):
        state_ref[0, hd] = state[hd]
    return jnp.concatenate(o_blocks, axis=0)


def _run_tiles(programs):
    programs = list(programs)
    results = [None] * len(programs)
    live = list(range(len(programs)))
    while live:
        for idx in list(live):
            try:
                next(programs[idx])
            except StopIteration as done:
                results[idx] = done.value
                live.remove(idx)
    return results


def _shifted_history(buf, rows, n_taps):
    with_carry = buf[0:SUBLANES + rows, :]
    return [pltpu.roll(with_carry, s, 0)[SUBLANES:, :] for s in range(1, n_taps)]


def _shifted_positions(x_rows, taps):
    n_seq = taps[0].shape[0]
    rows = x_rows.shape[0]
    ext = jnp.concatenate(list(taps) + [x_rows], axis=0)
    return [ext[(len(taps) - s) * n_seq:(len(taps) - s) * n_seq + rows] for s in range(1, len(taps) + 1)]


def _mixer_tile(ti, rows, seg_len, long_seq, masks, r, ab, rest):
    n_seq = rows // seg_len
    new_rows = slice(SUBLANES, SUBLANES + rows)
    row = lax.broadcasted_iota(jnp.int32, (rows, 1), 0)
    if long_seq:
        qb, sb = r.qbuf.at[ti], r.sbuf.at[ti]
        pos, row_step = jnp.bitwise_and(row, seg_len - 1), 1
    else:
        assert rows == CHUNK
        pos, row_step = jnp.right_shift(row, _log2(n_seq)), n_seq

    gate = r.gate[...]
    log_a = -jnp.exp(gate[0:1, :]) * _softplus(ab + gate[1:2, :])
    beta = _sigmoid(ab)
    g = log_a
    shift = 1
    while shift < seg_len:
        g = g + jnp.where(pos >= shift, pltpu.roll(g, shift * row_step, 0), 0.0)
        shift *= 2

    sc_pre = rest[:, R_SCC:R_SCC + SC_WIDTH] * rest[:, R_SCH:R_SCH + SC_WIDTH]
    if long_seq:
        sb[new_rows, :] = sc_pre
        qkv_pre = qb[new_rows, :]
        cwq = r.cwq[...]
        with_carry = qb[0:SUBLANES + rows, :]
        back1 = pltpu.roll(with_carry, 1, 0)
        pair = with_carry * cwq[1:2, :] + back1 * cwq[0:1, :]
        qkv = _silu(pltpu.roll(pair, 2, 0)[SUBLANES:, :] + back1[SUBLANES:, :] * cwq[2:3, :]
                    + qkv_pre * cwq[3:4, :])
        r.qtail[ti] = qb[rows:rows + SUBLANES, :]
        r.stail[ti] = sb[rows:rows + SUBLANES, :]
    else:
        qkv_pre = r.qkv
        cwq = r.cwq[...]
        q1, q2, q3 = _shifted_positions(qkv_pre, r.hq)
        qkv = _silu(q3 * cwq[0:1, :] + q2 * cwq[1:2, :] + q1 * cwq[2:3, :] + qkv_pre * cwq[3:4, :])
        for j in range(SC_CONV - 1):
            first = (seg_len - (SC_CONV - 1) + j) * n_seq
            r.stail[:, j, :] = sc_pre[first:first + n_seq]
    q_n, k_n, v_n = [], [], []
    for hd in range(GDN_HEADS):
        lo = hd * HEAD_DIM
        q_h = qkv[:, lo:lo + HEAD_DIM]
        k_h = qkv[:, GDN_WIDTH + lo:GDN_WIDTH + lo + HEAD_DIM]
        q_n.append(q_h * (lax.rsqrt(jnp.sum(q_h * q_h, axis=-1, keepdims=True) + EPS) * (HEAD_DIM ** -0.5)))
        k_n.append(k_h * lax.rsqrt(jnp.sum(k_h * k_h, axis=-1, keepdims=True) + EPS))
        v_n.append(qkv[:, 2 * GDN_WIDTH + lo:2 * GDN_WIDTH + lo + HEAD_DIM])
    yield

    state_ref = r.s_out.at[ti]
    if long_seq:
        o_all = yield from _wy_tile_packed(jnp.concatenate(q_n, axis=1), jnp.concatenate(k_n, axis=1),
                                           jnp.concatenate(v_n, axis=1), g, beta, state_ref, rows, masks,
                                           r.bd, r.kbd)
        o_h = [o_all[:, hd * HEAD_DIM:(hd + 1) * HEAD_DIM] for hd in range(GDN_HEADS)]
    else:
        g_t = g.T
        blocks = []
        for hd in range(GDN_HEADS):

            def load_state(s, hd=hd):
                return r.s_in[ti, s, hd]

            def store_state(s, val, hd=hd):
                state_ref[s, hd] = val

            blocks.append(_wy_block(q_n[hd], k_n[hd], v_n[hd], g[:, hd:hd + 1], g_t[hd:hd + 1, :],
                                    beta[:, GDN_HEADS + hd:GDN_HEADS + hd + 1], seg_len, load_state, store_state))
        o_h = _run_tiles(blocks)
    yield

    s1, s2 = _shifted_history(sb, rows, SC_CONV) if long_seq else _shifted_positions(sc_pre, r.hs)
    cws = r.cws[...]
    y_sc = rest[:, R_SCB:R_SCB + SC_WIDTH] * (s2 * cws[0:1, :] + s1 * cws[1:2, :] + sc_pre * cws[2:3, :])
    if long_seq:
        qb[0:SUBLANES, :] = qb[rows:rows + SUBLANES, :]
        sb[0:SUBLANES, :] = sb[rows:rows + SUBLANES, :]
    gnorm = r.gnorm[...]
    o_heads = []
    for hd in range(GDN_HEADS):
        z_h = rest[:, R_Z + hd * HEAD_DIM:R_Z + (hd + 1) * HEAD_DIM]
        o_heads.append(_rmsnorm(o_h[hd], gnorm) * _silu(z_h))
    return jnp.concatenate(o_heads + [y_sc], axis=1)


def _load_by_position(ref):
    n_seq, seg_len, _ = ref.shape
    per_tile = CHUNK // seg_len
    return jnp.concatenate([ref[b:b + per_tile, t, :] for b in range(0, n_seq, per_tile) for t in range(seg_len)],
                           axis=0)


def _store_by_position(ref, rows):
    n_seq, seg_len, _ = ref.shape
    per_tile = CHUNK // seg_len
    for b in range(0, n_seq, per_tile):
        for t in range(seg_len):
            first = b * seg_len + t * per_tile
            ref[b:b + per_tile, t, :] = rows[first:first + per_tile]


class _Refs:
    def __init__(self, **refs):
        self.__dict__.update(refs)


def _mixer_seq_kernel(n_tiles, rows, n_cast, *refs):
    n_in = 12
    x, hq, hs, s_in, nmix, w_ab, w_main, cwq, gate, gnorm, cws, wout = refs[:n_in]
    cast_in = refs[n_in:n_in + n_cast]
    xmid, qtail, stail, s_out = refs[n_in + n_cast:n_in + n_cast + 4]
    cast_out = refs[n_in + n_cast + 4:n_in + 2 * n_cast + 4]
    qbuf, sbuf, bd, kbd = refs[n_in + 2 * n_cast + 4:]
    for src, dst in zip(cast_in, cast_out):
        dst[...] = src[...].astype(BF16)

    @pl.when(pl.program_id(1) == 0)
    def _():
        s_out[...] = s_in[...]
        qbuf[:, 0:SUBLANES, :] = hq[...]
        sbuf[:, 0:SUBLANES, :] = hs[...]
        _fill_block_factors(bd, kbd)

    x_all = x[...].reshape(n_tiles * rows, D_MODEL)
    h = _rmsnorm(x_all, nmix[...]).astype(BF16)
    ab = jnp.dot(h, w_ab[...], preferred_element_type=F32)
    qbuf[:, SUBLANES:SUBLANES + rows, :] = jnp.dot(h, w_main[:, 0:QKV_DIM], preferred_element_type=F32).reshape(
        n_tiles, rows, QKV_DIM)
    rest = jnp.dot(h, w_main[:, QKV_DIM:], preferred_element_type=F32)
    r = _Refs(cwq=cwq, gate=gate, gnorm=gnorm, cws=cws, qtail=qtail, stail=stail, s_out=s_out, qbuf=qbuf, sbuf=sbuf,
              bd=bd, kbd=kbd)
    masks = _packed_masks()
    tile = lambda a, ti: a[ti * rows:(ti + 1) * rows]
    mix = _run_tiles(_mixer_tile(ti, rows, CHUNK, True, masks, r, tile(ab, ti), tile(rest, ti))
                     for ti in range(n_tiles))
    xmid[...] = (x_all + _mm(jnp.concatenate(mix, axis=0), wout[...])).reshape(n_tiles, rows, D_MODEL)


def _mixer_step_kernel(rows, seg_len,
                       ab, proj, hq, hs, s_in, cwq, gate, gnorm, cws,
                       mix, qtail, stail, s_out):
    n_seq = rows // seg_len
    qkv = proj[0, :, 0:QKV_DIM]
    for j in range(GDN_CONV - 1):
        first = (seg_len - (GDN_CONV - 1) + j) * n_seq
        qtail[j] = qkv[first:first + n_seq]
    r = _Refs(qkv=qkv, hq=[hq[j] for j in range(GDN_CONV - 1)], hs=[hs[:, j, :] for j in range(SC_CONV - 1)],
              cwq=cwq, gate=gate, gnorm=gnorm, cws=cws, stail=stail, s_in=s_in, s_out=s_out)
    mix[0] = _run_tiles([_mixer_tile(0, rows, seg_len, False, None, r, ab[0], proj[0, :, QKV_DIM:])])[0].astype(BF16)


def _ffn_body(x, p, nmlp_ref, wup_ref, wdown_ref, nple_ref, wg_ref, wp_ref, nf_ref):
    hn = _rmsnorm(x, nmlp_ref[...]).astype(BF16)
    acc = x
    for j in range(D_FF // D_MODEL):
        u = jnp.maximum(_mm(hn, wup_ref[:, j * D_MODEL:(j + 1) * D_MODEL]), 0.0)
        acc = acc + _mm(u * u, wdown_ref[j * D_MODEL:(j + 1) * D_MODEL, :])
    gate = _sigmoid(_mm(_rmsnorm(acc, nple_ref[...]), wg_ref[...]))
    x3 = acc + gate * _mm(p, wp_ref[...])
    return _rmsnorm(x3, nf_ref[...])


def _ffn_kernel(x_ref, p_ref, *rest):
    rest[-1][...] = _ffn_body(x_ref[...], p_ref[...], *rest[:-1])


def _out_ffn_kernel(x_ref, mix_ref, wout_ref, p_ref, *rest):
    x = _load_by_position(x_ref) + jnp.dot(mix_ref[...], wout_ref[...], preferred_element_type=F32)
    _store_by_position(rest[-1], _ffn_body(x, _load_by_position(p_ref), *rest[:-1]))


def _const_spec(shape):
    nd = len(shape)
    return pl.BlockSpec(shape, lambda *_: (0,) * nd, pipeline_mode=pl.Buffered(1))


def _state_spec(n_tiles, n_state):
    return pl.BlockSpec((n_tiles, n_state, GDN_HEADS, HEAD_DIM, HEAD_DIM), lambda b, i: (b, 0, 0, 0, 0))


def _mixer_seq_call(x3, hist_q, hist_s, s_in, weights, to_cast, *, n_tiles, rows):
    seqs, seq_rows, _ = x3.shape
    tiles = seq_rows // rows
    n_steps = (seqs // n_tiles) * tiles
    cast_specs = [pl.BlockSpec((w.shape[0] // n_steps, w.shape[1]), lambda b, i: (b * tiles + i, 0)) for w in to_cast]
    row_map = lambda b, i: (b, i, 0)
    seq_map = lambda b, i: (b, 0, 0)
    x_spec = pl.BlockSpec((n_tiles, rows, D_MODEL), row_map)
    hq_spec = pl.BlockSpec((n_tiles, SUBLANES, QKV_DIM), seq_map)
    hs_spec = pl.BlockSpec((n_tiles, SUBLANES, SC_WIDTH), seq_map)
    packed_w = GDN_HEADS * CHUNK
    return pl.pallas_call(
        functools.partial(_mixer_seq_kernel, n_tiles, rows, len(to_cast)),
        grid=(seqs // n_tiles, tiles),
        in_specs=[x_spec, hq_spec, hs_spec, _state_spec(n_tiles, 1)] + [_const_spec(w.shape) for w in weights]
        + cast_specs,
        out_specs=[x_spec, hq_spec, hs_spec, _state_spec(n_tiles, 1)] + cast_specs,
        out_shape=[jax.ShapeDtypeStruct(x3.shape, F32), jax.ShapeDtypeStruct(hist_q.shape, F32),
                   jax.ShapeDtypeStruct(hist_s.shape, F32), jax.ShapeDtypeStruct(s_in.shape, F32)]
        + [jax.ShapeDtypeStruct(w.shape, BF16) for w in to_cast],
        scratch_shapes=[pltpu.VMEM((n_tiles, rows + SUBLANES, QKV_DIM), F32),
                        pltpu.VMEM((n_tiles, rows + SUBLANES, SC_WIDTH), F32),
                        pltpu.VMEM((_log2(CHUNK), packed_w, packed_w), BF16),
                        pltpu.VMEM((packed_w, GDN_WIDTH), BF16)],
        compiler_params=pltpu.CompilerParams(
            dimension_semantics=("arbitrary", "arbitrary"), vmem_limit_bytes=VMEM_LIMIT_BYTES),
        name="mixer_seq",
    )(x3, hist_q, hist_s, s_in, *weights, *to_cast)


def _in_proj_kernel(n_qkv_steps, wt_ref, behind_ref, x_ref, nmix_ref, w_ab_ref, w_main_ref, ab_ref, proj_ref, h_ref):
    i = pl.program_id(0)

    @pl.when(i == 0)
    def _():
        h_ref[...] = _rmsnorm(_load_by_position(x_ref), nmix_ref[...]).astype(BF16)

    on_grid = wt_ref[...]
    behind = behind_ref[...]
    group = jnp.where(i < n_qkv_steps, on_grid, jnp.concatenate([on_grid[SUBLANES:], behind], axis=0))
    for c0 in range(0, IN_PROJ_COLS, SPLIT_COLS):
        w_main_ref[:, c0:c0 + SPLIT_COLS] = group[c0:c0 + SPLIT_COLS].T.astype(BF16)
    proj_ref[...] = jnp.dot(h_ref[...], w_main_ref[...], preferred_element_type=F32)

    @pl.when(i == n_qkv_steps - 1)
    def _():
        w_ab = jnp.concatenate([behind, jnp.zeros((LANES - SUBLANES, D_MODEL), F32)], axis=0).T.astype(BF16)
        w_ab_ref[...] = w_ab
        ab_ref[...] = jnp.dot(h_ref[...], w_ab, preferred_element_type=F32)


def _full_spec(a):
    nd = len(a.shape)
    return pl.BlockSpec(a.shape, lambda i: (0,) * nd)


def _in_proj_call(w_in, x3, nmix):
    rows, cols = w_in.shape
    n_seq, seg_len, _ = x3.shape
    n = n_seq * seg_len
    n_main = cols - 2 * GDN_HEADS
    assert 2 * GDN_HEADS == SUBLANES and QKV_DIM % IN_PROJ_COLS == 0 and n_main % IN_PROJ_COLS == 0
    w_t = w_in.T
    col_block = lambda height: pl.BlockSpec((height, IN_PROJ_COLS), lambda i: (0, i))
    outs = [jax.ShapeDtypeStruct((rows, LANES), BF16), jax.ShapeDtypeStruct((rows, n_main), BF16),
            jax.ShapeDtypeStruct((n, LANES), F32), jax.ShapeDtypeStruct((n, n_main), F32)]
    return pl.pallas_call(
        functools.partial(_in_proj_kernel, QKV_DIM // IN_PROJ_COLS),
        grid=(n_main // IN_PROJ_COLS,),
        in_specs=[pl.BlockSpec((IN_PROJ_COLS, rows), lambda i: (i, 0)),
                  pl.BlockSpec((SUBLANES, rows), lambda i: ((i + 1) * (IN_PROJ_COLS // SUBLANES), 0)),
                  _full_spec(x3), _full_spec(nmix)],
        out_specs=[_full_spec(outs[0]), col_block(rows), _full_spec(outs[2]), col_block(n)],
        out_shape=outs,
        scratch_shapes=[pltpu.VMEM((n, rows), BF16)],
        compiler_params=pltpu.CompilerParams(dimension_semantics=("arbitrary",), vmem_limit_bytes=VMEM_LIMIT_BYTES),
        name="in_proj",
    )(w_t, w_t, x3, nmix)


def _mixer_step_call(ab, proj, hist_q, hist_s, s_in, weights, *, seg_len):
    tiles, rows, _ = proj.shape
    tile_map = lambda b, i: (b, 0, 0)
    slab = lambda a: pl.BlockSpec((1, rows, a.shape[2]), tile_map)
    n_state = s_in.shape[1]
    hist_q_spec = pl.BlockSpec((GDN_CONV - 1, n_state, QKV_DIM), lambda b, i: (0, b, 0))
    hist_s_spec = pl.BlockSpec((n_state, SC_CONV - 1, SC_WIDTH), tile_map)
    mix = jax.ShapeDtypeStruct((tiles, rows, D_MODEL), BF16)
    like = lambda a: jax.ShapeDtypeStruct(a.shape, F32)
    return pl.pallas_call(
        functools.partial(_mixer_step_kernel, rows, seg_len),
        grid=(tiles, 1),
        in_specs=[slab(ab), slab(proj), hist_q_spec, hist_s_spec, _state_spec(1, n_state)]
        + [_const_spec(w.shape) for w in weights],
        out_specs=[slab(mix), hist_q_spec, hist_s_spec, _state_spec(1, n_state)],
        out_shape=[mix, like(hist_q), like(hist_s), like(s_in)],
        compiler_params=pltpu.CompilerParams(
            dimension_semantics=("arbitrary", "arbitrary"), vmem_limit_bytes=VMEM_LIMIT_BYTES),
        name="mixer_step",
    )(ab, proj, hist_q, hist_s, s_in, *weights)


def _ffn_call(x, p, weights, *, name, rows=None, mix=None, wout=None):
    if mix is None:
        steps = x.shape[0] // rows
        rows_spec = lambda width: pl.BlockSpec((rows, width), lambda i: (i, 0))
        x_spec, p_spec, pre_args, pre_specs, body = rows_spec(D_MODEL), rows_spec(PLE_DIM), [], [], _ffn_kernel
    else:
        steps = 1
        x_spec, p_spec, body = _full_spec(x), _full_spec(p), _out_ffn_kernel
        pre_args, pre_specs = [mix, wout], [_full_spec(mix), _const_spec(wout.shape)]
    return pl.pallas_call(
        body,
        grid=(steps,),
        in_specs=[x_spec] + pre_specs + [p_spec] + [_const_spec(w.shape) for w in weights],
        out_specs=x_spec,
        out_shape=jax.ShapeDtypeStruct(x.shape, F32),
        compiler_params=pltpu.CompilerParams(dimension_semantics=("arbitrary",), vmem_limit_bytes=VMEM_LIMIT_BYTES),
        name=name,
    )(x, *pre_args, p, *weights)


def _layer(x_prompt, x_sample, conv_qkv, s_gdn, conv_sc, p_prompt, p_sample, norm_mix, w_in, w_conv_qkv, a_log,
           dt_bias, w_gdn_norm, w_conv_sc, w_out, norm_mlp, w_up, w_down, norm_ple, w_ple_gate, w_ple_proj, norm_f):
    bp, tp, _ = x_prompt.shape
    bs, ts, _ = x_sample.shape
    nmix = norm_mix.reshape(1, D_MODEL)
    w_ab, w_main, ab_s, proj_s = _in_proj_call(w_in, x_sample, nmix)
    wout = w_out.astype(BF16)
    gate = jnp.zeros((SUBLANES, LANES), F32)
    gate = gate.at[0, :GDN_HEADS].set(a_log.astype(F32)).at[1, :GDN_HEADS].set(dt_bias.astype(F32))
    core_w = (w_conv_qkv, gate, w_gdn_norm.reshape(1, HEAD_DIM), w_conv_sc)

    zq = jnp.zeros((bp, SUBLANES, QKV_DIM), F32)
    zs = jnp.zeros((bp, SUBLANES, SC_WIDTH), F32)
    s0 = jnp.zeros((bp, 1, GDN_HEADS, HEAD_DIM, HEAD_DIM), F32)
    xm_p, qt_p, st_p, s_p, wup, wdown, wgate = _mixer_seq_call(
        x_prompt, zq, zs, s0, (nmix, w_ab, w_main) + core_w + (wout,), (w_up, w_down, w_ple_gate),
        n_tiles=PROMPT_SEQS_PER_STEP, rows=PROMPT_ROWS)
    ffn_w = (norm_mlp.reshape(1, D_MODEL), wup, wdown, norm_ple.reshape(1, D_MODEL), wgate,
             w_ple_proj.astype(BF16), norm_f.reshape(1, D_MODEL))
    y_p = _ffn_call(xm_p.reshape(bp * tp, D_MODEL), p_prompt.reshape(bp * tp, PLE_DIM), ffn_w, rows=FFN_ROWS,
                    name="ffn_prompt")

    seq_per_tile = CHUNK // ts
    tiles = bs // seq_per_tile
    tiled = lambda a: a.reshape(tiles, CHUNK, a.shape[-1])
    mix_s, qt_s, new_sc_s, s_s = _mixer_step_call(
        tiled(ab_s), tiled(proj_s), jnp.transpose(conv_qkv, (1, 0, 2)), conv_sc,
        s_gdn.reshape(tiles, seq_per_tile, GDN_HEADS, HEAD_DIM, HEAD_DIM), core_w, seg_len=ts)
    y_s = _ffn_call(x_sample, p_sample, ffn_w, name="ffn_sample", mix=mix_s.reshape(bs * ts, D_MODEL), wout=wout)

    new_conv_p = qt_p[:, SUBLANES - (GDN_CONV - 1):]
    new_sc_p = st_p[:, SUBLANES - (SC_CONV - 1):]
    new_conv_s = jnp.transpose(qt_s, (1, 0, 2))
    return (y_p.reshape(bp, tp, D_MODEL), y_s, new_conv_p,
            s_p.reshape(bp, GDN_HEADS, HEAD_DIM, HEAD_DIM), new_sc_p, new_conv_s,
            s_s.reshape(bs, GDN_HEADS, HEAD_DIM, HEAD_DIM), new_sc_s)


def kernel(x_prompt, x_sample, state_gdn_conv, state_gdn, state_sc_conv, p_prompt, p_sample, norm_mix, w_in, w_conv_qkv, a_log, dt_bias, w_gdn_norm, w_conv_sc, w_out, norm_mlp, w_up, w_down, norm_ple, w_ple_gate, w_ple_proj, norm_f):
    depth = w_in.shape[0]
    assert depth == 1, "one layer per call"
    assert x_sample.shape[1] >= GDN_CONV - 1 and CHUNK % x_sample.shape[1] == 0
    assert x_prompt.shape[1] % PROMPT_ROWS == 0 and x_prompt.shape[0] % PROMPT_SEQS_PER_STEP == 0
    outs = _layer(x_prompt, x_sample, state_gdn_conv[0], state_gdn[0], state_sc_conv[0], p_prompt[0], p_sample[0],
                  norm_mix[0], w_in[0], w_conv_qkv[0], a_log[0], dt_bias[0], w_gdn_norm[0], w_conv_sc[0], w_out[0],
                  norm_mlp[0], w_up[0], w_down[0], norm_ple[0], w_ple_gate[0], w_ple_proj[0], norm_f)
    y_p, y_s, c_p, s_p, sc_p, c_s, s_s, sc_s = outs
    return (y_p, y_s, c_p[None], s_p[None], sc_p[None], c_s[None], s_s[None], sc_s[None])
```

```python
import functools

import jax
import jax.numpy as jnp
from jax import lax
from jax.experimental import pallas as pl
from jax.experimental.pallas import tpu as pltpu

F32 = jnp.float32
BF16 = jnp.bfloat16

D_MODEL = 1024
PLE_DIM = 256
GDN_HEADS = 4
HEAD_DIM = 128
GDN_WIDTH = GDN_HEADS * HEAD_DIM
QKV_DIM = 3 * GDN_WIDTH
GDN_CONV = 4
SC_WIDTH = D_MODEL - GDN_WIDTH
SC_CONV = 3
D_FF = 4 * D_MODEL
EPS = 1e-6
NEG_LOG2_E = -1.4426950408889634
CHUNK = 64
LANES = 128
SUBLANES = 8
VMEM_LIMIT_BYTES = 56 * 1024 * 1024
PROMPT_ROWS = 128
PROMPT_SEQS_PER_STEP = 4
FFN_ROWS = 1024
IN_PROJ_COLS = 512
SPLIT_COLS = 256

R_Z = 0
R_SCB = R_Z + GDN_WIDTH
R_SCC = R_SCB + SC_WIDTH
R_SCH = R_SCC + SC_WIDTH


def _mm(a, b):
    return jnp.dot(a.astype(BF16), b.astype(BF16), preferred_element_type=F32)


def _mm_nt(a, b):
    return lax.dot_general(a.astype(BF16), b.astype(BF16), (((1,), (1,)), ((), ())), preferred_element_type=F32)


def _mm_tn(a, b):
    return lax.dot_general(a.astype(BF16), b.astype(BF16), (((0,), (0,)), ((), ())), preferred_element_type=F32)


def _rmsnorm(x, w_row):
    return x * lax.rsqrt(jnp.mean(x * x, axis=-1, keepdims=True) + EPS) * w_row


def _sigmoid(x):
    return 1.0 / (1.0 + jnp.exp2(x * NEG_LOG2_E))


def _silu(x):
    return x * _sigmoid(x)


def _softplus(x):
    return jnp.maximum(x, 0.0) + jnp.log1p(jnp.exp(-jnp.abs(x)))


def _log2(n):
    k = n.bit_length() - 1
    assert (1 << k) == n, n
    return k


def _wy_block(q, k, v, g_col, g_row, beta_col, seg_len, load_state, store_state):
    c = CHUNK
    n_seg = c // seg_len
    lg = _log2(seg_len)
    ls = _log2(n_seg)
    ri = lax.broadcasted_iota(jnp.int32, (c, c), 0)
    ci = lax.broadcasted_iota(jnp.int32, (c, c), 1)
    same = jnp.bitwise_and(ri, n_seg - 1) == jnp.bitwise_and(ci, n_seg - 1)
    lower = same & (ri >= ci)
    strict = same & (ri > ci)
    diff = g_col - g_row
    decay = jnp.where(lower, jnp.exp(jnp.where(lower, diff, 0.0)), 0.0)
    a_off = jnp.where(strict, beta_col * _mm_nt(k, k) * decay, 0.0)
    qk = _mm_nt(q, k) * decay
    yield
    eye = jnp.where(ri == ci, 1.0, 0.0).astype(F32)
    x_inv = eye
    for lb in range(lg):
        rb = jnp.right_shift(ri, ls + lb)
        cb = jnp.right_shift(ci, ls + lb)
        join = (jnp.bitwise_and(rb, 1) == 1) & (cb == rb - 1)
        b_lvl = jnp.where(join, a_off, 0.0)
        if lb == 0:
            x_inv = eye - b_lvl
        else:
            xb = _mm(x_inv, b_lvl)
            yield
            x_inv = x_inv - _mm(xb, x_inv)
            yield
    rhs = jnp.concatenate([v * beta_col, k * (beta_col * jnp.exp(g_col))], axis=1)
    sol = _mm(x_inv, rhs)
    yield
    u = sol[:, :HEAD_DIM]
    w = sol[:, HEAD_DIM:]
    stacked = jnp.concatenate([w, q * jnp.exp(g_col)], axis=0)
    row = lax.broadcasted_iota(jnp.int32, (c, 1), 0)
    states = [load_state(s) for s in range(n_seg)]
    w_s = None
    q_s = None
    for s in range(n_seg):
        r = _mm(stacked, states[s])
        if n_seg == 1:
            w_s, q_s = r[:c], r[c:]
        else:
            in_seg = jnp.bitwise_and(row, n_seg - 1) == s
            w_s = jnp.where(in_seg, r[:c], 0.0 if w_s is None else w_s)
            q_s = jnp.where(in_seg, r[c:], 0.0 if q_s is None else q_s)
    v_new = u - w_s
    yield
    o = q_s + _mm(qk, v_new)
    for s in range(n_seg):
        last = (seg_len - 1) * n_seg + s
        g_last = g_col[last:last + 1, :]
        if n_seg == 1:
            k_dec = k * jnp.exp(g_last - g_col)
        else:
            in_seg = jnp.bitwise_and(row, n_seg - 1) == s
            k_dec = jnp.where(in_seg, k * jnp.exp(jnp.where(in_seg, g_last - g_col, 0.0)), 0.0)
        store_state(s, states[s] * jnp.exp(g_last) + _mm_tn(k_dec, v_new))
    return o


def _expand_heads(slab, first_lane, width):
    rows = slab.shape[0]
    cols = [slab[:, first_lane + hd:first_lane + hd + 1] for hd in range(GDN_HEADS)]
    if width % LANES == 0:
        return jnp.concatenate([jnp.broadcast_to(c, (rows, width)) for c in cols], axis=1)
    total = GDN_HEADS * width
    lane_head = jnp.right_shift(lax.broadcasted_iota(jnp.int32, (rows, total), 1), _log2(width))
    out = jnp.broadcast_to(cols[-1], (rows, total))
    for hd in range(GDN_HEADS - 2, -1, -1):
        out = jnp.where(lane_head == hd, jnp.broadcast_to(cols[hd], (rows, total)), out)
    return out


def _packed_masks():
    c, nh = CHUNK, GDN_HEADS
    ri = lax.broadcasted_iota(jnp.int32, (c, nh * c), 0)
    cj = jnp.bitwise_and(lax.broadcasted_iota(jnp.int32, (c, nh * c), 1), c - 1)
    return dict(lower=ri >= cj, strict=ri > cj, eye=ri == cj, join0=(jnp.bitwise_and(ri, 1) == 1) & (cj == ri - 1))


def _fill_block_factors(bd_ref, kbd_ref):
    c, nh = CHUNK, GDN_HEADS
    w = nh * c
    lg = _log2(c)
    rw = lax.broadcasted_iota(jnp.int32, (w, w), 0)
    cw = lax.broadcasted_iota(jnp.int32, (w, w), 1)
    same_head = jnp.right_shift(rw, lg) == jnp.right_shift(cw, lg)
    rw, cw = jnp.bitwise_and(rw, c - 1), jnp.bitwise_and(cw, c - 1)
    one_zero = lambda m: jnp.where(m, 1.0, 0.0).astype(BF16)
    bd_ref[0] = one_zero(same_head)
    for lb in range(1, lg):
        bd_ref[lb] = one_zero(same_head & (jnp.bitwise_and(jnp.right_shift(rw, lb), 1) == 1)
                              & (jnp.right_shift(cw, lb) == jnp.right_shift(rw, lb) - 1))
    kbd_ref[...] = one_zero(jnp.right_shift(lax.broadcasted_iota(jnp.int32, (w, nh * HEAD_DIM), 0), lg)
                            == jnp.right_shift(lax.broadcasted_iota(jnp.int32, (w, nh * HEAD_DIM), 1),
                                               _log2(HEAD_DIM)))


def _wy_tile_packed(q_all, k_all, v_all, g, beta, state_ref, rows, masks, bd_ref, kbd_ref):
    c, nh = CHUNK, GDN_HEADS
    lg = _log2(c)
    n_chunks = rows // c
    lower, strict, eye = masks["lower"], masks["strict"], masks["eye"]

    def block_diag(x, factor):
        return jnp.concatenate([x.astype(BF16)] * nh, axis=0) * factor

    def head(x, hd, width):
        return x[:, hd * width:(hd + 1) * width]

    g_c = _expand_heads(g, 0, c)
    b_c = _expand_heads(beta, nh, c)
    g_d = _expand_heads(g, 0, HEAD_DIM)
    b_d = _expand_heads(beta, nh, HEAD_DIM)
    eg_d = jnp.exp(g_d)
    v_rhs = v_all * b_d
    k_rhs = k_all * (b_d * eg_d)
    q_g = q_all * eg_d

    a_off, qk_dec = [], []
    for cb in range(n_chunks):
        rs = slice(cb * c, (cb + 1) * c)
        g_row = jnp.sum(jnp.where(eye, g_c[rs], 0.0), axis=0, keepdims=True)
        decay = jnp.where(lower, jnp.exp(jnp.where(lower, g_c[rs] - g_row, 0.0)), 0.0)
        kq = _mm_nt(jnp.concatenate([k_all[rs], q_all[rs]], axis=0), block_diag(k_all[rs], kbd_ref[...]))
        a_off.append(jnp.where(strict, b_c[rs] * kq[:c] * decay, 0.0))
        qk_dec.append(kq[c:] * decay)
    yield

    eye_f = jnp.where(eye, 1.0, 0.0).astype(F32)
    x_inv = [eye_f - jnp.where(masks["join0"], a, 0.0) for a in a_off]
    for lb in range(1, lg):
        xb = [jnp.dot(x.astype(BF16), block_diag(a, bd_ref[lb]), preferred_element_type=F32)
              for x, a in zip(x_inv, a_off)]
        yield
        xbx = [jnp.dot(t.astype(BF16), block_diag(x, bd_ref[0]), preferred_element_type=F32)
               for t, x in zip(xb, x_inv)]
        x_inv = [x - t for x, t in zip(x_inv, xbx)]
        yield

    u, wk = [], []
    for cb in range(n_chunks):
        rs = slice(cb * c, (cb + 1) * c)
        sol = [_mm(head(x_inv[cb], hd, c),
                   jnp.concatenate([head(v_rhs[rs], hd, HEAD_DIM), head(k_rhs[rs], hd, HEAD_DIM)], axis=1))
               for hd in range(nh)]
        u.append([s[:, :HEAD_DIM] for s in sol])
        wk.append([s[:, HEAD_DIM:] for s in sol])
    yield

    state = [state_ref[0, hd] for hd in range(nh)]
    o_blocks = []
    for cb in range(n_chunks):
        rs = slice(cb * c, (cb + 1) * c)
        g_last = g_d[(cb + 1) * c - 1:(cb + 1) * c, :]
        k_dec = k_all[rs] * jnp.exp(g_last - g_d[rs])
        eg_last = jnp.exp(g_last)
        r = [_mm(jnp.concatenate([wk[cb][hd], head(q_g[rs], hd, HEAD_DIM)], axis=0), state[hd]) for hd in range(nh)]
        v_new = [u[cb][hd] - r[hd][:c] for hd in range(nh)]
        yield
        o_blocks.append(jnp.concatenate(
            [r[hd][c:] + _mm(head(qk_dec[cb], hd, c), v_new[hd]) for hd in range(nh)], axis=1))
        state = [state[hd] * head(eg_last, hd, HEAD_DIM) + _mm_tn(head(k_dec, hd, HEAD_DIM), v_new[hd])
                 for hd in range(nh)]
        yield
    for hd in range(nh):
        state_ref[0, hd] = state[hd]
    return jnp.concatenate(o_blocks, axis=0)


def _run_tiles(programs):
    programs = list(programs)
    results = [None] * len(programs)
    live = list(range(len(programs)))
    while live:
        for idx in list(live):
            try:
                next(programs[idx])
            except StopIteration as done:
                results[idx] = done.value
                live.remove(idx)
    return results


def _shifted_history(buf, rows, n_taps):
    with_carry = buf[0:SUBLANES + rows, :]
    return [pltpu.roll(with_carry, s, 0)[SUBLANES:, :] for s in range(1, n_taps)]


def _shifted_positions(x_rows, taps):
    n_seq = taps[0].shape[0]
    rows = x_rows.shape[0]
    ext = jnp.concatenate(list(taps) + [x_rows], axis=0)
    return [ext[(len(taps) - s) * n_seq:(len(taps) - s) * n_seq + rows] for s in range(1, len(taps) + 1)]


def _mixer_tile(ti, rows, seg_len, long_seq, masks, r, ab, rest):
    n_seq = rows // seg_len
    new_rows = slice(SUBLANES, SUBLANES + rows)
    row = lax.broadcasted_iota(jnp.int32, (rows, 1), 0)
    if long_seq:
        qb, sb = r.qbuf.at[ti], r.sbuf.at[ti]
        pos, row_step = jnp.bitwise_and(row, seg_len - 1), 1
    else:
        assert rows == CHUNK
        pos, row_step = jnp.right_shift(row, _log2(n_seq)), n_seq

    gate = r.gate[...]
    log_a = -jnp.exp(gate[0:1, :]) * _softplus(ab + gate[1:2, :])
    beta = _sigmoid(ab)
    g = log_a
    shift = 1
    while shift < seg_len:
        g = g + jnp.where(pos >= shift, pltpu.roll(g, shift * row_step, 0), 0.0)
        shift *= 2

    sc_pre = rest[:, R_SCC:R_SCC + SC_WIDTH] * rest[:, R_SCH:R_SCH + SC_WIDTH]
    if long_seq:
        sb[new_rows, :] = sc_pre
        qkv_pre = qb[new_rows, :]
        cwq = r.cwq[...]
        with_carry = qb[0:SUBLANES + rows, :]
        back1 = pltpu.roll(with_carry, 1, 0)
        pair = with_carry * cwq[1:2, :] + back1 * cwq[0:1, :]
        qkv = _silu(pltpu.roll(pair, 2, 0)[SUBLANES:, :] + back1[SUBLANES:, :] * cwq[2:3, :]
                    + qkv_pre * cwq[3:4, :])
        r.qtail[ti] = qb[rows:rows + SUBLANES, :]
        r.stail[ti] = sb[rows:rows + SUBLANES, :]
    else:
        qkv_pre = r.qkv
        cwq = r.cwq[...]
        q1, q2, q3 = _shifted_positions(qkv_pre, r.hq)
        qkv = _silu(q3 * cwq[0:1, :] + q2 * cwq[1:2, :] + q1 * cwq[2:3, :] + qkv_pre * cwq[3:4, :])
        for j in range(SC_CONV - 1):
            first = (seg_len - (SC_CONV - 1) + j) * n_seq
            r.stail[:, j, :] = sc_pre[first:first + n_seq]
    q_n, k_n, v_n = [], [], []
    for hd in range(GDN_HEADS):
        lo = hd * HEAD_DIM
        q_h = qkv[:, lo:lo + HEAD_DIM]
        k_h = qkv[:, GDN_WIDTH + lo:GDN_WIDTH + lo + HEAD_DIM]
        q_n.append(q_h * (lax.rsqrt(jnp.sum(q_h * q_h, axis=-1, keepdims=True) + EPS) * (HEAD_DIM ** -0.5)))
        k_n.append(k_h * lax.rsqrt(jnp.sum(k_h * k_h, axis=-1, keepdims=True) + EPS))
        v_n.append(qkv[:, 2 * GDN_WIDTH + lo:2 * GDN_WIDTH + lo + HEAD_DIM])
    yield

    state_ref = r.s_out.at[ti]
    if long_seq:
        o_all = yield from _wy_tile_packed(jnp.concatenate(q_n, axis=1), jnp.concatenate(k_n, axis=1),
                                           jnp.concatenate(v_n, axis=1), g, beta, state_ref, rows, masks,
                                           r.bd, r.kbd)
        o_h = [o_all[:, hd * HEAD_DIM:(hd + 1) * HEAD_DIM] for hd in range(GDN_HEADS)]
    else:
        g_t = g.T
        blocks = []
        for hd in range(GDN_HEADS):

            def load_state(s, hd=hd):
                return r.s_in[ti, s, hd]

            def store_state(s, val, hd=hd):
                state_ref[s, hd] = val

            blocks.append(_wy_block(q_n[hd], k_n[hd], v_n[hd], g[:, hd:hd + 1], g_t[hd:hd + 1, :],
                                    beta[:, GDN_HEADS + hd:GDN_HEADS + hd + 1], seg_len, load_state, store_state))
        o_h = _run_tiles(blocks)
    yield

    s1, s2 = _shifted_history(sb, rows, SC_CONV) if long_seq else _shifted_positions(sc_pre, r.hs)
    cws = r.cws[...]
    y_sc = rest[:, R_SCB:R_SCB + SC_WIDTH] * (s2 * cws[0:1, :] + s1 * cws[1:2, :] + sc_pre * cws[2:3, :])
    if long_seq:
        qb[0:SUBLANES, :] = qb[rows:rows + SUBLANES, :]
        sb[0:SUBLANES, :] = sb[rows:rows + SUBLANES, :]
    gnorm = r.gnorm[...]
    o_heads = []
    for hd in range(GDN_HEADS):
        z_h = rest[:, R_Z + hd * HEAD_DIM:R_Z + (hd + 1) * HEAD_DIM]
        o_heads.append(_rmsnorm(o_h[hd], gnorm) * _silu(z_h))
    return jnp.concatenate(o_heads + [y_sc], axis=1)


def _load_by_position(ref):
    n_seq, seg_len, _ = ref.shape
    per_tile = CHUNK // seg_len
    return jnp.concatenate([ref[b:b + per_tile, t, :] for b in range(0, n_seq, per_tile) for t in range(seg_len)],
                           axis=0)


def _store_by_position(ref, rows):
    n_seq, seg_len, _ = ref.shape
    per_tile = CHUNK // seg_len
    for b in range(0, n_seq, per_tile):
        for t in range(seg_len):
            first = b * seg_len + t * per_tile
            ref[b:b + per_tile, t, :] = rows[first:first + per_tile]


class _Refs:
    def __init__(self, **refs):
        self.__dict__.update(refs)


def _mixer_seq_kernel(n_tiles, rows, n_cast, *refs):
    n_in = 12
    x, hq, hs, s_in, nmix, w_ab, w_main, cwq, gate, gnorm, cws, wout = refs[:n_in]
    cast_in = refs[n_in:n_in + n_cast]
    xmid, qtail, stail, s_out = refs[n_in + n_cast:n_in + n_cast + 4]
    cast_out = refs[n_in + n_cast + 4:n_in + 2 * n_cast + 4]
    qbuf, sbuf, bd, kbd = refs[n_in + 2 * n_cast + 4:]
    for src, dst in zip(cast_in, cast_out):
        dst[...] = src[...].astype(BF16)

    @pl.when(pl.program_id(1) == 0)
    def _():
        s_out[...] = s_in[...]
        qbuf[:, 0:SUBLANES, :] = hq[...]
        sbuf[:, 0:SUBLANES, :] = hs[...]
        _fill_block_factors(bd, kbd)

    x_all = x[...].reshape(n_tiles * rows, D_MODEL)
    h = _rmsnorm(x_all, nmix[...]).astype(BF16)
    ab = jnp.dot(h, w_ab[...], preferred_element_type=F32)
    qbuf[:, SUBLANES:SUBLANES + rows, :] = jnp.dot(h, w_main[:, 0:QKV_DIM], preferred_element_type=F32).reshape(
        n_tiles, rows, QKV_DIM)
    rest = jnp.dot(h, w_main[:, QKV_DIM:], preferred_element_type=F32)
    r = _Refs(cwq=cwq, gate=gate, gnorm=gnorm, cws=cws, qtail=qtail, stail=stail, s_out=s_out, qbuf=qbuf, sbuf=sbuf,
              bd=bd, kbd=kbd)
    masks = _packed_masks()
    tile = lambda a, ti: a[ti * rows:(ti + 1) * rows]
    mix = _run_tiles(_mixer_tile(ti, rows, CHUNK, True, masks, r, tile(ab, ti), tile(rest, ti))
                     for ti in range(n_tiles))
    xmid[...] = (x_all + _mm(jnp.concatenate(mix, axis=0), wout[...])).reshape(n_tiles, rows, D_MODEL)


def _mixer_step_kernel(rows, seg_len,
                       ab, proj, hq, hs, s_in, cwq, gate, gnorm, cws,
                       mix, qtail, stail, s_out):
    n_seq = rows // seg_len
    qkv = proj[0, :, 0:QKV_DIM]
    for j in range(GDN_CONV - 1):
        first = (seg_len - (GDN_CONV - 1) + j) * n_seq
        qtail[j] = qkv[first:first + n_seq]
    r = _Refs(qkv=qkv, hq=[hq[j] for j in range(GDN_CONV - 1)], hs=[hs[:, j, :] for j in range(SC_CONV - 1)],
              cwq=cwq, gate=gate, gnorm=gnorm, cws=cws, stail=stail, s_in=s_in, s_out=s_out)
    mix[0] = _run_tiles([_mixer_tile(0, rows, seg_len, False, None, r, ab[0], proj[0, :, QKV_DIM:])])[0].astype(BF16)


def _ffn_body(x, p, nmlp_ref, wup_ref, wdown_ref, nple_ref, wg_ref, wp_ref, nf_ref):
    hn = _rmsnorm(x, nmlp_ref[...]).astype(BF16)
    acc = x
    for j in range(D_FF // D_MODEL):
        u = jnp.maximum(_mm(hn, wup_ref[:, j * D_MODEL:(j + 1) * D_MODEL]), 0.0)
        acc = acc + _mm(u * u, wdown_ref[j * D_MODEL:(j + 1) * D_MODEL, :])
    gate = _sigmoid(_mm(_rmsnorm(acc, nple_ref[...]), wg_ref[...]))
    x3 = acc + gate * _mm(p, wp_ref[...])
    return _rmsnorm(x3, nf_ref[...])


def _ffn_kernel(n_long_steps, x_ref, p_ref, xs_ref, mix_ref, wout_ref, ps_ref, *rest):
    weights, (y_ref, ys_ref) = rest[:-2], rest[-2:]
    i = pl.program_id(0)

    @pl.when(i < n_long_steps)
    def _():
        y_ref[...] = _ffn_body(x_ref[...], p_ref[...], *weights)

    @pl.when(i == n_long_steps)
    def _():
        x = _load_by_position(xs_ref) + jnp.dot(mix_ref[...], wout_ref[...], preferred_element_type=F32)
        _store_by_position(ys_ref, _ffn_body(x, _load_by_position(ps_ref), *weights))


def _const_spec(shape):
    nd = len(shape)
    return pl.BlockSpec(shape, lambda *_: (0,) * nd, pipeline_mode=pl.Buffered(1))


def _state_spec(n_tiles, n_state):
    return pl.BlockSpec((n_tiles, n_state, GDN_HEADS, HEAD_DIM, HEAD_DIM), lambda b, i: (b, 0, 0, 0, 0))


def _mixer_seq_call(x3, hist_q, hist_s, s_in, weights, to_cast, *, n_tiles, rows):
    seqs, seq_rows, _ = x3.shape
    tiles = seq_rows // rows
    n_steps = (seqs // n_tiles) * tiles
    cast_specs = [pl.BlockSpec((w.shape[0] // n_steps, w.shape[1]), lambda b, i: (b * tiles + i, 0)) for w in to_cast]
    row_map = lambda b, i: (b, i, 0)
    seq_map = lambda b, i: (b, 0, 0)
    x_spec = pl.BlockSpec((n_tiles, rows, D_MODEL), row_map)
    hq_spec = pl.BlockSpec((n_tiles, SUBLANES, QKV_DIM), seq_map)
    hs_spec = pl.BlockSpec((n_tiles, SUBLANES, SC_WIDTH), seq_map)
    packed_w = GDN_HEADS * CHUNK
    return pl.pallas_call(
        functools.partial(_mixer_seq_kernel, n_tiles, rows, len(to_cast)),
        grid=(seqs // n_tiles, tiles),
        in_specs=[x_spec, hq_spec, hs_spec, _state_spec(n_tiles, 1)] + [_const_spec(w.shape) for w in weights]
        + cast_specs,
        out_specs=[x_spec, hq_spec, hs_spec, _state_spec(n_tiles, 1)] + cast_specs,
        out_shape=[jax.ShapeDtypeStruct(x3.shape, F32), jax.ShapeDtypeStruct(hist_q.shape, F32),
                   jax.ShapeDtypeStruct(hist_s.shape, F32), jax.ShapeDtypeStruct(s_in.shape, F32)]
        + [jax.ShapeDtypeStruct(w.shape, BF16) for w in to_cast],
        scratch_shapes=[pltpu.VMEM((n_tiles, rows + SUBLANES, QKV_DIM), F32),
                        pltpu.VMEM((n_tiles, rows + SUBLANES, SC_WIDTH), F32),
                        pltpu.VMEM((_log2(CHUNK), packed_w, packed_w), BF16),
                        pltpu.VMEM((packed_w, GDN_WIDTH), BF16)],
        compiler_params=pltpu.CompilerParams(
            dimension_semantics=("arbitrary", "arbitrary"), vmem_limit_bytes=VMEM_LIMIT_BYTES),
        name="mixer_seq",
    )(x3, hist_q, hist_s, s_in, *weights, *to_cast)


def _in_proj_kernel(n_qkv_steps, wt_ref, behind_ref, x_ref, nmix_ref, w_ab_ref, w_main_ref, ab_ref, proj_ref, h_ref):
    i = pl.program_id(0)

    @pl.when(i == 0)
    def _():
        h_ref[...] = _rmsnorm(_load_by_position(x_ref), nmix_ref[...]).astype(BF16)

    on_grid = wt_ref[...]
    behind = behind_ref[...]
    group = jnp.where(i < n_qkv_steps, on_grid, jnp.concatenate([on_grid[SUBLANES:], behind], axis=0))
    for c0 in range(0, IN_PROJ_COLS, SPLIT_COLS):
        w_main_ref[:, c0:c0 + SPLIT_COLS] = group[c0:c0 + SPLIT_COLS].T.astype(BF16)
    proj_ref[...] = jnp.dot(h_ref[...], w_main_ref[...], preferred_element_type=F32)

    @pl.when(i == n_qkv_steps - 1)
    def _():
        w_ab = jnp.concatenate([behind, jnp.zeros((LANES - SUBLANES, D_MODEL), F32)], axis=0).T.astype(BF16)
        w_ab_ref[...] = w_ab
        ab_ref[...] = jnp.dot(h_ref[...], w_ab, preferred_element_type=F32)


def _full_spec(a):
    nd = len(a.shape)
    return pl.BlockSpec(a.shape, lambda i: (0,) * nd)


def _in_proj_call(w_in, x3, nmix):
    rows, cols = w_in.shape
    n_seq, seg_len, _ = x3.shape
    n = n_seq * seg_len
    n_main = cols - 2 * GDN_HEADS
    assert 2 * GDN_HEADS == SUBLANES and QKV_DIM % IN_PROJ_COLS == 0 and n_main % IN_PROJ_COLS == 0
    w_t = w_in.T
    col_block = lambda height: pl.BlockSpec((height, IN_PROJ_COLS), lambda i: (0, i))
    outs = [jax.ShapeDtypeStruct((rows, LANES), BF16), jax.ShapeDtypeStruct((rows, n_main), BF16),
            jax.ShapeDtypeStruct((n, LANES), F32), jax.ShapeDtypeStruct((n, n_main), F32)]
    return pl.pallas_call(
        functools.partial(_in_proj_kernel, QKV_DIM // IN_PROJ_COLS),
        grid=(n_main // IN_PROJ_COLS,),
        in_specs=[pl.BlockSpec((IN_PROJ_COLS, rows), lambda i: (i, 0)),
                  pl.BlockSpec((SUBLANES, rows), lambda i: ((i + 1) * (IN_PROJ_COLS // SUBLANES), 0)),
                  _full_spec(x3), _full_spec(nmix)],
        out_specs=[_full_spec(outs[0]), col_block(rows), _full_spec(outs[2]), col_block(n)],
        out_shape=outs,
        scratch_shapes=[pltpu.VMEM((n, rows), BF16)],
        compiler_params=pltpu.CompilerParams(dimension_semantics=("arbitrary",), vmem_limit_bytes=VMEM_LIMIT_BYTES),
        name="in_proj",
    )(w_t, w_t, x3, nmix)


def _mixer_step_call(ab, proj, hist_q, hist_s, s_in, weights, *, seg_len):
    tiles, rows, _ = proj.shape
    tile_map = lambda b, i: (b, 0, 0)
    slab = lambda a: pl.BlockSpec((1, rows, a.shape[2]), tile_map)
    n_state = s_in.shape[1]
    hist_q_spec = pl.BlockSpec((GDN_CONV - 1, n_state, QKV_DIM), lambda b, i: (0, b, 0))
    hist_s_spec = pl.BlockSpec((n_state, SC_CONV - 1, SC_WIDTH), tile_map)
    mix = jax.ShapeDtypeStruct((tiles, rows, D_MODEL), BF16)
    like = lambda a: jax.ShapeDtypeStruct(a.shape, F32)
    return pl.pallas_call(
        functools.partial(_mixer_step_kernel, rows, seg_len),
        grid=(tiles, 1),
        in_specs=[slab(ab), slab(proj), hist_q_spec, hist_s_spec, _state_spec(1, n_state)]
        + [_const_spec(w.shape) for w in weights],
        out_specs=[slab(mix), hist_q_spec, hist_s_spec, _state_spec(1, n_state)],
        out_shape=[mix, like(hist_q), like(hist_s), like(s_in)],
        compiler_params=pltpu.CompilerParams(
            dimension_semantics=("arbitrary", "arbitrary"), vmem_limit_bytes=VMEM_LIMIT_BYTES),
        name="mixer_step",
    )(ab, proj, hist_q, hist_s, s_in, *weights)


def _ffn_call(x, p, xs, ps, mix, wout, weights, *, rows):
    steps = x.shape[0] // rows
    rows_spec = lambda width: pl.BlockSpec((rows, width), lambda i: (jnp.minimum(i, steps - 1), 0))
    short = [xs, mix, wout, ps]
    return pl.pallas_call(
        functools.partial(_ffn_kernel, steps),
        grid=(steps + 1,),
        in_specs=[rows_spec(D_MODEL), rows_spec(PLE_DIM)] + [_const_spec(a.shape) for a in short]
        + [_const_spec(w.shape) for w in weights],
        out_specs=[rows_spec(D_MODEL), _const_spec(xs.shape)],
        out_shape=[jax.ShapeDtypeStruct(x.shape, F32), jax.ShapeDtypeStruct(xs.shape, F32)],
        compiler_params=pltpu.CompilerParams(dimension_semantics=("arbitrary",), vmem_limit_bytes=VMEM_LIMIT_BYTES),
        name="ffn",
    )(x, p, *short, *weights)


def _layer(x_prompt, x_sample, conv_qkv, s_gdn, conv_sc, p_prompt, p_sample, norm_mix, w_in, w_conv_qkv, a_log,
           dt_bias, w_gdn_norm, w_conv_sc, w_out, norm_mlp, w_up, w_down, norm_ple, w_ple_gate, w_ple_proj, norm_f):
    bp, tp, _ = x_prompt.shape
    bs, ts, _ = x_sample.shape
    nmix = norm_mix.reshape(1, D_MODEL)
    w_ab, w_main, ab_s, proj_s = _in_proj_call(w_in, x_sample, nmix)
    wout = w_out.astype(BF16)
    gate = jnp.zeros((SUBLANES, LANES), F32)
    gate = gate.at[0, :GDN_HEADS].set(a_log.astype(F32)).at[1, :GDN_HEADS].set(dt_bias.astype(F32))
    core_w = (w_conv_qkv, gate, w_gdn_norm.reshape(1, HEAD_DIM), w_conv_sc)

    zq = jnp.zeros((bp, SUBLANES, QKV_DIM), F32)
    zs = jnp.zeros((bp, SUBLANES, SC_WIDTH), F32)
    s0 = jnp.zeros((bp, 1, GDN_HEADS, HEAD_DIM, HEAD_DIM), F32)
    xm_p, qt_p, st_p, s_p, wup, wdown, wgate = _mixer_seq_call(
        x_prompt, zq, zs, s0, (nmix, w_ab, w_main) + core_w + (wout,), (w_up, w_down, w_ple_gate),
        n_tiles=PROMPT_SEQS_PER_STEP, rows=PROMPT_ROWS)

    seq_per_tile = CHUNK // ts
    tiles = bs // seq_per_tile
    tiled = lambda a: a.reshape(tiles, CHUNK, a.shape[-1])
    mix_s, qt_s, new_sc_s, s_s = _mixer_step_call(
        tiled(ab_s), tiled(proj_s), jnp.transpose(conv_qkv, (1, 0, 2)), conv_sc,
        s_gdn.reshape(tiles, seq_per_tile, GDN_HEADS, HEAD_DIM, HEAD_DIM), core_w, seg_len=ts)

    ffn_w = (norm_mlp.reshape(1, D_MODEL), wup, wdown, norm_ple.reshape(1, D_MODEL), wgate,
             w_ple_proj.astype(BF16), norm_f.reshape(1, D_MODEL))
    y_p, y_s = _ffn_call(xm_p.reshape(bp * tp, D_MODEL), p_prompt.reshape(bp * tp, PLE_DIM), x_sample, p_sample,
                         mix_s.reshape(bs * ts, D_MODEL), wout, ffn_w, rows=FFN_ROWS)

    new_conv_p = qt_p[:, SUBLANES - (GDN_CONV - 1):]
    new_sc_p = st_p[:, SUBLANES - (SC_CONV - 1):]
    new_conv_s = jnp.transpose(qt_s, (1, 0, 2))
    return (y_p.reshape(bp, tp, D_MODEL), y_s, new_conv_p,
            s_p.reshape(bp, GDN_HEADS, HEAD_DIM, HEAD_DIM), new_sc_p, new_conv_s,
            s_s.reshape(bs, GDN_HEADS, HEAD_DIM, HEAD_DIM), new_sc_s)


def kernel(x_prompt, x_sample, state_gdn_conv, state_gdn, state_sc_conv, p_prompt, p_sample, norm_mix, w_in, w_conv_qkv, a_log, dt_bias, w_gdn_norm, w_conv_sc, w_out, norm_mlp, w_up, w_down, norm_ple, w_ple_gate, w_ple_proj, norm_f):
    depth = w_in.shape[0]
    assert depth == 1, "one layer per call"
    assert x_sample.shape[1] >= GDN_CONV - 1 and CHUNK % x_sample.shape[1] == 0
    assert x_prompt.shape[1] % PROMPT_ROWS == 0 and x_prompt.shape[0] % PROMPT_SEQS_PER_STEP == 0
    outs = _layer(x_prompt, x_sample, state_gdn_conv[0], state_gdn[0], state_sc_conv[0], p_prompt[0], p_sample[0],
                  norm_mix[0], w_in[0], w_conv_qkv[0], a_log[0], dt_bias[0], w_gdn_norm[0], w_conv_sc[0], w_out[0],
                  norm_mlp[0], w_up[0], w_down[0], norm_ple[0], w_ple_gate[0], w_ple_proj[0], norm_f)
    y_p, y_s, c_p, s_p, sc_p, c_s, s_s, sc_s = outs
    return (y_p, y_s, c_p[None], s_p[None], sc_p[None], c_s[None], s_s[None], sc_s[None])
```

```python
import functools

import jax
import jax.numpy as jnp
from jax import lax
from jax.experimental import pallas as pl
from jax.experimental.pallas import tpu as pltpu

F32 = jnp.float32
BF16 = jnp.bfloat16

D_MODEL = 1024
PLE_DIM = 256
GDN_HEADS = 4
HEAD_DIM = 128
GDN_WIDTH = GDN_HEADS * HEAD_DIM
QKV_DIM = 3 * GDN_WIDTH
GDN_CONV = 4
SC_WIDTH = D_MODEL - GDN_WIDTH
SC_CONV = 3
D_FF = 4 * D_MODEL
EPS = 1e-6
NEG_LOG2_E = -1.4426950408889634
CHUNK = 64
LANES = 128
SUBLANES = 8
VMEM_LIMIT_BYTES = 56 * 1024 * 1024
PROMPT_ROWS = 128
PROMPT_SEQS_PER_STEP = 4
FFN_ROWS = 1024
IN_PROJ_COLS = 512
SPLIT_COLS = 256

R_Z = 0
R_SCB = R_Z + GDN_WIDTH
R_SCC = R_SCB + SC_WIDTH
R_SCH = R_SCC + SC_WIDTH


def _mm(a, b):
    return jnp.dot(a.astype(BF16), b.astype(BF16), preferred_element_type=F32)


def _mm_nt(a, b):
    return lax.dot_general(a.astype(BF16), b.astype(BF16), (((1,), (1,)), ((), ())), preferred_element_type=F32)


def _mm_tn(a, b):
    return lax.dot_general(a.astype(BF16), b.astype(BF16), (((0,), (0,)), ((), ())), preferred_element_type=F32)


def _rmsnorm(x, w_row):
    return x * lax.rsqrt(jnp.mean(x * x, axis=-1, keepdims=True) + EPS) * w_row


def _sigmoid(x):
    return 1.0 / (1.0 + jnp.exp2(x * NEG_LOG2_E))


def _silu(x):
    return x * _sigmoid(x)


def _softplus(x):
    return jnp.maximum(x, 0.0) + jnp.log1p(jnp.exp(-jnp.abs(x)))


def _log2(n):
    k = n.bit_length() - 1
    assert (1 << k) == n, n
    return k


def _wy_block(q, k, v, g_col, g_row, beta_col, seg_len, load_state, store_state):
    c = CHUNK
    n_seg = c // seg_len
    lg = _log2(seg_len)
    ls = _log2(n_seg)
    ri = lax.broadcasted_iota(jnp.int32, (c, c), 0)
    ci = lax.broadcasted_iota(jnp.int32, (c, c), 1)
    same = jnp.bitwise_and(ri, n_seg - 1) == jnp.bitwise_and(ci, n_seg - 1)
    lower = same & (ri >= ci)
    strict = same & (ri > ci)
    diff = g_col - g_row
    decay = jnp.where(lower, jnp.exp(jnp.where(lower, diff, 0.0)), 0.0)
    a_off = jnp.where(strict, beta_col * _mm_nt(k, k) * decay, 0.0)
    qk = _mm_nt(q, k) * decay
    yield
    eye = jnp.where(ri == ci, 1.0, 0.0).astype(F32)
    x_inv = eye
    for lb in range(lg):
        rb = jnp.right_shift(ri, ls + lb)
        cb = jnp.right_shift(ci, ls + lb)
        join = (jnp.bitwise_and(rb, 1) == 1) & (cb == rb - 1)
        b_lvl = jnp.where(join, a_off, 0.0)
        if lb == 0:
            x_inv = eye - b_lvl
        else:
            xb = _mm(x_inv, b_lvl)
            yield
            x_inv = x_inv - _mm(xb, x_inv)
            yield
    rhs = jnp.concatenate([v * beta_col, k * (beta_col * jnp.exp(g_col))], axis=1)
    sol = _mm(x_inv, rhs)
    yield
    u = sol[:, :HEAD_DIM]
    w = sol[:, HEAD_DIM:]
    stacked = jnp.concatenate([w, q * jnp.exp(g_col)], axis=0)
    row = lax.broadcasted_iota(jnp.int32, (c, 1), 0)
    states = [load_state(s) for s in range(n_seg)]
    w_s = None
    q_s = None
    for s in range(n_seg):
        r = _mm(stacked, states[s])
        if n_seg == 1:
            w_s, q_s = r[:c], r[c:]
        else:
            in_seg = jnp.bitwise_and(row, n_seg - 1) == s
            w_s = jnp.where(in_seg, r[:c], 0.0 if w_s is None else w_s)
            q_s = jnp.where(in_seg, r[c:], 0.0 if q_s is None else q_s)
    v_new = u - w_s
    yield
    o = q_s + _mm(qk, v_new)
    for s in range(n_seg):
        last = (seg_len - 1) * n_seg + s
        g_last = g_col[last:last + 1, :]
        if n_seg == 1:
            k_dec = k * jnp.exp(g_last - g_col)
        else:
            in_seg = jnp.bitwise_and(row, n_seg - 1) == s
            k_dec = jnp.where(in_seg, k * jnp.exp(jnp.where(in_seg, g_last - g_col, 0.0)), 0.0)
        store_state(s, states[s] * jnp.exp(g_last) + _mm_tn(k_dec, v_new))
    return o


def _expand_heads(slab, first_lane, width):
    rows = slab.shape[0]
    cols = [slab[:, first_lane + hd:first_lane + hd + 1] for hd in range(GDN_HEADS)]
    if width % LANES == 0:
        return jnp.concatenate([jnp.broadcast_to(c, (rows, width)) for c in cols], axis=1)
    total = GDN_HEADS * width
    lane_head = jnp.right_shift(lax.broadcasted_iota(jnp.int32, (rows, total), 1), _log2(width))
    out = jnp.broadcast_to(cols[-1], (rows, total))
    for hd in range(GDN_HEADS - 2, -1, -1):
        out = jnp.where(lane_head == hd, jnp.broadcast_to(cols[hd], (rows, total)), out)
    return out


def _packed_masks():
    c, nh = CHUNK, GDN_HEADS
    ri = lax.broadcasted_iota(jnp.int32, (c, nh * c), 0)
    cj = jnp.bitwise_and(lax.broadcasted_iota(jnp.int32, (c, nh * c), 1), c - 1)
    return dict(lower=ri >= cj, strict=ri > cj, eye=ri == cj, join0=(jnp.bitwise_and(ri, 1) == 1) & (cj == ri - 1))


def _fill_block_factors(bd_ref, kbd_ref):
    c, nh = CHUNK, GDN_HEADS
    w = nh * c
    lg = _log2(c)
    rw = lax.broadcasted_iota(jnp.int32, (w, w), 0)
    cw = lax.broadcasted_iota(jnp.int32, (w, w), 1)
    same_head = jnp.right_shift(rw, lg) == jnp.right_shift(cw, lg)
    rw, cw = jnp.bitwise_and(rw, c - 1), jnp.bitwise_and(cw, c - 1)
    one_zero = lambda m: jnp.where(m, 1.0, 0.0).astype(BF16)
    bd_ref[0] = one_zero(same_head)
    for lb in range(1, lg):
        bd_ref[lb] = one_zero(same_head & (jnp.bitwise_and(jnp.right_shift(rw, lb), 1) == 1)
                              & (jnp.right_shift(cw, lb) == jnp.right_shift(rw, lb) - 1))
    kbd_ref[...] = one_zero(jnp.right_shift(lax.broadcasted_iota(jnp.int32, (w, nh * HEAD_DIM), 0), lg)
                            == jnp.right_shift(lax.broadcasted_iota(jnp.int32, (w, nh * HEAD_DIM), 1),
                                               _log2(HEAD_DIM)))


def _wy_tile_packed(q_all, k_all, v_all, g, beta, state_ref, rows, masks, bd_ref, kbd_ref):
    c, nh = CHUNK, GDN_HEADS
    lg = _log2(c)
    n_chunks = rows // c
    lower, strict, eye = masks["lower"], masks["strict"], masks["eye"]

    def block_diag(x, factor):
        return jnp.concatenate([x.astype(BF16)] * nh, axis=0) * factor

    def head(x, hd, width):
        return x[:, hd * width:(hd + 1) * width]

    g_c = _expand_heads(g, 0, c)
    b_c = _expand_heads(beta, nh, c)
    g_d = _expand_heads(g, 0, HEAD_DIM)
    b_d = _expand_heads(beta, nh, HEAD_DIM)
    eg_d = jnp.exp(g_d)
    v_rhs = v_all * b_d
    k_rhs = k_all * (b_d * eg_d)
    q_g = q_all * eg_d

    a_off, qk_dec = [], []
    for cb in range(n_chunks):
        rs = slice(cb * c, (cb + 1) * c)
        g_row = jnp.sum(jnp.where(eye, g_c[rs], 0.0), axis=0, keepdims=True)
        decay = jnp.where(lower, jnp.exp(jnp.where(lower, g_c[rs] - g_row, 0.0)), 0.0)
        kq = _mm_nt(jnp.concatenate([k_all[rs], q_all[rs]], axis=0), block_diag(k_all[rs], kbd_ref[...]))
        a_off.append(jnp.where(strict, b_c[rs] * kq[:c] * decay, 0.0))
        qk_dec.append(kq[c:] * decay)
    yield

    eye_f = jnp.where(eye, 1.0, 0.0).astype(F32)
    x_inv = [eye_f - jnp.where(masks["join0"], a, 0.0) for a in a_off]
    for lb in range(1, lg):
        xb = [jnp.dot(x.astype(BF16), block_diag(a, bd_ref[lb]), preferred_element_type=F32)
              for x, a in zip(x_inv, a_off)]
        yield
        xbx = [jnp.dot(t.astype(BF16), block_diag(x, bd_ref[0]), preferred_element_type=F32)
               for t, x in zip(xb, x_inv)]
        x_inv = [x - t for x, t in zip(x_inv, xbx)]
        yield

    u, wk = [], []
    for cb in range(n_chunks):
        rs = slice(cb * c, (cb + 1) * c)
        sol = [_mm(head(x_inv[cb], hd, c),
                   jnp.concatenate([head(v_rhs[rs], hd, HEAD_DIM), head(k_rhs[rs], hd, HEAD_DIM)], axis=1))
               for hd in range(nh)]
        u.append([s[:, :HEAD_DIM] for s in sol])
        wk.append([s[:, HEAD_DIM:] for s in sol])
    yield

    state = [state_ref[0, hd] for hd in range(nh)]
    o_blocks = []
    for cb in range(n_chunks):
        rs = slice(cb * c, (cb + 1) * c)
        g_last = g_d[(cb + 1) * c - 1:(cb + 1) * c, :]
        k_dec = k_all[rs] * jnp.exp(g_last - g_d[rs])
        eg_last = jnp.exp(g_last)
        r = [_mm(jnp.concatenate([wk[cb][hd], head(q_g[rs], hd, HEAD_DIM)], axis=0), state[hd]) for hd in range(nh)]
        v_new = [u[cb][hd] - r[hd][:c] for hd in range(nh)]
        yield
        o_blocks.append(jnp.concatenate(
            [r[hd][c:] + _mm(head(qk_dec[cb], hd, c), v_new[hd]) for hd in range(nh)], axis=1))
        state = [state[hd] * head(eg_last, hd, HEAD_DIM) + _mm_tn(head(k_dec, hd, HEAD_DIM), v_new[hd])
                 for hd in range(nh)]
        yield
    for hd in range(nh):
        state_ref[0, hd] = state[hd]
    return jnp.concatenate(o_blocks, axis=0)


def _run_tiles(programs):
    programs = list(programs)
    results = [None] * len(programs)
    live = list(range(len(programs)))
    while live:
        for idx in list(live):
            try:
                next(programs[idx])
            except StopIteration as done:
                results[idx] = done.value
                live.remove(idx)
    return results


def _shifted_history(buf, rows, n_taps):
    with_carry = buf[0:SUBLANES + rows, :]
    return [pltpu.roll(with_carry, s, 0)[SUBLANES:, :] for s in range(1, n_taps)]


def _shifted_positions(x_rows, taps):
    n_seq = taps[0].shape[0]
    rows = x_rows.shape[0]
    ext = jnp.concatenate(list(taps) + [x_rows], axis=0)
    return [ext[(len(taps) - s) * n_seq:(len(taps) - s) * n_seq + rows] for s in range(1, len(taps) + 1)]


def _mixer_tile(ti, rows, seg_len, long_seq, masks, r, ab, rest):
    n_seq = rows // seg_len
    new_rows = slice(SUBLANES, SUBLANES + rows)
    row = lax.broadcasted_iota(jnp.int32, (rows, 1), 0)
    if long_seq:
        qb, sb = r.qbuf.at[ti], r.sbuf.at[ti]
        pos, row_step = jnp.bitwise_and(row, seg_len - 1), 1
    else:
        assert rows == CHUNK
        pos, row_step = jnp.right_shift(row, _log2(n_seq)), n_seq

    gate = r.gate[...]
    log_a = -jnp.exp(gate[0:1, :]) * _softplus(ab + gate[1:2, :])
    beta = _sigmoid(ab)
    g = log_a
    shift = 1
    while shift < seg_len:
        g = g + jnp.where(pos >= shift, pltpu.roll(g, shift * row_step, 0), 0.0)
        shift *= 2

    sc_pre = rest[:, R_SCC:R_SCC + SC_WIDTH] * rest[:, R_SCH:R_SCH + SC_WIDTH]
    if long_seq:
        sb[new_rows, :] = sc_pre
        qkv_pre = qb[new_rows, :]
        cwq = r.cwq[...]
        with_carry = qb[0:SUBLANES + rows, :]
        back1 = pltpu.roll(with_carry, 1, 0)
        pair = with_carry * cwq[1:2, :] + back1 * cwq[0:1, :]
        qkv = _silu(pltpu.roll(pair, 2, 0)[SUBLANES:, :] + back1[SUBLANES:, :] * cwq[2:3, :]
                    + qkv_pre * cwq[3:4, :])
        r.qtail[ti] = qb[rows:rows + SUBLANES, :]
        r.stail[ti] = sb[rows:rows + SUBLANES, :]
    else:
        qkv_pre = r.qkv
        cwq = r.cwq[...]
        q1, q2, q3 = _shifted_positions(qkv_pre, r.hq)
        qkv = _silu(q3 * cwq[0:1, :] + q2 * cwq[1:2, :] + q1 * cwq[2:3, :] + qkv_pre * cwq[3:4, :])
        for j in range(SC_CONV - 1):
            first = (seg_len - (SC_CONV - 1) + j) * n_seq
            r.stail[:, j, :] = sc_pre[first:first + n_seq]
    q_n, k_n, v_n = [], [], []
    for hd in range(GDN_HEADS):
        lo = hd * HEAD_DIM
        q_h = qkv[:, lo:lo + HEAD_DIM]
        k_h = qkv[:, GDN_WIDTH + lo:GDN_WIDTH + lo + HEAD_DIM]
        q_n.append(q_h * (lax.rsqrt(jnp.sum(q_h * q_h, axis=-1, keepdims=True) + EPS) * (HEAD_DIM ** -0.5)))
        k_n.append(k_h * lax.rsqrt(jnp.sum(k_h * k_h, axis=-1, keepdims=True) + EPS))
        v_n.append(qkv[:, 2 * GDN_WIDTH + lo:2 * GDN_WIDTH + lo + HEAD_DIM])
    yield

    state_ref = r.s_out.at[ti]
    if long_seq:
        o_all = yield from _wy_tile_packed(jnp.concatenate(q_n, axis=1), jnp.concatenate(k_n, axis=1),
                                           jnp.concatenate(v_n, axis=1), g, beta, state_ref, rows, masks,
                                           r.bd, r.kbd)
        o_h = [o_all[:, hd * HEAD_DIM:(hd + 1) * HEAD_DIM] for hd in range(GDN_HEADS)]
    else:
        g_t = g.T
        blocks = []
        for hd in range(GDN_HEADS):

            def load_state(s, hd=hd):
                return r.s_in[ti, s, hd]

            def store_state(s, val, hd=hd):
                state_ref[s, hd] = val

            blocks.append(_wy_block(q_n[hd], k_n[hd], v_n[hd], g[:, hd:hd + 1], g_t[hd:hd + 1, :],
                                    beta[:, GDN_HEADS + hd:GDN_HEADS + hd + 1], seg_len, load_state, store_state))
        o_h = _run_tiles(blocks)
    yield

    s1, s2 = _shifted_history(sb, rows, SC_CONV) if long_seq else _shifted_positions(sc_pre, r.hs)
    cws = r.cws[...]
    y_sc = rest[:, R_SCB:R_SCB + SC_WIDTH] * (s2 * cws[0:1, :] + s1 * cws[1:2, :] + sc_pre * cws[2:3, :])
    if long_seq:
        qb[0:SUBLANES, :] = qb[rows:rows + SUBLANES, :]
        sb[0:SUBLANES, :] = sb[rows:rows + SUBLANES, :]
    gnorm = r.gnorm[...]
    o_heads = []
    for hd in range(GDN_HEADS):
        z_h = rest[:, R_Z + hd * HEAD_DIM:R_Z + (hd + 1) * HEAD_DIM]
        o_heads.append(_rmsnorm(o_h[hd], gnorm) * _silu(z_h))
    return jnp.concatenate(o_heads + [y_sc], axis=1)


def _load_by_position(ref):
    n_seq, seg_len, _ = ref.shape
    per_tile = CHUNK // seg_len
    return jnp.concatenate([ref[b:b + per_tile, t, :] for b in range(0, n_seq, per_tile) for t in range(seg_len)],
                           axis=0)


def _store_by_position(ref, rows):
    n_seq, seg_len, _ = ref.shape
    per_tile = CHUNK // seg_len
    for b in range(0, n_seq, per_tile):
        for t in range(seg_len):
            first = b * seg_len + t * per_tile
            ref[b:b + per_tile, t, :] = rows[first:first + per_tile]


class _Refs:
    def __init__(self, **refs):
        self.__dict__.update(refs)


def _mixer_seq_kernel(n_tiles, rows, n_cast, *refs):
    n_in = 9
    x, nmix, w_ab, w_main, cwq, gate, gnorm, cws, wout = refs[:n_in]
    cast_in = refs[n_in:n_in + n_cast]
    xmid, qtail, stail, s_out = refs[n_in + n_cast:n_in + n_cast + 4]
    cast_out = refs[n_in + n_cast + 4:n_in + 2 * n_cast + 4]
    qbuf, sbuf, bd, kbd = refs[n_in + 2 * n_cast + 4:]
    for src, dst in zip(cast_in, cast_out):
        dst[...] = src[...].astype(BF16)

    @pl.when(pl.program_id(1) == 0)
    def _():
        s_out[...] = jnp.zeros(s_out.shape, F32)
        qbuf[:, 0:SUBLANES, :] = jnp.zeros((n_tiles, SUBLANES, QKV_DIM), F32)
        sbuf[:, 0:SUBLANES, :] = jnp.zeros((n_tiles, SUBLANES, SC_WIDTH), F32)
        _fill_block_factors(bd, kbd)

    x_all = x[...].reshape(n_tiles * rows, D_MODEL)
    h = _rmsnorm(x_all, nmix[...]).astype(BF16)
    ab = jnp.dot(h, w_ab[...], preferred_element_type=F32)
    qbuf[:, SUBLANES:SUBLANES + rows, :] = jnp.dot(h, w_main[:, 0:QKV_DIM], preferred_element_type=F32).reshape(
        n_tiles, rows, QKV_DIM)
    rest = jnp.dot(h, w_main[:, QKV_DIM:], preferred_element_type=F32)
    r = _Refs(cwq=cwq, gate=gate, gnorm=gnorm, cws=cws, qtail=qtail, stail=stail, s_out=s_out, qbuf=qbuf, sbuf=sbuf,
              bd=bd, kbd=kbd)
    masks = _packed_masks()
    tile = lambda a, ti: a[ti * rows:(ti + 1) * rows]
    mix = _run_tiles(_mixer_tile(ti, rows, CHUNK, True, masks, r, tile(ab, ti), tile(rest, ti))
                     for ti in range(n_tiles))
    xmid[...] = (x_all + _mm(jnp.concatenate(mix, axis=0), wout[...])).reshape(n_tiles, rows, D_MODEL)


def _mixer_step_kernel(rows, seg_len,
                       ab, proj, hq, hs, s_in, cwq, gate, gnorm, cws,
                       mix, qtail, stail, s_out):
    n_seq = rows // seg_len
    qkv = proj[0, :, 0:QKV_DIM]
    for j in range(GDN_CONV - 1):
        first = (seg_len - (GDN_CONV - 1) + j) * n_seq
        qtail[j] = qkv[first:first + n_seq]
    r = _Refs(qkv=qkv, hq=[hq[j] for j in range(GDN_CONV - 1)], hs=[hs[:, j, :] for j in range(SC_CONV - 1)],
              cwq=cwq, gate=gate, gnorm=gnorm, cws=cws, stail=stail, s_in=s_in, s_out=s_out)
    mix[0] = _run_tiles([_mixer_tile(0, rows, seg_len, False, None, r, ab[0], proj[0, :, QKV_DIM:])])[0].astype(BF16)


def _ffn_body(x, p, nmlp_ref, wup_ref, wdown_ref, nple_ref, wg_ref, wp_ref, nf_ref):
    hn = _rmsnorm(x, nmlp_ref[...]).astype(BF16)
    acc = x
    for j in range(D_FF // D_MODEL):
        u = jnp.maximum(_mm(hn, wup_ref[:, j * D_MODEL:(j + 1) * D_MODEL]), 0.0)
        acc = acc + _mm(u * u, wdown_ref[j * D_MODEL:(j + 1) * D_MODEL, :])
    gate = _sigmoid(_mm(_rmsnorm(acc, nple_ref[...]), wg_ref[...]))
    x3 = acc + gate * _mm(p, wp_ref[...])
    return _rmsnorm(x3, nf_ref[...])


def _ffn_kernel(x_ref, p_ref, *rest):
    rest[-1][...] = _ffn_body(x_ref[...], p_ref[...], *rest[:-1])


def _out_ffn_kernel(x_ref, mix_ref, wout_ref, p_ref, *rest):
    x = _load_by_position(x_ref) + jnp.dot(mix_ref[...], wout_ref[...], preferred_element_type=F32)
    _store_by_position(rest[-1], _ffn_body(x, _load_by_position(p_ref), *rest[:-1]))


def _const_spec(shape):
    nd = len(shape)
    return pl.BlockSpec(shape, lambda *_: (0,) * nd, pipeline_mode=pl.Buffered(1))


def _state_spec(n_tiles, n_state):
    return pl.BlockSpec((n_tiles, n_state, GDN_HEADS, HEAD_DIM, HEAD_DIM), lambda b, i: (b, 0, 0, 0, 0))


def _mixer_seq_call(x3, weights, to_cast, *, n_tiles, rows):
    seqs, seq_rows, _ = x3.shape
    tiles = seq_rows // rows
    n_steps = (seqs // n_tiles) * tiles
    cast_specs = [pl.BlockSpec((w.shape[0] // n_steps, w.shape[1]), lambda b, i: (b * tiles + i, 0)) for w in to_cast]
    row_map = lambda b, i: (b, i, 0)
    seq_map = lambda b, i: (b, 0, 0)
    x_spec = pl.BlockSpec((n_tiles, rows, D_MODEL), row_map)
    hq_spec = pl.BlockSpec((n_tiles, SUBLANES, QKV_DIM), seq_map)
    hs_spec = pl.BlockSpec((n_tiles, SUBLANES, SC_WIDTH), seq_map)
    packed_w = GDN_HEADS * CHUNK
    return pl.pallas_call(
        functools.partial(_mixer_seq_kernel, n_tiles, rows, len(to_cast)),
        grid=(seqs // n_tiles, tiles),
        in_specs=[x_spec] + [_const_spec(w.shape) for w in weights] + cast_specs,
        out_specs=[x_spec, hq_spec, hs_spec, _state_spec(n_tiles, 1)] + cast_specs,
        out_shape=[jax.ShapeDtypeStruct(x3.shape, F32), jax.ShapeDtypeStruct((seqs, SUBLANES, QKV_DIM), F32),
                   jax.ShapeDtypeStruct((seqs, SUBLANES, SC_WIDTH), F32),
                   jax.ShapeDtypeStruct((seqs, 1, GDN_HEADS, HEAD_DIM, HEAD_DIM), F32)]
        + [jax.ShapeDtypeStruct(w.shape, BF16) for w in to_cast],
        scratch_shapes=[pltpu.VMEM((n_tiles, rows + SUBLANES, QKV_DIM), F32),
                        pltpu.VMEM((n_tiles, rows + SUBLANES, SC_WIDTH), F32),
                        pltpu.VMEM((_log2(CHUNK), packed_w, packed_w), BF16),
                        pltpu.VMEM((packed_w, GDN_WIDTH), BF16)],
        compiler_params=pltpu.CompilerParams(
            dimension_semantics=("arbitrary", "arbitrary"), vmem_limit_bytes=VMEM_LIMIT_BYTES),
        name="mixer_seq",
    )(x3, *weights, *to_cast)


def _in_proj_kernel(n_qkv_steps, wt_ref, behind_ref, x_ref, nmix_ref, w_ab_ref, w_main_ref, ab_ref, proj_ref, h_ref):
    i = pl.program_id(0)

    @pl.when(i == 0)
    def _():
        h_ref[...] = _rmsnorm(_load_by_position(x_ref), nmix_ref[...]).astype(BF16)

    on_grid = wt_ref[...]
    behind = behind_ref[...]
    group = jnp.where(i < n_qkv_steps, on_grid, jnp.concatenate([on_grid[SUBLANES:], behind], axis=0))
    for c0 in range(0, IN_PROJ_COLS, SPLIT_COLS):
        w_main_ref[:, c0:c0 + SPLIT_COLS] = group[c0:c0 + SPLIT_COLS].T.astype(BF16)
    proj_ref[...] = jnp.dot(h_ref[...], w_main_ref[...], preferred_element_type=F32)

    @pl.when(i == n_qkv_steps - 1)
    def _():
        w_ab = jnp.concatenate([behind, jnp.zeros((LANES - SUBLANES, D_MODEL), F32)], axis=0).T.astype(BF16)
        w_ab_ref[...] = w_ab
        ab_ref[...] = jnp.dot(h_ref[...], w_ab, preferred_element_type=F32)


def _full_spec(a):
    nd = len(a.shape)
    return pl.BlockSpec(a.shape, lambda i: (0,) * nd)


def _in_proj_call(w_in, x3, nmix):
    rows, cols = w_in.shape
    n_seq, seg_len, _ = x3.shape
    n = n_seq * seg_len
    n_main = cols - 2 * GDN_HEADS
    assert 2 * GDN_HEADS == SUBLANES and QKV_DIM % IN_PROJ_COLS == 0 and n_main % IN_PROJ_COLS == 0
    w_t = w_in.T
    col_block = lambda height: pl.BlockSpec((height, IN_PROJ_COLS), lambda i: (0, i))
    outs = [jax.ShapeDtypeStruct((rows, LANES), BF16), jax.ShapeDtypeStruct((rows, n_main), BF16),
            jax.ShapeDtypeStruct((n, LANES), F32), jax.ShapeDtypeStruct((n, n_main), F32)]
    return pl.pallas_call(
        functools.partial(_in_proj_kernel, QKV_DIM // IN_PROJ_COLS),
        grid=(n_main // IN_PROJ_COLS,),
        in_specs=[pl.BlockSpec((IN_PROJ_COLS, rows), lambda i: (i, 0)),
                  pl.BlockSpec((SUBLANES, rows), lambda i: ((i + 1) * (IN_PROJ_COLS // SUBLANES), 0)),
                  _full_spec(x3), _full_spec(nmix)],
        out_specs=[_full_spec(outs[0]), col_block(rows), _full_spec(outs[2]), col_block(n)],
        out_shape=outs,
        scratch_shapes=[pltpu.VMEM((n, rows), BF16)],
        compiler_params=pltpu.CompilerParams(dimension_semantics=("arbitrary",), vmem_limit_bytes=VMEM_LIMIT_BYTES),
        name="in_proj",
    )(w_t, w_t, x3, nmix)


def _mixer_step_call(ab, proj, hist_q, hist_s, s_in, weights, *, seg_len):
    tiles, rows, _ = proj.shape
    tile_map = lambda b, i: (b, 0, 0)
    slab = lambda a: pl.BlockSpec((1, rows, a.shape[2]), tile_map)
    n_state = s_in.shape[1]
    hist_q_spec = pl.BlockSpec((GDN_CONV - 1, n_state, QKV_DIM), lambda b, i: (0, b, 0))
    hist_s_spec = pl.BlockSpec((n_state, SC_CONV - 1, SC_WIDTH), tile_map)
    mix = jax.ShapeDtypeStruct((tiles, rows, D_MODEL), BF16)
    like = lambda a: jax.ShapeDtypeStruct(a.shape, F32)
    return pl.pallas_call(
        functools.partial(_mixer_step_kernel, rows, seg_len),
        grid=(tiles, 1),
        in_specs=[slab(ab), slab(proj), hist_q_spec, hist_s_spec, _state_spec(1, n_state)]
        + [_const_spec(w.shape) for w in weights],
        out_specs=[slab(mix), hist_q_spec, hist_s_spec, _state_spec(1, n_state)],
        out_shape=[mix, like(hist_q), like(hist_s), like(s_in)],
        compiler_params=pltpu.CompilerParams(
            dimension_semantics=("arbitrary", "arbitrary"), vmem_limit_bytes=VMEM_LIMIT_BYTES),
        name="mixer_step",
    )(ab, proj, hist_q, hist_s, s_in, *weights)


def _ffn_call(x, p, weights, *, name, rows=None, mix=None, wout=None):
    if mix is None:
        steps = x.shape[0] // rows
        rows_spec = lambda width: pl.BlockSpec((rows, width), lambda i: (i, 0))
        x_spec, p_spec, pre_args, pre_specs, body = rows_spec(D_MODEL), rows_spec(PLE_DIM), [], [], _ffn_kernel
    else:
        steps = 1
        x_spec, p_spec, body = _full_spec(x), _full_spec(p), _out_ffn_kernel
        pre_args, pre_specs = [mix, wout], [_full_spec(mix), _const_spec(wout.shape)]
    return pl.pallas_call(
        body,
        grid=(steps,),
        in_specs=[x_spec] + pre_specs + [p_spec] + [_const_spec(w.shape) for w in weights],
        out_specs=x_spec,
        out_shape=jax.ShapeDtypeStruct(x.shape, F32),
        compiler_params=pltpu.CompilerParams(dimension_semantics=("arbitrary",), vmem_limit_bytes=VMEM_LIMIT_BYTES),
        name=name,
    )(x, *pre_args, p, *weights)


def _layer(x_prompt, x_sample, conv_qkv, s_gdn, conv_sc, p_prompt, p_sample, norm_mix, w_in, w_conv_qkv, a_log,
           dt_bias, w_gdn_norm, w_conv_sc, w_out, norm_mlp, w_up, w_down, norm_ple, w_ple_gate, w_ple_proj, norm_f):
    bp, tp, _ = x_prompt.shape
    bs, ts, _ = x_sample.shape
    nmix = norm_mix.reshape(1, D_MODEL)
    w_ab, w_main, ab_s, proj_s = _in_proj_call(w_in, x_sample, nmix)
    wout = w_out.astype(BF16)
    gate = jnp.zeros((SUBLANES, LANES), F32)
    gate = gate.at[0, :GDN_HEADS].set(a_log.astype(F32)).at[1, :GDN_HEADS].set(dt_bias.astype(F32))
    core_w = (w_conv_qkv, gate, w_gdn_norm.reshape(1, HEAD_DIM), w_conv_sc)

    xm_p, qt_p, st_p, s_p, wup, wdown, wgate = _mixer_seq_call(
        x_prompt, (nmix, w_ab, w_main) + core_w + (wout,), (w_up, w_down, w_ple_gate),
        n_tiles=PROMPT_SEQS_PER_STEP, rows=PROMPT_ROWS)
    ffn_w = (norm_mlp.reshape(1, D_MODEL), wup, wdown, norm_ple.reshape(1, D_MODEL), wgate,
             w_ple_proj.astype(BF16), norm_f.reshape(1, D_MODEL))
    y_p = _ffn_call(xm_p.reshape(bp * tp, D_MODEL), p_prompt.reshape(bp * tp, PLE_DIM), ffn_w, rows=FFN_ROWS,
                    name="ffn_prompt")

    seq_per_tile = CHUNK // ts
    tiles = bs // seq_per_tile
    tiled = lambda a: a.reshape(tiles, CHUNK, a.shape[-1])
    mix_s, qt_s, new_sc_s, s_s = _mixer_step_call(
        tiled(ab_s), tiled(proj_s), jnp.transpose(conv_qkv, (1, 0, 2)), conv_sc,
        s_gdn.reshape(tiles, seq_per_tile, GDN_HEADS, HEAD_DIM, HEAD_DIM), core_w, seg_len=ts)
    y_s = _ffn_call(x_sample, p_sample, ffn_w, name="ffn_sample", mix=mix_s.reshape(bs * ts, D_MODEL), wout=wout)

    new_conv_p = qt_p[:, SUBLANES - (GDN_CONV - 1):]
    new_sc_p = st_p[:, SUBLANES - (SC_CONV - 1):]
    new_conv_s = jnp.transpose(qt_s, (1, 0, 2))
    return (y_p.reshape(bp, tp, D_MODEL), y_s, new_conv_p,
            s_p.reshape(bp, GDN_HEADS, HEAD_DIM, HEAD_DIM), new_sc_p, new_conv_s,
            s_s.reshape(bs, GDN_HEADS, HEAD_DIM, HEAD_DIM), new_sc_s)


def kernel(x_prompt, x_sample, state_gdn_conv, state_gdn, state_sc_conv, p_prompt, p_sample, norm_mix, w_in, w_conv_qkv, a_log, dt_bias, w_gdn_norm, w_conv_sc, w_out, norm_mlp, w_up, w_down, norm_ple, w_ple_gate, w_ple_proj, norm_f):
    depth = w_in.shape[0]
    assert depth == 1, "one layer per call"
    assert x_sample.shape[1] >= GDN_CONV - 1 and CHUNK % x_sample.shape[1] == 0
    assert x_prompt.shape[1] % PROMPT_ROWS == 0 and x_prompt.shape[0] % PROMPT_SEQS_PER_STEP == 0
    outs = _layer(x_prompt, x_sample, state_gdn_conv[0], state_gdn[0], state_sc_conv[0], p_prompt[0], p_sample[0],
                  norm_mix[0], w_in[0], w_conv_qkv[0], a_log[0], dt_bias[0], w_gdn_norm[0], w_conv_sc[0], w_out[0],
                  norm_mlp[0], w_up[0], w_down[0], norm_ple[0], w_ple_gate[0], w_ple_proj[0], norm_f)
    y_p, y_s, c_p, s_p, sc_p, c_s, s_s, sc_s = outs
    return (y_p, y_s, c_p[None], s_p[None], sc_p[None], c_s[None], s_s[None], sc_s[None])
```

```python
import functools

import jax
import jax.numpy as jnp
from jax import lax
from jax.experimental import pallas as pl
from jax.experimental.pallas import tpu as pltpu

F32 = jnp.float32
BF16 = jnp.bfloat16

D_MODEL = 1024
PLE_DIM = 256
GDN_HEADS = 4
HEAD_DIM = 128
GDN_WIDTH = GDN_HEADS * HEAD_DIM
QKV_DIM = 3 * GDN_WIDTH
GDN_CONV = 4
SC_WIDTH = D_MODEL - GDN_WIDTH
SC_CONV = 3
D_FF = 4 * D_MODEL
EPS = 1e-6
NEG_LOG2_E = -1.4426950408889634
CHUNK = 64
LANES = 128
SUBLANES = 8
VMEM_LIMIT_BYTES = 56 * 1024 * 1024
PROMPT_ROWS = 128
PROMPT_SEQS_PER_STEP = 4
FFN_ROWS = 1024
IN_PROJ_COLS = 512
SPLIT_COLS = 256

R_Z = 0
R_SCB = R_Z + GDN_WIDTH
R_SCC = R_SCB + SC_WIDTH
R_SCH = R_SCC + SC_WIDTH


def _mm(a, b):
    return jnp.dot(a.astype(BF16), b.astype(BF16), preferred_element_type=F32)


def _mm_nt(a, b):
    return lax.dot_general(a.astype(BF16), b.astype(BF16), (((1,), (1,)), ((), ())), preferred_element_type=F32)


def _mm_tn(a, b):
    return lax.dot_general(a.astype(BF16), b.astype(BF16), (((0,), (0,)), ((), ())), preferred_element_type=F32)


def _rmsnorm(x, w_row):
    return x * lax.rsqrt(jnp.mean(x * x, axis=-1, keepdims=True) + EPS) * w_row


def _sigmoid(x):
    return 1.0 / (1.0 + jnp.exp2(x * NEG_LOG2_E))


def _silu(x):
    return x * _sigmoid(x)


def _softplus(x):
    return jnp.maximum(x, 0.0) + jnp.log1p(jnp.exp(-jnp.abs(x)))


def _log2(n):
    k = n.bit_length() - 1
    assert (1 << k) == n, n
    return k


def _wy_block(q, k, v, g_col, g_row, beta_col, seg_len, load_state, store_state):
    c = CHUNK
    n_seg = c // seg_len
    lg = _log2(seg_len)
    ls = _log2(n_seg)
    ri = lax.broadcasted_iota(jnp.int32, (c, c), 0)
    ci = lax.broadcasted_iota(jnp.int32, (c, c), 1)
    same = jnp.bitwise_and(ri, n_seg - 1) == jnp.bitwise_and(ci, n_seg - 1)
    lower = same & (ri >= ci)
    strict = same & (ri > ci)
    diff = g_col - g_row
    decay = jnp.where(lower, jnp.exp(jnp.where(lower, diff, 0.0)), 0.0)
    a_off = jnp.where(strict, beta_col * _mm_nt(k, k) * decay, 0.0)
    qk = _mm_nt(q, k) * decay
    yield
    eye = jnp.where(ri == ci, 1.0, 0.0).astype(F32)
    x_inv = eye
    for lb in range(lg):
        rb = jnp.right_shift(ri, ls + lb)
        cb = jnp.right_shift(ci, ls + lb)
        join = (jnp.bitwise_and(rb, 1) == 1) & (cb == rb - 1)
        b_lvl = jnp.where(join, a_off, 0.0)
        if lb == 0:
            x_inv = eye - b_lvl
        else:
            xb = _mm(x_inv, b_lvl)
            yield
            x_inv = x_inv - _mm(xb, x_inv)
            yield
    rhs = jnp.concatenate([v * beta_col, k * (beta_col * jnp.exp(g_col))], axis=1)
    sol = _mm(x_inv, rhs)
    yield
    u = sol[:, :HEAD_DIM]
    w = sol[:, HEAD_DIM:]
    stacked = jnp.concatenate([w, q * jnp.exp(g_col)], axis=0)
    row = lax.broadcasted_iota(jnp.int32, (c, 1), 0)
    states = [load_state(s) for s in range(n_seg)]
    w_s = None
    q_s = None
    for s in range(n_seg):
        r = _mm(stacked, states[s])
        if n_seg == 1:
            w_s, q_s = r[:c], r[c:]
        else:
            in_seg = jnp.bitwise_and(row, n_seg - 1) == s
            w_s = jnp.where(in_seg, r[:c], 0.0 if w_s is None else w_s)
            q_s = jnp.where(in_seg, r[c:], 0.0 if q_s is None else q_s)
    v_new = u - w_s
    yield
    o = q_s + _mm(qk, v_new)
    for s in range(n_seg):
        last = (seg_len - 1) * n_seg + s
        g_last = g_col[last:last + 1, :]
        if n_seg == 1:
            k_dec = k * jnp.exp(g_last - g_col)
        else:
            in_seg = jnp.bitwise_and(row, n_seg - 1) == s
            k_dec = jnp.where(in_seg, k * jnp.exp(jnp.where(in_seg, g_last - g_col, 0.0)), 0.0)
        store_state(s, states[s] * jnp.exp(g_last) + _mm_tn(k_dec, v_new))
    return o


def _expand_heads(slab, first_lane, width):
    rows = slab.shape[0]
    cols = [slab[:, first_lane + hd:first_lane + hd + 1] for hd in range(GDN_HEADS)]
    if width % LANES == 0:
        return jnp.concatenate([jnp.broadcast_to(c, (rows, width)) for c in cols], axis=1)
    total = GDN_HEADS * width
    lane_head = jnp.right_shift(lax.broadcasted_iota(jnp.int32, (rows, total), 1), _log2(width))
    out = jnp.broadcast_to(cols[-1], (rows, total))
    for hd in range(GDN_HEADS - 2, -1, -1):
        out = jnp.where(lane_head == hd, jnp.broadcast_to(cols[hd], (rows, total)), out)
    return out


def _packed_masks():
    c, nh = CHUNK, GDN_HEADS
    ri = lax.broadcasted_iota(jnp.int32, (c, nh * c), 0)
    cj = jnp.bitwise_and(lax.broadcasted_iota(jnp.int32, (c, nh * c), 1), c - 1)
    return dict(lower=ri >= cj, strict=ri > cj, eye=ri == cj, join0=(jnp.bitwise_and(ri, 1) == 1) & (cj == ri - 1))


def _fill_block_factors(bd_ref, kbd_ref):
    c, nh = CHUNK, GDN_HEADS
    w = nh * c
    lg = _log2(c)
    rw = lax.broadcasted_iota(jnp.int32, (w, w), 0)
    cw = lax.broadcasted_iota(jnp.int32, (w, w), 1)
    same_head = jnp.right_shift(rw, lg) == jnp.right_shift(cw, lg)
    rw, cw = jnp.bitwise_and(rw, c - 1), jnp.bitwise_and(cw, c - 1)
    one_zero = lambda m: jnp.where(m, 1.0, 0.0).astype(BF16)
    bd_ref[0] = one_zero(same_head)
    for lb in range(1, lg):
        bd_ref[lb] = one_zero(same_head & (jnp.bitwise_and(jnp.right_shift(rw, lb), 1) == 1)
                              & (jnp.right_shift(cw, lb) == jnp.right_shift(rw, lb) - 1))
    kbd_ref[...] = one_zero(jnp.right_shift(lax.broadcasted_iota(jnp.int32, (w, nh * HEAD_DIM), 0), lg)
                            == jnp.right_shift(lax.broadcasted_iota(jnp.int32, (w, nh * HEAD_DIM), 1),
                                               _log2(HEAD_DIM)))


def _wy_tile_packed(q_all, k_all, v_all, g, beta, state_ref, rows, masks, bd_ref, kbd_ref):
    c, nh = CHUNK, GDN_HEADS
    lg = _log2(c)
    n_chunks = rows // c
    lower, strict, eye = masks["lower"], masks["strict"], masks["eye"]

    def block_diag(x, factor):
        return jnp.concatenate([x.astype(BF16)] * nh, axis=0) * factor

    def head(x, hd, width):
        return x[:, hd * width:(hd + 1) * width]

    g_c = _expand_heads(g, 0, c)
    b_c = _expand_heads(beta, nh, c)
    g_d = _expand_heads(g, 0, HEAD_DIM)
    b_d = _expand_heads(beta, nh, HEAD_DIM)
    eg_d = jnp.exp(g_d)
    v_rhs = v_all * b_d
    k_rhs = k_all * (b_d * eg_d)
    q_g = q_all * eg_d

    a_off, qk_dec = [], []
    for cb in range(n_chunks):
        rs = slice(cb * c, (cb + 1) * c)
        g_row = jnp.sum(jnp.where(eye, g_c[rs], 0.0), axis=0, keepdims=True)
        decay = jnp.where(lower, jnp.exp(jnp.where(lower, g_c[rs] - g_row, 0.0)), 0.0)
        kq = _mm_nt(jnp.concatenate([k_all[rs], q_all[rs]], axis=0), block_diag(k_all[rs], kbd_ref[...]))
        a_off.append(jnp.where(strict, b_c[rs] * kq[:c] * decay, 0.0))
        qk_dec.append(kq[c:] * decay)
    yield

    eye_f = jnp.where(eye, 1.0, 0.0).astype(F32)
    x_inv = [eye_f - jnp.where(masks["join0"], a, 0.0) for a in a_off]
    for lb in range(1, lg):
        xb = [jnp.dot(x.astype(BF16), block_diag(a, bd_ref[lb]), preferred_element_type=F32)
              for x, a in zip(x_inv, a_off)]
        yield
        xbx = [jnp.dot(t.astype(BF16), block_diag(x, bd_ref[0]), preferred_element_type=F32)
               for t, x in zip(xb, x_inv)]
        x_inv = [x - t for x, t in zip(x_inv, xbx)]
        yield

    u, wk = [], []
    for cb in range(n_chunks):
        rs = slice(cb * c, (cb + 1) * c)
        sol = [_mm(head(x_inv[cb], hd, c),
                   jnp.concatenate([head(v_rhs[rs], hd, HEAD_DIM), head(k_rhs[rs], hd, HEAD_DIM)], axis=1))
               for hd in range(nh)]
        u.append([s[:, :HEAD_DIM] for s in sol])
        wk.append([s[:, HEAD_DIM:] for s in sol])
    yield

    state = [state_ref[0, hd] for hd in range(nh)]
    o_blocks = []
    for cb in range(n_chunks):
        rs = slice(cb * c, (cb + 1) * c)
        g_last = g_d[(cb + 1) * c - 1:(cb + 1) * c, :]
        k_dec = k_all[rs] * jnp.exp(g_last - g_d[rs])
        eg_last = jnp.exp(g_last)
        r = [_mm(jnp.concatenate([wk[cb][hd], head(q_g[rs], hd, HEAD_DIM)], axis=0), state[hd]) for hd in range(nh)]
        v_new = [u[cb][hd] - r[hd][:c] for hd in range(nh)]
        yield
        o_blocks.append(jnp.concatenate(
            [r[hd][c:] + _mm(head(qk_dec[cb], hd, c), v_new[hd]) for hd in range(nh)], axis=1))
        state = [state[hd] * head(eg_last, hd, HEAD_DIM) + _mm_tn(head(k_dec, hd, HEAD_DIM), v_new[hd])
                 for hd in range(nh)]
        yield
    for hd in range(nh):
        state_ref[0, hd] = state[hd]
    return jnp.concatenate(o_blocks, axis=0)


def _run_tiles(programs):
    programs = list(programs)
    results = [None] * len(programs)
    live = list(range(len(programs)))
    while live:
        for idx in list(live):
            try:
                next(programs[idx])
            except StopIteration as done:
                results[idx] = done.value
                live.remove(idx)
    return results


def _shifted_history(buf, rows, n_taps):
    with_carry = buf[0:SUBLANES + rows, :]
    return [pltpu.roll(with_carry, s, 0)[SUBLANES:, :] for s in range(1, n_taps)]


def _shifted_positions(x_rows, taps):
    n_seq = taps[0].shape[0]
    rows = x_rows.shape[0]
    ext = jnp.concatenate(list(taps) + [x_rows], axis=0)
    return [ext[(len(taps) - s) * n_seq:(len(taps) - s) * n_seq + rows] for s in range(1, len(taps) + 1)]


def _mixer_tile(ti, rows, seg_len, long_seq, masks, r, ab, rest):
    n_seq = rows // seg_len
    new_rows = slice(SUBLANES, SUBLANES + rows)
    row = lax.broadcasted_iota(jnp.int32, (rows, 1), 0)
    if long_seq:
        qb, sb = r.qbuf.at[ti], r.sbuf.at[ti]
        pos, row_step = jnp.bitwise_and(row, seg_len - 1), 1
    else:
        assert rows == CHUNK
        pos, row_step = jnp.right_shift(row, _log2(n_seq)), n_seq

    gate = r.gate[...]
    log_a = -jnp.exp(gate[0:1, :]) * _softplus(ab + gate[1:2, :])
    beta = _sigmoid(ab)
    g = log_a
    shift = 1
    while shift < seg_len:
        g = g + jnp.where(pos >= shift, pltpu.roll(g, shift * row_step, 0), 0.0)
        shift *= 2

    sc_pre = rest[:, R_SCC:R_SCC + SC_WIDTH] * rest[:, R_SCH:R_SCH + SC_WIDTH]
    if long_seq:
        sb[new_rows, :] = sc_pre
        qkv_pre = qb[new_rows, :]
        cwq = r.cwq[...]
        with_carry = qb[0:SUBLANES + rows, :]
        back1 = pltpu.roll(with_carry, 1, 0)
        pair = with_carry * cwq[1:2, :] + back1 * cwq[0:1, :]
        qkv = _silu(pltpu.roll(pair, 2, 0)[SUBLANES:, :] + back1[SUBLANES:, :] * cwq[2:3, :]
                    + qkv_pre * cwq[3:4, :])
        r.qtail[ti] = qb[rows:rows + SUBLANES, :]
        r.stail[ti] = sb[rows:rows + SUBLANES, :]
    else:
        qkv_pre = r.qkv
        cwq = r.cwq[...]
        q1, q2, q3 = _shifted_positions(qkv_pre, r.hq)
        qkv = _silu(q3 * cwq[0:1, :] + q2 * cwq[1:2, :] + q1 * cwq[2:3, :] + qkv_pre * cwq[3:4, :])
        for j in range(SC_CONV - 1):
            first = (seg_len - (SC_CONV - 1) + j) * n_seq
            r.stail[:, j, :] = sc_pre[first:first + n_seq]
    q_n, k_n, v_n = [], [], []
    for hd in range(GDN_HEADS):
        lo = hd * HEAD_DIM
        q_h = qkv[:, lo:lo + HEAD_DIM]
        k_h = qkv[:, GDN_WIDTH + lo:GDN_WIDTH + lo + HEAD_DIM]
        q_n.append(q_h * (lax.rsqrt(jnp.sum(q_h * q_h, axis=-1, keepdims=True) + EPS) * (HEAD_DIM ** -0.5)))
        k_n.append(k_h * lax.rsqrt(jnp.sum(k_h * k_h, axis=-1, keepdims=True) + EPS))
        v_n.append(qkv[:, 2 * GDN_WIDTH + lo:2 * GDN_WIDTH + lo + HEAD_DIM])
    yield

    state_ref = r.s_out.at[ti]
    if long_seq:
        o_all = yield from _wy_tile_packed(jnp.concatenate(q_n, axis=1), jnp.concatenate(k_n, axis=1),
                                           jnp.concatenate(v_n, axis=1), g, beta, state_ref, rows, masks,
                                           r.bd, r.kbd)
        o_h = [o_all[:, hd * HEAD_DIM:(hd + 1) * HEAD_DIM] for hd in range(GDN_HEADS)]
    else:
        g_t = g.T
        blocks = []
        for hd in range(GDN_HEADS):

            def load_state(s, hd=hd):
                return r.s_in[ti, s, hd]

            def store_state(s, val, hd=hd):
                state_ref[s, hd] = val

            blocks.append(_wy_block(q_n[hd], k_n[hd], v_n[hd], g[:, hd:hd + 1], g_t[hd:hd + 1, :],
                                    beta[:, GDN_HEADS + hd:GDN_HEADS + hd + 1], seg_len, load_state, store_state))
        o_h = _run_tiles(blocks)
    yield

    s1, s2 = _shifted_history(sb, rows, SC_CONV) if long_seq else _shifted_positions(sc_pre, r.hs)
    cws = r.cws[...]
    y_sc = rest[:, R_SCB:R_SCB + SC_WIDTH] * (s2 * cws[0:1, :] + s1 * cws[1:2, :] + sc_pre * cws[2:3, :])
    if long_seq:
        qb[0:SUBLANES, :] = qb[rows:rows + SUBLANES, :]
        sb[0:SUBLANES, :] = sb[rows:rows + SUBLANES, :]
    gnorm = r.gnorm[...]
    o_heads = []
    for hd in range(GDN_HEADS):
        z_h = rest[:, R_Z + hd * HEAD_DIM:R_Z + (hd + 1) * HEAD_DIM]
        o_heads.append(_rmsnorm(o_h[hd], gnorm) * _silu(z_h))
    return jnp.concatenate(o_heads + [y_sc], axis=1)


def _load_by_position(ref):
    n_seq, seg_len, _ = ref.shape
    per_tile = CHUNK // seg_len
    return jnp.concatenate([ref[b:b + per_tile, t, :] for b in range(0, n_seq, per_tile) for t in range(seg_len)],
                           axis=0)


def _store_by_position(ref, rows):
    n_seq, seg_len, _ = ref.shape
    per_tile = CHUNK // seg_len
    for b in range(0, n_seq, per_tile):
        for t in range(seg_len):
            first = b * seg_len + t * per_tile
            ref[b:b + per_tile, t, :] = rows[first:first + per_tile]


class _Refs:
    def __init__(self, **refs):
        self.__dict__.update(refs)


def _mixer_seq_kernel(n_tiles, rows, n_cast, *refs):
    n_in = 9
    x, nmix, w_ab, w_main, cwq, gate, gnorm, cws, wout = refs[:n_in]
    cast_in = refs[n_in:n_in + n_cast]
    xmid, qtail, stail, s_out = refs[n_in + n_cast:n_in + n_cast + 4]
    cast_out = refs[n_in + n_cast + 4:n_in + 2 * n_cast + 4]
    qbuf, sbuf, bd, kbd = refs[n_in + 2 * n_cast + 4:]
    for src, dst in zip(cast_in, cast_out):
        dst[...] = src[...].astype(BF16)

    @pl.when(pl.program_id(1) == 0)
    def _():
        s_out[...] = jnp.zeros(s_out.shape, F32)
        qbuf[:, 0:SUBLANES, :] = jnp.zeros((n_tiles, SUBLANES, QKV_DIM), F32)
        sbuf[:, 0:SUBLANES, :] = jnp.zeros((n_tiles, SUBLANES, SC_WIDTH), F32)
        _fill_block_factors(bd, kbd)

    x_all = x[...].reshape(n_tiles * rows, D_MODEL)
    h = _rmsnorm(x_all, nmix[...]).astype(BF16)
    ab = jnp.dot(h, w_ab[...], preferred_element_type=F32)
    qbuf[:, SUBLANES:SUBLANES + rows, :] = jnp.dot(h, w_main[:, 0:QKV_DIM], preferred_element_type=F32).reshape(
        n_tiles, rows, QKV_DIM)
    rest = jnp.dot(h, w_main[:, QKV_DIM:], preferred_element_type=F32)
    r = _Refs(cwq=cwq, gate=gate, gnorm=gnorm, cws=cws, qtail=qtail, stail=stail, s_out=s_out, qbuf=qbuf, sbuf=sbuf,
              bd=bd, kbd=kbd)
    masks = _packed_masks()
    tile = lambda a, ti: a[ti * rows:(ti + 1) * rows]
    mix = _run_tiles(_mixer_tile(ti, rows, CHUNK, True, masks, r, tile(ab, ti), tile(rest, ti))
                     for ti in range(n_tiles))
    xmid[...] = (x_all + _mm(jnp.concatenate(mix, axis=0), wout[...])).reshape(n_tiles, rows, D_MODEL)


def _mixer_step_kernel(rows, seg_len,
                       ab, proj, hq, hs, s_in, cwq, gate, gnorm, cws,
                       mix, qtail, stail, s_out):
    n_seq = rows // seg_len
    qkv = proj[0, :, 0:QKV_DIM]
    for j in range(GDN_CONV - 1):
        first = (seg_len - (GDN_CONV - 1) + j) * n_seq
        qtail[j] = qkv[first:first + n_seq]
    r = _Refs(qkv=qkv, hq=[hq[j] for j in range(GDN_CONV - 1)], hs=[hs[:, j, :] for j in range(SC_CONV - 1)],
              cwq=cwq, gate=gate, gnorm=gnorm, cws=cws, stail=stail, s_in=s_in, s_out=s_out)
    mix[0] = _run_tiles([_mixer_tile(0, rows, seg_len, False, None, r, ab[0], proj[0, :, QKV_DIM:])])[0].astype(BF16)


def _ffn_body(x, p, nmlp_ref, wup_ref, wdown_ref, nple_ref, wg_ref, wp_ref, nf_ref):
    hn = _rmsnorm(x, nmlp_ref[...]).astype(BF16)
    acc = x
    for j in range(D_FF // D_MODEL):
        u = jnp.maximum(_mm(hn, wup_ref[:, j * D_MODEL:(j + 1) * D_MODEL]), 0.0)
        acc = acc + _mm(u * u, wdown_ref[j * D_MODEL:(j + 1) * D_MODEL, :])
    gate = _sigmoid(_mm(_rmsnorm(acc, nple_ref[...]), wg_ref[...]))
    x3 = acc + gate * _mm(p, wp_ref[...])
    return _rmsnorm(x3, nf_ref[...])


def _ffn_kernel(x_ref, p_ref, *rest):
    rest[-1][...] = _ffn_body(x_ref[...], p_ref[...], *rest[:-1])


def _out_ffn_kernel(x_ref, mix_ref, wout_ref, p_ref, *rest):
    x = _load_by_position(x_ref) + jnp.dot(mix_ref[...], wout_ref[...], preferred_element_type=F32)
    _store_by_position(rest[-1], _ffn_body(x, _load_by_position(p_ref), *rest[:-1]))


def _const_spec(shape):
    nd = len(shape)
    return pl.BlockSpec(shape, lambda *_: (0,) * nd, pipeline_mode=pl.Buffered(1))


def _state_spec(n_tiles, n_state):
    return pl.BlockSpec((n_tiles, n_state, GDN_HEADS, HEAD_DIM, HEAD_DIM), lambda b, i: (b, 0, 0, 0, 0))


def _mixer_seq_call(x3, weights, to_cast, *, n_tiles, rows):
    seqs, seq_rows, _ = x3.shape
    tiles = seq_rows // rows
    n_steps = (seqs // n_tiles) * tiles
    cast_specs = [pl.BlockSpec((w.shape[0] // n_steps, w.shape[1]), lambda b, i: (b * tiles + i, 0)) for w in to_cast]
    row_map = lambda b, i: (b, i, 0)
    seq_map = lambda b, i: (b, 0, 0)
    x_spec = pl.BlockSpec((n_tiles, rows, D_MODEL), row_map)
    hq_spec = pl.BlockSpec((n_tiles, SUBLANES, QKV_DIM), seq_map)
    hs_spec = pl.BlockSpec((n_tiles, SUBLANES, SC_WIDTH), seq_map)
    packed_w = GDN_HEADS * CHUNK
    return pl.pallas_call(
        functools.partial(_mixer_seq_kernel, n_tiles, rows, len(to_cast)),
        grid=(seqs // n_tiles, tiles),
        in_specs=[x_spec] + [_const_spec(w.shape) for w in weights] + cast_specs,
        out_specs=[x_spec, hq_spec, hs_spec, _state_spec(n_tiles, 1)] + cast_specs,
        out_shape=[jax.ShapeDtypeStruct(x3.shape, F32), jax.ShapeDtypeStruct((seqs, SUBLANES, QKV_DIM), F32),
                   jax.ShapeDtypeStruct((seqs, SUBLANES, SC_WIDTH), F32),
                   jax.ShapeDtypeStruct((seqs, 1, GDN_HEADS, HEAD_DIM, HEAD_DIM), F32)]
        + [jax.ShapeDtypeStruct(w.shape, BF16) for w in to_cast],
        scratch_shapes=[pltpu.VMEM((n_tiles, rows + SUBLANES, QKV_DIM), F32),
                        pltpu.VMEM((n_tiles, rows + SUBLANES, SC_WIDTH), F32),
                        pltpu.VMEM((_log2(CHUNK), packed_w, packed_w), BF16),
                        pltpu.VMEM((packed_w, GDN_WIDTH), BF16)],
        compiler_params=pltpu.CompilerParams(
            dimension_semantics=("arbitrary", "arbitrary"), vmem_limit_bytes=VMEM_LIMIT_BYTES),
        name="mixer_seq",
    )(x3, *weights, *to_cast)


def _in_proj_kernel(n_qkv_steps, wt_ref, behind_ref, x_ref, nmix_ref, w_ab_ref, w_main_ref, ab_ref, proj_ref, h_ref):
    i = pl.program_id(0)

    @pl.when(i == 0)
    def _():
        h_ref[...] = _rmsnorm(_load_by_position(x_ref), nmix_ref[...]).astype(BF16)

    on_grid = wt_ref[...]
    behind = behind_ref[...]
    group = jnp.where(i < n_qkv_steps, on_grid, jnp.concatenate([on_grid[SUBLANES:], behind], axis=0))
    for c0 in range(0, IN_PROJ_COLS, SPLIT_COLS):
        w_main_ref[:, c0:c0 + SPLIT_COLS] = group[c0:c0 + SPLIT_COLS].T.astype(BF16)
    proj_ref[...] = jnp.dot(h_ref[...], w_main_ref[...], preferred_element_type=F32)

    @pl.when(i == n_qkv_steps - 1)
    def _():
        w_ab = jnp.concatenate([behind, jnp.zeros((LANES - SUBLANES, D_MODEL), F32)], axis=0).T.astype(BF16)
        w_ab_ref[...] = w_ab
        ab_ref[...] = jnp.dot(h_ref[...], w_ab, preferred_element_type=F32)


def _full_spec(a):
    nd = len(a.shape)
    return pl.BlockSpec(a.shape, lambda i: (0,) * nd)


def _in_proj_call(w_in, x3, nmix):
    rows, cols = w_in.shape
    n_seq, seg_len, _ = x3.shape
    n = n_seq * seg_len
    n_main = cols - 2 * GDN_HEADS
    assert 2 * GDN_HEADS == SUBLANES and QKV_DIM % IN_PROJ_COLS == 0 and n_main % IN_PROJ_COLS == 0
    w_t = w_in.T
    col_block = lambda height: pl.BlockSpec((height, IN_PROJ_COLS), lambda i: (0, i))
    outs = [jax.ShapeDtypeStruct((rows, LANES), BF16), jax.ShapeDtypeStruct((rows, n_main), BF16),
            jax.ShapeDtypeStruct((n, LANES), F32), jax.ShapeDtypeStruct((n, n_main), F32)]
    return pl.pallas_call(
        functools.partial(_in_proj_kernel, QKV_DIM // IN_PROJ_COLS),
        grid=(n_main // IN_PROJ_COLS,),
        in_specs=[pl.BlockSpec((IN_PROJ_COLS, rows), lambda i: (i, 0)),
                  pl.BlockSpec((SUBLANES, rows), lambda i: ((i + 1) * (IN_PROJ_COLS // SUBLANES), 0)),
                  _full_spec(x3), _full_spec(nmix)],
        out_specs=[_full_spec(outs[0]), col_block(rows), _full_spec(outs[2]), col_block(n)],
        out_shape=outs,
        scratch_shapes=[pltpu.VMEM((n, rows), BF16)],
        compiler_params=pltpu.CompilerParams(dimension_semantics=("arbitrary",), vmem_limit_bytes=VMEM_LIMIT_BYTES),
        name="in_proj",
    )(w_t, w_t, x3, nmix)


def _mixer_step_call(ab, proj, hist_q, hist_s, s_in, weights, *, seg_len):
    tiles, rows, _ = proj.shape
    tile_map = lambda b, i: (b, 0, 0)
    slab = lambda a: pl.BlockSpec((1, rows, a.shape[2]), tile_map)
    n_state = s_in.shape[1]
    hist_q_spec = pl.BlockSpec((GDN_CONV - 1, n_state, QKV_DIM), lambda b, i: (0, b, 0))
    hist_s_spec = pl.BlockSpec((n_state, SC_CONV - 1, SC_WIDTH), tile_map)
    mix = jax.ShapeDtypeStruct((tiles, rows, D_MODEL), BF16)
    like = lambda a: jax.ShapeDtypeStruct(a.shape, F32)
    return pl.pallas_call(
        functools.partial(_mixer_step_kernel, rows, seg_len),
        grid=(tiles, 1),
        in_specs=[slab(ab), slab(proj), hist_q_spec, hist_s_spec, _state_spec(1, n_state)]
        + [_const_spec(w.shape) for w in weights],
        out_specs=[slab(mix), hist_q_spec, hist_s_spec, _state_spec(1, n_state)],
        out_shape=[mix, like(hist_q), like(hist_s), like(s_in)],
        compiler_params=pltpu.CompilerParams(
            dimension_semantics=("arbitrary", "arbitrary"), vmem_limit_bytes=VMEM_LIMIT_BYTES),
        name="mixer_step",
    )(ab, proj, hist_q, hist_s, s_in, *weights)


def _ffn_call(x, p, weights, *, name, rows=None, mix=None, wout=None):
    if mix is None:
        steps = x.shape[0] // rows
        rows_spec = lambda width: pl.BlockSpec((rows, width), lambda i: (i, 0))
        x_spec, p_spec, pre_args, pre_specs, body = rows_spec(D_MODEL), rows_spec(PLE_DIM), [], [], _ffn_kernel
    else:
        steps = 1
        x_spec, p_spec, body = _full_spec(x), _full_spec(p), _out_ffn_kernel
        pre_args, pre_specs = [mix, wout], [_full_spec(mix), _const_spec(wout.shape)]
    return pl.pallas_call(
        body,
        grid=(steps,),
        in_specs=[x_spec] + pre_specs + [p_spec] + [_const_spec(w.shape) for w in weights],
        out_specs=x_spec,
        out_shape=jax.ShapeDtypeStruct(x.shape, F32),
        input_output_aliases={0: 0} if mix is None else {},
        compiler_params=pltpu.CompilerParams(dimension_semantics=("arbitrary",), vmem_limit_bytes=VMEM_LIMIT_BYTES),
        name=name,
    )(x, *pre_args, p, *weights)


def _layer(x_prompt, x_sample, conv_qkv, s_gdn, conv_sc, p_prompt, p_sample, norm_mix, w_in, w_conv_qkv, a_log,
           dt_bias, w_gdn_norm, w_conv_sc, w_out, norm_mlp, w_up, w_down, norm_ple, w_ple_gate, w_ple_proj, norm_f):
    bp, tp, _ = x_prompt.shape
    bs, ts, _ = x_sample.shape
    nmix = norm_mix.reshape(1, D_MODEL)
    w_ab, w_main, ab_s, proj_s = _in_proj_call(w_in, x_sample, nmix)
    wout = w_out.astype(BF16)
    gate = jnp.zeros((SUBLANES, LANES), F32)
    gate = gate.at[0, :GDN_HEADS].set(a_log.astype(F32)).at[1, :GDN_HEADS].set(dt_bias.astype(F32))
    core_w = (w_conv_qkv, gate, w_gdn_norm.reshape(1, HEAD_DIM), w_conv_sc)

    xm_p, qt_p, st_p, s_p, wup, wdown, wgate = _mixer_seq_call(
        x_prompt, (nmix, w_ab, w_main) + core_w + (wout,), (w_up, w_down, w_ple_gate),
        n_tiles=PROMPT_SEQS_PER_STEP, rows=PROMPT_ROWS)
    ffn_w = (norm_mlp.reshape(1, D_MODEL), wup, wdown, norm_ple.reshape(1, D_MODEL), wgate,
             w_ple_proj.astype(BF16), norm_f.reshape(1, D_MODEL))
    y_p = _ffn_call(xm_p.reshape(bp * tp, D_MODEL), p_prompt.reshape(bp * tp, PLE_DIM), ffn_w, rows=FFN_ROWS,
                    name="ffn_prompt")

    seq_per_tile = CHUNK // ts
    tiles = bs // seq_per_tile
    tiled = lambda a: a.reshape(tiles, CHUNK, a.shape[-1])
    mix_s, qt_s, new_sc_s, s_s = _mixer_step_call(
        tiled(ab_s), tiled(proj_s), jnp.transpose(conv_qkv, (1, 0, 2)), conv_sc,
        s_gdn.reshape(tiles, seq_per_tile, GDN_HEADS, HEAD_DIM, HEAD_DIM), core_w, seg_len=ts)
    y_s = _ffn_call(x_sample, p_sample, ffn_w, name="ffn_sample", mix=mix_s.reshape(bs * ts, D_MODEL), wout=wout)

    new_conv_p = qt_p[:, SUBLANES - (GDN_CONV - 1):]
    new_sc_p = st_p[:, SUBLANES - (SC_CONV - 1):]
    new_conv_s = jnp.transpose(qt_s, (1, 0, 2))
    return (y_p.reshape(bp, tp, D_MODEL), y_s, new_conv_p,
            s_p.reshape(bp, GDN_HEADS, HEAD_DIM, HEAD_DIM), new_sc_p, new_conv_s,
            s_s.reshape(bs, GDN_HEADS, HEAD_DIM, HEAD_DIM), new_sc_s)


def kernel(x_prompt, x_sample, state_gdn_conv, state_gdn, state_sc_conv, p_prompt, p_sample, norm_mix, w_in, w_conv_qkv, a_log, dt_bias, w_gdn_norm, w_conv_sc, w_out, norm_mlp, w_up, w_down, norm_ple, w_ple_gate, w_ple_proj, norm_f):
    depth = w_in.shape[0]
    assert depth == 1, "one layer per call"
    assert x_sample.shape[1] >= GDN_CONV - 1 and CHUNK % x_sample.shape[1] == 0
    assert x_prompt.shape[1] % PROMPT_ROWS == 0 and x_prompt.shape[0] % PROMPT_SEQS_PER_STEP == 0
    outs = _layer(x_prompt, x_sample, state_gdn_conv[0], state_gdn[0], state_sc_conv[0], p_prompt[0], p_sample[0],
                  norm_mix[0], w_in[0], w_conv_qkv[0], a_log[0], dt_bias[0], w_gdn_norm[0], w_conv_sc[0], w_out[0],
                  norm_mlp[0], w_up[0], w_down[0], norm_ple[0], w_ple_gate[0], w_ple_proj[0], norm_f)
    y_p, y_s, c_p, s_p, sc_p, c_s, s_s, sc_s = outs
    return (y_p, y_s, c_p[None], s_p[None], sc_p[None], c_s[None], s_s[None], sc_s[None])
```

```python
import functools

import jax
import jax.numpy as jnp
from jax import lax
from jax.experimental import pallas as pl
from jax.experimental.pallas import tpu as pltpu

F32 = jnp.float32
BF16 = jnp.bfloat16

D_MODEL = 1024
PLE_DIM = 256
GDN_HEADS = 4
HEAD_DIM = 128
GDN_WIDTH = GDN_HEADS * HEAD_DIM
QKV_DIM = 3 * GDN_WIDTH
GDN_CONV = 4
SC_WIDTH = D_MODEL - GDN_WIDTH
SC_CONV = 3
D_FF = 4 * D_MODEL
EPS = 1e-6
NEG_LOG2_E = -1.4426950408889634
CHUNK = 64
LANES = 128
SUBLANES = 8
VMEM_LIMIT_BYTES = 56 * 1024 * 1024
PROMPT_ROWS = 128
PROMPT_SEQS_PER_STEP = 4
FFN_ROWS = 1024
IN_PROJ_COLS = 512
SPLIT_COLS = 256

R_Z = 0
R_SCB = R_Z + GDN_WIDTH
R_SCC = R_SCB + SC_WIDTH
R_SCH = R_SCC + SC_WIDTH


def _mm(a, b):
    return jnp.dot(a.astype(BF16), b.astype(BF16), preferred_element_type=F32)


def _mm_nt(a, b):
    return lax.dot_general(a.astype(BF16), b.astype(BF16), (((1,), (1,)), ((), ())), preferred_element_type=F32)


def _mm_tn(a, b):
    return lax.dot_general(a.astype(BF16), b.astype(BF16), (((0,), (0,)), ((), ())), preferred_element_type=F32)


def _rmsnorm(x, w_row):
    return x * lax.rsqrt(jnp.mean(x * x, axis=-1, keepdims=True) + EPS) * w_row


def _sigmoid(x):
    return 1.0 / (1.0 + jnp.exp2(x * NEG_LOG2_E))


def _silu(x):
    return x * _sigmoid(x)


def _softplus(x):
    return jnp.maximum(x, 0.0) + jnp.log1p(jnp.exp(-jnp.abs(x)))


def _log2(n):
    k = n.bit_length() - 1
    assert (1 << k) == n, n
    return k


def _wy_block(q, k, v, g_col, g_row, beta_col, seg_len, load_state, store_state):
    c = CHUNK
    n_seg = c // seg_len
    lg = _log2(seg_len)
    ls = _log2(n_seg)
    ri = lax.broadcasted_iota(jnp.int32, (c, c), 0)
    ci = lax.broadcasted_iota(jnp.int32, (c, c), 1)
    same = jnp.bitwise_and(ri, n_seg - 1) == jnp.bitwise_and(ci, n_seg - 1)
    lower = same & (ri >= ci)
    strict = same & (ri > ci)
    diff = g_col - g_row
    decay = jnp.where(lower, jnp.exp(jnp.where(lower, diff, 0.0)), 0.0)
    a_off = jnp.where(strict, beta_col * _mm_nt(k, k) * decay, 0.0)
    qk = _mm_nt(q, k) * decay
    yield
    eye = jnp.where(ri == ci, 1.0, 0.0).astype(F32)
    x_inv = eye
    for lb in range(lg):
        rb = jnp.right_shift(ri, ls + lb)
        cb = jnp.right_shift(ci, ls + lb)
        join = (jnp.bitwise_and(rb, 1) == 1) & (cb == rb - 1)
        b_lvl = jnp.where(join, a_off, 0.0)
        if lb == 0:
            x_inv = eye - b_lvl
        else:
            xb = _mm(x_inv, b_lvl)
            yield
            x_inv = x_inv - _mm(xb, x_inv)
            yield
    rhs = jnp.concatenate([v * beta_col, k * (beta_col * jnp.exp(g_col))], axis=1)
    sol = _mm(x_inv, rhs)
    yield
    u = sol[:, :HEAD_DIM]
    w = sol[:, HEAD_DIM:]
    stacked = jnp.concatenate([w, q * jnp.exp(g_col)], axis=0)
    row = lax.broadcasted_iota(jnp.int32, (c, 1), 0)
    states = [load_state(s) for s in range(n_seg)]
    w_s = None
    q_s = None
    for s in range(n_seg):
        r = _mm(stacked, states[s])
        if n_seg == 1:
            w_s, q_s = r[:c], r[c:]
        else:
            in_seg = jnp.bitwise_and(row, n_seg - 1) == s
            w_s = jnp.where(in_seg, r[:c], 0.0 if w_s is None else w_s)
            q_s = jnp.where(in_seg, r[c:], 0.0 if q_s is None else q_s)
    v_new = u - w_s
    yield
    o = q_s + _mm(qk, v_new)
    for s in range(n_seg):
        last = (seg_len - 1) * n_seg + s
        g_last = g_col[last:last + 1, :]
        if n_seg == 1:
            k_dec = k * jnp.exp(g_last - g_col)
        else:
            in_seg = jnp.bitwise_and(row, n_seg - 1) == s
            k_dec = jnp.where(in_seg, k * jnp.exp(jnp.where(in_seg, g_last - g_col, 0.0)), 0.0)
        store_state(s, states[s] * jnp.exp(g_last) + _mm_tn(k_dec, v_new))
    return o


def _expand_heads(slab, first_lane, width):
    rows = slab.shape[0]
    cols = [slab[:, first_lane + hd:first_lane + hd + 1] for hd in range(GDN_HEADS)]
    if width % LANES == 0:
        return jnp.concatenate([jnp.broadcast_to(c, (rows, width)) for c in cols], axis=1)
    total = GDN_HEADS * width
    lane_head = jnp.right_shift(lax.broadcasted_iota(jnp.int32, (rows, total), 1), _log2(width))
    out = jnp.broadcast_to(cols[-1], (rows, total))
    for hd in range(GDN_HEADS - 2, -1, -1):
        out = jnp.where(lane_head == hd, jnp.broadcast_to(cols[hd], (rows, total)), out)
    return out


def _packed_masks():
    c, nh = CHUNK, GDN_HEADS
    ri = lax.broadcasted_iota(jnp.int32, (c, nh * c), 0)
    cj = jnp.bitwise_and(lax.broadcasted_iota(jnp.int32, (c, nh * c), 1), c - 1)
    return dict(lower=ri >= cj, strict=ri > cj, eye=ri == cj, join0=(jnp.bitwise_and(ri, 1) == 1) & (cj == ri - 1))


def _fill_block_factors(bd_ref, kbd_ref):
    c, nh = CHUNK, GDN_HEADS
    w = nh * c
    lg = _log2(c)
    rw = lax.broadcasted_iota(jnp.int32, (w, w), 0)
    cw = lax.broadcasted_iota(jnp.int32, (w, w), 1)
    same_head = jnp.right_shift(rw, lg) == jnp.right_shift(cw, lg)
    rw, cw = jnp.bitwise_and(rw, c - 1), jnp.bitwise_and(cw, c - 1)
    one_zero = lambda m: jnp.where(m, 1.0, 0.0).astype(BF16)
    bd_ref[0] = one_zero(same_head)
    for lb in range(1, lg):
        bd_ref[lb] = one_zero(same_head & (jnp.bitwise_and(jnp.right_shift(rw, lb), 1) == 1)
                              & (jnp.right_shift(cw, lb) == jnp.right_shift(rw, lb) - 1))
    kbd_ref[...] = one_zero(jnp.right_shift(lax.broadcasted_iota(jnp.int32, (w, nh * HEAD_DIM), 0), lg)
                            == jnp.right_shift(lax.broadcasted_iota(jnp.int32, (w, nh * HEAD_DIM), 1),
                                               _log2(HEAD_DIM)))


def _wy_tile_packed(q_all, k_all, v_all, g, beta, state_ref, rows, masks, bd_ref, kbd_ref):
    c, nh = CHUNK, GDN_HEADS
    lg = _log2(c)
    n_chunks = rows // c
    lower, strict, eye = masks["lower"], masks["strict"], masks["eye"]

    def block_diag(x, factor):
        return jnp.concatenate([x.astype(BF16)] * nh, axis=0) * factor

    def head(x, hd, width):
        return x[:, hd * width:(hd + 1) * width]

    g_c = _expand_heads(g, 0, c)
    b_c = _expand_heads(beta, nh, c)
    g_d = _expand_heads(g, 0, HEAD_DIM)
    b_d = _expand_heads(beta, nh, HEAD_DIM)
    eg_d = jnp.exp(g_d)
    v_rhs = v_all * b_d
    k_rhs = k_all * (b_d * eg_d)
    q_g = q_all * eg_d

    a_off, qk_dec = [], []
    for cb in range(n_chunks):
        rs = slice(cb * c, (cb + 1) * c)
        g_row = jnp.sum(jnp.where(eye, g_c[rs], 0.0), axis=0, keepdims=True)
        decay = jnp.where(lower, jnp.exp(jnp.where(lower, g_c[rs] - g_row, 0.0)), 0.0)
        kq = _mm_nt(jnp.concatenate([k_all[rs], q_all[rs]], axis=0), block_diag(k_all[rs], kbd_ref[...]))
        a_off.append(jnp.where(strict, b_c[rs] * kq[:c] * decay, 0.0))
        qk_dec.append(kq[c:] * decay)
    yield

    eye_f = jnp.where(eye, 1.0, 0.0).astype(F32)
    x_inv = [eye_f - jnp.where(masks["join0"], a, 0.0) for a in a_off]
    for lb in range(1, lg):
        xb = [jnp.dot(x.astype(BF16), block_diag(a, bd_ref[lb]), preferred_element_type=F32)
              for x, a in zip(x_inv, a_off)]
        yield
        xbx = [jnp.dot(t.astype(BF16), block_diag(x, bd_ref[0]), preferred_element_type=F32)
               for t, x in zip(xb, x_inv)]
        x_inv = [x - t for x, t in zip(x_inv, xbx)]
        yield

    u, wk = [], []
    for cb in range(n_chunks):
        rs = slice(cb * c, (cb + 1) * c)
        sol = [_mm(head(x_inv[cb], hd, c),
                   jnp.concatenate([head(v_rhs[rs], hd, HEAD_DIM), head(k_rhs[rs], hd, HEAD_DIM)], axis=1))
               for hd in range(nh)]
        u.append([s[:, :HEAD_DIM] for s in sol])
        wk.append([s[:, HEAD_DIM:] for s in sol])
    yield

    state = [state_ref[0, hd] for hd in range(nh)]
    o_blocks = []
    for cb in range(n_chunks):
        rs = slice(cb * c, (cb + 1) * c)
        g_last = g_d[(cb + 1) * c - 1:(cb + 1) * c, :]
        k_dec = k_all[rs] * jnp.exp(g_last - g_d[rs])
        eg_last = jnp.exp(g_last)
        r = [_mm(jnp.concatenate([wk[cb][hd], head(q_g[rs], hd, HEAD_DIM)], axis=0), state[hd]) for hd in range(nh)]
        v_new = [u[cb][hd] - r[hd][:c] for hd in range(nh)]
        yield
        o_blocks.append(jnp.concatenate(
            [r[hd][c:] + _mm(head(qk_dec[cb], hd, c), v_new[hd]) for hd in range(nh)], axis=1))
        state = [state[hd] * head(eg_last, hd, HEAD_DIM) + _mm_tn(head(k_dec, hd, HEAD_DIM), v_new[hd])
                 for hd in range(nh)]
        yield
    for hd in range(nh):
        state_ref[0, hd] = state[hd]
    return jnp.concatenate(o_blocks, axis=0)


def _run_tiles(programs):
    programs = list(programs)
    results = [None] * len(programs)
    live = list(range(len(programs)))
    while live:
        for idx in list(live):
            try:
                next(programs[idx])
            except StopIteration as done:
                results[idx] = done.value
                live.remove(idx)
    return results


def _shifted_history(buf, rows, n_taps):
    with_carry = buf[0:SUBLANES + rows, :]
    return [pltpu.roll(with_carry, s, 0)[SUBLANES:, :] for s in range(1, n_taps)]


def _shifted_positions(x_rows, taps):
    n_seq = taps[0].shape[0]
    rows = x_rows.shape[0]
    ext = jnp.concatenate(list(taps) + [x_rows], axis=0)
    return [ext[(len(taps) - s) * n_seq:(len(taps) - s) * n_seq + rows] for s in range(1, len(taps) + 1)]


def _mixer_tile(ti, rows, seg_len, long_seq, masks, r, ab, rest):
    n_seq = rows // seg_len
    new_rows = slice(SUBLANES, SUBLANES + rows)
    row = lax.broadcasted_iota(jnp.int32, (rows, 1), 0)
    if long_seq:
        qb, sb = r.qbuf.at[ti], r.sbuf.at[ti]
        pos, row_step = jnp.bitwise_and(row, seg_len - 1), 1
    else:
        assert rows == CHUNK
        pos, row_step = jnp.right_shift(row, _log2(n_seq)), n_seq

    gate = r.gate[...]
    log_a = -jnp.exp(gate[0:1, :]) * _softplus(ab + gate[1:2, :])
    beta = _sigmoid(ab)
    g = log_a
    shift = 1
    while shift < seg_len:
        g = g + jnp.where(pos >= shift, pltpu.roll(g, shift * row_step, 0), 0.0)
        shift *= 2

    sc_pre = rest[:, R_SCC:R_SCC + SC_WIDTH] * rest[:, R_SCH:R_SCH + SC_WIDTH]
    if long_seq:
        sb[new_rows, :] = sc_pre
        qkv_pre = qb[new_rows, :]
        cwq = r.cwq[...]
        with_carry = qb[0:SUBLANES + rows, :]
        back1 = pltpu.roll(with_carry, 1, 0)
        pair = with_carry * cwq[1:2, :] + back1 * cwq[0:1, :]
        qkv = _silu(pltpu.roll(pair, 2, 0)[SUBLANES:, :] + back1[SUBLANES:, :] * cwq[2:3, :]
                    + qkv_pre * cwq[3:4, :])
        r.qtail[ti] = qb[rows:rows + SUBLANES, :]
        r.stail[ti] = sb[rows + SUBLANES - (SC_CONV - 1):rows + SUBLANES, :]
    else:
        qkv_pre = r.qkv
        cwq = r.cwq[...]
        q1, q2, q3 = _shifted_positions(qkv_pre, r.hq)
        qkv = _silu(q3 * cwq[0:1, :] + q2 * cwq[1:2, :] + q1 * cwq[2:3, :] + qkv_pre * cwq[3:4, :])
        for j in range(SC_CONV - 1):
            first = (seg_len - (SC_CONV - 1) + j) * n_seq
            r.stail[:, j, :] = sc_pre[first:first + n_seq]
    q_n, k_n, v_n = [], [], []
    for hd in range(GDN_HEADS):
        lo = hd * HEAD_DIM
        q_h = qkv[:, lo:lo + HEAD_DIM]
        k_h = qkv[:, GDN_WIDTH + lo:GDN_WIDTH + lo + HEAD_DIM]
        q_n.append(q_h * (lax.rsqrt(jnp.sum(q_h * q_h, axis=-1, keepdims=True) + EPS) * (HEAD_DIM ** -0.5)))
        k_n.append(k_h * lax.rsqrt(jnp.sum(k_h * k_h, axis=-1, keepdims=True) + EPS))
        v_n.append(qkv[:, 2 * GDN_WIDTH + lo:2 * GDN_WIDTH + lo + HEAD_DIM])
    yield

    state_ref = r.s_out.at[ti]
    if long_seq:
        o_all = yield from _wy_tile_packed(jnp.concatenate(q_n, axis=1), jnp.concatenate(k_n, axis=1),
                                           jnp.concatenate(v_n, axis=1), g, beta, state_ref, rows, masks,
                                           r.bd, r.kbd)
        o_h = [o_all[:, hd * HEAD_DIM:(hd + 1) * HEAD_DIM] for hd in range(GDN_HEADS)]
    else:
        g_t = g.T
        blocks = []
        for hd in range(GDN_HEADS):

            def load_state(s, hd=hd):
                return r.s_in[ti, s, hd]

            def store_state(s, val, hd=hd):
                state_ref[s, hd] = val

            blocks.append(_wy_block(q_n[hd], k_n[hd], v_n[hd], g[:, hd:hd + 1], g_t[hd:hd + 1, :],
                                    beta[:, GDN_HEADS + hd:GDN_HEADS + hd + 1], seg_len, load_state, store_state))
        o_h = _run_tiles(blocks)
    yield

    s1, s2 = _shifted_history(sb, rows, SC_CONV) if long_seq else _shifted_positions(sc_pre, r.hs)
    cws = r.cws[...]
    y_sc = rest[:, R_SCB:R_SCB + SC_WIDTH] * (s2 * cws[0:1, :] + s1 * cws[1:2, :] + sc_pre * cws[2:3, :])
    if long_seq:
        qb[0:SUBLANES, :] = qb[rows:rows + SUBLANES, :]
        sb[0:SUBLANES, :] = sb[rows:rows + SUBLANES, :]
    gnorm = r.gnorm[...]
    o_heads = []
    for hd in range(GDN_HEADS):
        z_h = rest[:, R_Z + hd * HEAD_DIM:R_Z + (hd + 1) * HEAD_DIM]
        o_heads.append(_rmsnorm(o_h[hd], gnorm) * _silu(z_h))
    return jnp.concatenate(o_heads + [y_sc], axis=1)


def _load_by_position(ref):
    n_seq, seg_len, _ = ref.shape
    per_tile = CHUNK // seg_len
    return jnp.concatenate([ref[b:b + per_tile, t, :] for b in range(0, n_seq, per_tile) for t in range(seg_len)],
                           axis=0)


def _store_by_position(ref, rows):
    n_seq, seg_len, _ = ref.shape
    per_tile = CHUNK // seg_len
    for b in range(0, n_seq, per_tile):
        for t in range(seg_len):
            first = b * seg_len + t * per_tile
            ref[b:b + per_tile, t, :] = rows[first:first + per_tile]


class _Refs:
    def __init__(self, **refs):
        self.__dict__.update(refs)


def _mixer_seq_kernel(n_tiles, rows, n_cast, *refs):
    n_in = 9
    x, nmix, w_ab, w_main, cwq, gate, gnorm, cws, wout = refs[:n_in]
    cast_in = refs[n_in:n_in + n_cast]
    xmid, qtail, stail, s_out = refs[n_in + n_cast:n_in + n_cast + 4]
    cast_out = refs[n_in + n_cast + 4:n_in + 2 * n_cast + 4]
    qbuf, sbuf, bd, kbd = refs[n_in + 2 * n_cast + 4:]
    for src, dst in zip(cast_in, cast_out):
        dst[...] = src[...].astype(BF16)

    @pl.when(pl.program_id(1) == 0)
    def _():
        s_out[...] = jnp.zeros(s_out.shape, F32)
        qbuf[:, 0:SUBLANES, :] = jnp.zeros((n_tiles, SUBLANES, QKV_DIM), F32)
        sbuf[:, 0:SUBLANES, :] = jnp.zeros((n_tiles, SUBLANES, SC_WIDTH), F32)
        _fill_block_factors(bd, kbd)

    x_all = x[...].reshape(n_tiles * rows, D_MODEL)
    h = _rmsnorm(x_all, nmix[...]).astype(BF16)
    ab = jnp.dot(h, w_ab[...], preferred_element_type=F32)
    qbuf[:, SUBLANES:SUBLANES + rows, :] = jnp.dot(h, w_main[:, 0:QKV_DIM], preferred_element_type=F32).reshape(
        n_tiles, rows, QKV_DIM)
    rest = jnp.dot(h, w_main[:, QKV_DIM:], preferred_element_type=F32)
    r = _Refs(cwq=cwq, gate=gate, gnorm=gnorm, cws=cws, qtail=qtail, stail=stail, s_out=s_out, qbuf=qbuf, sbuf=sbuf,
              bd=bd, kbd=kbd)
    masks = _packed_masks()
    tile = lambda a, ti: a[ti * rows:(ti + 1) * rows]
    mix = _run_tiles(_mixer_tile(ti, rows, CHUNK, True, masks, r, tile(ab, ti), tile(rest, ti))
                     for ti in range(n_tiles))
    xmid[...] = (x_all + _mm(jnp.concatenate(mix, axis=0), wout[...])).reshape(n_tiles, rows, D_MODEL)


def _mixer_step_kernel(rows, seg_len,
                       ab, proj, hq, hs, s_in, cwq, gate, gnorm, cws,
                       mix, qtail, stail, s_out):
    n_seq = rows // seg_len
    qkv = proj[0, :, 0:QKV_DIM]
    for j in range(GDN_CONV - 1):
        first = (seg_len - (GDN_CONV - 1) + j) * n_seq
        qtail[j] = qkv[first:first + n_seq]
    r = _Refs(qkv=qkv, hq=[hq[j] for j in range(GDN_CONV - 1)], hs=[hs[:, j, :] for j in range(SC_CONV - 1)],
              cwq=cwq, gate=gate, gnorm=gnorm, cws=cws, stail=stail, s_in=s_in, s_out=s_out)
    mix[0] = _run_tiles([_mixer_tile(0, rows, seg_len, False, None, r, ab[0], proj[0, :, QKV_DIM:])])[0].astype(BF16)


def _ffn_body(x, p, nmlp_ref, wup_ref, wdown_ref, nple_ref, wg_ref, wp_ref, nf_ref):
    hn = _rmsnorm(x, nmlp_ref[...]).astype(BF16)
    acc = x
    for j in range(D_FF // D_MODEL):
        u = jnp.maximum(_mm(hn, wup_ref[:, j * D_MODEL:(j + 1) * D_MODEL]), 0.0)
        acc = acc + _mm(u * u, wdown_ref[j * D_MODEL:(j + 1) * D_MODEL, :])
    gate = _sigmoid(_mm(_rmsnorm(acc, nple_ref[...]), wg_ref[...]))
    x3 = acc + gate * _mm(p, wp_ref[...])
    return _rmsnorm(x3, nf_ref[...])


def _ffn_kernel(x_ref, p_ref, *rest):
    rest[-1][...] = _ffn_body(x_ref[...], p_ref[...], *rest[:-1])


def _out_ffn_kernel(x_ref, mix_ref, wout_ref, p_ref, *rest):
    x = _load_by_position(x_ref) + jnp.dot(mix_ref[...], wout_ref[...], preferred_element_type=F32)
    _store_by_position(rest[-1], _ffn_body(x, _load_by_position(p_ref), *rest[:-1]))


def _const_spec(shape):
    nd = len(shape)
    return pl.BlockSpec(shape, lambda *_: (0,) * nd, pipeline_mode=pl.Buffered(1))


def _state_spec(n_tiles, n_state):
    return pl.BlockSpec((n_tiles, n_state, GDN_HEADS, HEAD_DIM, HEAD_DIM), lambda b, i: (b, 0, 0, 0, 0))


def _mixer_seq_call(x3, weights, to_cast, *, n_tiles, rows):
    seqs, seq_rows, _ = x3.shape
    tiles = seq_rows // rows
    n_steps = (seqs // n_tiles) * tiles
    cast_specs = [pl.BlockSpec((w.shape[0] // n_steps, w.shape[1]), lambda b, i: (b * tiles + i, 0)) for w in to_cast]
    row_map = lambda b, i: (b, i, 0)
    seq_map = lambda b, i: (b, 0, 0)
    x_spec = pl.BlockSpec((n_tiles, rows, D_MODEL), row_map)
    hq_spec = pl.BlockSpec((n_tiles, SUBLANES, QKV_DIM), seq_map)
    hs_spec = pl.BlockSpec((n_tiles, SC_CONV - 1, SC_WIDTH), seq_map)
    packed_w = GDN_HEADS * CHUNK
    return pl.pallas_call(
        functools.partial(_mixer_seq_kernel, n_tiles, rows, len(to_cast)),
        grid=(seqs // n_tiles, tiles),
        in_specs=[x_spec] + [_const_spec(w.shape) for w in weights] + cast_specs,
        out_specs=[x_spec, hq_spec, hs_spec, _state_spec(n_tiles, 1)] + cast_specs,
        out_shape=[jax.ShapeDtypeStruct(x3.shape, F32), jax.ShapeDtypeStruct((seqs, SUBLANES, QKV_DIM), F32),
                   jax.ShapeDtypeStruct((seqs, SC_CONV - 1, SC_WIDTH), F32),
                   jax.ShapeDtypeStruct((seqs, 1, GDN_HEADS, HEAD_DIM, HEAD_DIM), F32)]
        + [jax.ShapeDtypeStruct(w.shape, BF16) for w in to_cast],
        scratch_shapes=[pltpu.VMEM((n_tiles, rows + SUBLANES, QKV_DIM), F32),
                        pltpu.VMEM((n_tiles, rows + SUBLANES, SC_WIDTH), F32),
                        pltpu.VMEM((_log2(CHUNK), packed_w, packed_w), BF16),
                        pltpu.VMEM((packed_w, GDN_WIDTH), BF16)],
        compiler_params=pltpu.CompilerParams(
            dimension_semantics=("arbitrary", "arbitrary"), vmem_limit_bytes=VMEM_LIMIT_BYTES),
        name="mixer_seq",
    )(x3, *weights, *to_cast)


def _in_proj_kernel(n_qkv_steps, wt_ref, behind_ref, x_ref, nmix_ref, w_ab_ref, w_main_ref, ab_ref, proj_ref, h_ref):
    i = pl.program_id(0)

    @pl.when(i == 0)
    def _():
        h_ref[...] = _rmsnorm(_load_by_position(x_ref), nmix_ref[...]).astype(BF16)

    on_grid = wt_ref[...]
    behind = behind_ref[...]
    group = jnp.where(i < n_qkv_steps, on_grid, jnp.concatenate([on_grid[SUBLANES:], behind], axis=0))
    for c0 in range(0, IN_PROJ_COLS, SPLIT_COLS):
        w_main_ref[:, c0:c0 + SPLIT_COLS] = group[c0:c0 + SPLIT_COLS].T.astype(BF16)
    proj_ref[...] = jnp.dot(h_ref[...], w_main_ref[...], preferred_element_type=F32)

    @pl.when(i == n_qkv_steps - 1)
    def _():
        w_ab = jnp.concatenate([behind, jnp.zeros((LANES - SUBLANES, D_MODEL), F32)], axis=0).T.astype(BF16)
        w_ab_ref[...] = w_ab
        ab_ref[...] = jnp.dot(h_ref[...], w_ab, preferred_element_type=F32)


def _full_spec(a):
    nd = len(a.shape)
    return pl.BlockSpec(a.shape, lambda i: (0,) * nd)


def _in_proj_call(w_in, x3, nmix):
    rows, cols = w_in.shape
    n_seq, seg_len, _ = x3.shape
    n = n_seq * seg_len
    n_main = cols - 2 * GDN_HEADS
    assert 2 * GDN_HEADS == SUBLANES and QKV_DIM % IN_PROJ_COLS == 0 and n_main % IN_PROJ_COLS == 0
    w_t = w_in.T
    col_block = lambda height: pl.BlockSpec((height, IN_PROJ_COLS), lambda i: (0, i))
    outs = [jax.ShapeDtypeStruct((rows, LANES), BF16), jax.ShapeDtypeStruct((rows, n_main), BF16),
            jax.ShapeDtypeStruct((n, LANES), F32), jax.ShapeDtypeStruct((n, n_main), F32)]
    return pl.pallas_call(
        functools.partial(_in_proj_kernel, QKV_DIM // IN_PROJ_COLS),
        grid=(n_main // IN_PROJ_COLS,),
        in_specs=[pl.BlockSpec((IN_PROJ_COLS, rows), lambda i: (i, 0)),
                  pl.BlockSpec((SUBLANES, rows), lambda i: ((i + 1) * (IN_PROJ_COLS // SUBLANES), 0)),
                  _full_spec(x3), _full_spec(nmix)],
        out_specs=[_full_spec(outs[0]), col_block(rows), _full_spec(outs[2]), col_block(n)],
        out_shape=outs,
        scratch_shapes=[pltpu.VMEM((n, rows), BF16)],
        compiler_params=pltpu.CompilerParams(dimension_semantics=("arbitrary",), vmem_limit_bytes=VMEM_LIMIT_BYTES),
        name="in_proj",
    )(w_t, w_t, x3, nmix)


def _mixer_step_call(ab, proj, hist_q, hist_s, s_in, weights, *, seg_len):
    tiles, rows, _ = proj.shape
    tile_map = lambda b, i: (b, 0, 0)
    slab = lambda a: pl.BlockSpec((1, rows, a.shape[2]), tile_map)
    n_state = s_in.shape[1]
    hist_q_spec = pl.BlockSpec((GDN_CONV - 1, n_state, QKV_DIM), lambda b, i: (0, b, 0))
    hist_s_spec = pl.BlockSpec((n_state, SC_CONV - 1, SC_WIDTH), tile_map)
    mix = jax.ShapeDtypeStruct((tiles, rows, D_MODEL), BF16)
    like = lambda a: jax.ShapeDtypeStruct(a.shape, F32)
    return pl.pallas_call(
        functools.partial(_mixer_step_kernel, rows, seg_len),
        grid=(tiles, 1),
        in_specs=[slab(ab), slab(proj), hist_q_spec, hist_s_spec, _state_spec(1, n_state)]
        + [_const_spec(w.shape) for w in weights],
        out_specs=[slab(mix), hist_q_spec, hist_s_spec, _state_spec(1, n_state)],
        out_shape=[mix, like(hist_q), like(hist_s), like(s_in)],
        compiler_params=pltpu.CompilerParams(
            dimension_semantics=("arbitrary", "arbitrary"), vmem_limit_bytes=VMEM_LIMIT_BYTES),
        name="mixer_step",
    )(ab, proj, hist_q, hist_s, s_in, *weights)


def _ffn_call(x, p, weights, *, name, rows=None, mix=None, wout=None):
    if mix is None:
        steps = x.shape[0] // rows
        rows_spec = lambda width: pl.BlockSpec((rows, width), lambda i: (i, 0))
        x_spec, p_spec, pre_args, pre_specs, body = rows_spec(D_MODEL), rows_spec(PLE_DIM), [], [], _ffn_kernel
    else:
        steps = 1
        x_spec, p_spec, body = _full_spec(x), _full_spec(p), _out_ffn_kernel
        pre_args, pre_specs = [mix, wout], [_full_spec(mix), _const_spec(wout.shape)]
    return pl.pallas_call(
        body,
        grid=(steps,),
        in_specs=[x_spec] + pre_specs + [p_spec] + [_const_spec(w.shape) for w in weights],
        out_specs=x_spec,
        out_shape=jax.ShapeDtypeStruct(x.shape, F32),
        compiler_params=pltpu.CompilerParams(dimension_semantics=("arbitrary",), vmem_limit_bytes=VMEM_LIMIT_BYTES),
        name=name,
    )(x, *pre_args, p, *weights)


def _layer(x_prompt, x_sample, conv_qkv, s_gdn, conv_sc, p_prompt, p_sample, norm_mix, w_in, w_conv_qkv, a_log,
           dt_bias, w_gdn_norm, w_conv_sc, w_out, norm_mlp, w_up, w_down, norm_ple, w_ple_gate, w_ple_proj, norm_f):
    bp, tp, _ = x_prompt.shape
    bs, ts, _ = x_sample.shape
    nmix = norm_mix.reshape(1, D_MODEL)
    w_ab, w_main, ab_s, proj_s = _in_proj_call(w_in, x_sample, nmix)
    wout = w_out.astype(BF16)
    gate = jnp.zeros((SUBLANES, LANES), F32)
    gate = gate.at[0, :GDN_HEADS].set(a_log.astype(F32)).at[1, :GDN_HEADS].set(dt_bias.astype(F32))
    core_w = (w_conv_qkv, gate, w_gdn_norm.reshape(1, HEAD_DIM), w_conv_sc)

    xm_p, qt_p, new_sc_p, s_p, wup, wdown, wgate = _mixer_seq_call(
        x_prompt, (nmix, w_ab, w_main) + core_w + (wout,), (w_up, w_down, w_ple_gate),
        n_tiles=PROMPT_SEQS_PER_STEP, rows=PROMPT_ROWS)
    ffn_w = (norm_mlp.reshape(1, D_MODEL), wup, wdown, norm_ple.reshape(1, D_MODEL), wgate,
             w_ple_proj.astype(BF16), norm_f.reshape(1, D_MODEL))
    y_p = _ffn_call(xm_p.reshape(bp * tp, D_MODEL), p_prompt.reshape(bp * tp, PLE_DIM), ffn_w, rows=FFN_ROWS,
                    name="ffn_prompt")

    seq_per_tile = CHUNK // ts
    tiles = bs // seq_per_tile
    tiled = lambda a: a.reshape(tiles, CHUNK, a.shape[-1])
    mix_s, qt_s, new_sc_s, s_s = _mixer_step_call(
        tiled(ab_s), tiled(proj_s), jnp.transpose(conv_qkv, (1, 0, 2)), conv_sc,
        s_gdn.reshape(tiles, seq_per_tile, GDN_HEADS, HEAD_DIM, HEAD_DIM), core_w, seg_len=ts)
    y_s = _ffn_call(x_sample, p_sample, ffn_w, name="ffn_sample", mix=mix_s.reshape(bs * ts, D_MODEL), wout=wout)

    new_conv_p = qt_p[:, SUBLANES - (GDN_CONV - 1):]
    new_conv_s = jnp.transpose(qt_s, (1, 0, 2))
    return (y_p.reshape(bp, tp, D_MODEL), y_s, new_conv_p,
            s_p.reshape(bp, GDN_HEADS, HEAD_DIM, HEAD_DIM), new_sc_p, new_conv_s,
            s_s.reshape(bs, GDN_HEADS, HEAD_DIM, HEAD_DIM), new_sc_s)


def kernel(x_prompt, x_sample, state_gdn_conv, state_gdn, state_sc_conv, p_prompt, p_sample, norm_mix, w_in, w_conv_qkv, a_log, dt_bias, w_gdn_norm, w_conv_sc, w_out, norm_mlp, w_up, w_down, norm_ple, w_ple_gate, w_ple_proj, norm_f):
    depth = w_in.shape[0]
    assert depth == 1, "one layer per call"
    assert x_sample.shape[1] >= GDN_CONV - 1 and CHUNK % x_sample.shape[1] == 0
    assert x_prompt.shape[1] % PROMPT_ROWS == 0 and x_prompt.shape[0] % PROMPT_SEQS_PER_STEP == 0
    outs = _layer(x_prompt, x_sample, state_gdn_conv[0], state_gdn[0], state_sc_conv[0], p_prompt[0], p_sample[0],
                  norm_mix[0], w_in[0], w_conv_qkv[0], a_log[0], dt_bias[0], w_gdn_norm[0], w_conv_sc[0], w_out[0],
                  norm_mlp[0], w_up[0], w_down[0], norm_ple[0], w_ple_gate[0], w_ple_proj[0], norm_f)
    y_p, y_s, c_p, s_p, sc_p, c_s, s_s, sc_s = outs
    return (y_p, y_s, c_p[None], s_p[None], sc_p[None], c_s[None], s_s[None], sc_s[None])
```

```python
import functools

import jax
import jax.numpy as jnp
from jax import lax
from jax.experimental import pallas as pl
from jax.experimental.pallas import tpu as pltpu

F32 = jnp.float32
BF16 = jnp.bfloat16

D_MODEL = 1024
PLE_DIM = 256
GDN_HEADS = 4
HEAD_DIM = 128
GDN_WIDTH = GDN_HEADS * HEAD_DIM
QKV_DIM = 3 * GDN_WIDTH
GDN_CONV = 4
SC_WIDTH = D_MODEL - GDN_WIDTH
SC_CONV = 3
D_FF = 4 * D_MODEL
EPS = 1e-6
NEG_LOG2_E = -1.4426950408889634
CHUNK = 64
LANES = 128
SUBLANES = 8
VMEM_LIMIT_BYTES = 56 * 1024 * 1024
PROMPT_ROWS = 128
PROMPT_SEQS_PER_STEP = 4
FFN_ROWS = 1024
IN_PROJ_COLS = 512
SPLIT_COLS = 256

R_Z = 0
R_SCB = R_Z + GDN_WIDTH
R_SCC = R_SCB + SC_WIDTH
R_SCH = R_SCC + SC_WIDTH


def _mm(a, b):
    return jnp.dot(a.astype(BF16), b.astype(BF16), preferred_element_type=F32)


def _mm_nt(a, b):
    return lax.dot_general(a.astype(BF16), b.astype(BF16), (((1,), (1,)), ((), ())), preferred_element_type=F32)


def _mm_tn(a, b):
    return lax.dot_general(a.astype(BF16), b.astype(BF16), (((0,), (0,)), ((), ())), preferred_element_type=F32)


def _rmsnorm(x, w_row):
    return x * lax.rsqrt(jnp.mean(x * x, axis=-1, keepdims=True) + EPS) * w_row


def _sigmoid(x):
    return 1.0 / (1.0 + jnp.exp2(x * NEG_LOG2_E))


def _silu(x):
    return x * _sigmoid(x)


def _softplus(x):
    return jnp.maximum(x, 0.0) + jnp.log1p(jnp.exp(-jnp.abs(x)))


def _log2(n):
    k = n.bit_length() - 1
    assert (1 << k) == n, n
    return k


def _wy_block(q, k, v, g_col, g_row, beta_col, seg_len, load_state, store_state):
    c = CHUNK
    n_seg = c // seg_len
    lg = _log2(seg_len)
    ls = _log2(n_seg)
    ri = lax.broadcasted_iota(jnp.int32, (c, c), 0)
    ci = lax.broadcasted_iota(jnp.int32, (c, c), 1)
    same = jnp.bitwise_and(ri, n_seg - 1) == jnp.bitwise_and(ci, n_seg - 1)
    lower = same & (ri >= ci)
    strict = same & (ri > ci)
    diff = g_col - g_row
    decay = jnp.where(lower, jnp.exp(jnp.where(lower, diff, 0.0)), 0.0)
    a_off = jnp.where(strict, beta_col * _mm_nt(k, k) * decay, 0.0)
    qk = _mm_nt(q, k) * decay
    yield
    eye = jnp.where(ri == ci, 1.0, 0.0).astype(F32)
    x_inv = eye
    for lb in range(lg):
        rb = jnp.right_shift(ri, ls + lb)
        cb = jnp.right_shift(ci, ls + lb)
        join = (jnp.bitwise_and(rb, 1) == 1) & (cb == rb - 1)
        b_lvl = jnp.where(join, a_off, 0.0)
        if lb == 0:
            x_inv = eye - b_lvl
        else:
            xb = _mm(x_inv, b_lvl)
            yield
            x_inv = x_inv - _mm(xb, x_inv)
            yield
    rhs = jnp.concatenate([v * beta_col, k * (beta_col * jnp.exp(g_col))], axis=1)
    sol = _mm(x_inv, rhs)
    yield
    u = sol[:, :HEAD_DIM]
    w = sol[:, HEAD_DIM:]
    stacked = jnp.concatenate([w, q * jnp.exp(g_col)], axis=0)
    row = lax.broadcasted_iota(jnp.int32, (c, 1), 0)
    states = [load_state(s) for s in range(n_seg)]
    w_s = None
    q_s = None
    for s in range(n_seg):
        r = _mm(stacked, states[s])
        if n_seg == 1:
            w_s, q_s = r[:c], r[c:]
        else:
            in_seg = jnp.bitwise_and(row, n_seg - 1) == s
            w_s = jnp.where(in_seg, r[:c], 0.0 if w_s is None else w_s)
            q_s = jnp.where(in_seg, r[c:], 0.0 if q_s is None else q_s)
    v_new = u - w_s
    yield
    o = q_s + _mm(qk, v_new)
    for s in range(n_seg):
        last = (seg_len - 1) * n_seg + s
        g_last = g_col[last:last + 1, :]
        if n_seg == 1:
            k_dec = k * jnp.exp(g_last - g_col)
        else:
            in_seg = jnp.bitwise_and(row, n_seg - 1) == s
            k_dec = jnp.where(in_seg, k * jnp.exp(jnp.where(in_seg, g_last - g_col, 0.0)), 0.0)
        store_state(s, states[s] * jnp.exp(g_last) + _mm_tn(k_dec, v_new))
    return o


def _expand_heads(slab, first_lane, width):
    rows = slab.shape[0]
    cols = [slab[:, first_lane + hd:first_lane + hd + 1] for hd in range(GDN_HEADS)]
    if width % LANES == 0:
        return jnp.concatenate([jnp.broadcast_to(c, (rows, width)) for c in cols], axis=1)
    total = GDN_HEADS * width
    lane_head = jnp.right_shift(lax.broadcasted_iota(jnp.int32, (rows, total), 1), _log2(width))
    out = jnp.broadcast_to(cols[-1], (rows, total))
    for hd in range(GDN_HEADS - 2, -1, -1):
        out = jnp.where(lane_head == hd, jnp.broadcast_to(cols[hd], (rows, total)), out)
    return out


def _packed_masks():
    c, nh = CHUNK, GDN_HEADS
    ri = lax.broadcasted_iota(jnp.int32, (c, nh * c), 0)
    cj = jnp.bitwise_and(lax.broadcasted_iota(jnp.int32, (c, nh * c), 1), c - 1)
    return dict(lower=ri >= cj, strict=ri > cj, eye=ri == cj, join0=(jnp.bitwise_and(ri, 1) == 1) & (cj == ri - 1))


def _fill_block_factors(bd_ref, kbd_ref):
    c, nh = CHUNK, GDN_HEADS
    w = nh * c
    lg = _log2(c)
    rw = lax.broadcasted_iota(jnp.int32, (w, w), 0)
    cw = lax.broadcasted_iota(jnp.int32, (w, w), 1)
    same_head = jnp.right_shift(rw, lg) == jnp.right_shift(cw, lg)
    rw, cw = jnp.bitwise_and(rw, c - 1), jnp.bitwise_and(cw, c - 1)
    one_zero = lambda m: jnp.where(m, 1.0, 0.0).astype(BF16)
    bd_ref[0] = one_zero(same_head)
    for lb in range(1, lg):
        bd_ref[lb] = one_zero(same_head & (jnp.bitwise_and(jnp.right_shift(rw, lb), 1) == 1)
                              & (jnp.right_shift(cw, lb) == jnp.right_shift(rw, lb) - 1))
    kbd_ref[...] = one_zero(jnp.right_shift(lax.broadcasted_iota(jnp.int32, (w, nh * HEAD_DIM), 0), lg)
                            == jnp.right_shift(lax.broadcasted_iota(jnp.int32, (w, nh * HEAD_DIM), 1),
                                               _log2(HEAD_DIM)))


def _wy_tile_packed(q_all, k_all, v_all, g, beta, state_ref, rows, masks, bd_ref, kbd_ref):
    c, nh = CHUNK, GDN_HEADS
    lg = _log2(c)
    n_chunks = rows // c
    lower, strict, eye = masks["lower"], masks["strict"], masks["eye"]

    def block_diag(x, factor):
        return jnp.concatenate([x.astype(BF16)] * nh, axis=0) * factor

    def head(x, hd, width):
        return x[:, hd * width:(hd + 1) * width]

    g_c = _expand_heads(g, 0, c)
    b_c = _expand_heads(beta, nh, c)
    g_d = _expand_heads(g, 0, HEAD_DIM)
    b_d = _expand_heads(beta, nh, HEAD_DIM)
    eg_d = jnp.exp(g_d)
    v_rhs = v_all * b_d
    k_rhs = k_all * (b_d * eg_d)
    q_g = q_all * eg_d

    a_off, qk_dec = [], []
    for cb in range(n_chunks):
        rs = slice(cb * c, (cb + 1) * c)
        g_row = jnp.sum(jnp.where(eye, g_c[rs], 0.0), axis=0, keepdims=True)
        decay = jnp.where(lower, jnp.exp(jnp.where(lower, g_c[rs] - g_row, 0.0)), 0.0)
        kq = _mm_nt(jnp.concatenate([k_all[rs], q_all[rs]], axis=0), block_diag(k_all[rs], kbd_ref[...]))
        a_off.append(jnp.where(strict, b_c[rs] * kq[:c] * decay, 0.0))
        qk_dec.append(kq[c:] * decay)
    yield

    eye_f = jnp.where(eye, 1.0, 0.0).astype(F32)
    x_inv = [eye_f - jnp.where(masks["join0"], a, 0.0) for a in a_off]
    for lb in range(1, lg):
        xb = [jnp.dot(x.astype(BF16), block_diag(a, bd_ref[lb]), preferred_element_type=F32)
              for x, a in zip(x_inv, a_off)]
        yield
        xbx = [jnp.dot(t.astype(BF16), block_diag(x, bd_ref[0]), preferred_element_type=F32)
               for t, x in zip(xb, x_inv)]
        x_inv = [x - t for x, t in zip(x_inv, xbx)]
        yield

    u, wk = [], []
    for cb in range(n_chunks):
        rs = slice(cb * c, (cb + 1) * c)
        sol = [_mm(head(x_inv[cb], hd, c),
                   jnp.concatenate([head(v_rhs[rs], hd, HEAD_DIM), head(k_rhs[rs], hd, HEAD_DIM)], axis=1))
               for hd in range(nh)]
        u.append([s[:, :HEAD_DIM] for s in sol])
        wk.append([s[:, HEAD_DIM:] for s in sol])
    yield

    state = [state_ref[0, hd] for hd in range(nh)]
    o_blocks = []
    for cb in range(n_chunks):
        rs = slice(cb * c, (cb + 1) * c)
        g_last = g_d[(cb + 1) * c - 1:(cb + 1) * c, :]
        k_dec = k_all[rs] * jnp.exp(g_last - g_d[rs])
        eg_last = jnp.exp(g_last)
        r = [_mm(jnp.concatenate([wk[cb][hd], head(q_g[rs], hd, HEAD_DIM)], axis=0), state[hd]) for hd in range(nh)]
        v_new = [u[cb][hd] - r[hd][:c] for hd in range(nh)]
        yield
        o_blocks.append(jnp.concatenate(
            [r[hd][c:] + _mm(head(qk_dec[cb], hd, c), v_new[hd]) for hd in range(nh)], axis=1))
        state = [state[hd] * head(eg_last, hd, HEAD_DIM) + _mm_tn(head(k_dec, hd, HEAD_DIM), v_new[hd])
                 for hd in range(nh)]
        yield
    for hd in range(nh):
        state_ref[0, hd] = state[hd]
    return jnp.concatenate(o_blocks, axis=0)


def _run_tiles(programs):
    programs = list(programs)
    results = [None] * len(programs)
    live = list(range(len(programs)))
    while live:
        for idx in list(live):
            try:
                next(programs[idx])
            except StopIteration as done:
                results[idx] = done.value
                live.remove(idx)
    return results


def _shifted_history(buf, rows, n_taps):
    with_carry = buf[0:SUBLANES + rows, :]
    return [pltpu.roll(with_carry, s, 0)[SUBLANES:, :] for s in range(1, n_taps)]


def _shifted_positions(x_rows, taps):
    n_seq = taps[0].shape[0]
    rows = x_rows.shape[0]
    ext = jnp.concatenate(list(taps) + [x_rows], axis=0)
    return [ext[(len(taps) - s) * n_seq:(len(taps) - s) * n_seq + rows] for s in range(1, len(taps) + 1)]


def _mixer_tile(ti, rows, seg_len, long_seq, masks, r, ab, rest):
    n_seq = rows // seg_len
    new_rows = slice(SUBLANES, SUBLANES + rows)
    row = lax.broadcasted_iota(jnp.int32, (rows, 1), 0)
    if long_seq:
        qb, sb = r.qbuf.at[ti], r.sbuf.at[ti]
        pos, row_step = jnp.bitwise_and(row, seg_len - 1), 1
    else:
        assert rows == CHUNK
        pos, row_step = jnp.right_shift(row, _log2(n_seq)), n_seq

    gate = r.gate[...]
    log_a = -jnp.exp(gate[0:1, :]) * _softplus(ab + gate[1:2, :])
    beta = _sigmoid(ab)
    g = log_a
    shift = 1
    while shift < seg_len:
        g = g + jnp.where(pos >= shift, pltpu.roll(g, shift * row_step, 0), 0.0)
        shift *= 2

    sc_pre = rest[:, R_SCC:R_SCC + SC_WIDTH] * rest[:, R_SCH:R_SCH + SC_WIDTH]
    if long_seq:
        sb[new_rows, :] = sc_pre
        qkv_pre = qb[new_rows, :]
        cwq = r.cwq[...]
        with_carry = qb[0:SUBLANES + rows, :]
        back1 = pltpu.roll(with_carry, 1, 0)
        pair = with_carry * cwq[1:2, :] + back1 * cwq[0:1, :]
        qkv = _silu(pltpu.roll(pair, 2, 0)[SUBLANES:, :] + back1[SUBLANES:, :] * cwq[2:3, :]
                    + qkv_pre * cwq[3:4, :])
        r.qtail[ti] = qb[rows:rows + SUBLANES, :]
        r.stail[ti] = sb[rows + SUBLANES - (SC_CONV - 1):rows + SUBLANES, :]
    else:
        qkv_pre = r.qkv
        cwq = r.cwq[...]
        q1, q2, q3 = _shifted_positions(qkv_pre, r.hq)
        qkv = _silu(q3 * cwq[0:1, :] + q2 * cwq[1:2, :] + q1 * cwq[2:3, :] + qkv_pre * cwq[3:4, :])
        for j in range(SC_CONV - 1):
            first = (seg_len - (SC_CONV - 1) + j) * n_seq
            r.stail[:, j, :] = sc_pre[first:first + n_seq]
    q_n, k_n, v_n = [], [], []
    for hd in range(GDN_HEADS):
        lo = hd * HEAD_DIM
        q_h = qkv[:, lo:lo + HEAD_DIM]
        k_h = qkv[:, GDN_WIDTH + lo:GDN_WIDTH + lo + HEAD_DIM]
        q_n.append(q_h * (lax.rsqrt(jnp.sum(q_h * q_h, axis=-1, keepdims=True) + EPS) * (HEAD_DIM ** -0.5)))
        k_n.append(k_h * lax.rsqrt(jnp.sum(k_h * k_h, axis=-1, keepdims=True) + EPS))
        v_n.append(qkv[:, 2 * GDN_WIDTH + lo:2 * GDN_WIDTH + lo + HEAD_DIM])
    yield

    state_ref = r.s_out.at[ti]
    if long_seq:
        o_all = yield from _wy_tile_packed(jnp.concatenate(q_n, axis=1), jnp.concatenate(k_n, axis=1),
                                           jnp.concatenate(v_n, axis=1), g, beta, state_ref, rows, masks,
                                           r.bd, r.kbd)
        o_h = [o_all[:, hd * HEAD_DIM:(hd + 1) * HEAD_DIM] for hd in range(GDN_HEADS)]
    else:
        g_t = g.T
        blocks = []
        for hd in range(GDN_HEADS):

            def load_state(s, hd=hd):
                return r.s_in[ti, s, hd]

            def store_state(s, val, hd=hd):
                state_ref[s, hd] = val

            blocks.append(_wy_block(q_n[hd], k_n[hd], v_n[hd], g[:, hd:hd + 1], g_t[hd:hd + 1, :],
                                    beta[:, GDN_HEADS + hd:GDN_HEADS + hd + 1], seg_len, load_state, store_state))
        o_h = _run_tiles(blocks)
    yield

    s1, s2 = _shifted_history(sb, rows, SC_CONV) if long_seq else _shifted_positions(sc_pre, r.hs)
    cws = r.cws[...]
    y_sc = rest[:, R_SCB:R_SCB + SC_WIDTH] * (s2 * cws[0:1, :] + s1 * cws[1:2, :] + sc_pre * cws[2:3, :])
    if long_seq:
        qb[0:SUBLANES, :] = qb[rows:rows + SUBLANES, :]
        sb[0:SUBLANES, :] = sb[rows:rows + SUBLANES, :]
    gnorm = r.gnorm[...]
    o_heads = []
    for hd in range(GDN_HEADS):
        z_h = rest[:, R_Z + hd * HEAD_DIM:R_Z + (hd + 1) * HEAD_DIM]
        o_heads.append(_rmsnorm(o_h[hd], gnorm) * _silu(z_h))
    return jnp.concatenate(o_heads + [y_sc], axis=1)


def _load_by_position(ref):
    n_seq, seg_len, _ = ref.shape
    per_tile = CHUNK // seg_len
    return jnp.concatenate([ref[b:b + per_tile, t, :] for b in range(0, n_seq, per_tile) for t in range(seg_len)],
                           axis=0)


def _store_by_position(ref, rows):
    n_seq, seg_len, _ = ref.shape
    per_tile = CHUNK // seg_len
    for b in range(0, n_seq, per_tile):
        for t in range(seg_len):
            first = b * seg_len + t * per_tile
            ref[b:b + per_tile, t, :] = rows[first:first + per_tile]


class _Refs:
    def __init__(self, **refs):
        self.__dict__.update(refs)


def _mixer_seq_kernel(n_tiles, rows, n_cast, *refs):
    n_in = 9
    x, nmix, w_ab, w_main, cwq, gate, gnorm, cws, wout = refs[:n_in]
    cast_in = refs[n_in:n_in + n_cast]
    xmid, qtail, stail, s_out = refs[n_in + n_cast:n_in + n_cast + 4]
    cast_out = refs[n_in + n_cast + 4:n_in + 2 * n_cast + 4]
    qbuf, sbuf, bd, kbd = refs[n_in + 2 * n_cast + 4:]
    for src, dst in zip(cast_in, cast_out):
        dst[...] = src[...].astype(BF16)

    @pl.when(pl.program_id(1) == 0)
    def _():
        s_out[...] = jnp.zeros(s_out.shape, F32)
        qbuf[:, 0:SUBLANES, :] = jnp.zeros((n_tiles, SUBLANES, QKV_DIM), F32)
        sbuf[:, 0:SUBLANES, :] = jnp.zeros((n_tiles, SUBLANES, SC_WIDTH), F32)
        _fill_block_factors(bd, kbd)

    x_all = x[...].reshape(n_tiles * rows, D_MODEL)
    h = _rmsnorm(x_all, nmix[...]).astype(BF16)
    ab = jnp.dot(h, w_ab[...], preferred_element_type=F32)
    qbuf[:, SUBLANES:SUBLANES + rows, :] = jnp.dot(h, w_main[:, 0:QKV_DIM], preferred_element_type=F32).reshape(
        n_tiles, rows, QKV_DIM)
    rest = jnp.dot(h, w_main[:, QKV_DIM:], preferred_element_type=F32)
    r = _Refs(cwq=cwq, gate=gate, gnorm=gnorm, cws=cws, qtail=qtail, stail=stail, s_out=s_out, qbuf=qbuf, sbuf=sbuf,
              bd=bd, kbd=kbd)
    masks = _packed_masks()
    tile = lambda a, ti: a[ti * rows:(ti + 1) * rows]
    mix = _run_tiles(_mixer_tile(ti, rows, CHUNK, True, masks, r, tile(ab, ti), tile(rest, ti))
                     for ti in range(n_tiles))
    xmid[...] = (x_all + _mm(jnp.concatenate(mix, axis=0), wout[...])).reshape(n_tiles, rows, D_MODEL)


def _mixer_step_kernel(rows, seg_len,
                       ab, proj, hq, hs, s_in, cwq, gate, gnorm, cws,
                       mix, qtail, stail, s_out):
    n_seq = rows // seg_len
    qkv = proj[0, :, 0:QKV_DIM]
    for j in range(GDN_CONV - 1):
        first = (seg_len - (GDN_CONV - 1) + j) * n_seq
        qtail[j] = qkv[first:first + n_seq]
    r = _Refs(qkv=qkv, hq=[hq[j] for j in range(GDN_CONV - 1)], hs=[hs[:, j, :] for j in range(SC_CONV - 1)],
              cwq=cwq, gate=gate, gnorm=gnorm, cws=cws, stail=stail, s_in=s_in, s_out=s_out)
    mix[0] = _run_tiles([_mixer_tile(0, rows, seg_len, False, None, r, ab[0], proj[0, :, QKV_DIM:])])[0].astype(BF16)


def _ffn_body(x, p, nmlp_ref, wup_ref, wdown_ref, nple_ref, wg_ref, wp_ref, nf_ref, before_chunk=None):
    hn = _rmsnorm(x, nmlp_ref[...]).astype(BF16)
    acc = x
    for j in range(D_FF // D_MODEL):
        if before_chunk is not None:
            before_chunk(j)
        u = jnp.maximum(_mm(hn, wup_ref[:, j * D_MODEL:(j + 1) * D_MODEL]), 0.0)
        acc = acc + _mm(u * u, wdown_ref[j * D_MODEL:(j + 1) * D_MODEL, :])
    gate = _sigmoid(_mm(_rmsnorm(acc, nple_ref[...]), wg_ref[...]))
    x3 = acc + gate * _mm(p, wp_ref[...])
    return _rmsnorm(x3, nf_ref[...])


def _ffn_kernel(x_ref, p_ref, *rest):
    rest[-1][...] = _ffn_body(x_ref[...], p_ref[...], *rest[:-1])


def _out_ffn_kernel(x_ref, mix_ref, wout_ref, p_ref, nmlp_ref, wup_hbm, wdown_hbm, nple_ref, wg_ref, wp_ref, nf_ref,
                    y_ref, wup_buf, wdown_buf, sems):
    copies = []
    for j in range(D_FF // D_MODEL):
        part = slice(j * D_MODEL, (j + 1) * D_MODEL)
        copies.append((pltpu.make_async_copy(wup_hbm.at[:, part], wup_buf.at[:, part], sems.at[0, j]),
                       pltpu.make_async_copy(wdown_hbm.at[part, :], wdown_buf.at[part, :], sems.at[1, j])))
    for up, down in copies:
        up.start()
        down.start()

    def wait_slice(j):
        copies[j][0].wait()
        copies[j][1].wait()

    x = _load_by_position(x_ref) + jnp.dot(mix_ref[...], wout_ref[...], preferred_element_type=F32)
    _store_by_position(y_ref, _ffn_body(x, _load_by_position(p_ref), nmlp_ref, wup_buf, wdown_buf, nple_ref, wg_ref,
                                        wp_ref, nf_ref, before_chunk=wait_slice))


def _const_spec(shape):
    nd = len(shape)
    return pl.BlockSpec(shape, lambda *_: (0,) * nd, pipeline_mode=pl.Buffered(1))


def _state_spec(n_tiles, n_state):
    return pl.BlockSpec((n_tiles, n_state, GDN_HEADS, HEAD_DIM, HEAD_DIM), lambda b, i: (b, 0, 0, 0, 0))


def _mixer_seq_call(x3, weights, to_cast, *, n_tiles, rows):
    seqs, seq_rows, _ = x3.shape
    tiles = seq_rows // rows
    n_steps = (seqs // n_tiles) * tiles
    cast_specs = [pl.BlockSpec((w.shape[0] // n_steps, w.shape[1]), lambda b, i: (b * tiles + i, 0)) for w in to_cast]
    row_map = lambda b, i: (b, i, 0)
    seq_map = lambda b, i: (b, 0, 0)
    x_spec = pl.BlockSpec((n_tiles, rows, D_MODEL), row_map)
    hq_spec = pl.BlockSpec((n_tiles, SUBLANES, QKV_DIM), seq_map)
    hs_spec = pl.BlockSpec((n_tiles, SC_CONV - 1, SC_WIDTH), seq_map)
    packed_w = GDN_HEADS * CHUNK
    return pl.pallas_call(
        functools.partial(_mixer_seq_kernel, n_tiles, rows, len(to_cast)),
        grid=(seqs // n_tiles, tiles),
        in_specs=[x_spec] + [_const_spec(w.shape) for w in weights] + cast_specs,
        out_specs=[x_spec, hq_spec, hs_spec, _state_spec(n_tiles, 1)] + cast_specs,
        out_shape=[jax.ShapeDtypeStruct(x3.shape, F32), jax.ShapeDtypeStruct((seqs, SUBLANES, QKV_DIM), F32),
                   jax.ShapeDtypeStruct((seqs, SC_CONV - 1, SC_WIDTH), F32),
                   jax.ShapeDtypeStruct((seqs, 1, GDN_HEADS, HEAD_DIM, HEAD_DIM), F32)]
        + [jax.ShapeDtypeStruct(w.shape, BF16) for w in to_cast],
        scratch_shapes=[pltpu.VMEM((n_tiles, rows + SUBLANES, QKV_DIM), F32),
                        pltpu.VMEM((n_tiles, rows + SUBLANES, SC_WIDTH), F32),
                        pltpu.VMEM((_log2(CHUNK), packed_w, packed_w), BF16),
                        pltpu.VMEM((packed_w, GDN_WIDTH), BF16)],
        compiler_params=pltpu.CompilerParams(
            dimension_semantics=("arbitrary", "arbitrary"), vmem_limit_bytes=VMEM_LIMIT_BYTES),
        name="mixer_seq",
    )(x3, *weights, *to_cast)


def _in_proj_kernel(n_qkv_steps, wt_ref, behind_ref, x_ref, nmix_ref, w_ab_ref, w_main_ref, ab_ref, proj_ref, h_ref):
    i = pl.program_id(0)

    @pl.when(i == 0)
    def _():
        h_ref[...] = _rmsnorm(_load_by_position(x_ref), nmix_ref[...]).astype(BF16)

    on_grid = wt_ref[...]
    behind = behind_ref[...]
    group = jnp.where(i < n_qkv_steps, on_grid, jnp.concatenate([on_grid[SUBLANES:], behind], axis=0))
    for c0 in range(0, IN_PROJ_COLS, SPLIT_COLS):
        w_main_ref[:, c0:c0 + SPLIT_COLS] = group[c0:c0 + SPLIT_COLS].T.astype(BF16)
    proj_ref[...] = jnp.dot(h_ref[...], w_main_ref[...], preferred_element_type=F32)

    @pl.when(i == n_qkv_steps - 1)
    def _():
        w_ab = jnp.concatenate([behind, jnp.zeros((LANES - SUBLANES, D_MODEL), F32)], axis=0).T.astype(BF16)
        w_ab_ref[...] = w_ab
        ab_ref[...] = jnp.dot(h_ref[...], w_ab, preferred_element_type=F32)


def _full_spec(a):
    nd = len(a.shape)
    return pl.BlockSpec(a.shape, lambda i: (0,) * nd)


def _in_proj_call(w_in, x3, nmix):
    rows, cols = w_in.shape
    n_seq, seg_len, _ = x3.shape
    n = n_seq * seg_len
    n_main = cols - 2 * GDN_HEADS
    assert 2 * GDN_HEADS == SUBLANES and QKV_DIM % IN_PROJ_COLS == 0 and n_main % IN_PROJ_COLS == 0
    w_t = w_in.T
    col_block = lambda height: pl.BlockSpec((height, IN_PROJ_COLS), lambda i: (0, i))
    outs = [jax.ShapeDtypeStruct((rows, LANES), BF16), jax.ShapeDtypeStruct((rows, n_main), BF16),
            jax.ShapeDtypeStruct((n, LANES), F32), jax.ShapeDtypeStruct((n, n_main), F32)]
    return pl.pallas_call(
        functools.partial(_in_proj_kernel, QKV_DIM // IN_PROJ_COLS),
        grid=(n_main // IN_PROJ_COLS,),
        in_specs=[pl.BlockSpec((IN_PROJ_COLS, rows), lambda i: (i, 0)),
                  pl.BlockSpec((SUBLANES, rows), lambda i: ((i + 1) * (IN_PROJ_COLS // SUBLANES), 0)),
                  _full_spec(x3), _full_spec(nmix)],
        out_specs=[_full_spec(outs[0]), col_block(rows), _full_spec(outs[2]), col_block(n)],
        out_shape=outs,
        scratch_shapes=[pltpu.VMEM((n, rows), BF16)],
        compiler_params=pltpu.CompilerParams(dimension_semantics=("arbitrary",), vmem_limit_bytes=VMEM_LIMIT_BYTES),
        name="in_proj",
    )(w_t, w_t, x3, nmix)


def _mixer_step_call(ab, proj, hist_q, hist_s, s_in, weights, *, seg_len):
    tiles, rows, _ = proj.shape
    tile_map = lambda b, i: (b, 0, 0)
    slab = lambda a: pl.BlockSpec((1, rows, a.shape[2]), tile_map)
    n_state = s_in.shape[1]
    hist_q_spec = pl.BlockSpec((GDN_CONV - 1, n_state, QKV_DIM), lambda b, i: (0, b, 0))
    hist_s_spec = pl.BlockSpec((n_state, SC_CONV - 1, SC_WIDTH), tile_map)
    mix = jax.ShapeDtypeStruct((tiles, rows, D_MODEL), BF16)
    like = lambda a: jax.ShapeDtypeStruct(a.shape, F32)
    return pl.pallas_call(
        functools.partial(_mixer_step_kernel, rows, seg_len),
        grid=(tiles, 1),
        in_specs=[slab(ab), slab(proj), hist_q_spec, hist_s_spec, _state_spec(1, n_state)]
        + [_const_spec(w.shape) for w in weights],
        out_specs=[slab(mix), hist_q_spec, hist_s_spec, _state_spec(1, n_state)],
        out_shape=[mix, like(hist_q), like(hist_s), like(s_in)],
        compiler_params=pltpu.CompilerParams(
            dimension_semantics=("arbitrary", "arbitrary"), vmem_limit_bytes=VMEM_LIMIT_BYTES),
        name="mixer_step",
    )(ab, proj, hist_q, hist_s, s_in, *weights)


def _ffn_call(x, p, weights, *, name, rows=None, mix=None, wout=None):
    w_specs = [_const_spec(w.shape) for w in weights]
    if mix is None:
        steps = x.shape[0] // rows
        rows_spec = lambda width: pl.BlockSpec((rows, width), lambda i: (i, 0))
        x_spec, p_spec, pre_args, pre_specs, body = rows_spec(D_MODEL), rows_spec(PLE_DIM), [], [], _ffn_kernel
        scratch = []
    else:
        steps = 1
        x_spec, p_spec, body = _full_spec(x), _full_spec(p), _out_ffn_kernel
        pre_args, pre_specs = [mix, wout], [_full_spec(mix), _const_spec(wout.shape)]
        wup, wdown = weights[1], weights[2]
        w_specs[1] = w_specs[2] = pl.BlockSpec(memory_space=pl.ANY)
        scratch = [pltpu.VMEM(wup.shape, wup.dtype), pltpu.VMEM(wdown.shape, wdown.dtype),
                   pltpu.SemaphoreType.DMA((2, D_FF // D_MODEL))]
    return pl.pallas_call(
        body,
        grid=(steps,),
        in_specs=[x_spec] + pre_specs + [p_spec] + w_specs,
        out_specs=x_spec,
        out_shape=jax.ShapeDtypeStruct(x.shape, F32),
        scratch_shapes=scratch,
        compiler_params=pltpu.CompilerParams(dimension_semantics=("arbitrary",), vmem_limit_bytes=VMEM_LIMIT_BYTES),
        name=name,
    )(x, *pre_args, p, *weights)


def _layer(x_prompt, x_sample, conv_qkv, s_gdn, conv_sc, p_prompt, p_sample, norm_mix, w_in, w_conv_qkv, a_log,
           dt_bias, w_gdn_norm, w_conv_sc, w_out, norm_mlp, w_up, w_down, norm_ple, w_ple_gate, w_ple_proj, norm_f):
    bp, tp, _ = x_prompt.shape
    bs, ts, _ = x_sample.shape
    nmix = norm_mix.reshape(1, D_MODEL)
    w_ab, w_main, ab_s, proj_s = _in_proj_call(w_in, x_sample, nmix)
    wout = w_out.astype(BF16)
    gate = jnp.zeros((SUBLANES, LANES), F32)
    gate = gate.at[0, :GDN_HEADS].set(a_log.astype(F32)).at[1, :GDN_HEADS].set(dt_bias.astype(F32))
    core_w = (w_conv_qkv, gate, w_gdn_norm.reshape(1, HEAD_DIM), w_conv_sc)

    xm_p, qt_p, new_sc_p, s_p, wup, wdown, wgate = _mixer_seq_call(
        x_prompt, (nmix, w_ab, w_main) + core_w + (wout,), (w_up, w_down, w_ple_gate),
        n_tiles=PROMPT_SEQS_PER_STEP, rows=PROMPT_ROWS)
    ffn_w = (norm_mlp.reshape(1, D_MODEL), wup, wdown, norm_ple.reshape(1, D_MODEL), wgate,
             w_ple_proj.astype(BF16), norm_f.reshape(1, D_MODEL))
    y_p = _ffn_call(xm_p.reshape(bp * tp, D_MODEL), p_prompt.reshape(bp * tp, PLE_DIM), ffn_w, rows=FFN_ROWS,
                    name="ffn_prompt")

    seq_per_tile = CHUNK // ts
    tiles = bs // seq_per_tile
    tiled = lambda a: a.reshape(tiles, CHUNK, a.shape[-1])
    mix_s, qt_s, new_sc_s, s_s = _mixer_step_call(
        tiled(ab_s), tiled(proj_s), jnp.transpose(conv_qkv, (1, 0, 2)), conv_sc,
        s_gdn.reshape(tiles, seq_per_tile, GDN_HEADS, HEAD_DIM, HEAD_DIM), core_w, seg_len=ts)
    y_s = _ffn_call(x_sample, p_sample, ffn_w, name="ffn_sample", mix=mix_s.reshape(bs * ts, D_MODEL), wout=wout)

    new_conv_p = qt_p[:, SUBLANES - (GDN_CONV - 1):]
    new_conv_s = jnp.transpose(qt_s, (1, 0, 2))
    return (y_p.reshape(bp, tp, D_MODEL), y_s, new_conv_p,
            s_p.reshape(bp, GDN_HEADS, HEAD_DIM, HEAD_DIM), new_sc_p, new_conv_s,
            s_s.reshape(bs, GDN_HEADS, HEAD_DIM, HEAD_DIM), new_sc_s)


def kernel(x_prompt, x_sample, state_gdn_conv, state_gdn, state_sc_conv, p_prompt, p_sample, norm_mix, w_in, w_conv_qkv, a_log, dt_bias, w_gdn_norm, w_conv_sc, w_out, norm_mlp, w_up, w_down, norm_ple, w_ple_gate, w_ple_proj, norm_f):
    depth = w_in.shape[0]
    assert depth == 1, "one layer per call"
    assert x_sample.shape[1] >= GDN_CONV - 1 and CHUNK % x_sample.shape[1] == 0
    assert x_prompt.shape[1] % PROMPT_ROWS == 0 and x_prompt.shape[0] % PROMPT_SEQS_PER_STEP == 0
    outs = _layer(x_prompt, x_sample, state_gdn_conv[0], state_gdn[0], state_sc_conv[0], p_prompt[0], p_sample[0],
                  norm_mix[0], w_in[0], w_conv_qkv[0], a_log[0], dt_bias[0], w_gdn_norm[0], w_conv_sc[0], w_out[0],
                  norm_mlp[0], w_up[0], w_down[0], norm_ple[0], w_ple_gate[0], w_ple_proj[0], norm_f)
    y_p, y_s, c_p, s_p, sc_p, c_s, s_s, sc_s = outs
    return (y_p, y_s, c_p[None], s_p[None], sc_p[None], c_s[None], s_s[None], sc_s[None])
```

```python
import functools

import jax
import jax.numpy as jnp
from jax import lax
from jax.experimental import pallas as pl
from jax.experimental.pallas import tpu as pltpu

F32 = jnp.float32
BF16 = jnp.bfloat16

D_MODEL = 1024
PLE_DIM = 256
GDN_HEADS = 4
HEAD_DIM = 128
GDN_WIDTH = GDN_HEADS * HEAD_DIM
QKV_DIM = 3 * GDN_WIDTH
GDN_CONV = 4
SC_WIDTH = D_MODEL - GDN_WIDTH
SC_CONV = 3
D_FF = 4 * D_MODEL
EPS = 1e-6
NEG_LOG2_E = -1.4426950408889634
CHUNK = 64
LANES = 128
SUBLANES = 8
VMEM_LIMIT_BYTES = 56 * 1024 * 1024
PROMPT_ROWS = 128
PROMPT_SEQS_PER_STEP = 4
FFN_ROWS = 1024
IN_PROJ_COLS = 512
SPLIT_COLS = 256

R_Z = 0
R_SCB = R_Z + GDN_WIDTH
R_SCC = R_SCB + SC_WIDTH
R_SCH = R_SCC + SC_WIDTH


def _mm(a, b):
    return jnp.dot(a.astype(BF16), b.astype(BF16), preferred_element_type=F32)


def _mm_nt(a, b):
    return lax.dot_general(a.astype(BF16), b.astype(BF16), (((1,), (1,)), ((), ())), preferred_element_type=F32)


def _mm_tn(a, b):
    return lax.dot_general(a.astype(BF16), b.astype(BF16), (((0,), (0,)), ((), ())), preferred_element_type=F32)


def _rmsnorm(x, w_row):
    return x * lax.rsqrt(jnp.mean(x * x, axis=-1, keepdims=True) + EPS) * w_row


def _sigmoid(x):
    return 1.0 / (1.0 + jnp.exp2(x * NEG_LOG2_E))


def _silu(x):
    return x * _sigmoid(x)


def _softplus(x):
    return jnp.maximum(x, 0.0) + jnp.log1p(jnp.exp(-jnp.abs(x)))


def _log2(n):
    k = n.bit_length() - 1
    assert (1 << k) == n, n
    return k


def _wy_block(q, k, v, g_col, g_row, beta_col, seg_len, load_state, store_state):
    c = CHUNK
    n_seg = c // seg_len
    lg = _log2(seg_len)
    ls = _log2(n_seg)
    ri = lax.broadcasted_iota(jnp.int32, (c, c), 0)
    ci = lax.broadcasted_iota(jnp.int32, (c, c), 1)
    same = jnp.bitwise_and(ri, n_seg - 1) == jnp.bitwise_and(ci, n_seg - 1)
    lower = same & (ri >= ci)
    strict = same & (ri > ci)
    diff = g_col - g_row
    decay = jnp.where(lower, jnp.exp(jnp.where(lower, diff, 0.0)), 0.0)
    a_off = jnp.where(strict, beta_col * _mm_nt(k, k) * decay, 0.0)
    qk = _mm_nt(q, k) * decay
    yield
    eye = jnp.where(ri == ci, 1.0, 0.0).astype(F32)
    x_inv = eye
    for lb in range(lg):
        rb = jnp.right_shift(ri, ls + lb)
        cb = jnp.right_shift(ci, ls + lb)
        join = (jnp.bitwise_and(rb, 1) == 1) & (cb == rb - 1)
        b_lvl = jnp.where(join, a_off, 0.0)
        if lb == 0:
            x_inv = eye - b_lvl
        else:
            xb = _mm(x_inv, b_lvl)
            yield
            x_inv = x_inv - _mm(xb, x_inv)
            yield
    rhs = jnp.concatenate([v * beta_col, k * (beta_col * jnp.exp(g_col))], axis=1)
    sol = _mm(x_inv, rhs)
    yield
    u = sol[:, :HEAD_DIM]
    w = sol[:, HEAD_DIM:]
    stacked = jnp.concatenate([w, q * jnp.exp(g_col)], axis=0)
    row = lax.broadcasted_iota(jnp.int32, (c, 1), 0)
    states = [load_state(s) for s in range(n_seg)]
    w_s = None
    q_s = None
    for s in range(n_seg):
        r = _mm(stacked, states[s])
        if n_seg == 1:
            w_s, q_s = r[:c], r[c:]
        else:
            in_seg = jnp.bitwise_and(row, n_seg - 1) == s
            w_s = jnp.where(in_seg, r[:c], 0.0 if w_s is None else w_s)
            q_s = jnp.where(in_seg, r[c:], 0.0 if q_s is None else q_s)
    v_new = u - w_s
    yield
    o = q_s + _mm(qk, v_new)
    for s in range(n_seg):
        last = (seg_len - 1) * n_seg + s
        g_last = g_col[last:last + 1, :]
        if n_seg == 1:
            k_dec = k * jnp.exp(g_last - g_col)
        else:
            in_seg = jnp.bitwise_and(row, n_seg - 1) == s
            k_dec = jnp.where(in_seg, k * jnp.exp(jnp.where(in_seg, g_last - g_col, 0.0)), 0.0)
        store_state(s, states[s] * jnp.exp(g_last) + _mm_tn(k_dec, v_new))
    return o


def _expand_heads(slab, first_lane, width):
    rows = slab.shape[0]
    cols = [slab[:, first_lane + hd:first_lane + hd + 1] for hd in range(GDN_HEADS)]
    if width % LANES == 0:
        return jnp.concatenate([jnp.broadcast_to(c, (rows, width)) for c in cols], axis=1)
    total = GDN_HEADS * width
    lane_head = jnp.right_shift(lax.broadcasted_iota(jnp.int32, (rows, total), 1), _log2(width))
    out = jnp.broadcast_to(cols[-1], (rows, total))
    for hd in range(GDN_HEADS - 2, -1, -1):
        out = jnp.where(lane_head == hd, jnp.broadcast_to(cols[hd], (rows, total)), out)
    return out


def _packed_masks():
    c, nh = CHUNK, GDN_HEADS
    ri = lax.broadcasted_iota(jnp.int32, (c, nh * c), 0)
    cj = jnp.bitwise_and(lax.broadcasted_iota(jnp.int32, (c, nh * c), 1), c - 1)
    return dict(lower=ri >= cj, strict=ri > cj, eye=ri == cj, join0=(jnp.bitwise_and(ri, 1) == 1) & (cj == ri - 1))


def _fill_block_factors(bd_ref, kbd_ref):
    c, nh = CHUNK, GDN_HEADS
    w = nh * c
    lg = _log2(c)
    rw = lax.broadcasted_iota(jnp.int32, (w, w), 0)
    cw = lax.broadcasted_iota(jnp.int32, (w, w), 1)
    same_head = jnp.right_shift(rw, lg) == jnp.right_shift(cw, lg)
    rw, cw = jnp.bitwise_and(rw, c - 1), jnp.bitwise_and(cw, c - 1)
    one_zero = lambda m: jnp.where(m, 1.0, 0.0).astype(BF16)
    bd_ref[0] = one_zero(same_head)
    for lb in range(1, lg):
        bd_ref[lb] = one_zero(same_head & (jnp.bitwise_and(jnp.right_shift(rw, lb), 1) == 1)
                              & (jnp.right_shift(cw, lb) == jnp.right_shift(rw, lb) - 1))
    kbd_ref[...] = one_zero(jnp.right_shift(lax.broadcasted_iota(jnp.int32, (w, nh * HEAD_DIM), 0), lg)
                            == jnp.right_shift(lax.broadcasted_iota(jnp.int32, (w, nh * HEAD_DIM), 1),
                                               _log2(HEAD_DIM)))


def _wy_tile_packed(q_all, k_all, v_all, g, beta, state_ref, rows, masks, bd_ref, kbd_ref):
    c, nh = CHUNK, GDN_HEADS
    lg = _log2(c)
    n_chunks = rows // c
    lower, strict, eye = masks["lower"], masks["strict"], masks["eye"]

    def block_diag(x, factor):
        return jnp.concatenate([x.astype(BF16)] * nh, axis=0) * factor

    def head(x, hd, width):
        return x[:, hd * width:(hd + 1) * width]

    g_c = _expand_heads(g, 0, c)
    b_c = _expand_heads(beta, nh, c)
    g_d = _expand_heads(g, 0, HEAD_DIM)
    b_d = _expand_heads(beta, nh, HEAD_DIM)
    eg_d = jnp.exp(g_d)
    v_rhs = v_all * b_d
    k_rhs = k_all * (b_d * eg_d)
    q_g = q_all * eg_d

    a_off, qk_dec = [], []
    for cb in range(n_chunks):
        rs = slice(cb * c, (cb + 1) * c)
        g_row = jnp.sum(jnp.where(eye, g_c[rs], 0.0), axis=0, keepdims=True)
        decay = jnp.where(lower, jnp.exp(jnp.where(lower, g_c[rs] - g_row, 0.0)), 0.0)
        kq = _mm_nt(jnp.concatenate([k_all[rs], q_all[rs]], axis=0), block_diag(k_all[rs], kbd_ref[...]))
        a_off.append(jnp.where(strict, b_c[rs] * kq[:c] * decay, 0.0))
        qk_dec.append(kq[c:] * decay)
    yield

    eye_f = jnp.where(eye, 1.0, 0.0).astype(F32)
    x_inv = [eye_f - jnp.where(masks["join0"], a, 0.0) for a in a_off]
    for lb in range(1, lg):
        xb = [jnp.dot(x.astype(BF16), block_diag(a, bd_ref[lb]), preferred_element_type=F32)
              for x, a in zip(x_inv, a_off)]
        yield
        xbx = [jnp.dot(t.astype(BF16), block_diag(x, bd_ref[0]), preferred_element_type=F32)
               for t, x in zip(xb, x_inv)]
        x_inv = [x - t for x, t in zip(x_inv, xbx)]
        yield

    u, wk = [], []
    for cb in range(n_chunks):
        rs = slice(cb * c, (cb + 1) * c)
        sol = [_mm(head(x_inv[cb], hd, c),
                   jnp.concatenate([head(v_rhs[rs], hd, HEAD_DIM), head(k_rhs[rs], hd, HEAD_DIM)], axis=1))
               for hd in range(nh)]
        u.append([s[:, :HEAD_DIM] for s in sol])
        wk.append([s[:, HEAD_DIM:] for s in sol])
    yield

    state = [state_ref[0, hd] for hd in range(nh)]
    o_blocks = []
    for cb in range(n_chunks):
        rs = slice(cb * c, (cb + 1) * c)
        g_last = g_d[(cb + 1) * c - 1:(cb + 1) * c, :]
        k_dec = k_all[rs] * jnp.exp(g_last - g_d[rs])
        eg_last = jnp.exp(g_last)
        r = [_mm(jnp.concatenate([wk[cb][hd], head(q_g[rs], hd, HEAD_DIM)], axis=0), state[hd]) for hd in range(nh)]
        v_new = [u[cb][hd] - r[hd][:c] for hd in range(nh)]
        yield
        o_blocks.append(jnp.concatenate(
            [r[hd][c:] + _mm(head(qk_dec[cb], hd, c), v_new[hd]) for hd in range(nh)], axis=1))
        state = [state[hd] * head(eg_last, hd, HEAD_DIM) + _mm_tn(head(k_dec, hd, HEAD_DIM), v_new[hd])
                 for hd in range(nh)]
        yield
    for hd in range(nh):
        state_ref[0, hd] = state[hd]
    return jnp.concatenate(o_blocks, axis=0)


def _run_tiles(programs):
    programs = list(programs)
    results = [None] * len(programs)
    live = list(range(len(programs)))
    while live:
        for idx in list(live):
            try:
                next(programs[idx])
            except StopIteration as done:
                results[idx] = done.value
                live.remove(idx)
    return results


def _shifted_history(buf, rows, n_taps):
    with_carry = buf[0:SUBLANES + rows, :]
    return [pltpu.roll(with_carry, s, 0)[SUBLANES:, :] for s in range(1, n_taps)]


def _shifted_positions(x_rows, taps):
    n_seq = taps[0].shape[0]
    rows = x_rows.shape[0]
    ext = jnp.concatenate(list(taps) + [x_rows], axis=0)
    return [ext[(len(taps) - s) * n_seq:(len(taps) - s) * n_seq + rows] for s in range(1, len(taps) + 1)]


def _mixer_tile(ti, rows, seg_len, long_seq, masks, r, ab, rest):
    n_seq = rows // seg_len
    new_rows = slice(SUBLANES, SUBLANES + rows)
    row = lax.broadcasted_iota(jnp.int32, (rows, 1), 0)
    if long_seq:
        qb, sb = r.qbuf.at[ti], r.sbuf.at[ti]
        pos, row_step = jnp.bitwise_and(row, seg_len - 1), 1
    else:
        assert rows == CHUNK
        pos, row_step = jnp.right_shift(row, _log2(n_seq)), n_seq

    gate = r.gate[...]
    log_a = -jnp.exp(gate[0:1, :]) * _softplus(ab + gate[1:2, :])
    beta = _sigmoid(ab)
    g = log_a
    shift = 1
    while shift < seg_len:
        g = g + jnp.where(pos >= shift, pltpu.roll(g, shift * row_step, 0), 0.0)
        shift *= 2

    sc_pre = rest[:, R_SCC:R_SCC + SC_WIDTH] * rest[:, R_SCH:R_SCH + SC_WIDTH]
    if long_seq:
        sb[new_rows, :] = sc_pre
        qkv_pre = qb[new_rows, :]
        cwq = r.cwq[...]
        with_carry = qb[0:SUBLANES + rows, :]
        back1 = pltpu.roll(with_carry, 1, 0)
        pair = with_carry * cwq[1:2, :] + back1 * cwq[0:1, :]
        qkv = _silu(pltpu.roll(pair, 2, 0)[SUBLANES:, :] + back1[SUBLANES:, :] * cwq[2:3, :]
                    + qkv_pre * cwq[3:4, :])
        r.qtail[ti] = qb[rows:rows + SUBLANES, :]
        r.stail[ti] = sb[rows + SUBLANES - (SC_CONV - 1):rows + SUBLANES, :]
    else:
        qkv_pre = r.qkv
        cwq = r.cwq[...]
        q1, q2, q3 = _shifted_positions(qkv_pre, r.hq)
        qkv = _silu(q3 * cwq[0:1, :] + q2 * cwq[1:2, :] + q1 * cwq[2:3, :] + qkv_pre * cwq[3:4, :])
        for j in range(SC_CONV - 1):
            first = (seg_len - (SC_CONV - 1) + j) * n_seq
            r.stail[:, j, :] = sc_pre[first:first + n_seq]
    q_n, k_n, v_n = [], [], []
    for hd in range(GDN_HEADS):
        lo = hd * HEAD_DIM
        q_h = qkv[:, lo:lo + HEAD_DIM]
        k_h = qkv[:, GDN_WIDTH + lo:GDN_WIDTH + lo + HEAD_DIM]
        q_n.append(q_h * (lax.rsqrt(jnp.sum(q_h * q_h, axis=-1, keepdims=True) + EPS) * (HEAD_DIM ** -0.5)))
        k_n.append(k_h * lax.rsqrt(jnp.sum(k_h * k_h, axis=-1, keepdims=True) + EPS))
        v_n.append(qkv[:, 2 * GDN_WIDTH + lo:2 * GDN_WIDTH + lo + HEAD_DIM])
    yield

    state_ref = r.s_out.at[ti]
    if long_seq:
        o_all = yield from _wy_tile_packed(jnp.concatenate(q_n, axis=1), jnp.concatenate(k_n, axis=1),
                                           jnp.concatenate(v_n, axis=1), g, beta, state_ref, rows, masks,
                                           r.bd, r.kbd)
        o_h = [o_all[:, hd * HEAD_DIM:(hd + 1) * HEAD_DIM] for hd in range(GDN_HEADS)]
    else:
        g_t = g.T
        blocks = []
        for hd in range(GDN_HEADS):

            def load_state(s, hd=hd):
                return r.s_in[ti, s, hd]

            def store_state(s, val, hd=hd):
                state_ref[s, hd] = val

            blocks.append(_wy_block(q_n[hd], k_n[hd], v_n[hd], g[:, hd:hd + 1], g_t[hd:hd + 1, :],
                                    beta[:, GDN_HEADS + hd:GDN_HEADS + hd + 1], seg_len, load_state, store_state))
        o_h = _run_tiles(blocks)
    yield

    s1, s2 = _shifted_history(sb, rows, SC_CONV) if long_seq else _shifted_positions(sc_pre, r.hs)
    cws = r.cws[...]
    y_sc = rest[:, R_SCB:R_SCB + SC_WIDTH] * (s2 * cws[0:1, :] + s1 * cws[1:2, :] + sc_pre * cws[2:3, :])
    if long_seq:
        qb[0:SUBLANES, :] = qb[rows:rows + SUBLANES, :]
        sb[0:SUBLANES, :] = sb[rows:rows + SUBLANES, :]
    gnorm = r.gnorm[...]
    o_heads = []
    for hd in range(GDN_HEADS):
        z_h = rest[:, R_Z + hd * HEAD_DIM:R_Z + (hd + 1) * HEAD_DIM]
        o_heads.append(_rmsnorm(o_h[hd], gnorm) * _silu(z_h))
    return jnp.concatenate(o_heads + [y_sc], axis=1)


def _load_by_position(ref):
    n_seq, seg_len, _ = ref.shape
    per_tile = CHUNK // seg_len
    return jnp.concatenate([ref[b:b + per_tile, t, :] for b in range(0, n_seq, per_tile) for t in range(seg_len)],
                           axis=0)


def _store_by_position(ref, rows):
    n_seq, seg_len, _ = ref.shape
    per_tile = CHUNK // seg_len
    for b in range(0, n_seq, per_tile):
        for t in range(seg_len):
            first = b * seg_len + t * per_tile
            ref[b:b + per_tile, t, :] = rows[first:first + per_tile]


class _Refs:
    def __init__(self, **refs):
        self.__dict__.update(refs)


def _mixer_seq_kernel(n_tiles, rows, n_cast, *refs):
    n_in = 9
    x, nmix, w_ab, w_main, cwq, gate, gnorm, cws, wout = refs[:n_in]
    cast_in = refs[n_in:n_in + n_cast]
    xmid, qtail, stail, s_out = refs[n_in + n_cast:n_in + n_cast + 4]
    cast_out = refs[n_in + n_cast + 4:n_in + 2 * n_cast + 4]
    qbuf, sbuf, bd, kbd = refs[n_in + 2 * n_cast + 4:]
    for src, dst in zip(cast_in, cast_out):
        dst[...] = src[...].astype(BF16)

    @pl.when(pl.program_id(1) == 0)
    def _():
        s_out[...] = jnp.zeros(s_out.shape, F32)
        qbuf[:, 0:SUBLANES, :] = jnp.zeros((n_tiles, SUBLANES, QKV_DIM), F32)
        sbuf[:, 0:SUBLANES, :] = jnp.zeros((n_tiles, SUBLANES, SC_WIDTH), F32)
        _fill_block_factors(bd, kbd)

    x_all = x[...].reshape(n_tiles * rows, D_MODEL)
    h = _rmsnorm(x_all, nmix[...]).astype(BF16)
    ab = jnp.dot(h, w_ab[...], preferred_element_type=F32)
    qbuf[:, SUBLANES:SUBLANES + rows, :] = jnp.dot(h, w_main[:, 0:QKV_DIM], preferred_element_type=F32).reshape(
        n_tiles, rows, QKV_DIM)
    rest = jnp.dot(h, w_main[:, QKV_DIM:], preferred_element_type=F32)
    r = _Refs(cwq=cwq, gate=gate, gnorm=gnorm, cws=cws, qtail=qtail, stail=stail, s_out=s_out, qbuf=qbuf, sbuf=sbuf,
              bd=bd, kbd=kbd)
    masks = _packed_masks()
    tile = lambda a, ti: a[ti * rows:(ti + 1) * rows]
    mix = _run_tiles(_mixer_tile(ti, rows, CHUNK, True, masks, r, tile(ab, ti), tile(rest, ti))
                     for ti in range(n_tiles))
    xmid[...] = (x_all + _mm(jnp.concatenate(mix, axis=0), wout[...])).reshape(n_tiles, rows, D_MODEL)


def _mixer_step_kernel(rows, seg_len,
                       ab, proj, hq, hs, s_in, cwq, gate, gnorm, cws,
                       mix, qtail, stail, s_out):
    n_seq = rows // seg_len
    qkv = proj[0, :, 0:QKV_DIM]
    for j in range(GDN_CONV - 1):
        first = (seg_len - (GDN_CONV - 1) + j) * n_seq
        qtail[j] = qkv[first:first + n_seq]
    r = _Refs(qkv=qkv, hq=[hq[j] for j in range(GDN_CONV - 1)], hs=[hs[:, j, :] for j in range(SC_CONV - 1)],
              cwq=cwq, gate=gate, gnorm=gnorm, cws=cws, stail=stail, s_in=s_in, s_out=s_out)
    mix[0] = _run_tiles([_mixer_tile(0, rows, seg_len, False, None, r, ab[0], proj[0, :, QKV_DIM:])])[0].astype(BF16)


def _ffn_body(x, p, nmlp_ref, wup_ref, wdown_ref, nple_ref, wg_ref, wp_ref, nf_ref, before_chunk=None):
    hn = _rmsnorm(x, nmlp_ref[...]).astype(BF16)
    acc = x
    for j in range(D_FF // D_MODEL):
        if before_chunk is not None:
            before_chunk(j)
        u = jnp.maximum(_mm(hn, wup_ref[:, j * D_MODEL:(j + 1) * D_MODEL]), 0.0)
        acc = acc + _mm(u * u, wdown_ref[j * D_MODEL:(j + 1) * D_MODEL, :])
    gate = _sigmoid(_mm(_rmsnorm(acc, nple_ref[...]), wg_ref[...]))
    x3 = acc + gate * _mm(p, wp_ref[...])
    return _rmsnorm(x3, nf_ref[...])


def _ffn_kernel(x_ref, p_ref, *rest):
    rest[-1][...] = _ffn_body(x_ref[...], p_ref[...], *rest[:-1])


def _out_ffn_kernel(x_ref, mix_ref, wout_ref, p_ref, nmlp_ref, wup_hbm, wdown_hbm, nple_ref, wg_ref, wp_ref, nf_ref,
                    y_ref, wup_buf, wdown_buf, sems):
    copies = []
    for j in range(D_FF // D_MODEL):
        part = slice(j * D_MODEL, (j + 1) * D_MODEL)
        copies.append((pltpu.make_async_copy(wup_hbm.at[:, part], wup_buf.at[:, part], sems.at[0, j]),
                       pltpu.make_async_copy(wdown_hbm.at[part, :], wdown_buf.at[part, :], sems.at[1, j])))
    for up, down in copies:
        up.start()
        down.start()

    def wait_slice(j):
        copies[j][0].wait()
        copies[j][1].wait()

    x = _load_by_position(x_ref) + jnp.dot(mix_ref[...], wout_ref[...], preferred_element_type=F32)
    _store_by_position(y_ref, _ffn_body(x, _load_by_position(p_ref), nmlp_ref, wup_buf, wdown_buf, nple_ref, wg_ref,
                                        wp_ref, nf_ref, before_chunk=wait_slice))


def _const_spec(shape):
    nd = len(shape)
    return pl.BlockSpec(shape, lambda *_: (0,) * nd, pipeline_mode=pl.Buffered(1))


def _state_spec(n_tiles, n_state):
    return pl.BlockSpec((n_tiles, n_state, GDN_HEADS, HEAD_DIM, HEAD_DIM), lambda b, i: (b, 0, 0, 0, 0))


def _mixer_seq_call(x3, weights, to_cast, *, n_tiles, rows):
    seqs, seq_rows, _ = x3.shape
    tiles = seq_rows // rows
    n_steps = (seqs // n_tiles) * tiles
    cast_specs = [pl.BlockSpec((w.shape[0] // n_steps, w.shape[1]), lambda b, i: (b * tiles + i, 0)) for w in to_cast]
    row_map = lambda b, i: (b, i, 0)
    seq_map = lambda b, i: (b, 0, 0)
    x_spec = pl.BlockSpec((n_tiles, rows, D_MODEL), row_map)
    hq_spec = pl.BlockSpec((n_tiles, SUBLANES, QKV_DIM), seq_map)
    hs_spec = pl.BlockSpec((n_tiles, SC_CONV - 1, SC_WIDTH), seq_map)
    packed_w = GDN_HEADS * CHUNK
    return pl.pallas_call(
        functools.partial(_mixer_seq_kernel, n_tiles, rows, len(to_cast)),
        grid=(seqs // n_tiles, tiles),
        in_specs=[x_spec] + [_const_spec(w.shape) for w in weights] + cast_specs,
        out_specs=[x_spec, hq_spec, hs_spec, _state_spec(n_tiles, 1)] + cast_specs,
        out_shape=[jax.ShapeDtypeStruct(x3.shape, F32), jax.ShapeDtypeStruct((seqs, SUBLANES, QKV_DIM), F32),
                   jax.ShapeDtypeStruct((seqs, SC_CONV - 1, SC_WIDTH), F32),
                   jax.ShapeDtypeStruct((seqs, 1, GDN_HEADS, HEAD_DIM, HEAD_DIM), F32)]
        + [jax.ShapeDtypeStruct(w.shape, BF16) for w in to_cast],
        scratch_shapes=[pltpu.VMEM((n_tiles, rows + SUBLANES, QKV_DIM), F32),
                        pltpu.VMEM((n_tiles, rows + SUBLANES, SC_WIDTH), F32),
                        pltpu.VMEM((_log2(CHUNK), packed_w, packed_w), BF16),
                        pltpu.VMEM((packed_w, GDN_WIDTH), BF16)],
        compiler_params=pltpu.CompilerParams(
            dimension_semantics=("arbitrary", "arbitrary"), vmem_limit_bytes=VMEM_LIMIT_BYTES),
        name="mixer_seq",
    )(x3, *weights, *to_cast)


def _in_proj_kernel(n_qkv_steps, wt_ref, behind_ref, x_ref, nmix_ref, w_ab_ref, w_main_ref, ab_ref, proj_ref, h_ref):
    i = pl.program_id(0)

    @pl.when(i == 0)
    def _():
        h_ref[...] = _rmsnorm(_load_by_position(x_ref), nmix_ref[...]).astype(BF16)

    on_grid = wt_ref[...]
    behind = behind_ref[...]
    group = jnp.where(i < n_qkv_steps, on_grid, jnp.concatenate([on_grid[SUBLANES:], behind], axis=0))
    for c0 in range(0, IN_PROJ_COLS, SPLIT_COLS):
        w_main_ref[:, c0:c0 + SPLIT_COLS] = group[c0:c0 + SPLIT_COLS].T.astype(BF16)
    proj_ref[...] = jnp.dot(h_ref[...], w_main_ref[...], preferred_element_type=F32)

    @pl.when(i == n_qkv_steps - 1)
    def _():
        w_ab = jnp.concatenate([behind, jnp.zeros((LANES - SUBLANES, D_MODEL), F32)], axis=0).T.astype(BF16)
        w_ab_ref[...] = w_ab
        ab_ref[...] = jnp.dot(h_ref[...], w_ab, preferred_element_type=F32)


def _full_spec(a):
    nd = len(a.shape)
    return pl.BlockSpec(a.shape, lambda i: (0,) * nd)


def _in_proj_call(w_in, x3, nmix):
    rows, cols = w_in.shape
    n_seq, seg_len, _ = x3.shape
    n = n_seq * seg_len
    n_main = cols - 2 * GDN_HEADS
    assert 2 * GDN_HEADS == SUBLANES and QKV_DIM % IN_PROJ_COLS == 0 and n_main % IN_PROJ_COLS == 0
    w_t = w_in.T
    col_block = lambda height: pl.BlockSpec((height, IN_PROJ_COLS), lambda i: (0, i))
    outs = [jax.ShapeDtypeStruct((rows, LANES), BF16), jax.ShapeDtypeStruct((rows, n_main), BF16),
            jax.ShapeDtypeStruct((n, LANES), F32), jax.ShapeDtypeStruct((n, n_main), F32)]
    return pl.pallas_call(
        functools.partial(_in_proj_kernel, QKV_DIM // IN_PROJ_COLS),
        grid=(n_main // IN_PROJ_COLS,),
        in_specs=[pl.BlockSpec((IN_PROJ_COLS, rows), lambda i: (i, 0)),
                  pl.BlockSpec((SUBLANES, rows), lambda i: ((i + 1) * (IN_PROJ_COLS // SUBLANES), 0)),
                  _full_spec(x3), _full_spec(nmix)],
        out_specs=[_full_spec(outs[0]), col_block(rows), _full_spec(outs[2]), col_block(n)],
        out_shape=outs,
        scratch_shapes=[pltpu.VMEM((n, rows), BF16)],
        compiler_params=pltpu.CompilerParams(dimension_semantics=("arbitrary",), vmem_limit_bytes=VMEM_LIMIT_BYTES),
        name="in_proj",
    )(w_t, w_t, x3, nmix)


def _mixer_step_call(ab, proj, hist_q, hist_s, s_in, weights, *, seg_len):
    tiles, rows, _ = proj.shape
    tile_map = lambda b, i: (b, 0, 0)
    slab = lambda a: pl.BlockSpec((1, rows, a.shape[2]), tile_map)
    n_state = s_in.shape[1]
    hist_q_spec = pl.BlockSpec((GDN_CONV - 1, n_state, QKV_DIM), lambda b, i: (0, b, 0))
    hist_s_spec = pl.BlockSpec((n_state, SC_CONV - 1, SC_WIDTH), tile_map)
    mix = jax.ShapeDtypeStruct((tiles, rows, D_MODEL), BF16)
    like = lambda a: jax.ShapeDtypeStruct(a.shape, F32)
    return pl.pallas_call(
        functools.partial(_mixer_step_kernel, rows, seg_len),
        grid=(tiles, 1),
        in_specs=[slab(ab), slab(proj), hist_q_spec, hist_s_spec, _state_spec(1, n_state)]
        + [_const_spec(w.shape) for w in weights],
        out_specs=[slab(mix), hist_q_spec, hist_s_spec, _state_spec(1, n_state)],
        out_shape=[mix, like(hist_q), like(hist_s), like(s_in)],
        compiler_params=pltpu.CompilerParams(
            dimension_semantics=("arbitrary", "arbitrary"), vmem_limit_bytes=VMEM_LIMIT_BYTES),
        name="mixer_step",
    )(ab, proj, hist_q, hist_s, s_in, *weights)


def _ffn_call(x, p, weights, *, name, rows=None, mix=None, wout=None):
    w_specs = [_const_spec(w.shape) for w in weights]
    if mix is None:
        steps = x.shape[0] // rows
        rows_spec = lambda width: pl.BlockSpec((rows, width), lambda i: (i, 0))
        x_spec, p_spec, pre_args, pre_specs, body = rows_spec(D_MODEL), rows_spec(PLE_DIM), [], [], _ffn_kernel
        scratch = []
    else:
        steps = 1
        x_spec, p_spec, body = _full_spec(x), _full_spec(p), _out_ffn_kernel
        pre_args, pre_specs = [mix, wout], [_full_spec(mix), _const_spec(wout.shape)]
        wup, wdown = weights[1], weights[2]
        w_specs[1] = w_specs[2] = pl.BlockSpec(memory_space=pl.ANY)
        scratch = [pltpu.VMEM(wup.shape, wup.dtype), pltpu.VMEM(wdown.shape, wdown.dtype),
                   pltpu.SemaphoreType.DMA((2, D_FF // D_MODEL))]
    return pl.pallas_call(
        body,
        grid=(steps,),
        in_specs=[x_spec] + pre_specs + [p_spec] + w_specs,
        out_specs=x_spec,
        out_shape=jax.ShapeDtypeStruct(x.shape, F32),
        scratch_shapes=scratch,
        input_output_aliases={0: 0} if mix is None else {},
        compiler_params=pltpu.CompilerParams(dimension_semantics=("arbitrary",), vmem_limit_bytes=VMEM_LIMIT_BYTES),
        name=name,
    )(x, *pre_args, p, *weights)


def _layer(x_prompt, x_sample, conv_qkv, s_gdn, conv_sc, p_prompt, p_sample, norm_mix, w_in, w_conv_qkv, a_log,
           dt_bias, w_gdn_norm, w_conv_sc, w_out, norm_mlp, w_up, w_down, norm_ple, w_ple_gate, w_ple_proj, norm_f):
    bp, tp, _ = x_prompt.shape
    bs, ts, _ = x_sample.shape
    nmix = norm_mix.reshape(1, D_MODEL)
    w_ab, w_main, ab_s, proj_s = _in_proj_call(w_in, x_sample, nmix)
    wout = w_out.astype(BF16)
    gate = jnp.zeros((SUBLANES, LANES), F32)
    gate = gate.at[0, :GDN_HEADS].set(a_log.astype(F32)).at[1, :GDN_HEADS].set(dt_bias.astype(F32))
    core_w = (w_conv_qkv, gate, w_gdn_norm.reshape(1, HEAD_DIM), w_conv_sc)

    xm_p, qt_p, new_sc_p, s_p, wup, wdown, wgate = _mixer_seq_call(
        x_prompt, (nmix, w_ab, w_main) + core_w + (wout,), (w_up, w_down, w_ple_gate),
        n_tiles=PROMPT_SEQS_PER_STEP, rows=PROMPT_ROWS)
    ffn_w = (norm_mlp.reshape(1, D_MODEL), wup, wdown, norm_ple.reshape(1, D_MODEL), wgate,
             w_ple_proj.astype(BF16), norm_f.reshape(1, D_MODEL))
    y_p = _ffn_call(xm_p.reshape(bp * tp, D_MODEL), p_prompt.reshape(bp * tp, PLE_DIM), ffn_w, rows=FFN_ROWS,
                    name="ffn_prompt")

    seq_per_tile = CHUNK // ts
    tiles = bs // seq_per_tile
    tiled = lambda a: a.reshape(tiles, CHUNK, a.shape[-1])
    mix_s, qt_s, new_sc_s, s_s = _mixer_step_call(
        tiled(ab_s), tiled(proj_s), jnp.transpose(conv_qkv, (1, 0, 2)), conv_sc,
        s_gdn.reshape(tiles, seq_per_tile, GDN_HEADS, HEAD_DIM, HEAD_DIM), core_w, seg_len=ts)
    y_s = _ffn_call(x_sample, p_sample, ffn_w, name="ffn_sample", mix=mix_s.reshape(bs * ts, D_MODEL), wout=wout)

    new_conv_p = qt_p[:, SUBLANES - (GDN_CONV - 1):]
    new_conv_s = jnp.transpose(qt_s, (1, 0, 2))
    return (y_p.reshape(bp, tp, D_MODEL), y_s, new_conv_p,
            s_p.reshape(bp, GDN_HEADS, HEAD_DIM, HEAD_DIM), new_sc_p, new_conv_s,
            s_s.reshape(bs, GDN_HEADS, HEAD_DIM, HEAD_DIM), new_sc_s)


def kernel(x_prompt, x_sample, state_gdn_conv, state_gdn, state_sc_conv, p_prompt, p_sample, norm_mix, w_in, w_conv_qkv, a_log, dt_bias, w_gdn_norm, w_conv_sc, w_out, norm_mlp, w_up, w_down, norm_ple, w_ple_gate, w_ple_proj, norm_f):
    depth = w_in.shape[0]
    assert depth == 1, "one layer per call"
    assert x_sample.shape[1] >= GDN_CONV - 1 and CHUNK % x_sample.shape[1] == 0
    assert x_prompt.shape[1] % PROMPT_ROWS == 0 and x_prompt.shape[0] % PROMPT_SEQS_PER_STEP == 0
    outs = _layer(x_prompt, x_sample, state_gdn_conv[0], state_gdn[0], state_sc_conv[0], p_prompt[0], p_sample[0],
                  norm_mix[0], w_in[0], w_conv_qkv[0], a_log[0], dt_bias[0], w_gdn_norm[0], w_conv_sc[0], w_out[0],
                  norm_mlp[0], w_up[0], w_down[0], norm_ple[0], w_ple_gate[0], w_ple_proj[0], norm_f)
    y_p, y_s, c_p, s_p, sc_p, c_s, s_s, sc_s = outs
    return (y_p, y_s, c_p[None], s_p[None], sc_p[None], c_s[None], s_s[None], sc_s[None])
```

```python
import functools

import jax
import jax.numpy as jnp
from jax import lax
from jax.experimental import pallas as pl
from jax.experimental.pallas import tpu as pltpu

F32 = jnp.float32
BF16 = jnp.bfloat16

D_MODEL = 1024
PLE_DIM = 256
GDN_HEADS = 4
HEAD_DIM = 128
GDN_WIDTH = GDN_HEADS * HEAD_DIM
QKV_DIM = 3 * GDN_WIDTH
GDN_CONV = 4
SC_WIDTH = D_MODEL - GDN_WIDTH
SC_CONV = 3
D_FF = 4 * D_MODEL
EPS = 1e-6
NEG_LOG2_E = -1.4426950408889634
CHUNK = 64
LANES = 128
SUBLANES = 8
VMEM_LIMIT_BYTES = 56 * 1024 * 1024
PROMPT_ROWS = 128
PROMPT_SEQS_PER_STEP = 4
FFN_ROWS = 1024
IN_PROJ_COLS = 512
SPLIT_COLS = 256

R_Z = 0
R_SCB = R_Z + GDN_WIDTH
R_SCC = R_SCB + SC_WIDTH
R_SCH = R_SCC + SC_WIDTH


def _mm(a, b):
    return jnp.dot(a.astype(BF16), b.astype(BF16), preferred_element_type=F32)


def _mm_nt(a, b):
    return lax.dot_general(a.astype(BF16), b.astype(BF16), (((1,), (1,)), ((), ())), preferred_element_type=F32)


def _mm_tn(a, b):
    return lax.dot_general(a.astype(BF16), b.astype(BF16), (((0,), (0,)), ((), ())), preferred_element_type=F32)


def _rmsnorm(x, w_row):
    return x * lax.rsqrt(jnp.mean(x * x, axis=-1, keepdims=True) + EPS) * w_row


def _sigmoid(x):
    return 1.0 / (1.0 + jnp.exp2(x * NEG_LOG2_E))


def _silu(x):
    return x * _sigmoid(x)


def _softplus(x):
    return jnp.maximum(x, 0.0) + jnp.log1p(jnp.exp(-jnp.abs(x)))


def _log2(n):
    k = n.bit_length() - 1
    assert (1 << k) == n, n
    return k


def _wy_block(q, k, v, g_col, g_row, beta_col, seg_len, load_state, store_state):
    c = CHUNK
    n_seg = c // seg_len
    lg = _log2(seg_len)
    ls = _log2(n_seg)
    ri = lax.broadcasted_iota(jnp.int32, (c, c), 0)
    ci = lax.broadcasted_iota(jnp.int32, (c, c), 1)
    same = jnp.bitwise_and(ri, n_seg - 1) == jnp.bitwise_and(ci, n_seg - 1)
    lower = same & (ri >= ci)
    strict = same & (ri > ci)
    diff = g_col - g_row
    decay = jnp.where(lower, jnp.exp(jnp.where(lower, diff, 0.0)), 0.0)
    a_off = jnp.where(strict, beta_col * _mm_nt(k, k) * decay, 0.0)
    qk = _mm_nt(q, k) * decay
    yield
    eye = jnp.where(ri == ci, 1.0, 0.0).astype(F32)
    x_inv = eye
    for lb in range(lg):
        rb = jnp.right_shift(ri, ls + lb)
        cb = jnp.right_shift(ci, ls + lb)
        join = (jnp.bitwise_and(rb, 1) == 1) & (cb == rb - 1)
        b_lvl = jnp.where(join, a_off, 0.0)
        if lb == 0:
            x_inv = eye - b_lvl
        else:
            xb = _mm(x_inv, b_lvl)
            yield
            x_inv = x_inv - _mm(xb, x_inv)
            yield
    rhs = jnp.concatenate([v * beta_col, k * (beta_col * jnp.exp(g_col))], axis=1)
    sol = _mm(x_inv, rhs)
    yield
    u = sol[:, :HEAD_DIM]
    w = sol[:, HEAD_DIM:]
    stacked = jnp.concatenate([w, q * jnp.exp(g_col)], axis=0)
    row = lax.broadcasted_iota(jnp.int32, (c, 1), 0)
    states = [load_state(s) for s in range(n_seg)]
    w_s = None
    q_s = None
    for s in range(n_seg):
        r = _mm(stacked, states[s])
        if n_seg == 1:
            w_s, q_s = r[:c], r[c:]
        else:
            in_seg = jnp.bitwise_and(row, n_seg - 1) == s
            w_s = jnp.where(in_seg, r[:c], 0.0 if w_s is None else w_s)
            q_s = jnp.where(in_seg, r[c:], 0.0 if q_s is None else q_s)
    v_new = u - w_s
    yield
    o = q_s + _mm(qk, v_new)
    for s in range(n_seg):
        last = (seg_len - 1) * n_seg + s
        g_last = g_col[last:last + 1, :]
        if n_seg == 1:
            k_dec = k * jnp.exp(g_last - g_col)
        else:
            in_seg = jnp.bitwise_and(row, n_seg - 1) == s
            k_dec = jnp.where(in_seg, k * jnp.exp(jnp.where(in_seg, g_last - g_col, 0.0)), 0.0)
        store_state(s, states[s] * jnp.exp(g_last) + _mm_tn(k_dec, v_new))
    return o


def _expand_heads(slab, first_lane, width):
    rows = slab.shape[0]
    cols = [slab[:, first_lane + hd:first_lane + hd + 1] for hd in range(GDN_HEADS)]
    if width % LANES == 0:
        return jnp.concatenate([jnp.broadcast_to(c, (rows, width)) for c in cols], axis=1)
    total = GDN_HEADS * width
    lane_head = jnp.right_shift(lax.broadcasted_iota(jnp.int32, (rows, total), 1), _log2(width))
    out = jnp.broadcast_to(cols[-1], (rows, total))
    for hd in range(GDN_HEADS - 2, -1, -1):
        out = jnp.where(lane_head == hd, jnp.broadcast_to(cols[hd], (rows, total)), out)
    return out


def _packed_masks():
    c, nh = CHUNK, GDN_HEADS
    ri = lax.broadcasted_iota(jnp.int32, (c, nh * c), 0)
    cj = jnp.bitwise_and(lax.broadcasted_iota(jnp.int32, (c, nh * c), 1), c - 1)
    return dict(lower=ri >= cj, strict=ri > cj, eye=ri == cj, join0=(jnp.bitwise_and(ri, 1) == 1) & (cj == ri - 1))


def _fill_block_factors(bd_ref, kbd_ref):
    c, nh = CHUNK, GDN_HEADS
    w = nh * c
    lg = _log2(c)
    rw = lax.broadcasted_iota(jnp.int32, (w, w), 0)
    cw = lax.broadcasted_iota(jnp.int32, (w, w), 1)
    same_head = jnp.right_shift(rw, lg) == jnp.right_shift(cw, lg)
    rw, cw = jnp.bitwise_and(rw, c - 1), jnp.bitwise_and(cw, c - 1)
    one_zero = lambda m: jnp.where(m, 1.0, 0.0).astype(BF16)
    bd_ref[0] = one_zero(same_head)
    for lb in range(1, lg):
        bd_ref[lb] = one_zero(same_head & (jnp.bitwise_and(jnp.right_shift(rw, lb), 1) == 1)
                              & (jnp.right_shift(cw, lb) == jnp.right_shift(rw, lb) - 1))
    kbd_ref[...] = one_zero(jnp.right_shift(lax.broadcasted_iota(jnp.int32, (w, nh * HEAD_DIM), 0), lg)
                            == jnp.right_shift(lax.broadcasted_iota(jnp.int32, (w, nh * HEAD_DIM), 1),
                                               _log2(HEAD_DIM)))


def _wy_tile_packed(q_all, k_all, v_all, g, beta, state_ref, rows, masks, bd_ref, kbd_ref):
    c, nh = CHUNK, GDN_HEADS
    lg = _log2(c)
    n_chunks = rows // c
    lower, strict, eye = masks["lower"], masks["strict"], masks["eye"]

    def block_diag(x, factor):
        return jnp.concatenate([x.astype(BF16)] * nh, axis=0) * factor

    def head(x, hd, width):
        return x[:, hd * width:(hd + 1) * width]

    g_c = _expand_heads(g, 0, c)
    b_c = _expand_heads(beta, nh, c)
    g_d = _expand_heads(g, 0, HEAD_DIM)
    b_d = _expand_heads(beta, nh, HEAD_DIM)
    eg_d = jnp.exp(g_d)
    v_rhs = v_all * b_d
    k_rhs = k_all * (b_d * eg_d)
    q_g = q_all * eg_d

    a_off, qk_dec = [], []
    for cb in range(n_chunks):
        rs = slice(cb * c, (cb + 1) * c)
        g_row = jnp.sum(jnp.where(eye, g_c[rs], 0.0), axis=0, keepdims=True)
        decay = jnp.where(lower, jnp.exp(jnp.where(lower, g_c[rs] - g_row, 0.0)), 0.0)
        kq = _mm_nt(jnp.concatenate([k_all[rs], q_all[rs]], axis=0), block_diag(k_all[rs], kbd_ref[...]))
        a_off.append(jnp.where(strict, b_c[rs] * kq[:c] * decay, 0.0))
        qk_dec.append(kq[c:] * decay)
    yield

    eye_f = jnp.where(eye, 1.0, 0.0).astype(F32)
    x_inv = [eye_f - jnp.where(masks["join0"], a, 0.0) for a in a_off]
    for lb in range(1, lg):
        xb = [jnp.dot(x.astype(BF16), block_diag(a, bd_ref[lb]), preferred_element_type=F32)
              for x, a in zip(x_inv, a_off)]
        yield
        xbx = [jnp.dot(t.astype(BF16), block_diag(x, bd_ref[0]), preferred_element_type=F32)
               for t, x in zip(xb, x_inv)]
        x_inv = [x - t for x, t in zip(x_inv, xbx)]
        yield

    u, wk = [], []
    for cb in range(n_chunks):
        rs = slice(cb * c, (cb + 1) * c)
        sol = [_mm(head(x_inv[cb], hd, c),
                   jnp.concatenate([head(v_rhs[rs], hd, HEAD_DIM), head(k_rhs[rs], hd, HEAD_DIM)], axis=1))
               for hd in range(nh)]
        u.append([s[:, :HEAD_DIM] for s in sol])
        wk.append([s[:, HEAD_DIM:] for s in sol])
    yield

    state = [state_ref[0, hd] for hd in range(nh)]
    o_blocks = []
    for cb in range(n_chunks):
        rs = slice(cb * c, (cb + 1) * c)
        g_last = g_d[(cb + 1) * c - 1:(cb + 1) * c, :]
        k_dec = k_all[rs] * jnp.exp(g_last - g_d[rs])
        eg_last = jnp.exp(g_last)
        r = [_mm(jnp.concatenate([wk[cb][hd], head(q_g[rs], hd, HEAD_DIM)], axis=0), state[hd]) for hd in range(nh)]
        v_new = [u[cb][hd] - r[hd][:c] for hd in range(nh)]
        yield
        o_blocks.append(jnp.concatenate(
            [r[hd][c:] + _mm(head(qk_dec[cb], hd, c), v_new[hd]) for hd in range(nh)], axis=1))
        state = [state[hd] * head(eg_last, hd, HEAD_DIM) + _mm_tn(head(k_dec, hd, HEAD_DIM), v_new[hd])
                 for hd in range(nh)]
        yield
    for hd in range(nh):
        state_ref[0, hd] = state[hd]
    return jnp.concatenate(o_blocks, axis=0)


def _run_tiles(programs):
    programs = list(programs)
    results = [None] * len(programs)
    live = list(range(len(programs)))
    while live:
        for idx in list(live):
            try:
                next(programs[idx])
            except StopIteration as done:
                results[idx] = done.value
                live.remove(idx)
    return results


def _shifted_history(buf, rows, n_taps):
    with_carry = buf[0:SUBLANES + rows, :]
    return [pltpu.roll(with_carry, s, 0)[SUBLANES:, :] for s in range(1, n_taps)]


def _shifted_positions(x_rows, taps):
    n_seq = taps[0].shape[0]
    rows = x_rows.shape[0]
    ext = jnp.concatenate(list(taps) + [x_rows], axis=0)
    return [ext[(len(taps) - s) * n_seq:(len(taps) - s) * n_seq + rows] for s in range(1, len(taps) + 1)]


def _mixer_tile(ti, rows, seg_len, long_seq, masks, r, ab, rest):
    n_seq = rows // seg_len
    new_rows = slice(SUBLANES, SUBLANES + rows)
    row = lax.broadcasted_iota(jnp.int32, (rows, 1), 0)
    if long_seq:
        qb, sb = r.qbuf.at[ti], r.sbuf.at[ti]
        pos, row_step = jnp.bitwise_and(row, seg_len - 1), 1
    else:
        assert rows == CHUNK
        pos, row_step = jnp.right_shift(row, _log2(n_seq)), n_seq

    gate = r.gate[...]
    log_a = -jnp.exp(gate[0:1, :]) * _softplus(ab + gate[1:2, :])
    beta = _sigmoid(ab)
    g = log_a
    shift = 1
    while shift < seg_len:
        g = g + jnp.where(pos >= shift, pltpu.roll(g, shift * row_step, 0), 0.0)
        shift *= 2

    sc_pre = rest[:, R_SCC:R_SCC + SC_WIDTH] * rest[:, R_SCH:R_SCH + SC_WIDTH]
    if long_seq:
        sb[new_rows, :] = sc_pre
        qkv_pre = qb[new_rows, :]
        cwq = r.cwq[...]
        with_carry = qb[0:SUBLANES + rows, :]
        back1 = pltpu.roll(with_carry, 1, 0)
        pair = with_carry * cwq[1:2, :] + back1 * cwq[0:1, :]
        qkv = _silu(pltpu.roll(pair, 2, 0)[SUBLANES:, :] + back1[SUBLANES:, :] * cwq[2:3, :]
                    + qkv_pre * cwq[3:4, :])
        r.qtail[ti] = qb[rows:rows + SUBLANES, :]
        r.stail[ti] = sb[rows + SUBLANES - (SC_CONV - 1):rows + SUBLANES, :]
    else:
        qkv_pre = r.qkv
        cwq = r.cwq[...]
        q1, q2, q3 = _shifted_positions(qkv_pre, r.hq)
        qkv = _silu(q3 * cwq[0:1, :] + q2 * cwq[1:2, :] + q1 * cwq[2:3, :] + qkv_pre * cwq[3:4, :])
        for j in range(SC_CONV - 1):
            first = (seg_len - (SC_CONV - 1) + j) * n_seq
            r.stail[:, j, :] = sc_pre[first:first + n_seq]
    q_n, k_n, v_n = [], [], []
    for hd in range(GDN_HEADS):
        lo = hd * HEAD_DIM
        q_h = qkv[:, lo:lo + HEAD_DIM]
        k_h = qkv[:, GDN_WIDTH + lo:GDN_WIDTH + lo + HEAD_DIM]
        q_n.append(q_h * (lax.rsqrt(jnp.sum(q_h * q_h, axis=-1, keepdims=True) + EPS) * (HEAD_DIM ** -0.5)))
        k_n.append(k_h * lax.rsqrt(jnp.sum(k_h * k_h, axis=-1, keepdims=True) + EPS))
        v_n.append(qkv[:, 2 * GDN_WIDTH + lo:2 * GDN_WIDTH + lo + HEAD_DIM])
    yield

    state_ref = r.s_out.at[ti]
    if long_seq:
        o_all = yield from _wy_tile_packed(jnp.concatenate(q_n, axis=1), jnp.concatenate(k_n, axis=1),
                                           jnp.concatenate(v_n, axis=1), g, beta, state_ref, rows, masks,
                                           r.bd, r.kbd)
        o_h = [o_all[:, hd * HEAD_DIM:(hd + 1) * HEAD_DIM] for hd in range(GDN_HEADS)]
    else:
        g_t = g.T
        blocks = []
        for hd in range(GDN_HEADS):

            def load_state(s, hd=hd):
                return r.s_in[ti, s, hd]

            def store_state(s, val, hd=hd):
                state_ref[s, hd] = val

            blocks.append(_wy_block(q_n[hd], k_n[hd], v_n[hd], g[:, hd:hd + 1], g_t[hd:hd + 1, :],
                                    beta[:, GDN_HEADS + hd:GDN_HEADS + hd + 1], seg_len, load_state, store_state))
        o_h = _run_tiles(blocks)
    yield

    s1, s2 = _shifted_history(sb, rows, SC_CONV) if long_seq else _shifted_positions(sc_pre, r.hs)
    cws = r.cws[...]
    y_sc = rest[:, R_SCB:R_SCB + SC_WIDTH] * (s2 * cws[0:1, :] + s1 * cws[1:2, :] + sc_pre * cws[2:3, :])
    if long_seq:
        qb[0:SUBLANES, :] = qb[rows:rows + SUBLANES, :]
        sb[0:SUBLANES, :] = sb[rows:rows + SUBLANES, :]
    gnorm = r.gnorm[...]
    o_heads = []
    for hd in range(GDN_HEADS):
        z_h = rest[:, R_Z + hd * HEAD_DIM:R_Z + (hd + 1) * HEAD_DIM]
        o_heads.append(_rmsnorm(o_h[hd], gnorm) * _silu(z_h))
    return jnp.concatenate(o_heads + [y_sc], axis=1)


def _load_by_position(ref):
    n_seq, seg_len, _ = ref.shape
    per_tile = CHUNK // seg_len
    return jnp.concatenate([ref[b:b + per_tile, t, :] for b in range(0, n_seq, per_tile) for t in range(seg_len)],
                           axis=0)


def _store_by_position(ref, rows):
    n_seq, seg_len, _ = ref.shape
    per_tile = CHUNK // seg_len
    for b in range(0, n_seq, per_tile):
        for t in range(seg_len):
            first = b * seg_len + t * per_tile
            ref[b:b + per_tile, t, :] = rows[first:first + per_tile]


class _Refs:
    def __init__(self, **refs):
        self.__dict__.update(refs)


def _mixer_seq_kernel(n_tiles, rows, n_cast, *refs):
    n_in = 9
    x, nmix, w_ab, w_main, cwq, gate, gnorm, cws, wout = refs[:n_in]
    cast_in = refs[n_in:n_in + n_cast]
    xmid, qtail, stail, s_out = refs[n_in + n_cast:n_in + n_cast + 4]
    cast_out = refs[n_in + n_cast + 4:n_in + 2 * n_cast + 4]
    qbuf, sbuf, bd, kbd = refs[n_in + 2 * n_cast + 4:]
    for src, dst in zip(cast_in, cast_out):
        dst[...] = src[...].astype(BF16)

    @pl.when(pl.program_id(1) == 0)
    def _():
        s_out[...] = jnp.zeros(s_out.shape, F32)
        qbuf[:, 0:SUBLANES, :] = jnp.zeros((n_tiles, SUBLANES, QKV_DIM), F32)
        sbuf[:, 0:SUBLANES, :] = jnp.zeros((n_tiles, SUBLANES, SC_WIDTH), F32)
        _fill_block_factors(bd, kbd)

    x_all = x[...].reshape(n_tiles * rows, D_MODEL)
    h = _rmsnorm(x_all, nmix[...]).astype(BF16)
    ab = jnp.dot(h, w_ab[...], preferred_element_type=F32)
    qbuf[:, SUBLANES:SUBLANES + rows, :] = jnp.dot(h, w_main[:, 0:QKV_DIM], preferred_element_type=F32).reshape(
        n_tiles, rows, QKV_DIM)
    rest = jnp.dot(h, w_main[:, QKV_DIM:], preferred_element_type=F32)
    r = _Refs(cwq=cwq, gate=gate, gnorm=gnorm, cws=cws, qtail=qtail, stail=stail, s_out=s_out, qbuf=qbuf, sbuf=sbuf,
              bd=bd, kbd=kbd)
    masks = _packed_masks()
    tile = lambda a, ti: a[ti * rows:(ti + 1) * rows]
    mix = _run_tiles(_mixer_tile(ti, rows, CHUNK, True, masks, r, tile(ab, ti), tile(rest, ti))
                     for ti in range(n_tiles))
    xmid[...] = (x_all + _mm(jnp.concatenate(mix, axis=0), wout[...])).reshape(n_tiles, rows, D_MODEL)


def _mixer_step_kernel(rows, seg_len,
                       ab, proj, hq, hs, s_in, cwq, gate, gnorm, cws,
                       mix, qtail, stail, s_out):
    n_seq = rows // seg_len
    qkv = proj[0, :, 0:QKV_DIM]
    for j in range(GDN_CONV - 1):
        first = (seg_len - (GDN_CONV - 1) + j) * n_seq
        qtail[j] = qkv[first:first + n_seq]
    r = _Refs(qkv=qkv, hq=[hq[j] for j in range(GDN_CONV - 1)], hs=[hs[:, j, :] for j in range(SC_CONV - 1)],
              cwq=cwq, gate=gate, gnorm=gnorm, cws=cws, stail=stail, s_in=s_in, s_out=s_out)
    mix[0] = _run_tiles([_mixer_tile(0, rows, seg_len, False, None, r, ab[0], proj[0, :, QKV_DIM:])])[0].astype(BF16)


def _ffn_body(x, p, nmlp_ref, wup_ref, wdown_ref, nple_ref, wg_ref, wp_ref, nf_ref, before_chunk=None):
    hn = _rmsnorm(x, nmlp_ref[...]).astype(BF16)
    acc = x
    n_chunks = D_FF // D_MODEL
    for j in range(n_chunks + 1):
        if before_chunk is not None:
            before_chunk(j)
        if j == n_chunks:
            break
        u = jnp.maximum(_mm(hn, wup_ref[:, j * D_MODEL:(j + 1) * D_MODEL]), 0.0)
        acc = acc + _mm(u * u, wdown_ref[j * D_MODEL:(j + 1) * D_MODEL, :])
    gate = _sigmoid(_mm(_rmsnorm(acc, nple_ref[...]), wg_ref[...]))
    x3 = acc + gate * _mm(p, wp_ref[...])
    return _rmsnorm(x3, nf_ref[...])


def _ffn_kernel(x_ref, p_ref, *rest):
    rest[-1][...] = _ffn_body(x_ref[...], p_ref[...], *rest[:-1])


def _out_ffn_kernel(x_ref, mix_ref, wout_ref, p_ref, nmlp_ref, wup_hbm, wdown_hbm, nple_ref, wg_hbm, wp_hbm, nf_ref,
                    y_ref, wup_buf, wdown_buf, wg_buf, wp_buf, sems):
    n_chunks = D_FF // D_MODEL
    groups = []
    for j in range(n_chunks):
        part = slice(j * D_MODEL, (j + 1) * D_MODEL)
        groups.append((pltpu.make_async_copy(wup_hbm.at[:, part], wup_buf.at[:, part], sems.at[0, j]),
                       pltpu.make_async_copy(wdown_hbm.at[part, :], wdown_buf.at[part, :], sems.at[1, j])))
    groups.append((pltpu.make_async_copy(wg_hbm, wg_buf, sems.at[0, n_chunks]),
                   pltpu.make_async_copy(wp_hbm, wp_buf, sems.at[1, n_chunks])))
    for first, second in groups:
        first.start(priority=0)
        second.start(priority=1)

    def wait_group(j):
        groups[j][0].wait()
        groups[j][1].wait()

    x = _load_by_position(x_ref) + jnp.dot(mix_ref[...], wout_ref[...], preferred_element_type=F32)
    _store_by_position(y_ref, _ffn_body(x, _load_by_position(p_ref), nmlp_ref, wup_buf, wdown_buf, nple_ref, wg_buf,
                                        wp_buf, nf_ref, before_chunk=wait_group))


def _const_spec(shape):
    nd = len(shape)
    return pl.BlockSpec(shape, lambda *_: (0,) * nd, pipeline_mode=pl.Buffered(1))


def _state_spec(n_tiles, n_state):
    return pl.BlockSpec((n_tiles, n_state, GDN_HEADS, HEAD_DIM, HEAD_DIM), lambda b, i: (b, 0, 0, 0, 0))


def _mixer_seq_call(x3, weights, to_cast, *, n_tiles, rows):
    seqs, seq_rows, _ = x3.shape
    tiles = seq_rows // rows
    n_steps = (seqs // n_tiles) * tiles
    cast_specs = [pl.BlockSpec((w.shape[0] // n_steps, w.shape[1]), lambda b, i: (b * tiles + i, 0)) for w in to_cast]
    row_map = lambda b, i: (b, i, 0)
    seq_map = lambda b, i: (b, 0, 0)
    x_spec = pl.BlockSpec((n_tiles, rows, D_MODEL), row_map)
    hq_spec = pl.BlockSpec((n_tiles, SUBLANES, QKV_DIM), seq_map)
    hs_spec = pl.BlockSpec((n_tiles, SC_CONV - 1, SC_WIDTH), seq_map)
    packed_w = GDN_HEADS * CHUNK
    return pl.pallas_call(
        functools.partial(_mixer_seq_kernel, n_tiles, rows, len(to_cast)),
        grid=(seqs // n_tiles, tiles),
        in_specs=[x_spec] + [_const_spec(w.shape) for w in weights] + cast_specs,
        out_specs=[x_spec, hq_spec, hs_spec, _state_spec(n_tiles, 1)] + cast_specs,
        out_shape=[jax.ShapeDtypeStruct(x3.shape, F32), jax.ShapeDtypeStruct((seqs, SUBLANES, QKV_DIM), F32),
                   jax.ShapeDtypeStruct((seqs, SC_CONV - 1, SC_WIDTH), F32),
                   jax.ShapeDtypeStruct((seqs, 1, GDN_HEADS, HEAD_DIM, HEAD_DIM), F32)]
        + [jax.ShapeDtypeStruct(w.shape, BF16) for w in to_cast],
        scratch_shapes=[pltpu.VMEM((n_tiles, rows + SUBLANES, QKV_DIM), F32),
                        pltpu.VMEM((n_tiles, rows + SUBLANES, SC_WIDTH), F32),
                        pltpu.VMEM((_log2(CHUNK), packed_w, packed_w), BF16),
                        pltpu.VMEM((packed_w, GDN_WIDTH), BF16)],
        compiler_params=pltpu.CompilerParams(
            dimension_semantics=("arbitrary", "arbitrary"), vmem_limit_bytes=VMEM_LIMIT_BYTES),
        name="mixer_seq",
    )(x3, *weights, *to_cast)


def _in_proj_kernel(n_qkv_steps, wt_ref, behind_ref, x_ref, nmix_ref, w_ab_ref, w_main_ref, ab_ref, proj_ref, h_ref):
    i = pl.program_id(0)

    @pl.when(i == 0)
    def _():
        h_ref[...] = _rmsnorm(_load_by_position(x_ref), nmix_ref[...]).astype(BF16)

    on_grid = wt_ref[...]
    behind = behind_ref[...]
    group = jnp.where(i < n_qkv_steps, on_grid, jnp.concatenate([on_grid[SUBLANES:], behind], axis=0))
    for c0 in range(0, IN_PROJ_COLS, SPLIT_COLS):
        w_main_ref[:, c0:c0 + SPLIT_COLS] = group[c0:c0 + SPLIT_COLS].T.astype(BF16)
    proj_ref[...] = jnp.dot(h_ref[...], w_main_ref[...], preferred_element_type=F32)

    @pl.when(i == n_qkv_steps - 1)
    def _():
        w_ab = jnp.concatenate([behind, jnp.zeros((LANES - SUBLANES, D_MODEL), F32)], axis=0).T.astype(BF16)
        w_ab_ref[...] = w_ab
        ab_ref[...] = jnp.dot(h_ref[...], w_ab, preferred_element_type=F32)


def _full_spec(a):
    nd = len(a.shape)
    return pl.BlockSpec(a.shape, lambda i: (0,) * nd)


def _in_proj_call(w_in, x3, nmix):
    rows, cols = w_in.shape
    n_seq, seg_len, _ = x3.shape
    n = n_seq * seg_len
    n_main = cols - 2 * GDN_HEADS
    assert 2 * GDN_HEADS == SUBLANES and QKV_DIM % IN_PROJ_COLS == 0 and n_main % IN_PROJ_COLS == 0
    w_t = w_in.T
    col_block = lambda height: pl.BlockSpec((height, IN_PROJ_COLS), lambda i: (0, i))
    outs = [jax.ShapeDtypeStruct((rows, LANES), BF16), jax.ShapeDtypeStruct((rows, n_main), BF16),
            jax.ShapeDtypeStruct((n, LANES), F32), jax.ShapeDtypeStruct((n, n_main), F32)]
    return pl.pallas_call(
        functools.partial(_in_proj_kernel, QKV_DIM // IN_PROJ_COLS),
        grid=(n_main // IN_PROJ_COLS,),
        in_specs=[pl.BlockSpec((IN_PROJ_COLS, rows), lambda i: (i, 0)),
                  pl.BlockSpec((SUBLANES, rows), lambda i: ((i + 1) * (IN_PROJ_COLS // SUBLANES), 0)),
                  _full_spec(x3), _full_spec(nmix)],
        out_specs=[_full_spec(outs[0]), col_block(rows), _full_spec(outs[2]), col_block(n)],
        out_shape=outs,
        scratch_shapes=[pltpu.VMEM((n, rows), BF16)],
        compiler_params=pltpu.CompilerParams(dimension_semantics=("arbitrary",), vmem_limit_bytes=VMEM_LIMIT_BYTES),
        name="in_proj",
    )(w_t, w_t, x3, nmix)


def _mixer_step_call(ab, proj, hist_q, hist_s, s_in, weights, *, seg_len):
    tiles, rows, _ = proj.shape
    tile_map = lambda b, i: (b, 0, 0)
    slab = lambda a: pl.BlockSpec((1, rows, a.shape[2]), tile_map)
    n_state = s_in.shape[1]
    hist_q_spec = pl.BlockSpec((GDN_CONV - 1, n_state, QKV_DIM), lambda b, i: (0, b, 0))
    hist_s_spec = pl.BlockSpec((n_state, SC_CONV - 1, SC_WIDTH), tile_map)
    mix = jax.ShapeDtypeStruct((tiles, rows, D_MODEL), BF16)
    like = lambda a: jax.ShapeDtypeStruct(a.shape, F32)
    return pl.pallas_call(
        functools.partial(_mixer_step_kernel, rows, seg_len),
        grid=(tiles, 1),
        in_specs=[slab(ab), slab(proj), hist_q_spec, hist_s_spec, _state_spec(1, n_state)]
        + [_const_spec(w.shape) for w in weights],
        out_specs=[slab(mix), hist_q_spec, hist_s_spec, _state_spec(1, n_state)],
        out_shape=[mix, like(hist_q), like(hist_s), like(s_in)],
        compiler_params=pltpu.CompilerParams(
            dimension_semantics=("arbitrary", "arbitrary"), vmem_limit_bytes=VMEM_LIMIT_BYTES),
        name="mixer_step",
    )(ab, proj, hist_q, hist_s, s_in, *weights)


def _ffn_call(x, p, weights, *, name, rows=None, mix=None, wout=None):
    w_specs = [_const_spec(w.shape) for w in weights]
    if mix is None:
        steps = x.shape[0] // rows
        rows_spec = lambda width: pl.BlockSpec((rows, width), lambda i: (i, 0))
        x_spec, p_spec, pre_args, pre_specs, body = rows_spec(D_MODEL), rows_spec(PLE_DIM), [], [], _ffn_kernel
        scratch = []
    else:
        steps = 1
        x_spec, p_spec, body = _full_spec(x), _full_spec(p), _out_ffn_kernel
        pre_args, pre_specs = [mix, wout], [_full_spec(mix), _const_spec(wout.shape)]
        copied = (1, 2, 4, 5)
        for k in copied:
            w_specs[k] = pl.BlockSpec(memory_space=pl.ANY)
        scratch = [pltpu.VMEM(weights[k].shape, weights[k].dtype) for k in copied]
        scratch.append(pltpu.SemaphoreType.DMA((2, D_FF // D_MODEL + 1)))
    return pl.pallas_call(
        body,
        grid=(steps,),
        in_specs=[x_spec] + pre_specs + [p_spec] + w_specs,
        out_specs=x_spec,
        out_shape=jax.ShapeDtypeStruct(x.shape, F32),
        scratch_shapes=scratch,
        compiler_params=pltpu.CompilerParams(dimension_semantics=("arbitrary",), vmem_limit_bytes=VMEM_LIMIT_BYTES),
        name=name,
    )(x, *pre_args, p, *weights)


def _layer(x_prompt, x_sample, conv_qkv, s_gdn, conv_sc, p_prompt, p_sample, norm_mix, w_in, w_conv_qkv, a_log,
           dt_bias, w_gdn_norm, w_conv_sc, w_out, norm_mlp, w_up, w_down, norm_ple, w_ple_gate, w_ple_proj, norm_f):
    bp, tp, _ = x_prompt.shape
    bs, ts, _ = x_sample.shape
    nmix = norm_mix.reshape(1, D_MODEL)
    w_ab, w_main, ab_s, proj_s = _in_proj_call(w_in, x_sample, nmix)
    wout = w_out.astype(BF16)
    gate = jnp.zeros((SUBLANES, LANES), F32)
    gate = gate.at[0, :GDN_HEADS].set(a_log.astype(F32)).at[1, :GDN_HEADS].set(dt_bias.astype(F32))
    core_w = (w_conv_qkv, gate, w_gdn_norm.reshape(1, HEAD_DIM), w_conv_sc)

    xm_p, qt_p, new_sc_p, s_p, wup, wdown, wgate = _mixer_seq_call(
        x_prompt, (nmix, w_ab, w_main) + core_w + (wout,), (w_up, w_down, w_ple_gate),
        n_tiles=PROMPT_SEQS_PER_STEP, rows=PROMPT_ROWS)
    ffn_w = (norm_mlp.reshape(1, D_MODEL), wup, wdown, norm_ple.reshape(1, D_MODEL), wgate,
             w_ple_proj.astype(BF16), norm_f.reshape(1, D_MODEL))
    y_p = _ffn_call(xm_p.reshape(bp * tp, D_MODEL), p_prompt.reshape(bp * tp, PLE_DIM), ffn_w, rows=FFN_ROWS,
                    name="ffn_prompt")

    seq_per_tile = CHUNK // ts
    tiles = bs // seq_per_tile
    tiled = lambda a: a.reshape(tiles, CHUNK, a.shape[-1])
    mix_s, qt_s, new_sc_s, s_s = _mixer_step_call(
        tiled(ab_s), tiled(proj_s), jnp.transpose(conv_qkv, (1, 0, 2)), conv_sc,
        s_gdn.reshape(tiles, seq_per_tile, GDN_HEADS, HEAD_DIM, HEAD_DIM), core_w, seg_len=ts)
    y_s = _ffn_call(x_sample, p_sample, ffn_w, name="ffn_sample", mix=mix_s.reshape(bs * ts, D_MODEL), wout=wout)

    new_conv_p = qt_p[:, SUBLANES - (GDN_CONV - 1):]
    new_conv_s = jnp.transpose(qt_s, (1, 0, 2))
    return (y_p.reshape(bp, tp, D_MODEL), y_s, new_conv_p,
            s_p.reshape(bp, GDN_HEADS, HEAD_DIM, HEAD_DIM), new_sc_p, new_conv_s,
            s_s.reshape(bs, GDN_HEADS, HEAD_DIM, HEAD_DIM), new_sc_s)


def kernel(x_prompt, x_sample, state_gdn_conv, state_gdn, state_sc_conv, p_prompt, p_sample, norm_mix, w_in, w_conv_qkv, a_log, dt_bias, w_gdn_norm, w_conv_sc, w_out, norm_mlp, w_up, w_down, norm_ple, w_ple_gate, w_ple_proj, norm_f):
    depth = w_in.shape[0]
    assert depth == 1, "one layer per call"
    assert x_sample.shape[1] >= GDN_CONV - 1 and CHUNK % x_sample.shape[1] == 0
    assert x_prompt.shape[1] % PROMPT_ROWS == 0 and x_prompt.shape[0] % PROMPT_SEQS_PER_STEP == 0
    outs = _layer(x_prompt, x_sample, state_gdn_conv[0], state_gdn[0], state_sc_conv[0], p_prompt[0], p_sample[0],
                  norm_mix[0], w_in[0], w_conv_qkv[0], a_log[0], dt_bias[0], w_gdn_norm[0], w_conv_sc[0], w_out[0],
                  norm_mlp[0], w_up[0], w_down[0], norm_ple[0], w_ple_gate[0], w_ple_proj[0], norm_f)
    y_p, y_s, c_p, s_p, sc_p, c_s, s_s, sc_s = outs
    return (y_p, y_s, c_p[None], s_p[None], sc_p[None], c_s[None], s_s[None], sc_s[None])
```
